```python
import math
import jax, jax.numpy as jnp
from jax import lax
import numpy as np

D_MODEL = 2048
BATCH = 4
SEQ = 4096
DEPTH = 2

HEAD_DIM = 64
WA_HEADS = 12
WA_KV_HEADS = 4
WA_WIDTH = WA_HEADS * HEAD_DIM
WA_KV_WIDTH = WA_KV_HEADS * HEAD_DIM
WINDOW = 128
WA_BLOCK = 128
T5_BUCKETS = 32
T5_MAX_DIST = 128
ML_HEADS = 4
ML_HEAD_DIM = 128
ML_WIDTH = ML_HEADS * ML_HEAD_DIM
ML_CHUNK = 64
NA_HEADS = 12
NA_WIDTH = NA_HEADS * HEAD_DIM
GRID_W = 64
NA_KH_MAX = 8
NA_KW = 16

MIX_WIDTH = WA_WIDTH + ML_WIDTH + NA_WIDTH
IN_SIZES = (
    WA_WIDTH, WA_KV_WIDTH, WA_KV_WIDTH, WA_WIDTH,
    ML_WIDTH, ML_WIDTH, ML_WIDTH, ML_WIDTH, ML_WIDTH, 4 * ML_HEADS,
    NA_WIDTH, NA_WIDTH, NA_WIDTH, NA_WIDTH,
)
IN_WIDTH = sum(IN_SIZES)
DEEPNORM_ALPHA = (2 * DEPTH) ** 0.25
DEEPNORM_BETA = (8 * DEPTH) ** -0.25
LN_EPS = 1e-5

kernel_name = "hybrid_bidir_wgqa_mlstm_natten_deepnorm"


def layer_norm(x, g, b):
    xf = x.astype(jnp.float32)
    mu = jnp.mean(xf, axis=-1, keepdims=True)
    var = jnp.mean(jnp.square(xf - mu), axis=-1, keepdims=True)
    y = (xf - mu) * lax.rsqrt(var + LN_EPS) * g.astype(jnp.float32) + b.astype(jnp.float32)
    return y.astype(x.dtype)


def split_columns(u):
    pts = []
    acc = 0
    for s in IN_SIZES[:-1]:
        acc += s
        pts.append(acc)
    return jnp.split(u, pts, axis=-1)


def t5_bucket(rel):
    half = T5_BUCKETS // 2
    max_exact = half // 2
    ret = jnp.where(rel > 0, half, 0)
    n = jnp.abs(rel)
    nf = jnp.maximum(n, 1).astype(jnp.float32)
    large = max_exact + (jnp.log(nf / max_exact) / math.log(T5_MAX_DIST / max_exact)
                         * (half - max_exact)).astype(jnp.int32)
    large = jnp.minimum(large, half - 1)
    return ret + jnp.where(n < max_exact, n, large)


def window_gqa(q, k, v, t5_table, sink):
    B, S = q.shape[0], q.shape[1]
    blk = WA_BLOCK
    nb = S // blk
    G = WA_HEADS // WA_KV_HEADS
    qb = q.reshape(B, nb, blk, WA_KV_HEADS, G, HEAD_DIM)
    pad = ((0, 0), (blk, blk), (0, 0), (0, 0))
    kp = jnp.pad(k, pad).reshape(B, nb + 2, blk, WA_KV_HEADS, HEAD_DIM)
    vp = jnp.pad(v, pad).reshape(B, nb + 2, blk, WA_KV_HEADS, HEAD_DIM)
    kb = jnp.concatenate([kp[:, :-2], kp[:, 1:-1], kp[:, 2:]], axis=2)
    vb = jnp.concatenate([vp[:, :-2], vp[:, 1:-1], vp[:, 2:]], axis=2)
    s = jnp.einsum('bnqhgd,bnkhd->bnhgqk', qb, kb,
                   preferred_element_type=jnp.float32) * (HEAD_DIM ** -0.5)
    qi = jnp.arange(blk)
    kj = jnp.arange(3 * blk)
    rel = (kj[None, :] - blk) - qi[:, None]
    bias = t5_table.astype(jnp.float32)[t5_bucket(rel)]
    bias = bias.transpose(2, 0, 1).reshape(WA_KV_HEADS, G, blk, 3 * blk)
    key_pos = (jnp.arange(nb)[:, None] - 1) * blk + kj[None, :]
    in_seq = (key_pos >= 0) & (key_pos < S)
    valid = (jnp.abs(rel) <= WINDOW)[None] & in_seq[:, None, :]
    s = jnp.where(valid[None, :, None, None], s + bias[None, None], -jnp.inf)
    sk = sink.astype(jnp.float32).reshape(1, 1, WA_KV_HEADS, G, 1, 1)
    m = jnp.maximum(jnp.max(s, axis=-1, keepdims=True), sk)
    p = jnp.exp(s - m)
    den = jnp.sum(p, axis=-1, keepdims=True) + jnp.exp(sk - m)
    o = jnp.einsum('bnhgqk,bnkhd->bnqhgd', p / den, vb.astype(jnp.float32))
    return o.reshape(B, S, WA_HEADS * HEAD_DIM)


def mlstm_direction(q, k, v, i_pre, f_pre):
    B, H, S, d = q.shape
    L = ML_CHUNK
    nc = S // L
    q = q.reshape(B, H, nc, L, d) * (d ** -0.5)
    k = k.reshape(B, H, nc, L, d)
    v = v.reshape(B, H, nc, L, d)
    ig = i_pre.reshape(B, H, nc, L)
    b = jnp.cumsum(jax.nn.log_sigmoid(f_pre).reshape(B, H, nc, L), axis=-1)
    g = b[..., -1]
    a = g[..., None] - b + ig
    m_loc = jnp.max(a, axis=-1)
    w = jnp.exp(a - m_loc[..., None])
    C_loc = jnp.einsum('bhcs,bhcsd,bhcse->bhcde', w, v, k)
    n_loc = jnp.einsum('bhcs,bhcsd->bhcd', w, k)

    def step(carry, xs):
        C, n, m = carry
        g_c, m_l, C_l, n_l = xs
        m_new = jnp.maximum(g_c + m, m_l)
        sp = jnp.exp(g_c + m - m_new)
        sl = jnp.exp(m_l - m_new)
        C_new = sp[..., None, None] * C + sl[..., None, None] * C_l
        n_new = sp[..., None] * n + sl[..., None] * n_l
        return (C_new, n_new, m_new), (C, n, m)

    init = (jnp.zeros((B, H, d, d), jnp.float32), jnp.zeros((B, H, d), jnp.float32),
            jnp.zeros((B, H), jnp.float32))
    xs = (jnp.moveaxis(g, 2, 0), jnp.moveaxis(m_loc, 2, 0),
          jnp.moveaxis(C_loc, 2, 0), jnp.moveaxis(n_loc, 2, 0))
    _, (C_prev, n_prev, m_prev) = lax.scan(step, init, xs)
    C_prev = jnp.moveaxis(C_prev, 0, 2)
    n_prev = jnp.moveaxis(n_prev, 0, 2)
    m_prev = jnp.moveaxis(m_prev, 0, 2)

    D = b[..., :, None] - b[..., None, :] + ig[..., None, :]
    lower = jnp.tril(jnp.ones((L, L), dtype=bool))
    D = jnp.where(lower, D, -jnp.inf)
    m_inter = b + m_prev[..., None]
    m_t = jnp.maximum(m_inter, jnp.max(D, axis=-1))
    qk = jnp.einsum('bhctd,bhcsd->bhcts', q, k) * jnp.exp(D - m_t[..., None])
    inter_w = jnp.exp(m_inter - m_t)
    num = (jnp.einsum('bhcts,bhcsd->bhctd', qk, v)
           + inter_w[..., None] * jnp.einsum('bhcde,bhcte->bhctd', C_prev, q))
    den = jnp.sum(qk, axis=-1) + inter_w * jnp.einsum('bhcd,bhctd->bhct', n_prev, q)
    h = num / jnp.maximum(jnp.abs(den), jnp.exp(-m_t))[..., None]
    return h.reshape(B, H, S, d)


def mlstm_mixer(q, k, v, o_pre, if_pre, f_bias, norm_g):
    B, S = q.shape[0], q.shape[1]

    def heads(t):
        return t.reshape(B, S, ML_HEADS, ML_HEAD_DIM).transpose(0, 2, 1, 3).astype(jnp.float32)

    qh, kh, vh = heads(q), heads(k), heads(v)
    gates = if_pre.astype(jnp.float32).reshape(B, S, 4, ML_HEADS).transpose(2, 0, 3, 1)
    fb = f_bias.astype(jnp.float32)
    i_f, f_f = gates[0], gates[1] + fb[0][:, None]
    i_b, f_b = gates[2], gates[3] + fb[1][:, None]

    def rev(t):
        return jnp.flip(t, axis=2)

    h_f = mlstm_direction(qh, kh, vh, i_f, f_f)
    h_b = rev(mlstm_direction(rev(qh), rev(kh), rev(vh), rev(i_b), rev(f_b)))
    h = (h_f + h_b).transpose(0, 2, 1, 3)
    h = jax.nn.sigmoid(o_pre.astype(jnp.float32)).reshape(B, S, ML_HEADS, ML_HEAD_DIM) * h
    mu = jnp.mean(h, axis=-1, keepdims=True)
    var = jnp.mean(jnp.square(h - mu), axis=-1, keepdims=True)
    h = (h - mu) * lax.rsqrt(var + LN_EPS)
    return h.reshape(B, S, ML_WIDTH) * norm_g.astype(jnp.float32)


def neighbourhood_attn(q, k, v, rpb):
    B, S, H, dh = q.shape
    rows = S // GRID_W
    kh = min(NA_KH_MAX, rows)
    kw = NA_KW
    r = jnp.arange(rows)
    c = jnp.arange(GRID_W)
    row_start = jnp.clip(r - kh // 2, 0, rows - kh)
    key_rows = row_start[:, None] + jnp.arange(kh)[None, :]
    col_start = jnp.clip(c - kw // 2, 0, GRID_W - kw)
    col_ok = (c[None, :] >= col_start[:, None]) & (c[None, :] < col_start[:, None] + kw)
    qg = q.reshape(B, rows, GRID_W, H, dh)
    kg = k.reshape(B, rows, GRID_W, H, dh)[:, key_rows]
    vg = v.reshape(B, rows, GRID_W, H, dh)[:, key_rows]
    s = jnp.einsum('brchd,brkxhd->bhrckx', qg, kg,
                   preferred_element_type=jnp.float32) * (dh ** -0.5)
    dr_idx = (key_rows - r[:, None]) + NA_KH_MAX - 1
    dc_idx = jnp.clip(c[None, :] - c[:, None], -(kw - 1), kw - 1) + kw - 1
    bias = rpb.astype(jnp.float32)[:, dr_idx[:, None, :, None], dc_idx[None, :, None, :]]
    s = jnp.where(col_ok[None, None, None, :, None, :], s + bias[None], -jnp.inf)
    p = jax.nn.softmax(s.reshape(B, H, rows, GRID_W, kh * GRID_W), axis=-1)
    p = p.reshape(B, H, rows, GRID_W, kh, GRID_W)
    o = jnp.einsum('bhrckx,brkxhd->brchd', p, vg.astype(jnp.float32))
    return o.reshape(B, S, H * dh)


def hybrid_layer(x, w_in, b_in, w_out, b_out, ln_g, ln_b, t5_table, sink, f_bias, ml_norm_g, rpb):
    B, S, _ = x.shape
    u = jnp.matmul(x, w_in) + b_in
    (wa_q, wa_k, wa_v, wa_z, ml_q, ml_k, ml_v, ml_o, ml_z, ml_if,
     na_q, na_k, na_v, na_z) = split_columns(u)
    ya = window_gqa(wa_q.reshape(B, S, WA_HEADS, HEAD_DIM),
                    wa_k.reshape(B, S, WA_KV_HEADS, HEAD_DIM),
                    wa_v.reshape(B, S, WA_KV_HEADS, HEAD_DIM), t5_table, sink)
    yb = mlstm_mixer(ml_q, ml_k, ml_v, ml_o, ml_if, f_bias, ml_norm_g)
    yc = neighbourhood_attn(na_q.reshape(B, S, NA_HEADS, HEAD_DIM),
                            na_k.reshape(B, S, NA_HEADS, HEAD_DIM),
                            na_v.reshape(B, S, NA_HEADS, HEAD_DIM), rpb)
    y = jnp.concatenate([ya * jax.nn.silu(wa_z.astype(jnp.float32)),
                         yb * jax.nn.silu(ml_z.astype(jnp.float32)),
                         yc * jax.nn.silu(na_z.astype(jnp.float32))], axis=-1).astype(x.dtype)
    out = jnp.matmul(y, w_out) + b_out
    return layer_norm(DEEPNORM_ALPHA * x + out, ln_g, ln_b)


def setup_inputs(seed: int = 0) -> dict:
    key = jax.random.key(seed)
    ks = jax.random.split(key, 14)
    nrm = jax.random.normal
    f32 = jnp.float32
    x = nrm(ks[0], (BATCH, SEQ, D_MODEL), f32)
    emb_ln_g = 1.0 + 0.02 * nrm(ks[1], (D_MODEL,), f32)
    emb_ln_b = 0.02 * nrm(ks[2], (D_MODEL,), f32)
    w_in = nrm(ks[3], (DEPTH, D_MODEL, IN_WIDTH), f32) * (D_MODEL ** -0.5)
    b_in = 0.02 * nrm(ks[4], (DEPTH, IN_WIDTH), f32)
    w_out = nrm(ks[5], (DEPTH, MIX_WIDTH, D_MODEL), f32) * (MIX_WIDTH ** -0.5) * DEEPNORM_BETA
    b_out = 0.02 * nrm(ks[6], (DEPTH, D_MODEL), f32)
    ln_g = 1.0 + 0.02 * nrm(ks[7], (DEPTH, D_MODEL), f32)
    ln_b = 0.02 * nrm(ks[8], (DEPTH, D_MODEL), f32)
    t5_bias = 0.1 * nrm(ks[9], (T5_BUCKETS, WA_HEADS), f32)
    sink = 0.5 * nrm(ks[10], (DEPTH, WA_HEADS), f32)
    ml_f_bias = (jnp.linspace(3.0, 6.0, ML_HEADS, dtype=f32)[None, None, :]
                 + 0.1 * nrm(ks[11], (DEPTH, 2, ML_HEADS), f32))
    ml_norm_g = 1.0 + 0.02 * nrm(ks[12], (DEPTH, ML_WIDTH), f32)
    na_rpb = 0.1 * nrm(ks[13], (DEPTH, NA_HEADS, 2 * NA_KH_MAX - 1, 2 * NA_KW - 1), f32)
    return {"x": x, "emb_ln_g": emb_ln_g, "emb_ln_b": emb_ln_b, "w_in": w_in, "b_in": b_in,
            "w_out": w_out, "b_out": b_out, "ln_g": ln_g, "ln_b": ln_b, "t5_bias": t5_bias,
            "sink": sink, "ml_f_bias": ml_f_bias, "ml_norm_g": ml_norm_g, "na_rpb": na_rpb}


def reference(x, emb_ln_g, emb_ln_b, w_in, b_in, w_out, b_out, ln_g, ln_b, t5_bias,
              sink, ml_f_bias, ml_norm_g, na_rpb):
    h = layer_norm(x, emb_ln_g, emb_ln_b)
    for l in range(DEPTH):
        h = hybrid_layer(h, w_in[l], b_in[l], w_out[l], b_out[l], ln_g[l], ln_b[l],
                         t5_bias, sink[l], ml_f_bias[l], ml_norm_g[l], na_rpb[l])
    return h
```

```python
import functools
import math

import numpy as np
import jax
import jax.numpy as jnp
from jax import lax
from jax.experimental import pallas as pl
from jax.experimental.pallas import tpu as pltpu

F32 = jnp.float32
BF16 = jnp.bfloat16

D_MODEL = 2048
HEAD_DIM = 64
LN_EPS = 1e-5
NEG = -1e30

WA_HEADS = 12
WA_KV_HEADS = 4
WA_GROUP = WA_HEADS // WA_KV_HEADS
WA_WIDTH = WA_HEADS * HEAD_DIM
WA_KV_WIDTH = WA_KV_HEADS * HEAD_DIM
WA_BLOCK = 128
WINDOW = 128
T5_BUCKETS = 32
T5_MAX_DIST = 128
ML_HEADS = 4
ML_HEAD_DIM = 128
ML_WIDTH = ML_HEADS * ML_HEAD_DIM
ML_CHUNK = 128
ML_AUG = 2 * ML_HEAD_DIM
NA_HEADS = 12
NA_WIDTH = NA_HEADS * HEAD_DIM
GRID_W = 64
NA_KH_MAX = 8
NA_KW = 16
NA_ROWS = 4
NA_BLOCK = NA_ROWS * GRID_W

MIX_WIDTH = WA_WIDTH + ML_WIDTH + NA_WIDTH
GATE_PAD = 128

A_Q, A_Z, A_K, A_V = 0, 768, 1536, 1792
B_Q, B_K, B_V, B_O, B_Z = 2048, 2560, 3072, 3584, 4096
C_Q, C_K, C_V, C_Z = 4608, 5376, 6144, 6912
U_WIDTH = 7680
_SRC_RANGES = ((0, 768), (1280, 2048), (768, 1280), (2048, 4608), (4624, 7696))
_SRC_GATES = (4608, 4624)

VMEM_LIMIT = 56 * 1024 * 1024


def _cparams(sem):
    return pltpu.CompilerParams(dimension_semantics=sem, vmem_limit_bytes=VMEM_LIMIT)


def _layer_norm_rows(x, g, b):
    mu = jnp.mean(x, axis=-1, keepdims=True)
    xc = x - mu
    var = jnp.mean(xc * xc, axis=-1, keepdims=True)
    return xc * lax.rsqrt(var + LN_EPS) * g + b


def _ln_kernel(x_ref, g_ref, b_ref, of_ref, ob_ref):
    y = _layer_norm_rows(x_ref[...].astype(F32), g_ref[...], b_ref[...])
    of_ref[...] = y
    ob_ref[...] = y.astype(BF16)


def _input_norm(x2, g, b):
    m, d = x2.shape
    tm = min(512, m)
    return pl.pallas_call(
        _ln_kernel,
        grid=(m // tm,),
        in_specs=[pl.BlockSpec((tm, d), lambda i: (i, 0)),
                  pl.BlockSpec((1, d), lambda i: (0, 0)),
                  pl.BlockSpec((1, d), lambda i: (0, 0))],
        out_specs=[pl.BlockSpec((tm, d), lambda i: (i, 0)),
                   pl.BlockSpec((tm, d), lambda i: (i, 0))],
        out_shape=[jax.ShapeDtypeStruct((m, d), F32), jax.ShapeDtypeStruct((m, d), BF16)],
        compiler_params=_cparams(("parallel",)),
        name="input_norm",
    )(x2, g.reshape(1, d), b.reshape(1, d))


def _proj_kernel(h_ref, w_ref, b_ref, o_ref):
    acc = jnp.dot(h_ref[...], w_ref[...], preferred_element_type=F32)
    o_ref[...] = (acc + b_ref[...]).astype(o_ref.dtype)


def _projection(hb, w, b, out_dtype, tn, name):
    m, d = hb.shape
    n = w.shape[1]
    tm = min(1024, m)
    return pl.pallas_call(
        _proj_kernel,
        grid=(n // tn, m // tm),
        in_specs=[pl.BlockSpec((tm, d), lambda j, i: (i, 0)),
                  pl.BlockSpec((d, tn), lambda j, i: (0, j)),
                  pl.BlockSpec((1, tn), lambda j, i: (0, j))],
        out_specs=pl.BlockSpec((tm, tn), lambda j, i: (i, j)),
        out_shape=jax.ShapeDtypeStruct((m, n), out_dtype),
        compiler_params=_cparams(("parallel", "parallel")),
        name=name,
    )(hb, w, b.reshape(1, n))


def _outproj_kernel(ya_ref, yb_ref, yc_ref, w_ref, b_ref, h_ref, g_ref, beta_ref, *out_refs, alpha):
    y = jnp.concatenate([ya_ref[...], yb_ref[...], yc_ref[...]], axis=1)
    out = jnp.dot(y, w_ref[...], preferred_element_type=F32) + b_ref[...]
    r = _layer_norm_rows(alpha * h_ref[...] + out, g_ref[...], beta_ref[...])
    out_refs[0][...] = r
    if len(out_refs) > 1:
        out_refs[1][...] = r.astype(BF16)


def _out_projection(ya, yb, yc, w, b, hres, g, beta, alpha, want_bf16):
    m, d = hres.shape
    tm = min(256, m)
    row = lambda i: (i, 0)
    const = lambda i: (0, 0)
    out_specs = [pl.BlockSpec((tm, d), row)]
    out_shape = [jax.ShapeDtypeStruct((m, d), F32)]
    if want_bf16:
        out_specs.append(pl.BlockSpec((tm, d), row))
        out_shape.append(jax.ShapeDtypeStruct((m, d), BF16))
    return pl.pallas_call(
        functools.partial(_outproj_kernel, alpha=alpha),
        grid=(m // tm,),
        in_specs=[pl.BlockSpec((tm, WA_WIDTH), row),
                  pl.BlockSpec((tm, ML_WIDTH), row),
                  pl.BlockSpec((tm, NA_WIDTH), row),
                  pl.BlockSpec((MIX_WIDTH, d), const),
                  pl.BlockSpec((1, d), const),
                  pl.BlockSpec((tm, d), row),
                  pl.BlockSpec((1, d), const),
                  pl.BlockSpec((1, d), const)],
        out_specs=out_specs,
        out_shape=out_shape,
        compiler_params=_cparams(("parallel",)),
        name="out_projection",
    )(ya, yb, yc, w, b.reshape(1, d), hres, g.reshape(1, d), beta.reshape(1, d))


def _silu(z):
    return z * jax.nn.sigmoid(z)


def _wattn_kernel(q_ref, z_ref, kp_ref, kc_ref, kn_ref, vp_ref, vc_ref, vn_ref, bias_ref, sink_ref,
                  o_ref, *, nb):
    n = pl.program_id(1)
    col = lax.broadcasted_iota(jnp.int32, (1, 3 * WA_BLOCK), 1)
    outside = ((n == 0) & (col < WA_BLOCK)) | ((n == nb - 1) & (col >= 2 * WA_BLOCK))
    pen = jnp.where(outside, NEG, 0.0).astype(F32)
    outs = []
    for g in range(WA_KV_HEADS):
        ks = slice(g * HEAD_DIM, (g + 1) * HEAD_DIM)
        k3 = jnp.concatenate([kp_ref[:, ks], kc_ref[:, ks], kn_ref[:, ks]], axis=0)
        v3 = jnp.concatenate([vp_ref[:, ks], vc_ref[:, ks], vn_ref[:, ks]], axis=0)
        for j in range(WA_GROUP):
            h = g * WA_GROUP + j
            q = q_ref[:, h * HEAD_DIM:(h + 1) * HEAD_DIM] * (HEAD_DIM ** -0.5)
            s = lax.dot_general(q, k3, (((1,), (1,)), ((), ())), preferred_element_type=F32)
            s = s + bias_ref[h] + pen
            sk = sink_ref[0, h]
            m = jnp.maximum(jnp.max(s, axis=-1, keepdims=True), sk)
            p = jnp.exp(s - m)
            den = jnp.sum(p, axis=-1, keepdims=True) + jnp.exp(sk - m)
            o = jnp.dot(p.astype(BF16), v3, preferred_element_type=F32)
            outs.append(o / den)
    o_all = jnp.concatenate(outs, axis=1)
    o_ref[...] = (o_all * _silu(z_ref[...].astype(F32))).astype(BF16)


def _window_attention(u3, bias, sink):
    bsz, s, _ = u3.shape
    nb = s // WA_BLOCK
    qw, kw = WA_WIDTH, WA_KV_WIDTH
    prev = lambda b, n: jnp.maximum(n - 1, 0)
    nxt = lambda b, n: jnp.minimum(n + 1, nb - 1)
    kspec = lambda f, c: pl.BlockSpec((None, WA_BLOCK, kw), lambda b, n: (b, f(b, n), c))
    cur = lambda b, n: n
    return pl.pallas_call(
        functools.partial(_wattn_kernel, nb=nb),
        grid=(bsz, nb),
        in_specs=[pl.BlockSpec((None, WA_BLOCK, qw), lambda b, n: (b, n, A_Q // qw)),
                  pl.BlockSpec((None, WA_BLOCK, qw), lambda b, n: (b, n, A_Z // qw)),
                  kspec(prev, A_K // kw), kspec(cur, A_K // kw), kspec(nxt, A_K // kw),
                  kspec(prev, A_V // kw), kspec(cur, A_V // kw), kspec(nxt, A_V // kw),
                  pl.BlockSpec((WA_HEADS, WA_BLOCK, 3 * WA_BLOCK), lambda b, n: (0, 0, 0)),
                  pl.BlockSpec(memory_space=pltpu.SMEM)],
        out_specs=pl.BlockSpec((None, WA_BLOCK, qw), lambda b, n: (b, n, 0)),
        out_shape=jax.ShapeDtypeStruct((bsz, s, qw), BF16),
        compiler_params=_cparams(("parallel", "parallel")),
        name="window_attention",
    )(u3, u3, u3, u3, u3, u3, u3, u3, bias, sink.reshape(1, WA_HEADS))


def _t5_bucket_np(rel):
    half = T5_BUCKETS // 2
    max_exact = half // 2
    ret = np.where(rel > 0, half, 0)
    n = np.abs(rel)
    nf = np.maximum(n, 1).astype(np.float64)
    v = np.log(nf / max_exact) / math.log(T5_MAX_DIST / max_exact) * (half - max_exact)
    vr = np.round(v)
    v = np.where(np.abs(v - vr) < 1e-9, vr, v)
    large = np.minimum(max_exact + np.trunc(v).astype(np.int64), half - 1)
    return ret + np.where(n < max_exact, n, large)


def _window_bias(t5_table):
    qi = np.arange(WA_BLOCK)
    kj = np.arange(3 * WA_BLOCK)
    rel = (kj[None, :] - WA_BLOCK) - qi[:, None]
    bucket = _t5_bucket_np(rel).astype(np.int32)
    valid = np.abs(rel) <= WINDOW
    bias = t5_table.astype(F32)[bucket]
    bias = jnp.where(valid[:, :, None], bias, NEG)
    return bias.transpose(2, 0, 1)


def _natten_kernel(q_ref, z_ref, kp_ref, kc_ref, kn_ref, vp_ref, vc_ref, vn_ref, bias_ref, o_ref):
    outs = []
    for h in range(NA_HEADS):
        hs = slice(h * HEAD_DIM, (h + 1) * HEAD_DIM)
        k3 = jnp.concatenate([kp_ref[:, hs], kc_ref[:, hs], kn_ref[:, hs]], axis=0)
        v3 = jnp.concatenate([vp_ref[:, hs], vc_ref[:, hs], vn_ref[:, hs]], axis=0)
        q = q_ref[:, hs] * (HEAD_DIM ** -0.5)
        s = lax.dot_general(q, k3, (((1,), (1,)), ((), ())), preferred_element_type=F32)
        s = s + bias_ref[h]
        m = jnp.max(s, axis=-1, keepdims=True)
        p = jnp.exp(s - m)
        den = jnp.sum(p, axis=-1, keepdims=True)
        o = jnp.dot(p.astype(BF16), v3, preferred_element_type=F32)
        outs.append(o / den)
    o_all = jnp.concatenate(outs, axis=1)
    o_ref[...] = (o_all * _silu(z_ref[...].astype(F32))).astype(BF16)


def _neighbourhood_attention(u3, bias):
    bsz, s, _ = u3.shape
    nblk = s // NA_BLOCK
    w = NA_WIDTH
    prev = lambda b, n: jnp.maximum(n - 1, 0)
    cur = lambda b, n: n
    nxt = lambda b, n: jnp.minimum(n + 1, nblk - 1)
    spec = lambda f, c: pl.BlockSpec((None, NA_BLOCK, w), lambda b, n: (b, f(b, n), c))
    btype = lambda b, n: (jnp.where(n == 0, 0, jnp.where(n == nblk - 1, 2, 1)), 0, 0, 0)
    return pl.pallas_call(
        _natten_kernel,
        grid=(bsz, nblk),
        in_specs=[spec(cur, C_Q // w), spec(cur, C_Z // w),
                  spec(prev, C_K // w), spec(cur, C_K // w), spec(nxt, C_K // w),
                  spec(prev, C_V // w), spec(cur, C_V // w), spec(nxt, C_V // w),
                  pl.BlockSpec((None, NA_HEADS, NA_BLOCK, 3 * NA_BLOCK), btype)],
        out_specs=pl.BlockSpec((None, NA_BLOCK, w), lambda b, n: (b, n, 0)),
        out_shape=jax.ShapeDtypeStruct((bsz, s, w), BF16),
        compiler_params=_cparams(("parallel", "arbitrary")),
        name="neighbourhood_attention",
    )(u3, u3, u3, u3, u3, u3, u3, u3, bias)


def _na_bias(rpb, rows):
    kh = min(NA_KH_MAX, rows)
    nblk = rows // NA_ROWS
    c = np.arange(GRID_W)
    col_start = np.clip(c - NA_KW // 2, 0, GRID_W - NA_KW)
    col_ok = (c[None, :] >= col_start[:, None]) & (c[None, :] < col_start[:, None] + NA_KW)
    dc = np.clip(c[None, :] - c[:, None], -(NA_KW - 1), NA_KW - 1) + NA_KW - 1
    colb = jnp.where(col_ok, rpb.astype(F32)[:, :, dc], NEG)
    out = []
    for j in (0, min(1, nblk - 1), nblk - 1):
        r = NA_ROWS * j + np.arange(NA_ROWS)[:, None]
        kr = NA_ROWS * (j - 1) + np.arange(3 * NA_ROWS)[None, :]
        rs = np.clip(r - kh // 2, 0, rows - kh)
        valid = (kr >= rs) & (kr < rs + kh)
        dr = np.clip(kr - r + NA_KH_MAX - 1, 0, 2 * NA_KH_MAX - 2)
        tiles = jnp.where(valid[None, :, :, None, None], colb[:, dr], NEG)
        out.append(tiles.transpose(0, 1, 3, 2, 4).reshape(NA_HEADS, NA_BLOCK, 3 * NA_BLOCK))
    return jnp.stack(out)


def _log_sigmoid(x):
    return jnp.minimum(x, 0.0) - jnp.log(1.0 + jnp.exp(-jnp.abs(x)))


def _tri(n, upper):
    r = lax.broadcasted_iota(jnp.int32, (n, n), 0)
    c = lax.broadcasted_iota(jnp.int32, (n, n), 1)
    return jnp.where((r <= c) if upper else (r >= c), 1.0, 0.0).astype(F32)


def _dot_hi(a, b):
    return jnp.dot(a, b, preferred_element_type=F32, precision=lax.Precision.HIGHEST)


def _scan_max(x, axis, reverse, size):
    idx = lax.broadcasted_iota(jnp.int32, x.shape, axis)
    k = 1
    while k < size:
        if reverse:
            shifted = pltpu.roll(x, x.shape[axis] - k, axis)
            ok = idx < size - k
        else:
            shifted = pltpu.roll(x, k, axis)
            ok = idx >= k
        x = jnp.maximum(x, jnp.where(ok, shifted, NEG))
        k *= 2
    return x


def _mlstm_kernel(q_ref, k_ref, v_ref, o_ref, z_ref, gr_ref, gt_ref, fb_ref, ng_ref, y_ref,
                  hdir_ref, state_ref, er_ref, xt_ref, *, seq):
    L = ML_CHUNK
    nc = seq // L
    head = pl.program_id(1)
    scale = ML_HEAD_DIM ** -0.5
    lane = lax.broadcasted_iota(jnp.int32, (1, 128), 1)
    chunk_ok = lane < nc

    for d in range(2):
        rev = d == 1
        fb = fb_ref[d, head]
        lf_r = _log_sigmoid(gr_ref[8 * d + 4 + head] + fb)
        b_r = _dot_hi(lf_r, _tri(L, upper=not rev))
        er_ref[d] = gr_ref[8 * d + head] - b_r
        i_t = gt_ref[8 * d + head]
        lf_t = _log_sigmoid(gt_ref[8 * d + 4 + head] + fb)
        b_t = _dot_hi(_tri(L, upper=rev), lf_t)
        g = jnp.sum(lf_t, axis=0, keepdims=True)
        a_t = g - b_t + i_t
        m_loc = jnp.max(a_t, axis=0, keepdims=True)
        w_t = jnp.exp(a_t - m_loc)
        jr = lax.broadcasted_iota(jnp.int32, (128, 128), 0)
        jc = lax.broadcasted_iota(jnp.int32, (128, 128), 1)
        before = ((jr > jc) if rev else (jr < jc)) & (jr < nc)
        g8 = jnp.broadcast_to(g, (8, 128))
        g_ex = _dot_hi(g8, jnp.where(before, 1.0, 0.0).astype(F32))[0:1]
        x = jnp.where(chunk_ok, m_loc - (g_ex + g), NEG)
        x8 = jnp.broadcast_to(x, (8, 128))
        if rev:
            x_prev = jnp.where(lane < nc - 1, pltpu.roll(x8, 127, 1), NEG)
        else:
            x_prev = jnp.where(lane >= 1, pltpu.roll(x8, 1, 1), NEG)
        pm = _scan_max(x_prev, 1, rev, nc)[0:1]
        m_prev = g_ex + jnp.maximum(pm, 0.0)
        m_after = jnp.maximum(g + m_prev, m_loc)
        sp = jnp.exp(g + m_prev - m_after)
        sl = jnp.exp(m_loc - m_after)
        cm = _scan_max(i_t - b_t, 0, rev, L)
        mu = jnp.maximum(m_prev, cm)
        xt_ref[d, 0 * L:1 * L] = mu
        xt_ref[d, 1 * L:2 * L] = jnp.exp(m_prev - mu) * scale
        xt_ref[d, 2 * L:3 * L] = jnp.exp(-b_t - mu)
        xt_ref[d, 3 * L:4 * L] = w_t
        xt_ref[d, 4 * L:4 * L + 8] = jnp.concatenate(
            [sp, sl, jnp.zeros((6, 128), F32)], axis=0)
        state_ref[d] = jnp.zeros((ML_HEAD_DIM, ML_AUG), F32)

    rr = lax.broadcasted_iota(jnp.int32, (L, L), 0)
    cc = lax.broadcasted_iota(jnp.int32, (L, L), 1)
    ones_col = jnp.where(lax.broadcasted_iota(jnp.int32, (L, ML_HEAD_DIM), 1) == 0, 1.0, 0.0).astype(BF16)

    def chunk_step(j, carry):
        for d in range(2):
            c = j if d == 0 else nc - 1 - j
            r0 = pl.multiple_of(c * L, L)
            qc = q_ref[pl.ds(r0, L), :]
            kc = k_ref[pl.ds(r0, L), :]
            vaug = jnp.concatenate([v_ref[pl.ds(r0, L), :], ones_col], axis=1)
            cols = jnp.sum(jnp.where(lane == c, xt_ref[d], 0.0), axis=1, keepdims=True)
            mu, iw, fl, w = (cols[i * L:(i + 1) * L] for i in range(4))
            sp = cols[4 * L:4 * L + 1]
            sl = cols[4 * L + 1:4 * L + 2]
            e_row = er_ref[d, pl.ds(c, 1), :]
            s = lax.dot_general(qc, kc, (((1,), (1,)), ((), ())), preferred_element_type=F32)
            mask = (cc >= rr) if d == 1 else (cc <= rr)
            p = jnp.where(mask, jnp.exp(e_row - mu), 0.0)
            w2 = (s * p * scale).astype(BF16)
            st = state_ref[d]
            out = (jnp.dot(w2, vaug, preferred_element_type=F32)
                   + iw * jnp.dot(qc, st.astype(BF16), preferred_element_type=F32))
            num = out[:, :ML_HEAD_DIM]
            den = out[:, ML_HEAD_DIM:ML_HEAD_DIM + 1]
            hdir_ref[d, pl.ds(r0, L), :] = num / jnp.maximum(jnp.abs(den), fl)
            wv = (w * vaug.astype(F32)).astype(BF16)
            s_loc = lax.dot_general(kc, wv, (((0,), (0,)), ((), ())), preferred_element_type=F32)
            state_ref[d] = sp * st + sl * s_loc
        return carry

    lax.fori_loop(0, nc, chunk_step, 0)

    tile = min(512, seq)

    def finish(t, carry):
        r0 = pl.multiple_of(t * tile, tile)
        rows = pl.ds(r0, tile)
        hs = hdir_ref[0, rows, :] + hdir_ref[1, rows, :]
        hs = jax.nn.sigmoid(o_ref[rows, :].astype(F32)) * hs
        mu = jnp.mean(hs, axis=-1, keepdims=True)
        hc = hs - mu
        var = jnp.mean(hc * hc, axis=-1, keepdims=True)
        hn = hc * lax.rsqrt(var + LN_EPS) * ng_ref[...]
        y_ref[rows, :] = (hn * _silu(z_ref[rows, :].astype(F32))).astype(BF16)
        return carry

    lax.fori_loop(0, seq // tile, finish, 0)


def _mlstm(u3, gates_r, gates_t, f_bias, norm_g):
    bsz, s, _ = u3.shape
    d = ML_HEAD_DIM
    nc = s // ML_CHUNK
    col = lambda base: pl.BlockSpec((None, s, d), lambda b, h: (b, 0, base // d + h))
    return pl.pallas_call(
        functools.partial(_mlstm_kernel, seq=s),
        grid=(bsz, ML_HEADS),
        in_specs=[col(B_Q), col(B_K), col(B_V), col(B_O), col(B_Z),
                  pl.BlockSpec((None, 16, nc, ML_CHUNK), lambda b, h: (b, 0, 0, 0)),
                  pl.BlockSpec((None, 16, ML_CHUNK, 128), lambda b, h: (b, 0, 0, 0)),
                  pl.BlockSpec(memory_space=pltpu.SMEM),
                  pl.BlockSpec((1, d), lambda b, h: (0, h))],
        out_specs=pl.BlockSpec((None, s, d), lambda b, h: (b, 0, h)),
        out_shape=jax.ShapeDtypeStruct((bsz, s, ML_WIDTH), BF16),
        scratch_shapes=[pltpu.VMEM((2, s, d), F32),
                        pltpu.VMEM((2, d, ML_AUG), F32),
                        pltpu.VMEM((2, nc, ML_CHUNK), F32),
                        pltpu.VMEM((2, 4 * ML_CHUNK + 8, 128), F32)],
        compiler_params=_cparams(("parallel", "parallel")),
        name="mlstm",
    )(u3, u3, u3, u3, u3, gates_r, gates_t, f_bias, norm_g.reshape(1, ML_WIDTH))


def _reorder_in_proj(w, b):
    wm = jnp.concatenate([w[:, a:e] for a, e in _SRC_RANGES], axis=1).astype(BF16)
    bm = jnp.concatenate([b[a:e] for a, e in _SRC_RANGES])
    g0, g1 = _SRC_GATES
    wg = jnp.pad(w[:, g0:g1], ((0, 0), (0, GATE_PAD - (g1 - g0)))).astype(BF16)
    bg = jnp.pad(b[g0:g1], (0, GATE_PAD - (g1 - g0)))
    return wm, bm, wg, bg


def kernel(x, emb_ln_g, emb_ln_b, w_in, b_in, w_out, b_out, ln_g, ln_b, t5_bias, sink, ml_f_bias,
           ml_norm_g, na_rpb):
    bsz, s, d = x.shape
    depth = w_in.shape[0]
    alpha = (2 * depth) ** 0.25
    m = bsz * s
    nc = s // ML_CHUNK
    assert d == D_MODEL and s % NA_BLOCK == 0 and s // GRID_W >= NA_KH_MAX and nc <= 128

    hf, hb = _input_norm(x.reshape(m, d), emb_ln_g, emb_ln_b)
    bias_a = _window_bias(t5_bias)
    for l in range(depth):
        wm, bm, wg, bg = _reorder_in_proj(w_in[l], b_in[l])
        u3 = _projection(hb, wm, bm, BF16, 1536, "in_projection").reshape(bsz, s, U_WIDTH)
        gates = _projection(hb, wg, bg, F32, GATE_PAD, "gate_projection")[:, :16]
        gates = gates.reshape(bsz, s, 16).transpose(0, 2, 1).reshape(bsz, 16, nc, ML_CHUNK)
        gates_t = jnp.pad(gates.transpose(0, 1, 3, 2), ((0, 0), (0, 0), (0, 0), (0, 128 - nc)))
        ya = _window_attention(u3, bias_a, sink[l])
        yb = _mlstm(u3, gates, gates_t, ml_f_bias[l], ml_norm_g[l])
        yc = _neighbourhood_attention(u3, _na_bias(na_rpb[l], s // GRID_W))
        last = l == depth - 1
        res = _out_projection(ya.reshape(m, WA_WIDTH), yb.reshape(m, ML_WIDTH), yc.reshape(m, NA_WIDTH),
                              w_out[l].astype(BF16), b_out[l], hf, ln_g[l], ln_b[l], alpha, not last)
        hf = res[0]
        hb = None if last else res[1]
    return hf.reshape(bsz, s, d)
```

```python
import functools
import math

import numpy as np
import jax
import jax.numpy as jnp
from jax import lax
from jax.experimental import pallas as pl
from jax.experimental.pallas import tpu as pltpu

F32 = jnp.float32
BF16 = jnp.bfloat16

D_MODEL = 2048
HEAD_DIM = 64
LN_EPS = 1e-5
NEG = -1e30

WA_HEADS = 12
WA_KV_HEADS = 4
WA_GROUP = WA_HEADS // WA_KV_HEADS
WA_WIDTH = WA_HEADS * HEAD_DIM
WA_KV_WIDTH = WA_KV_HEADS * HEAD_DIM
WA_BLOCK = 128
WINDOW = 128
T5_BUCKETS = 32
T5_MAX_DIST = 128
ML_HEADS = 4
ML_HEAD_DIM = 128
ML_WIDTH = ML_HEADS * ML_HEAD_DIM
ML_CHUNK = 128
ML_AUG = 2 * ML_HEAD_DIM
NA_HEADS = 12
NA_WIDTH = NA_HEADS * HEAD_DIM
GRID_W = 64
NA_KH_MAX = 8
NA_KW = 16
NA_ROWS = 4
NA_BLOCK = NA_ROWS * GRID_W

MIX_WIDTH = WA_WIDTH + ML_WIDTH + NA_WIDTH
GATE_PAD = 128

A_Q, A_Z, A_K, A_V = 0, 768, 1536, 1792
B_Q, B_K, B_V, B_O, B_Z = 2048, 2560, 3072, 3584, 4096
C_Q, C_K, C_V, C_Z = 4608, 5376, 6144, 6912
U_WIDTH = 7680
_SRC_RANGES = ((0, 768), (1280, 2048), (768, 1280), (2048, 4608), (4624, 7696))
_SRC_GATES = (4608, 4624)

VMEM_LIMIT = 56 * 1024 * 1024


def _cparams(sem):
    return pltpu.CompilerParams(dimension_semantics=sem, vmem_limit_bytes=VMEM_LIMIT)


def _dot_hi(a, b):
    return jnp.dot(a, b, preferred_element_type=F32, precision=lax.Precision.HIGHEST)


def _layer_norm_rows(x, g, b):
    mu = jnp.mean(x, axis=-1, keepdims=True)
    xc = x - mu
    var = jnp.mean(xc * xc, axis=-1, keepdims=True)
    return xc * lax.rsqrt(var + LN_EPS) * g + b


def _ln_kernel(x_ref, g_ref, b_ref, of_ref, ob_ref):
    y = _layer_norm_rows(x_ref[...].astype(F32), g_ref[...], b_ref[...])
    of_ref[...] = y
    ob_ref[...] = y.astype(BF16)


def _input_norm(x2, g, b):
    m, d = x2.shape
    tm = min(512, m)
    return pl.pallas_call(
        _ln_kernel,
        grid=(m // tm,),
        in_specs=[pl.BlockSpec((tm, d), lambda i: (i, 0)),
                  pl.BlockSpec((1, d), lambda i: (0, 0)),
                  pl.BlockSpec((1, d), lambda i: (0, 0))],
        out_specs=[pl.BlockSpec((tm, d), lambda i: (i, 0)),
                   pl.BlockSpec((tm, d), lambda i: (i, 0))],
        out_shape=[jax.ShapeDtypeStruct((m, d), F32), jax.ShapeDtypeStruct((m, d), BF16)],
        compiler_params=_cparams(("parallel",)),
        name="input_norm",
    )(x2, g.reshape(1, d), b.reshape(1, d))


def _proj_kernel(h_ref, w_ref, b_ref, o_ref):
    acc = jnp.dot(h_ref[...], w_ref[...], preferred_element_type=F32)
    o_ref[...] = (acc + b_ref[...]).astype(o_ref.dtype)


def _projection(hb, w, b, out_dtype, tn, name):
    m, d = hb.shape
    n = w.shape[1]
    tm = min(1024, m)
    return pl.pallas_call(
        _proj_kernel,
        grid=(n // tn, m // tm),
        in_specs=[pl.BlockSpec((tm, d), lambda j, i: (i, 0)),
                  pl.BlockSpec((d, tn), lambda j, i: (0, j)),
                  pl.BlockSpec((1, tn), lambda j, i: (0, j))],
        out_specs=pl.BlockSpec((tm, tn), lambda j, i: (i, j)),
        out_shape=jax.ShapeDtypeStruct((m, n), out_dtype),
        compiler_params=_cparams(("parallel", "parallel")),
        name=name,
    )(hb, w, b.reshape(1, n))


def _outproj_kernel(ya_ref, yb_ref, yc_ref, w_ref, b_ref, h_ref, g_ref, beta_ref, *out_refs, alpha):
    y = jnp.concatenate([ya_ref[...], yb_ref[...], yc_ref[...]], axis=1)
    out = jnp.dot(y, w_ref[...], preferred_element_type=F32) + b_ref[...]
    r = _layer_norm_rows(alpha * h_ref[...] + out, g_ref[...], beta_ref[...])
    out_refs[0][...] = r
    if len(out_refs) > 1:
        out_refs[1][...] = r.astype(BF16)


def _out_projection(ya, yb, yc, w, b, hres, g, beta, alpha, want_bf16):
    m, d = hres.shape
    tm = min(256, m)
    row = lambda i: (i, 0)
    const = lambda i: (0, 0)
    out_specs = [pl.BlockSpec((tm, d), row)]
    out_shape = [jax.ShapeDtypeStruct((m, d), F32)]
    if want_bf16:
        out_specs.append(pl.BlockSpec((tm, d), row))
        out_shape.append(jax.ShapeDtypeStruct((m, d), BF16))
    return pl.pallas_call(
        functools.partial(_outproj_kernel, alpha=alpha),
        grid=(m // tm,),
        in_specs=[pl.BlockSpec((tm, WA_WIDTH), row),
                  pl.BlockSpec((tm, ML_WIDTH), row),
                  pl.BlockSpec((tm, NA_WIDTH), row),
                  pl.BlockSpec((MIX_WIDTH, d), const),
                  pl.BlockSpec((1, d), const),
                  pl.BlockSpec((tm, d), row),
                  pl.BlockSpec((1, d), const),
                  pl.BlockSpec((1, d), const)],
        out_specs=out_specs,
        out_shape=out_shape,
        compiler_params=_cparams(("parallel",)),
        name="out_projection",
    )(ya, yb, yc, w, b.reshape(1, d), hres, g.reshape(1, d), beta.reshape(1, d))


def _silu(z):
    return z * jax.nn.sigmoid(z)


def _wattn_kernel(q_ref, z_ref, kp_ref, kc_ref, kn_ref, vp_ref, vc_ref, vn_ref, bias_ref, sink_ref,
                  o_ref, *, nb):
    n = pl.program_id(1)
    col = lax.broadcasted_iota(jnp.int32, (1, 3 * WA_BLOCK), 1)
    outside = ((n == 0) & (col < WA_BLOCK)) | ((n == nb - 1) & (col >= 2 * WA_BLOCK))
    pen = jnp.where(outside, NEG, 0.0).astype(F32)
    outs = []
    for g in range(WA_KV_HEADS):
        ks = slice(g * HEAD_DIM, (g + 1) * HEAD_DIM)
        k3 = jnp.concatenate([kp_ref[:, ks], kc_ref[:, ks], kn_ref[:, ks]], axis=0)
        v3 = jnp.concatenate([vp_ref[:, ks], vc_ref[:, ks], vn_ref[:, ks]], axis=0)
        for j in range(WA_GROUP):
            h = g * WA_GROUP + j
            q = q_ref[:, h * HEAD_DIM:(h + 1) * HEAD_DIM] * (HEAD_DIM ** -0.5)
            s = lax.dot_general(q, k3, (((1,), (1,)), ((), ())), preferred_element_type=F32)
            s = s + bias_ref[h] + pen
            sk = sink_ref[0, h]
            m = jnp.maximum(jnp.max(s, axis=-1, keepdims=True), sk)
            p = jnp.exp(s - m)
            den = jnp.sum(p, axis=-1, keepdims=True) + jnp.exp(sk - m)
            o = jnp.dot(p.astype(BF16), v3, preferred_element_type=F32)
            outs.append(o / den)
    o_all = jnp.concatenate(outs, axis=1)
    o_ref[...] = (o_all * _silu(z_ref[...].astype(F32))).astype(BF16)


def _window_attention(u3, bias, sink):
    bsz, s, _ = u3.shape
    nb = s // WA_BLOCK
    qw, kw = WA_WIDTH, WA_KV_WIDTH
    prev = lambda b, n: jnp.maximum(n - 1, 0)
    nxt = lambda b, n: jnp.minimum(n + 1, nb - 1)
    kspec = lambda f, c: pl.BlockSpec((None, WA_BLOCK, kw), lambda b, n: (b, f(b, n), c))
    cur = lambda b, n: n
    return pl.pallas_call(
        functools.partial(_wattn_kernel, nb=nb),
        grid=(bsz, nb),
        in_specs=[pl.BlockSpec((None, WA_BLOCK, qw), lambda b, n: (b, n, A_Q // qw)),
                  pl.BlockSpec((None, WA_BLOCK, qw), lambda b, n: (b, n, A_Z // qw)),
                  kspec(prev, A_K // kw), kspec(cur, A_K // kw), kspec(nxt, A_K // kw),
                  kspec(prev, A_V // kw), kspec(cur, A_V // kw), kspec(nxt, A_V // kw),
                  pl.BlockSpec((WA_HEADS, WA_BLOCK, 3 * WA_BLOCK), lambda b, n: (0, 0, 0)),
                  pl.BlockSpec(memory_space=pltpu.SMEM)],
        out_specs=pl.BlockSpec((None, WA_BLOCK, qw), lambda b, n: (b, n, 0)),
        out_shape=jax.ShapeDtypeStruct((bsz, s, qw), BF16),
        compiler_params=_cparams(("parallel", "parallel")),
        name="window_attention",
    )(u3, u3, u3, u3, u3, u3, u3, u3, bias, sink.reshape(1, WA_HEADS))


def _t5_bucket_np(rel):
    half = T5_BUCKETS // 2
    max_exact = half // 2
    ret = np.where(rel > 0, half, 0)
    n = np.abs(rel)
    nf = np.maximum(n, 1).astype(np.float64)
    v = np.log(nf / max_exact) / math.log(T5_MAX_DIST / max_exact) * (half - max_exact)
    vr = np.round(v)
    v = np.where(np.abs(v - vr) < 1e-9, vr, v)
    large = np.minimum(max_exact + np.trunc(v).astype(np.int64), half - 1)
    return ret + np.where(n < max_exact, n, large)


def _wbias_kernel(t5t_ref, bucket_ref, o_ref):
    width = 4 * WA_BLOCK
    bk = bucket_ref[...]
    e = lax.broadcasted_iota(jnp.int32, (T5_BUCKETS, width), 0)
    onehot = jnp.where(e == bk, 1.0, 0.0).astype(F32)
    g = _dot_hi(t5t_ref[...], onehot) + jnp.where(bk < 0, NEG, 0.0)
    for h in range(WA_HEADS):
        row = jnp.broadcast_to(g[h:h + 1, :], (WA_BLOCK, width))
        t = pltpu.roll(row, 3 * WA_BLOCK, 1, stride=1, stride_axis=0)
        o_ref[h] = t[:, :3 * WA_BLOCK]


def _window_bias(t5_table):
    rel = np.arange(4 * WA_BLOCK) - 2 * WA_BLOCK
    bucket = np.where(np.abs(rel) <= WINDOW, _t5_bucket_np(rel), -1).astype(np.int32)
    t5t = jnp.pad(t5_table.astype(F32).T, ((0, 16 - WA_HEADS), (0, 0)))
    return pl.pallas_call(
        _wbias_kernel,
        out_shape=jax.ShapeDtypeStruct((WA_HEADS, WA_BLOCK, 3 * WA_BLOCK), F32),
        name="window_bias",
    )(t5t, jnp.asarray(bucket).reshape(1, -1))


def _natten_kernel(q_ref, z_ref, kp_ref, kc_ref, kn_ref, vp_ref, vc_ref, vn_ref, bias_ref, o_ref):
    outs = []
    for h in range(NA_HEADS):
        hs = slice(h * HEAD_DIM, (h + 1) * HEAD_DIM)
        k3 = jnp.concatenate([kp_ref[:, hs], kc_ref[:, hs], kn_ref[:, hs]], axis=0)
        v3 = jnp.concatenate([vp_ref[:, hs], vc_ref[:, hs], vn_ref[:, hs]], axis=0)
        q = q_ref[:, hs] * (HEAD_DIM ** -0.5)
        s = lax.dot_general(q, k3, (((1,), (1,)), ((), ())), preferred_element_type=F32)
        s = s + bias_ref[h]
        m = jnp.max(s, axis=-1, keepdims=True)
        p = jnp.exp(s - m)
        den = jnp.sum(p, axis=-1, keepdims=True)
        o = jnp.dot(p.astype(BF16), v3, preferred_element_type=F32)
        outs.append(o / den)
    o_all = jnp.concatenate(outs, axis=1)
    o_ref[...] = (o_all * _silu(z_ref[...].astype(F32))).astype(BF16)


def _neighbourhood_attention(u3, bias, layer):
    bsz, s, _ = u3.shape
    nblk = s // NA_BLOCK
    w = NA_WIDTH
    prev = lambda b, n: jnp.maximum(n - 1, 0)
    cur = lambda b, n: n
    nxt = lambda b, n: jnp.minimum(n + 1, nblk - 1)
    spec = lambda f, c: pl.BlockSpec((None, NA_BLOCK, w), lambda b, n: (b, f(b, n), c))
    btype = lambda b, n: (layer, jnp.where(n == 0, 0, jnp.where(n == nblk - 1, 2, 1)), 0, 0, 0)
    return pl.pallas_call(
        _natten_kernel,
        grid=(bsz, nblk),
        in_specs=[spec(cur, C_Q // w), spec(cur, C_Z // w),
                  spec(prev, C_K // w), spec(cur, C_K // w), spec(nxt, C_K // w),
                  spec(prev, C_V // w), spec(cur, C_V // w), spec(nxt, C_V // w),
                  pl.BlockSpec((None, None, NA_HEADS, NA_BLOCK, 3 * NA_BLOCK), btype)],
        out_specs=pl.BlockSpec((None, NA_BLOCK, w), lambda b, n: (b, n, 0)),
        out_shape=jax.ShapeDtypeStruct((bsz, s, w), BF16),
        compiler_params=_cparams(("parallel", "arbitrary")),
        name="neighbourhood_attention",
    )(u3, u3, u3, u3, u3, u3, u3, u3, bias)


def _na_valid_rows(rows):
    kh = min(NA_KH_MAX, rows)
    nblk = rows // NA_ROWS
    out = []
    for j in (0, min(1, nblk - 1), nblk - 1):
        r = NA_ROWS * j + np.arange(NA_ROWS)[:, None]
        kr = NA_ROWS * (j - 1) + np.arange(3 * NA_ROWS)[None, :]
        rs = np.clip(r - kh // 2, 0, rows - kh)
        out.append((kr >= rs) & (kr < rs + kh))
    return np.stack(out)


def _nabias_kernel(rpb_ref, o_ref, *, valid):
    w = GRID_W
    j = lax.broadcasted_iota(jnp.int32, (32, 2 * w), 1)
    e = lax.broadcasted_iota(jnp.int32, (32, 2 * w), 0)
    dc = jnp.clip(j - w, -(NA_KW - 1), NA_KW - 1) + NA_KW - 1
    g = _dot_hi(rpb_ref[...], jnp.where(e == dc, 1.0, 0.0).astype(F32))
    lane = lax.broadcasted_iota(jnp.int32, (w, 2 * w), 1)
    qc = lax.broadcasted_iota(jnp.int32, (w, 2 * w), 0)
    kc = lane & (w - 1)
    col_start = jnp.clip(qc - NA_KW // 2, 0, w - NA_KW)
    col_ok = (kc >= col_start) & (kc < col_start + NA_KW)
    left = lane < w
    neg = jnp.full((w, 2 * w), NEG, F32)

    def toeplitz(dr, shift):
        row = jnp.broadcast_to(g[dr:dr + 1, :], (w, 2 * w))
        return pltpu.roll(row, shift, 1, stride=1, stride_axis=0)

    pair = [jnp.where(col_ok, jnp.where(left, toeplitz(d, w), toeplitz(d + 1, 0)), NEG)
            for d in range(2 * NA_KH_MAX - 2)]
    for ty in range(3):
        for rl in range(NA_ROWS):
            for t in range(3 * NA_ROWS // 2):
                d = 2 * t - rl + NA_KH_MAX - 1 - NA_ROWS
                v0, v1 = bool(valid[ty, rl, 2 * t]), bool(valid[ty, rl, 2 * t + 1])
                if v0 and v1:
                    tile = pair[d]
                elif v0:
                    tile = jnp.where(left, pair[d], NEG)
                elif v1:
                    tile = jnp.where(left, NEG, pair[d])
                else:
                    tile = neg
                o_ref[ty, rl * w:(rl + 1) * w, 2 * t * w:(2 * t + 2) * w] = tile


def _na_bias(rpb_all, rows):
    depth = rpb_all.shape[0]
    rpb_p = jnp.pad(rpb_all.astype(F32), ((0, 0), (0, 0), (0, 1), (0, 1)))
    return pl.pallas_call(
        functools.partial(_nabias_kernel, valid=_na_valid_rows(rows)),
        grid=(depth, NA_HEADS),
        in_specs=[pl.BlockSpec((None, None, 16, 32), lambda l, h: (l, h, 0, 0))],
        out_specs=pl.BlockSpec((None, 3, None, NA_BLOCK, 3 * NA_BLOCK), lambda l, h: (l, 0, h, 0, 0)),
        out_shape=jax.ShapeDtypeStruct((depth, 3, NA_HEADS, NA_BLOCK, 3 * NA_BLOCK), F32),
        compiler_params=_cparams(("parallel", "parallel")),
        name="na_bias",
    )(rpb_p)


def _log_sigmoid(x):
    return jnp.minimum(x, 0.0) - jnp.log(1.0 + jnp.exp(-jnp.abs(x)))


def _tri(n, upper):
    r = lax.broadcasted_iota(jnp.int32, (n, n), 0)
    c = lax.broadcasted_iota(jnp.int32, (n, n), 1)
    return jnp.where((r <= c) if upper else (r >= c), 1.0, 0.0).astype(F32)


def _scan_max(x, axis, reverse, size):
    idx = lax.broadcasted_iota(jnp.int32, x.shape, axis)
    k = 1
    while k < size:
        if reverse:
            shifted = pltpu.roll(x, x.shape[axis] - k, axis)
            ok = idx < size - k
        else:
            shifted = pltpu.roll(x, k, axis)
            ok = idx >= k
        x = jnp.maximum(x, jnp.where(ok, shifted, NEG))
        k *= 2
    return x


def _mlstm_kernel(q_ref, k_ref, v_ref, o_ref, z_ref, gr_ref, gt_ref, fb_ref, ng_ref, y_ref,
                  hdir_ref, state_ref, er_ref, xt_ref, *, seq):
    L = ML_CHUNK
    nc = seq // L
    head = pl.program_id(1)
    scale = ML_HEAD_DIM ** -0.5
    lane = lax.broadcasted_iota(jnp.int32, (1, 128), 1)
    chunk_ok = lane < nc

    for d in range(2):
        rev = d == 1
        fb = fb_ref[d, head]
        lf_r = _log_sigmoid(gr_ref[8 * d + 4 + head] + fb)
        b_r = _dot_hi(lf_r, _tri(L, upper=not rev))
        er_ref[d] = gr_ref[8 * d + head] - b_r
        i_t = gt_ref[8 * d + head]
        lf_t = _log_sigmoid(gt_ref[8 * d + 4 + head] + fb)
        b_t = _dot_hi(_tri(L, upper=rev), lf_t)
        g = jnp.sum(lf_t, axis=0, keepdims=True)
        a_t = g - b_t + i_t
        m_loc = jnp.max(a_t, axis=0, keepdims=True)
        w_t = jnp.exp(a_t - m_loc)
        jr = lax.broadcasted_iota(jnp.int32, (128, 128), 0)
        jc = lax.broadcasted_iota(jnp.int32, (128, 128), 1)
        before = ((jr > jc) if rev else (jr < jc)) & (jr < nc)
        g8 = jnp.broadcast_to(g, (8, 128))
        g_ex = _dot_hi(g8, jnp.where(before, 1.0, 0.0).astype(F32))[0:1]
        x = jnp.where(chunk_ok, m_loc - (g_ex + g), NEG)
        x8 = jnp.broadcast_to(x, (8, 128))
        if rev:
            x_prev = jnp.where(lane < nc - 1, pltpu.roll(x8, 127, 1), NEG)
        else:
            x_prev = jnp.where(lane >= 1, pltpu.roll(x8, 1, 1), NEG)
        pm = _scan_max(x_prev, 1, rev, nc)[0:1]
        m_prev = g_ex + jnp.maximum(pm, 0.0)
        m_after = jnp.maximum(g + m_prev, m_loc)
        sp = jnp.exp(g + m_prev - m_after)
        sl = jnp.exp(m_loc - m_after)
        cm = _scan_max(i_t - b_t, 0, rev, L)
        mu = jnp.maximum(m_prev, cm)
        xt_ref[d, 0 * L:1 * L] = mu
        xt_ref[d, 1 * L:2 * L] = jnp.exp(m_prev - mu) * scale
        xt_ref[d, 2 * L:3 * L] = jnp.exp(-b_t - mu)
        xt_ref[d, 3 * L:4 * L] = w_t
        xt_ref[d, 4 * L:4 * L + 8] = jnp.concatenate(
            [sp, sl, jnp.zeros((6, 128), F32)], axis=0)
        state_ref[d] = jnp.zeros((ML_HEAD_DIM, ML_AUG), F32)

    rr = lax.broadcasted_iota(jnp.int32, (L, L), 0)
    cc = lax.broadcasted_iota(jnp.int32, (L, L), 1)
    ones_col = jnp.where(lax.broadcasted_iota(jnp.int32, (L, ML_HEAD_DIM), 1) == 0, 1.0, 0.0).astype(BF16)

    def chunk_step(j, carry):
        for d in range(2):
            c = j if d == 0 else nc - 1 - j
            r0 = pl.multiple_of(c * L, L)
            qc = q_ref[pl.ds(r0, L), :]
            kc = k_ref[pl.ds(r0, L), :]
            vaug = jnp.concatenate([v_ref[pl.ds(r0, L), :], ones_col], axis=1)
            cols = jnp.sum(jnp.where(lane == c, xt_ref[d], 0.0), axis=1, keepdims=True)
            mu, iw, fl, w = (cols[i * L:(i + 1) * L] for i in range(4))
            sp = cols[4 * L:4 * L + 1]
            sl = cols[4 * L + 1:4 * L + 2]
            e_row = er_ref[d, pl.ds(c, 1), :]
            s = lax.dot_general(qc, kc, (((1,), (1,)), ((), ())), preferred_element_type=F32)
            mask = (cc >= rr) if d == 1 else (cc <= rr)
            p = jnp.where(mask, jnp.exp(e_row - mu), 0.0)
            w2 = (s * p * scale).astype(BF16)
            st = state_ref[d]
            out = (jnp.dot(w2, vaug, preferred_element_type=F32)
                   + iw * jnp.dot(qc, st.astype(BF16), preferred_element_type=F32))
            num = out[:, :ML_HEAD_DIM]
            den = out[:, ML_HEAD_DIM:ML_HEAD_DIM + 1]
            hdir_ref[d, pl.ds(r0, L), :] = num / jnp.maximum(jnp.abs(den), fl)
            wv = (w * vaug.astype(F32)).astype(BF16)
            s_loc = lax.dot_general(kc, wv, (((0,), (0,)), ((), ())), preferred_element_type=F32)
            state_ref[d] = sp * st + sl * s_loc
        return carry

    lax.fori_loop(0, nc, chunk_step, 0)

    tile = min(512, seq)

    def finish(t, carry):
        r0 = pl.multiple_of(t * tile, tile)
        rows = pl.ds(r0, tile)
        hs = hdir_ref[0, rows, :] + hdir_ref[1, rows, :]
        hs = jax.nn.sigmoid(o_ref[rows, :].astype(F32)) * hs
        mu = jnp.mean(hs, axis=-1, keepdims=True)
        hc = hs - mu
        var = jnp.mean(hc * hc, axis=-1, keepdims=True)
        hn = hc * lax.rsqrt(var + LN_EPS) * ng_ref[...]
        y_ref[rows, :] = (hn * _silu(z_ref[rows, :].astype(F32))).astype(BF16)
        return carry

    lax.fori_loop(0, seq // tile, finish, 0)


def _mlstm(u3, gates_r, gates_t, f_bias, norm_g):
    bsz, s, _ = u3.shape
    d = ML_HEAD_DIM
    nc = s // ML_CHUNK
    col = lambda base: pl.BlockSpec((None, s, d), lambda b, h: (b, 0, base // d + h))
    return pl.pallas_call(
        functools.partial(_mlstm_kernel, seq=s),
        grid=(bsz, ML_HEADS),
        in_specs=[col(B_Q), col(B_K), col(B_V), col(B_O), col(B_Z),
                  pl.BlockSpec((None, 16, nc, ML_CHUNK), lambda b, h: (b, 0, 0, 0)),
                  pl.BlockSpec((None, 16, ML_CHUNK, 128), lambda b, h: (b, 0, 0, 0)),
                  pl.BlockSpec(memory_space=pltpu.SMEM),
                  pl.BlockSpec((1, d), lambda b, h: (0, h))],
        out_specs=pl.BlockSpec((None, s, d), lambda b, h: (b, 0, h)),
        out_shape=jax.ShapeDtypeStruct((bsz, s, ML_WIDTH), BF16),
        scratch_shapes=[pltpu.VMEM((2, s, d), F32),
                        pltpu.VMEM((2, d, ML_AUG), F32),
                        pltpu.VMEM((2, nc, ML_CHUNK), F32),
                        pltpu.VMEM((2, 4 * ML_CHUNK + 8, 128), F32)],
        compiler_params=_cparams(("parallel", "parallel")),
        name="mlstm",
    )(u3, u3, u3, u3, u3, gates_r, gates_t, f_bias, norm_g.reshape(1, ML_WIDTH))


def _reorder_in_proj(w, b):
    wm = jnp.concatenate([w[:, a:e] for a, e in _SRC_RANGES], axis=1).astype(BF16)
    bm = jnp.concatenate([b[a:e] for a, e in _SRC_RANGES])
    g0, g1 = _SRC_GATES
    wg = jnp.pad(w[:, g0:g1], ((0, 0), (0, GATE_PAD - (g1 - g0)))).astype(BF16)
    bg = jnp.pad(b[g0:g1], (0, GATE_PAD - (g1 - g0)))
    return wm, bm, wg, bg


def kernel(x, emb_ln_g, emb_ln_b, w_in, b_in, w_out, b_out, ln_g, ln_b, t5_bias, sink, ml_f_bias,
           ml_norm_g, na_rpb):
    bsz, s, d = x.shape
    depth = w_in.shape[0]
    alpha = (2 * depth) ** 0.25
    m = bsz * s
    nc = s // ML_CHUNK
    assert d == D_MODEL and s % NA_BLOCK == 0 and s // GRID_W >= NA_KH_MAX and nc <= 128

    hf, hb = _input_norm(x.reshape(m, d), emb_ln_g, emb_ln_b)
    bias_a = _window_bias(t5_bias)
    bias_c = _na_bias(na_rpb, s // GRID_W)
    for l in range(depth):
        wm, bm, wg, bg = _reorder_in_proj(w_in[l], b_in[l])
        u3 = _projection(hb, wm, bm, BF16, 1536, "in_projection").reshape(bsz, s, U_WIDTH)
        gates = _projection(hb, wg, bg, F32, GATE_PAD, "gate_projection")[:, :16]
        gates = gates.reshape(bsz, s, 16).transpose(0, 2, 1).reshape(bsz, 16, nc, ML_CHUNK)
        gates_t = jnp.pad(gates.transpose(0, 1, 3, 2), ((0, 0), (0, 0), (0, 0), (0, 128 - nc)))
        ya = _window_attention(u3, bias_a, sink[l])
        yb = _mlstm(u3, gates, gates_t, ml_f_bias[l], ml_norm_g[l])
        yc = _neighbourhood_attention(u3, bias_c, l)
        last = l == depth - 1
        res = _out_projection(ya.reshape(m, WA_WIDTH), yb.reshape(m, ML_WIDTH), yc.reshape(m, NA_WIDTH),
                              w_out[l].astype(BF16), b_out[l], hf, ln_g[l], ln_b[l], alpha, not last)
        hf = res[0]
        hb = None if last else res[1]
    return hf.reshape(bsz, s, d)
```

```python
import functools
import math

import numpy as np
import jax
import jax.numpy as jnp
from jax import lax
from jax.experimental import pallas as pl
from jax.experimental.pallas import tpu as pltpu

F32 = jnp.float32
BF16 = jnp.bfloat16

D_MODEL = 2048
HEAD_DIM = 64
LN_EPS = 1e-5
NEG = -1e30

WA_HEADS = 12
WA_KV_HEADS = 4
WA_GROUP = WA_HEADS // WA_KV_HEADS
WA_WIDTH = WA_HEADS * HEAD_DIM
WA_KV_WIDTH = WA_KV_HEADS * HEAD_DIM
WA_BLOCK = 128
WINDOW = 128
T5_BUCKETS = 32
T5_MAX_DIST = 128
ML_HEADS = 4
ML_HEAD_DIM = 128
ML_WIDTH = ML_HEADS * ML_HEAD_DIM
ML_CHUNK = 128
ML_AUG = 2 * ML_HEAD_DIM
NA_HEADS = 12
NA_WIDTH = NA_HEADS * HEAD_DIM
GRID_W = 64
NA_KH_MAX = 8
NA_KW = 16
NA_ROWS = 4
NA_BLOCK = NA_ROWS * GRID_W

MIX_WIDTH = WA_WIDTH + ML_WIDTH + NA_WIDTH
GATE_PAD = 128

A_Q, A_Z, A_K, A_V = 0, 768, 1536, 1792
B_Q, B_K, B_V, B_O, B_Z = 2048, 2560, 3072, 3584, 4096
C_Q, C_K, C_V, C_Z = 4608, 5376, 6144, 6912
U_WIDTH = 7680
_A_HEAD_ORDER = tuple(6 * i + 3 * half + j for i in range(2) for j in range(3) for half in range(2))
_A_HEAD_COLS = tuple((h * HEAD_DIM, (h + 1) * HEAD_DIM) for h in _A_HEAD_ORDER)
_SRC_RANGES = (_A_HEAD_COLS + tuple((1280 + a, 1280 + e) for a, e in _A_HEAD_COLS)
               + ((768, 1280), (2048, 4608), (4624, 7696)))
_SRC_GATES = (4608, 4624)

VMEM_LIMIT = 56 * 1024 * 1024


def _cparams(sem, flags=None):
    return pltpu.CompilerParams(dimension_semantics=sem, vmem_limit_bytes=VMEM_LIMIT, flags=flags)


def _dot_hi(a, b):
    return jnp.dot(a, b, preferred_element_type=F32, precision=lax.Precision.HIGHEST)


def _layer_norm_rows(x, g, b):
    mu = jnp.mean(x, axis=-1, keepdims=True)
    xc = x - mu
    var = jnp.mean(xc * xc, axis=-1, keepdims=True)
    return xc * lax.rsqrt(var + LN_EPS) * g + b


def _ln_kernel(x_ref, g_ref, b_ref, of_ref, ob_ref):
    y = _layer_norm_rows(x_ref[...].astype(F32), g_ref[...], b_ref[...])
    of_ref[...] = y
    ob_ref[...] = y.astype(BF16)


def _input_norm(x2, g, b):
    m, d = x2.shape
    tm = min(512, m)
    return pl.pallas_call(
        _ln_kernel,
        grid=(m // tm,),
        in_specs=[pl.BlockSpec((tm, d), lambda i: (i, 0)),
                  pl.BlockSpec((1, d), lambda i: (0, 0)),
                  pl.BlockSpec((1, d), lambda i: (0, 0))],
        out_specs=[pl.BlockSpec((tm, d), lambda i: (i, 0)),
                   pl.BlockSpec((tm, d), lambda i: (i, 0))],
        out_shape=[jax.ShapeDtypeStruct((m, d), F32), jax.ShapeDtypeStruct((m, d), BF16)],
        compiler_params=_cparams(("parallel",)),
        name="input_norm",
    )(x2, g.reshape(1, d), b.reshape(1, d))


def _proj_kernel(h_ref, w_ref, b_ref, o_ref):
    acc = jnp.dot(h_ref[...], w_ref[...], preferred_element_type=F32)
    o_ref[...] = (acc + b_ref[...]).astype(o_ref.dtype)


def _projection(hb, w, b, out_dtype, tn, name):
    m, d = hb.shape
    n = w.shape[1]
    tm = min(1024, m)
    return pl.pallas_call(
        _proj_kernel,
        grid=(n // tn, m // tm),
        in_specs=[pl.BlockSpec((tm, d), lambda j, i: (i, 0)),
                  pl.BlockSpec((d, tn), lambda j, i: (0, j)),
                  pl.BlockSpec((1, tn), lambda j, i: (0, j))],
        out_specs=pl.BlockSpec((tm, tn), lambda j, i: (i, j)),
        out_shape=jax.ShapeDtypeStruct((m, n), out_dtype),
        compiler_params=_cparams(("parallel", "parallel")),
        name=name,
    )(hb, w, b.reshape(1, n))


def _outproj_kernel(ya_ref, yb_ref, yc_ref, w_ref, b_ref, h_ref, g_ref, beta_ref, *out_refs, alpha):
    y = jnp.concatenate([ya_ref[...], yb_ref[...], yc_ref[...]], axis=1)
    out = jnp.dot(y, w_ref[...], preferred_element_type=F32) + b_ref[...]
    r = _layer_norm_rows(alpha * h_ref[...] + out, g_ref[...], beta_ref[...])
    out_refs[0][...] = r
    if len(out_refs) > 1:
        out_refs[1][...] = r.astype(BF16)


def _out_projection(ya, yb, yc, w, b, hres, g, beta, alpha, want_bf16):
    m, d = hres.shape
    tm = min(256, m)
    row = lambda i: (i, 0)
    const = lambda i: (0, 0)
    out_specs = [pl.BlockSpec((tm, d), row)]
    out_shape = [jax.ShapeDtypeStruct((m, d), F32)]
    if want_bf16:
        out_specs.append(pl.BlockSpec((tm, d), row))
        out_shape.append(jax.ShapeDtypeStruct((m, d), BF16))
    return pl.pallas_call(
        functools.partial(_outproj_kernel, alpha=alpha),
        grid=(m // tm,),
        in_specs=[pl.BlockSpec((tm, WA_WIDTH), row),
                  pl.BlockSpec((tm, ML_WIDTH), row),
                  pl.BlockSpec((tm, NA_WIDTH), row),
                  pl.BlockSpec((MIX_WIDTH, d), const),
                  pl.BlockSpec((1, d), const),
                  pl.BlockSpec((tm, d), row),
                  pl.BlockSpec((1, d), const),
                  pl.BlockSpec((1, d), const)],
        out_specs=out_specs,
        out_shape=out_shape,
        compiler_params=_cparams(("parallel",)),
        name="out_projection",
    )(ya, yb, yc, w, b.reshape(1, d), hres, g.reshape(1, d), beta.reshape(1, d))


def _silu(z):
    return z * jax.nn.sigmoid(z)


def _wattn_kernel(q_ref, z_ref, kp_ref, kc_ref, kn_ref, vp_ref, vc_ref, vn_ref, bias_ref, sink_ref,
                  o_ref, s0_ref, s1_ref):
    n = pl.program_id(1)
    blk = WA_BLOCK
    npair = WA_KV_HEADS // 2
    rows = 2 * WA_GROUP * blk
    lane = lax.broadcasted_iota(jnp.int32, (blk, 2 * HEAD_DIM), 1)
    left = lane < HEAD_DIM
    rblk = lax.broadcasted_iota(jnp.int32, (rows, 1), 0) // blk

    def step(s_read, s_write):
        for i in range(npair):
            ks = slice(i * 2 * HEAD_DIM, (i + 1) * 2 * HEAD_DIM)
            k3 = jnp.concatenate([kp_ref[:, ks], kc_ref[:, ks], kn_ref[:, ks]], axis=0)
            tiles = [q_ref[:, (WA_GROUP * i + j) * 2 * HEAD_DIM:(WA_GROUP * i + j + 1) * 2 * HEAD_DIM]
                     * (HEAD_DIM ** -0.5) for j in range(WA_GROUP)]
            zero = jnp.zeros_like(tiles[0])
            lhs = jnp.concatenate([jnp.where(left, t, zero) for t in tiles]
                                  + [jnp.where(left, zero, t) for t in tiles], axis=0)
            s_write[i] = lax.dot_general(lhs, k3, (((1,), (1,)), ((), ())), preferred_element_type=F32)
        for i in range(npair):
            ks = slice(i * 2 * HEAD_DIM, (i + 1) * 2 * HEAD_DIM)
            v3 = jnp.concatenate([vp_ref[:, ks], vc_ref[:, ks], vn_ref[:, ks]], axis=0)
            s = s_read[i] + bias_ref[i]
            sk = jnp.zeros((rows, 1), F32)
            for r in range(2 * WA_GROUP):
                sk = jnp.where(rblk == r, sink_ref[0, 2 * WA_GROUP * i + r], sk)
            m = jnp.maximum(jnp.max(s, axis=-1, keepdims=True), sk)
            p = jnp.exp(s - m)
            den = jnp.sum(p, axis=-1, keepdims=True) + jnp.exp(sk - m)
            o = jnp.dot(p.astype(BF16), v3, preferred_element_type=F32) / den
            for j in range(WA_GROUP):
                t = WA_GROUP * i + j
                ot = jnp.where(left, o[j * blk:(j + 1) * blk], o[(WA_GROUP + j) * blk:(WA_GROUP + j + 1) * blk])
                cs = slice(t * 2 * HEAD_DIM, (t + 1) * 2 * HEAD_DIM)
                o_ref[:, cs] = (ot * _silu(z_ref[:, cs].astype(F32))).astype(BF16)

    @pl.when(n == 0)
    def _():
        s1_ref[...] = jnp.zeros(s1_ref.shape, F32)

    @pl.when(n % 2 == 0)
    def _():
        step(s1_ref, s0_ref)

    @pl.when(n % 2 == 1)
    def _():
        step(s0_ref, s1_ref)


def _window_attention(u3, bias, sink):
    bsz, s, _ = u3.shape
    nb = s // WA_BLOCK
    qw, kw = WA_WIDTH, WA_KV_WIDTH
    npair = WA_KV_HEADS // 2
    rows = 2 * WA_GROUP * WA_BLOCK
    clamp = lambda i: jnp.clip(i, 0, nb - 1)
    kspec = lambda off, c: pl.BlockSpec((None, WA_BLOCK, kw), lambda b, n: (b, clamp(n + off), c))
    btype = lambda b, n: (jnp.where(n <= 1, 0, jnp.where(n == nb, 2, 1)), 0, 0, 0)
    return pl.pallas_call(
        _wattn_kernel,
        grid=(bsz, nb + 1),
        in_specs=[pl.BlockSpec((None, WA_BLOCK, qw), lambda b, n: (b, clamp(n), A_Q // qw)),
                  pl.BlockSpec((None, WA_BLOCK, qw), lambda b, n: (b, clamp(n - 1), A_Z // qw)),
                  kspec(-1, A_K // kw), kspec(0, A_K // kw), kspec(1, A_K // kw),
                  kspec(-2, A_V // kw), kspec(-1, A_V // kw), kspec(0, A_V // kw),
                  pl.BlockSpec((None, npair, rows, 3 * WA_BLOCK), btype),
                  pl.BlockSpec(memory_space=pltpu.SMEM)],
        out_specs=pl.BlockSpec((None, WA_BLOCK, qw), lambda b, n: (b, clamp(n - 1), 0)),
        out_shape=jax.ShapeDtypeStruct((bsz, s, qw), BF16),
        scratch_shapes=[pltpu.VMEM((npair, rows, 3 * WA_BLOCK), F32),
                        pltpu.VMEM((npair, rows, 3 * WA_BLOCK), F32)],
        compiler_params=_cparams(("parallel", "arbitrary")),
        name="window_attention",
    )(u3, u3, u3, u3, u3, u3, u3, u3, bias.reshape(3, npair, rows, 3 * WA_BLOCK), sink.reshape(1, WA_HEADS))


def _t5_bucket_np(rel):
    half = T5_BUCKETS // 2
    max_exact = half // 2
    ret = np.where(rel > 0, half, 0)
    n = np.abs(rel)
    nf = np.maximum(n, 1).astype(np.float64)
    v = np.log(nf / max_exact) / math.log(T5_MAX_DIST / max_exact) * (half - max_exact)
    vr = np.round(v)
    v = np.where(np.abs(v - vr) < 1e-9, vr, v)
    large = np.minimum(max_exact + np.trunc(v).astype(np.int64), half - 1)
    return ret + np.where(n < max_exact, n, large)


def _wbias_kernel(t5t_ref, bucket_ref, o_ref):
    width = 4 * WA_BLOCK
    bk = bucket_ref[...]
    e = lax.broadcasted_iota(jnp.int32, (T5_BUCKETS, width), 0)
    onehot = jnp.where(e == bk, 1.0, 0.0).astype(F32)
    g = _dot_hi(t5t_ref[...], onehot) + jnp.where(bk < 0, NEG, 0.0)
    col = lax.broadcasted_iota(jnp.int32, (WA_BLOCK, 3 * WA_BLOCK), 1)
    for h in range(WA_HEADS):
        row = jnp.broadcast_to(g[h:h + 1, :], (WA_BLOCK, width))
        t = pltpu.roll(row, 3 * WA_BLOCK, 1, stride=1, stride_axis=0)[:, :3 * WA_BLOCK]
        o_ref[0, h] = jnp.where(col < WA_BLOCK, NEG, t)
        o_ref[1, h] = t
        o_ref[2, h] = jnp.where(col >= 2 * WA_BLOCK, NEG, t)


def _window_bias(t5_table):
    rel = np.arange(4 * WA_BLOCK) - 2 * WA_BLOCK
    bucket = np.where(np.abs(rel) <= WINDOW, _t5_bucket_np(rel), -1).astype(np.int32)
    t5t = jnp.pad(t5_table.astype(F32).T, ((0, 16 - WA_HEADS), (0, 0)))
    return pl.pallas_call(
        _wbias_kernel,
        out_shape=jax.ShapeDtypeStruct((3, WA_HEADS, WA_BLOCK, 3 * WA_BLOCK), F32),
        name="window_bias",
    )(t5t, jnp.asarray(bucket).reshape(1, -1))


def _natten_kernel(q_ref, z_ref, kp_ref, kc_ref, kn_ref, vp_ref, vc_ref, vn_ref, bias_ref, o_ref):
    outs = []
    for h in range(NA_HEADS):
        hs = slice(h * HEAD_DIM, (h + 1) * HEAD_DIM)
        k3 = jnp.concatenate([kp_ref[:, hs], kc_ref[:, hs], kn_ref[:, hs]], axis=0)
        v3 = jnp.concatenate([vp_ref[:, hs], vc_ref[:, hs], vn_ref[:, hs]], axis=0)
        q = q_ref[:, hs] * (HEAD_DIM ** -0.5)
        s = lax.dot_general(q, k3, (((1,), (1,)), ((), ())), preferred_element_type=F32)
        s = s + bias_ref[h]
        m = jnp.max(s, axis=-1, keepdims=True)
        p = jnp.exp(s - m)
        den = jnp.sum(p, axis=-1, keepdims=True)
        o = jnp.dot(p.astype(BF16), v3, preferred_element_type=F32)
        outs.append(o / den)
    o_all = jnp.concatenate(outs, axis=1)
    o_ref[...] = (o_all * _silu(z_ref[...].astype(F32))).astype(BF16)


def _neighbourhood_attention(u3, bias, layer):
    bsz, s, _ = u3.shape
    nblk = s // NA_BLOCK
    w = NA_WIDTH
    prev = lambda b, n: jnp.maximum(n - 1, 0)
    cur = lambda b, n: n
    nxt = lambda b, n: jnp.minimum(n + 1, nblk - 1)
    spec = lambda f, c: pl.BlockSpec((None, NA_BLOCK, w), lambda b, n: (b, f(b, n), c))
    btype = lambda b, n: (layer, jnp.where(n == 0, 0, jnp.where(n == nblk - 1, 2, 1)), 0, 0, 0)
    return pl.pallas_call(
        _natten_kernel,
        grid=(bsz, nblk),
        in_specs=[spec(cur, C_Q // w), spec(cur, C_Z // w),
                  spec(prev, C_K // w), spec(cur, C_K // w), spec(nxt, C_K // w),
                  spec(prev, C_V // w), spec(cur, C_V // w), spec(nxt, C_V // w),
                  pl.BlockSpec((None, None, NA_HEADS, NA_BLOCK, 3 * NA_BLOCK), btype)],
        out_specs=pl.BlockSpec((None, NA_BLOCK, w), lambda b, n: (b, n, 0)),
        out_shape=jax.ShapeDtypeStruct((bsz, s, w), BF16),
        compiler_params=_cparams(("parallel", "arbitrary")),
        name="neighbourhood_attention",
    )(u3, u3, u3, u3, u3, u3, u3, u3, bias)


def _na_valid_rows(rows):
    kh = min(NA_KH_MAX, rows)
    nblk = rows // NA_ROWS
    out = []
    for j in (0, min(1, nblk - 1), nblk - 1):
        r = NA_ROWS * j + np.arange(NA_ROWS)[:, None]
        kr = NA_ROWS * (j - 1) + np.arange(3 * NA_ROWS)[None, :]
        rs = np.clip(r - kh // 2, 0, rows - kh)
        out.append((kr >= rs) & (kr < rs + kh))
    return np.stack(out)


def _nabias_kernel(rpb_ref, o_ref, *, valid):
    w = GRID_W
    j = lax.broadcasted_iota(jnp.int32, (32, 2 * w), 1)
    e = lax.broadcasted_iota(jnp.int32, (32, 2 * w), 0)
    dc = jnp.clip(j - w, -(NA_KW - 1), NA_KW - 1) + NA_KW - 1
    g = _dot_hi(rpb_ref[...], jnp.where(e == dc, 1.0, 0.0).astype(F32))
    lane = lax.broadcasted_iota(jnp.int32, (w, 2 * w), 1)
    qc = lax.broadcasted_iota(jnp.int32, (w, 2 * w), 0)
    kc = lane & (w - 1)
    col_start = jnp.clip(qc - NA_KW // 2, 0, w - NA_KW)
    col_ok = (kc >= col_start) & (kc < col_start + NA_KW)
    left = lane < w
    neg = jnp.full((w, 2 * w), NEG, F32)

    def toeplitz(dr, shift):
        row = jnp.broadcast_to(g[dr:dr + 1, :], (w, 2 * w))
        return pltpu.roll(row, shift, 1, stride=1, stride_axis=0)

    pair = [jnp.where(col_ok, jnp.where(left, toeplitz(d, w), toeplitz(d + 1, 0)), NEG)
            for d in range(2 * NA_KH_MAX - 2)]
    for ty in range(3):
        for rl in range(NA_ROWS):
            for t in range(3 * NA_ROWS // 2):
                d = 2 * t - rl + NA_KH_MAX - 1 - NA_ROWS
                v0, v1 = bool(valid[ty, rl, 2 * t]), bool(valid[ty, rl, 2 * t + 1])
                if v0 and v1:
                    tile = pair[d]
                elif v0:
                    tile = jnp.where(left, pair[d], NEG)
                elif v1:
                    tile = jnp.where(left, NEG, pair[d])
                else:
                    tile = neg
                o_ref[ty, rl * w:(rl + 1) * w, 2 * t * w:(2 * t + 2) * w] = tile


def _na_bias(rpb_all, rows):
    depth = rpb_all.shape[0]
    rpb_p = jnp.pad(rpb_all.astype(F32), ((0, 0), (0, 0), (0, 1), (0, 1)))
    return pl.pallas_call(
        functools.partial(_nabias_kernel, valid=_na_valid_rows(rows)),
        grid=(depth, NA_HEADS),
        in_specs=[pl.BlockSpec((None, None, 16, 32), lambda l, h: (l, h, 0, 0))],
        out_specs=pl.BlockSpec((None, 3, None, NA_BLOCK, 3 * NA_BLOCK), lambda l, h: (l, 0, h, 0, 0)),
        out_shape=jax.ShapeDtypeStruct((depth, 3, NA_HEADS, NA_BLOCK, 3 * NA_BLOCK), F32),
        compiler_params=_cparams(("parallel", "parallel")),
        name="na_bias",
    )(rpb_p)


def _log_sigmoid(x):
    return jnp.minimum(x, 0.0) - jnp.log(1.0 + jnp.exp(-jnp.abs(x)))


def _tri(n, upper):
    r = lax.broadcasted_iota(jnp.int32, (n, n), 0)
    c = lax.broadcasted_iota(jnp.int32, (n, n), 1)
    return jnp.where((r <= c) if upper else (r >= c), 1.0, 0.0).astype(F32)


def _scan_max(x, axis, reverse, size):
    idx = lax.broadcasted_iota(jnp.int32, x.shape, axis)
    k = 1
    while k < size:
        if reverse:
            shifted = pltpu.roll(x, x.shape[axis] - k, axis)
            ok = idx < size - k
        else:
            shifted = pltpu.roll(x, k, axis)
            ok = idx >= k
        x = jnp.maximum(x, jnp.where(ok, shifted, NEG))
        k *= 2
    return x


def _mlstm_kernel(q_ref, k_ref, v_ref, o_ref, z_ref, gr_ref, gt_ref, fb_ref, ng_ref, y_ref,
                  hdir_ref, state_ref, er_ref, xt_ref, *, seq):
    L = ML_CHUNK
    nc = seq // L
    head = pl.program_id(1)
    scale = ML_HEAD_DIM ** -0.5
    lane = lax.broadcasted_iota(jnp.int32, (1, 128), 1)
    chunk_ok = lane < nc

    for d in range(2):
        rev = d == 1
        fb = fb_ref[d, head]
        lf_r = _log_sigmoid(gr_ref[8 * d + 4 + head] + fb)
        b_r = _dot_hi(lf_r, _tri(L, upper=not rev))
        er_ref[d] = gr_ref[8 * d + head] - b_r
        i_t = gt_ref[8 * d + head]
        lf_t = _log_sigmoid(gt_ref[8 * d + 4 + head] + fb)
        b_t = _dot_hi(_tri(L, upper=rev), lf_t)
        g = jnp.sum(lf_t, axis=0, keepdims=True)
        a_t = g - b_t + i_t
        m_loc = jnp.max(a_t, axis=0, keepdims=True)
        w_t = jnp.exp(a_t - m_loc)
        jr = lax.broadcasted_iota(jnp.int32, (128, 128), 0)
        jc = lax.broadcasted_iota(jnp.int32, (128, 128), 1)
        before = ((jr > jc) if rev else (jr < jc)) & (jr < nc)
        g8 = jnp.broadcast_to(g, (8, 128))
        g_ex = _dot_hi(g8, jnp.where(before, 1.0, 0.0).astype(F32))[0:1]
        x = jnp.where(chunk_ok, m_loc - (g_ex + g), NEG)
        x8 = jnp.broadcast_to(x, (8, 128))
        if rev:
            x_prev = jnp.where(lane < nc - 1, pltpu.roll(x8, 127, 1), NEG)
        else:
            x_prev = jnp.where(lane >= 1, pltpu.roll(x8, 1, 1), NEG)
        pm = _scan_max(x_prev, 1, rev, nc)[0:1]
        m_prev = g_ex + jnp.maximum(pm, 0.0)
        m_after = jnp.maximum(g + m_prev, m_loc)
        sp = jnp.exp(g + m_prev - m_after)
        sl = jnp.exp(m_loc - m_after)
        cm = _scan_max(i_t - b_t, 0, rev, L)
        mu = jnp.maximum(m_prev, cm)
        xt_ref[d, 0 * L:1 * L] = mu
        xt_ref[d, 1 * L:2 * L] = jnp.exp(m_prev - mu) * scale
        xt_ref[d, 2 * L:3 * L] = jnp.exp(-b_t - mu)
        xt_ref[d, 3 * L:4 * L] = w_t
        xt_ref[d, 4 * L:4 * L + 8] = jnp.concatenate(
            [sp, sl, jnp.zeros((6, 128), F32)], axis=0)
        state_ref[d] = jnp.zeros((ML_HEAD_DIM, ML_AUG), F32)

    rr = lax.broadcasted_iota(jnp.int32, (L, L), 0)
    cc = lax.broadcasted_iota(jnp.int32, (L, L), 1)
    ones_col = jnp.where(lax.broadcasted_iota(jnp.int32, (L, ML_HEAD_DIM), 1) == 0, 1.0, 0.0).astype(BF16)

    def chunk_step(j, carry):
        for d in range(2):
            c = j if d == 0 else nc - 1 - j
            r0 = pl.multiple_of(c * L, L)
            qc = q_ref[pl.ds(r0, L), :]
            kc = k_ref[pl.ds(r0, L), :]
            vaug = jnp.concatenate([v_ref[pl.ds(r0, L), :], ones_col], axis=1)
            cols = jnp.sum(jnp.where(lane == c, xt_ref[d], 0.0), axis=1, keepdims=True)
            mu, iw, fl, w = (cols[i * L:(i + 1) * L] for i in range(4))
            sp = cols[4 * L:4 * L + 1]
            sl = cols[4 * L + 1:4 * L + 2]
            e_row = er_ref[d, pl.ds(c, 1), :]
            s = lax.dot_general(qc, kc, (((1,), (1,)), ((), ())), preferred_element_type=F32)
            mask = (cc >= rr) if d == 1 else (cc <= rr)
            p = jnp.where(mask, jnp.exp(e_row - mu), 0.0)
            w2 = (s * p * scale).astype(BF16)
            st = state_ref[d]
            out = (jnp.dot(w2, vaug, preferred_element_type=F32)
                   + iw * jnp.dot(qc, st.astype(BF16), preferred_element_type=F32))
            num = out[:, :ML_HEAD_DIM]
            den = out[:, ML_HEAD_DIM:ML_HEAD_DIM + 1]
            hdir_ref[d, pl.ds(r0, L), :] = num / jnp.maximum(jnp.abs(den), fl)
            wv = (w * vaug.astype(F32)).astype(BF16)
            s_loc = lax.dot_general(kc, wv, (((0,), (0,)), ((), ())), preferred_element_type=F32)
            state_ref[d] = sp * st + sl * s_loc
        return carry

    lax.fori_loop(0, nc, chunk_step, 0)

    tile = min(512, seq)

    def finish(t, carry):
        r0 = pl.multiple_of(t * tile, tile)
        rows = pl.ds(r0, tile)
        hs = hdir_ref[0, rows, :] + hdir_ref[1, rows, :]
        hs = jax.nn.sigmoid(o_ref[rows, :].astype(F32)) * hs
        mu = jnp.mean(hs, axis=-1, keepdims=True)
        hc = hs - mu
        var = jnp.mean(hc * hc, axis=-1, keepdims=True)
        hn = hc * lax.rsqrt(var + LN_EPS) * ng_ref[...]
        y_ref[rows, :] = (hn * _silu(z_ref[rows, :].astype(F32))).astype(BF16)
        return carry

    lax.fori_loop(0, seq // tile, finish, 0)


def _mlstm(u3, gates_r, gates_t, f_bias, norm_g):
    bsz, s, _ = u3.shape
    d = ML_HEAD_DIM
    nc = s // ML_CHUNK
    col = lambda base: pl.BlockSpec((None, s, d), lambda b, h: (b, 0, base // d + h))
    return pl.pallas_call(
        functools.partial(_mlstm_kernel, seq=s),
        grid=(bsz, ML_HEADS),
        in_specs=[col(B_Q), col(B_K), col(B_V), col(B_O), col(B_Z),
                  pl.BlockSpec((None, 16, nc, ML_CHUNK), lambda b, h: (b, 0, 0, 0)),
                  pl.BlockSpec((None, 16, ML_CHUNK, 128), lambda b, h: (b, 0, 0, 0)),
                  pl.BlockSpec(memory_space=pltpu.SMEM),
                  pl.BlockSpec((1, d), lambda b, h: (0, h))],
        out_specs=pl.BlockSpec((None, s, d), lambda b, h: (b, 0, h)),
        out_shape=jax.ShapeDtypeStruct((bsz, s, ML_WIDTH), BF16),
        scratch_shapes=[pltpu.VMEM((2, s, d), F32),
                        pltpu.VMEM((2, d, ML_AUG), F32),
                        pltpu.VMEM((2, nc, ML_CHUNK), F32),
                        pltpu.VMEM((2, 4 * ML_CHUNK + 8, 128), F32)],
        compiler_params=_cparams(("parallel", "parallel")),
        name="mlstm",
    )(u3, u3, u3, u3, u3, gates_r, gates_t, f_bias, norm_g.reshape(1, ML_WIDTH))


def _reorder_in_proj(w, b):
    wm = jnp.concatenate([w[:, a:e] for a, e in _SRC_RANGES], axis=1).astype(BF16)
    bm = jnp.concatenate([b[a:e] for a, e in _SRC_RANGES])
    g0, g1 = _SRC_GATES
    wg = jnp.pad(w[:, g0:g1], ((0, 0), (0, GATE_PAD - (g1 - g0)))).astype(BF16)
    bg = jnp.pad(b[g0:g1], (0, GATE_PAD - (g1 - g0)))
    return wm, bm, wg, bg


def _reorder_out_proj(w):
    return jnp.concatenate([w[a:e] for a, e in _A_HEAD_COLS] + [w[WA_WIDTH:]], axis=0).astype(BF16)


def kernel(x, emb_ln_g, emb_ln_b, w_in, b_in, w_out, b_out, ln_g, ln_b, t5_bias, sink, ml_f_bias,
           ml_norm_g, na_rpb):
    bsz, s, d = x.shape
    depth = w_in.shape[0]
    alpha = (2 * depth) ** 0.25
    m = bsz * s
    nc = s // ML_CHUNK
    assert d == D_MODEL and s % NA_BLOCK == 0 and s // GRID_W >= NA_KH_MAX and nc <= 128

    hf, hb = _input_norm(x.reshape(m, d), emb_ln_g, emb_ln_b)
    bias_a = _window_bias(t5_bias)
    bias_c = _na_bias(na_rpb, s // GRID_W)
    for l in range(depth):
        wm, bm, wg, bg = _reorder_in_proj(w_in[l], b_in[l])
        u3 = _projection(hb, wm, bm, BF16, 1536, "in_projection").reshape(bsz, s, U_WIDTH)
        gates = _projection(hb, wg, bg, F32, GATE_PAD, "gate_projection")[:, :16]
        gates = gates.reshape(bsz, s, 16).transpose(0, 2, 1).reshape(bsz, 16, nc, ML_CHUNK)
        gates_t = jnp.pad(gates.transpose(0, 1, 3, 2), ((0, 0), (0, 0), (0, 0), (0, 128 - nc)))
        ya = _window_attention(u3, bias_a, sink[l])
        yb = _mlstm(u3, gates, gates_t, ml_f_bias[l], ml_norm_g[l])
        yc = _neighbourhood_attention(u3, bias_c, l)
        last = l == depth - 1
        res = _out_projection(ya.reshape(m, WA_WIDTH), yb.reshape(m, ML_WIDTH), yc.reshape(m, NA_WIDTH),
                              _reorder_out_proj(w_out[l]), b_out[l], hf, ln_g[l], ln_b[l], alpha, not last)
        hf = res[0]
        hb = None if last else res[1]
    return hf.reshape(bsz, s, d)
```

```python
import functools
import math

import numpy as np
import jax
import jax.numpy as jnp
from jax import lax
from jax.experimental import pallas as pl
from jax.experimental.pallas import tpu as pltpu

F32 = jnp.float32
BF16 = jnp.bfloat16

D_MODEL = 2048
HEAD_DIM = 64
LN_EPS = 1e-5
NEG = -1e30
LOG2E = math.log2(math.e)
SM_CHUNK = 64

WA_HEADS = 12
WA_KV_HEADS = 4
WA_GROUP = WA_HEADS // WA_KV_HEADS
WA_WIDTH = WA_HEADS * HEAD_DIM
WA_KV_WIDTH = WA_KV_HEADS * HEAD_DIM
WA_BLOCK = 128
WINDOW = 128
T5_BUCKETS = 32
T5_MAX_DIST = 128
ML_HEADS = 4
ML_HEAD_DIM = 128
ML_WIDTH = ML_HEADS * ML_HEAD_DIM
ML_CHUNK = 128
ML_AUG = 2 * ML_HEAD_DIM
NA_HEADS = 12
NA_WIDTH = NA_HEADS * HEAD_DIM
GRID_W = 64
NA_KH_MAX = 8
NA_KW = 16
NA_ROWS = 4
NA_BLOCK = NA_ROWS * GRID_W

MIX_WIDTH = WA_WIDTH + ML_WIDTH + NA_WIDTH
GATE_PAD = 128

A_Q, A_Z, A_K, A_V = 0, 768, 1536, 1792
B_Q, B_K, B_V, B_O, B_Z = 2048, 2560, 3072, 3584, 4096
C_Q, C_K, C_V, C_Z = 4608, 5376, 6144, 6912
U_WIDTH = 7680
_SRC_GATES = (4608, 4624)

VMEM_LIMIT = 56 * 1024 * 1024


def _cparams(sem, flags=None):
    return pltpu.CompilerParams(dimension_semantics=sem, vmem_limit_bytes=VMEM_LIMIT, flags=flags)


def _dot_hi(a, b):
    return jnp.dot(a, b, preferred_element_type=F32, precision=lax.Precision.HIGHEST)


def _layer_norm_rows(x, g, b):
    mu = jnp.mean(x, axis=-1, keepdims=True)
    xc = x - mu
    var = jnp.mean(xc * xc, axis=-1, keepdims=True)
    return xc * lax.rsqrt(var + LN_EPS) * g + b


def _ln_kernel(x_ref, g_ref, b_ref, of_ref, ob_ref):
    y = _layer_norm_rows(x_ref[...].astype(F32), g_ref[...], b_ref[...])
    of_ref[...] = y
    ob_ref[...] = y.astype(BF16)


def _input_norm(x2, g, b):
    m, d = x2.shape
    tm = min(512, m)
    return pl.pallas_call(
        _ln_kernel,
        grid=(m // tm,),
        in_specs=[pl.BlockSpec((tm, d), lambda i: (i, 0)),
                  pl.BlockSpec((1, d), lambda i: (0, 0)),
                  pl.BlockSpec((1, d), lambda i: (0, 0))],
        out_specs=[pl.BlockSpec((tm, d), lambda i: (i, 0)),
                   pl.BlockSpec((tm, d), lambda i: (i, 0))],
        out_shape=[jax.ShapeDtypeStruct((m, d), F32), jax.ShapeDtypeStruct((m, d), BF16)],
        compiler_params=_cparams(("parallel",)),
        name="input_norm",
    )(x2, g.reshape(1, d), b.reshape(1, d))


def _proj_kernel(h_ref, w_ref, b_ref, o_ref):
    acc = jnp.dot(h_ref[...], w_ref[...], preferred_element_type=F32)
    o_ref[...] = (acc + b_ref[...]).astype(o_ref.dtype)


def _projection(hb, w, b, out_dtype, tn, name):
    m, d = hb.shape
    n = w.shape[1]
    tm = min(1024, m)
    return pl.pallas_call(
        _proj_kernel,
        grid=(n // tn, m // tm),
        in_specs=[pl.BlockSpec((tm, d), lambda j, i: (i, 0)),
                  pl.BlockSpec((d, tn), lambda j, i: (0, j)),
                  pl.BlockSpec((1, tn), lambda j, i: (0, j))],
        out_specs=pl.BlockSpec((tm, tn), lambda j, i: (i, j)),
        out_shape=jax.ShapeDtypeStruct((m, n), out_dtype),
        compiler_params=_cparams(("parallel", "parallel")),
        name=name,
    )(hb, w, b.reshape(1, n))


def _outproj_kernel(ya_ref, yb_ref, yc_ref, w_ref, b_ref, h_ref, g_ref, beta_ref, *out_refs, alpha):
    y = jnp.concatenate([ya_ref[...], yb_ref[...], yc_ref[...]], axis=1)
    out = jnp.dot(y, w_ref[...], preferred_element_type=F32) + b_ref[...]
    r = _layer_norm_rows(alpha * h_ref[...] + out, g_ref[...], beta_ref[...])
    out_refs[0][...] = r
    if len(out_refs) > 1:
        out_refs[1][...] = r.astype(BF16)


def _out_projection(ya, yb, yc, w, b, hres, g, beta, alpha, want_bf16):
    m, d = hres.shape
    tm = min(256, m)
    row = lambda i: (i, 0)
    const = lambda i: (0, 0)
    out_specs = [pl.BlockSpec((tm, d), row)]
    out_shape = [jax.ShapeDtypeStruct((m, d), F32)]
    if want_bf16:
        out_specs.append(pl.BlockSpec((tm, d), row))
        out_shape.append(jax.ShapeDtypeStruct((m, d), BF16))
    return pl.pallas_call(
        functools.partial(_outproj_kernel, alpha=alpha),
        grid=(m // tm,),
        in_specs=[pl.BlockSpec((tm, WA_WIDTH), row),
                  pl.BlockSpec((tm, ML_WIDTH), row),
                  pl.BlockSpec((tm, NA_WIDTH), row),
                  pl.BlockSpec((MIX_WIDTH, d), const),
                  pl.BlockSpec((1, d), const),
                  pl.BlockSpec((tm, d), row),
                  pl.BlockSpec((1, d), const),
                  pl.BlockSpec((1, d), const)],
        out_specs=out_specs,
        out_shape=out_shape,
        compiler_params=_cparams(("parallel",)),
        name="out_projection",
    )(ya, yb, yc, w, b.reshape(1, d), hres, g.reshape(1, d), beta.reshape(1, d))


def _silu(z):
    return z * jax.nn.sigmoid(z)


def _wattn_kernel(q_ref, z_ref, kp_ref, kc_ref, kn_ref, vp_ref, vc_ref, vn_ref, bias_ref, sink_ref,
                  o_ref, s0_ref, s1_ref, p_ref):
    n = pl.program_id(1)
    blk = WA_BLOCK
    npair = WA_KV_HEADS // 2
    rows = 2 * WA_GROUP * blk
    lane = lax.broadcasted_iota(jnp.int32, (blk, 2 * HEAD_DIM), 1)
    left = lane < HEAD_DIM

    r2 = lax.broadcasted_iota(jnp.int32, (blk, 4 * HEAD_DIM), 0)
    c2 = lax.broadcasted_iota(jnp.int32, (blk, 4 * HEAD_DIM), 1)
    sink_rows = jnp.where((r2 == 0) & (c2 == 2 * HEAD_DIM), 1.0, 0.0).astype(BF16)
    ones_col = jnp.where(lax.broadcasted_iota(jnp.int32, (3 * blk, 2 * HEAD_DIM), 1) == 0, 1.0, 0.0).astype(BF16)
    lane0 = lax.broadcasted_iota(jnp.int32, (SM_CHUNK, 2 * HEAD_DIM), 1) == 0

    def step(s_read, s_write):
        for i in range(npair):
            ks = slice(i * 2 * HEAD_DIM, (i + 1) * 2 * HEAD_DIM)
            k3 = jnp.concatenate([kp_ref[:, ks], kc_ref[:, ks], kn_ref[:, ks]], axis=0)
            tiles = [q_ref[:, (WA_GROUP * i + j) * 2 * HEAD_DIM:(WA_GROUP * i + j + 1) * 2 * HEAD_DIM]
                     * (HEAD_DIM ** -0.5 * LOG2E) for j in range(WA_GROUP)]
            zero = jnp.zeros_like(tiles[0])
            lhs = jnp.concatenate([jnp.where(left, t, zero) for t in tiles]
                                  + [jnp.where(left, zero, t) for t in tiles], axis=0)
            s_write[i] = lax.dot_general(lhs, k3, (((1,), (1,)), ((), ())), preferred_element_type=F32)
            v3 = jnp.concatenate([vp_ref[:, ks], vc_ref[:, ks], vn_ref[:, ks]], axis=0)
            vaug = jnp.concatenate([jnp.concatenate([v3, ones_col], axis=1), sink_rows], axis=0)
            for c in range(rows // SM_CHUNK):
                rs = slice(c * SM_CHUNK, (c + 1) * SM_CHUNK)
                sk = sink_ref[0, 2 * WA_GROUP * i + c * SM_CHUNK // blk] * LOG2E
                sc = s_read[i, rs, :] + bias_ref[i, rs, :]
                m = jnp.maximum(jnp.max(sc, axis=-1, keepdims=True), sk)
                p_ref[i, rs, :3 * blk] = jnp.exp2(sc - m).astype(BF16)
                p_ref[i, rs, 3 * blk:] = jnp.where(lane0, jnp.exp2(sk - m), 0.0).astype(BF16)
            out = jnp.dot(p_ref[i], vaug, preferred_element_type=F32)
            o = out[:, :2 * HEAD_DIM] / out[:, 2 * HEAD_DIM:2 * HEAD_DIM + 1]
            for j in range(WA_GROUP):
                t = WA_GROUP * i + j
                ot = jnp.where(left, o[j * blk:(j + 1) * blk], o[(WA_GROUP + j) * blk:(WA_GROUP + j + 1) * blk])
                cs = slice(t * 2 * HEAD_DIM, (t + 1) * 2 * HEAD_DIM)
                o_ref[:, cs] = (ot * _silu(z_ref[:, cs].astype(F32))).astype(BF16)

    @pl.when(n == 0)
    def _():
        s1_ref[...] = jnp.zeros(s1_ref.shape, F32)

    @pl.when(n % 2 == 0)
    def _():
        step(s1_ref, s0_ref)

    @pl.when(n % 2 == 1)
    def _():
        step(s0_ref, s1_ref)


def _window_attention(u3, bias, sink):
    bsz, s, _ = u3.shape
    nb = s // WA_BLOCK
    qw, kw = WA_WIDTH, WA_KV_WIDTH
    npair = WA_KV_HEADS // 2
    rows = 2 * WA_GROUP * WA_BLOCK
    clamp = lambda i: jnp.clip(i, 0, nb - 1)
    kspec = lambda off, c: pl.BlockSpec((None, WA_BLOCK, kw), lambda b, n: (b, clamp(n + off), c))
    btype = lambda b, n: (jnp.where(n <= 1, 0, jnp.where(n == nb, 2, 1)), 0, 0, 0)
    return pl.pallas_call(
        _wattn_kernel,
        grid=(bsz, nb + 1),
        in_specs=[pl.BlockSpec((None, WA_BLOCK, qw), lambda b, n: (b, clamp(n), A_Q // qw)),
                  pl.BlockSpec((None, WA_BLOCK, qw), lambda b, n: (b, clamp(n - 1), A_Z // qw)),
                  kspec(-1, A_K // kw), kspec(0, A_K // kw), kspec(1, A_K // kw),
                  kspec(-2, A_V // kw), kspec(-1, A_V // kw), kspec(0, A_V // kw),
                  pl.BlockSpec((None, npair, rows, 3 * WA_BLOCK), btype),
                  pl.BlockSpec(memory_space=pltpu.SMEM)],
        out_specs=pl.BlockSpec((None, WA_BLOCK, qw), lambda b, n: (b, clamp(n - 1), 0)),
        out_shape=jax.ShapeDtypeStruct((bsz, s, qw), BF16),
        scratch_shapes=[pltpu.VMEM((npair, rows, 3 * WA_BLOCK), F32),
                        pltpu.VMEM((npair, rows, 3 * WA_BLOCK), F32),
                        pltpu.VMEM((npair, rows, 4 * WA_BLOCK), BF16)],
        compiler_params=_cparams(("parallel", "arbitrary")),
        name="window_attention",
    )(u3, u3, u3, u3, u3, u3, u3, u3, bias.reshape(3, npair, rows, 3 * WA_BLOCK), sink.reshape(1, WA_HEADS))


def _t5_bucket_np(rel):
    half = T5_BUCKETS // 2
    max_exact = half // 2
    ret = np.where(rel > 0, half, 0)
    n = np.abs(rel)
    nf = np.maximum(n, 1).astype(np.float64)
    v = np.log(nf / max_exact) / math.log(T5_MAX_DIST / max_exact) * (half - max_exact)
    vr = np.round(v)
    v = np.where(np.abs(v - vr) < 1e-9, vr, v)
    large = np.minimum(max_exact + np.trunc(v).astype(np.int64), half - 1)
    return ret + np.where(n < max_exact, n, large)


def _wbias_kernel(t5t_ref, bucket_ref, o_ref):
    width = 4 * WA_BLOCK
    bk = bucket_ref[...]
    e = lax.broadcasted_iota(jnp.int32, (T5_BUCKETS, width), 0)
    onehot = jnp.where(e == bk, 1.0, 0.0).astype(F32)
    g = _dot_hi(t5t_ref[...], onehot) * LOG2E + jnp.where(bk < 0, NEG, 0.0)
    col = lax.broadcasted_iota(jnp.int32, (WA_BLOCK, 3 * WA_BLOCK), 1)
    for h in range(WA_HEADS):
        row = jnp.broadcast_to(g[h:h + 1, :], (WA_BLOCK, width))
        t = pltpu.roll(row, 3 * WA_BLOCK, 1, stride=1, stride_axis=0)[:, :3 * WA_BLOCK]
        o_ref[0, h] = jnp.where(col < WA_BLOCK, NEG, t)
        o_ref[1, h] = t
        o_ref[2, h] = jnp.where(col >= 2 * WA_BLOCK, NEG, t)


def _window_bias(t5_table):
    rel = np.arange(4 * WA_BLOCK) - 2 * WA_BLOCK
    bucket = np.where(np.abs(rel) <= WINDOW, _t5_bucket_np(rel), -1).astype(np.int32)
    t5t = jnp.pad(t5_table.astype(F32).T, ((0, 16 - WA_HEADS), (0, 0)))
    return pl.pallas_call(
        _wbias_kernel,
        out_shape=jax.ShapeDtypeStruct((3, WA_HEADS, WA_BLOCK, 3 * WA_BLOCK), F32),
        name="window_bias",
    )(t5t, jnp.asarray(bucket).reshape(1, -1))


def _natten_kernel(q_ref, z_ref, kp_ref, kc_ref, kn_ref, vp_ref, vc_ref, vn_ref, bias_ref, o_ref,
                   s0_ref, s1_ref, p_ref):
    n = pl.program_id(1)
    blk = NA_BLOCK
    lane = lax.broadcasted_iota(jnp.int32, (blk, 2 * HEAD_DIM), 1)
    left = lane < HEAD_DIM

    ones_col = jnp.where(lax.broadcasted_iota(jnp.int32, (3 * blk, 2 * HEAD_DIM), 1) == 0, 1.0, 0.0).astype(BF16)

    def step(s_read, s_write):
        for i in range(NA_HEADS // 2):
            cs = slice(i * 2 * HEAD_DIM, (i + 1) * 2 * HEAD_DIM)
            k3 = jnp.concatenate([kp_ref[:, cs], kc_ref[:, cs], kn_ref[:, cs]], axis=0)
            t = q_ref[:, cs] * (HEAD_DIM ** -0.5 * LOG2E)
            zero = jnp.zeros_like(t)
            lhs = jnp.concatenate([jnp.where(left, t, zero), jnp.where(left, zero, t)], axis=0)
            s_write[i] = lax.dot_general(lhs, k3, (((1,), (1,)), ((), ())), preferred_element_type=F32)
            pb = p_ref.at[i % 3]
            for c in range(2 * blk // SM_CHUNK):
                rs = slice(c * SM_CHUNK, (c + 1) * SM_CHUNK)
                sc = s_read[i, rs, :] + bias_ref[i, rs, :]
                m = jnp.max(sc, axis=-1, keepdims=True)
                pb[rs, :] = jnp.exp2(sc - m).astype(BF16)
            if i > 0:
                weighted_values(i - 1)
        weighted_values(NA_HEADS // 2 - 1)

    def weighted_values(i):
        cs = slice(i * 2 * HEAD_DIM, (i + 1) * 2 * HEAD_DIM)
        v3 = jnp.concatenate([vp_ref[:, cs], vc_ref[:, cs], vn_ref[:, cs]], axis=0)
        vaug = jnp.concatenate([v3, ones_col], axis=1)
        out = jnp.dot(p_ref[i % 3], vaug, preferred_element_type=F32)
        o = out[:, :2 * HEAD_DIM] / out[:, 2 * HEAD_DIM:2 * HEAD_DIM + 1]
        ot = jnp.where(left, o[:blk], o[blk:])
        o_ref[:, cs] = (ot * _silu(z_ref[:, cs].astype(F32))).astype(BF16)

    @pl.when(n == 0)
    def _():
        s1_ref[...] = jnp.zeros(s1_ref.shape, F32)

    @pl.when(n % 2 == 0)
    def _():
        step(s1_ref, s0_ref)

    @pl.when(n % 2 == 1)
    def _():
        step(s0_ref, s1_ref)


def _neighbourhood_attention(u3, bias, layer):
    bsz, s, _ = u3.shape
    nblk = s // NA_BLOCK
    w = NA_WIDTH
    npair = NA_HEADS // 2
    clamp = lambda i: jnp.clip(i, 0, nblk - 1)
    spec = lambda off, c: pl.BlockSpec((None, NA_BLOCK, w), lambda b, n: (b, clamp(n + off), c))
    btype = lambda b, n: (layer, jnp.where(n <= 1, 0, jnp.where(n == nblk, 2, 1)), 0, 0, 0)
    sshape = (npair, 2 * NA_BLOCK, 3 * NA_BLOCK)
    return pl.pallas_call(
        _natten_kernel,
        grid=(bsz, nblk + 1),
        in_specs=[spec(0, C_Q // w), spec(-1, C_Z // w),
                  spec(-1, C_K // w), spec(0, C_K // w), spec(1, C_K // w),
                  spec(-2, C_V // w), spec(-1, C_V // w), spec(0, C_V // w),
                  pl.BlockSpec((None, None) + sshape, btype)],
        out_specs=pl.BlockSpec((None, NA_BLOCK, w), lambda b, n: (b, clamp(n - 1), 0)),
        out_shape=jax.ShapeDtypeStruct((bsz, s, w), BF16),
        scratch_shapes=[pltpu.VMEM(sshape, F32), pltpu.VMEM(sshape, F32),
                        pltpu.VMEM((3,) + sshape[1:], BF16)],
        compiler_params=_cparams(("parallel", "arbitrary")),
        name="neighbourhood_attention",
    )(u3, u3, u3, u3, u3, u3, u3, u3, bias.reshape(bias.shape[:2] + sshape))


def _na_valid_rows(rows):
    kh = min(NA_KH_MAX, rows)
    nblk = rows // NA_ROWS
    out = []
    for j in (0, min(1, nblk - 1), nblk - 1):
        r = NA_ROWS * j + np.arange(NA_ROWS)[:, None]
        kr = NA_ROWS * (j - 1) + np.arange(3 * NA_ROWS)[None, :]
        rs = np.clip(r - kh // 2, 0, rows - kh)
        out.append((kr >= rs) & (kr < rs + kh))
    return np.stack(out)


def _nabias_kernel(rpb_ref, o_ref, *, valid):
    w = GRID_W
    j = lax.broadcasted_iota(jnp.int32, (32, 2 * w), 1)
    e = lax.broadcasted_iota(jnp.int32, (32, 2 * w), 0)
    dc = jnp.clip(j - w, -(NA_KW - 1), NA_KW - 1) + NA_KW - 1
    g = _dot_hi(rpb_ref[...], jnp.where(e == dc, 1.0, 0.0).astype(F32)) * LOG2E
    lane = lax.broadcasted_iota(jnp.int32, (w, 2 * w), 1)
    qc = lax.broadcasted_iota(jnp.int32, (w, 2 * w), 0)
    kc = lane & (w - 1)
    col_start = jnp.clip(qc - NA_KW // 2, 0, w - NA_KW)
    col_ok = (kc >= col_start) & (kc < col_start + NA_KW)
    left = lane < w
    neg = jnp.full((w, 2 * w), NEG, F32)

    def toeplitz(dr, shift):
        row = jnp.broadcast_to(g[dr:dr + 1, :], (w, 2 * w))
        return pltpu.roll(row, shift, 1, stride=1, stride_axis=0)

    pair = [jnp.where(col_ok, jnp.where(left, toeplitz(d, w), toeplitz(d + 1, 0)), NEG)
            for d in range(2 * NA_KH_MAX - 2)]
    for ty in range(3):
        for rl in range(NA_ROWS):
            for t in range(3 * NA_ROWS // 2):
                d = 2 * t - rl + NA_KH_MAX - 1 - NA_ROWS
                v0, v1 = bool(valid[ty, rl, 2 * t]), bool(valid[ty, rl, 2 * t + 1])
                if v0 and v1:
                    tile = pair[d]
                elif v0:
                    tile = jnp.where(left, pair[d], NEG)
                elif v1:
                    tile = jnp.where(left, NEG, pair[d])
                else:
                    tile = neg
                o_ref[ty, rl * w:(rl + 1) * w, 2 * t * w:(2 * t + 2) * w] = tile


def _na_bias(rpb_all, rows):
    depth = rpb_all.shape[0]
    rpb_p = jnp.pad(rpb_all.astype(F32), ((0, 0), (0, 0), (0, 1), (0, 1)))
    return pl.pallas_call(
        functools.partial(_nabias_kernel, valid=_na_valid_rows(rows)),
        grid=(depth, NA_HEADS),
        in_specs=[pl.BlockSpec((None, None, 16, 32), lambda l, h: (l, h, 0, 0))],
        out_specs=pl.BlockSpec((None, 3, None, NA_BLOCK, 3 * NA_BLOCK), lambda l, h: (l, 0, h, 0, 0)),
        out_shape=jax.ShapeDtypeStruct((depth, 3, NA_HEADS, NA_BLOCK, 3 * NA_BLOCK), F32),
        compiler_params=_cparams(("parallel", "parallel")),
        name="na_bias",
    )(rpb_p)


def _log_sigmoid(x):
    return jnp.minimum(x, 0.0) - jnp.log(1.0 + jnp.exp(-jnp.abs(x)))


def _tri(n, upper):
    r = lax.broadcasted_iota(jnp.int32, (n, n), 0)
    c = lax.broadcasted_iota(jnp.int32, (n, n), 1)
    return jnp.where((r <= c) if upper else (r >= c), 1.0, 0.0).astype(F32)


def _scan_max(x, axis, reverse, size):
    idx = lax.broadcasted_iota(jnp.int32, x.shape, axis)
    k = 1
    while k < size:
        if reverse:
            shifted = pltpu.roll(x, x.shape[axis] - k, axis)
            ok = idx < size - k
        else:
            shifted = pltpu.roll(x, k, axis)
            ok = idx >= k
        x = jnp.maximum(x, jnp.where(ok, shifted, NEG))
        k *= 2
    return x


def _mlstm_kernel(q_ref, k_ref, v_ref, o_ref, z_ref, gr_ref, gt_ref, fb_ref, ng_ref, y_ref,
                  hdir_ref, state_ref, er_ref, xt_ref, *, seq):
    L = ML_CHUNK
    nc = seq // L
    head = pl.program_id(1)
    scale = ML_HEAD_DIM ** -0.5
    lane = lax.broadcasted_iota(jnp.int32, (1, 128), 1)
    chunk_ok = lane < nc

    for d in range(2):
        rev = d == 1
        fb = fb_ref[d, head]
        lf_r = _log_sigmoid(gr_ref[8 * d + 4 + head] + fb)
        b_r = _dot_hi(lf_r, _tri(L, upper=not rev))
        er_ref[d] = gr_ref[8 * d + head] - b_r
        i_t = gt_ref[8 * d + head]
        lf_t = _log_sigmoid(gt_ref[8 * d + 4 + head] + fb)
        b_t = _dot_hi(_tri(L, upper=rev), lf_t)
        g = jnp.sum(lf_t, axis=0, keepdims=True)
        a_t = g - b_t + i_t
        m_loc = jnp.max(a_t, axis=0, keepdims=True)
        w_t = jnp.exp(a_t - m_loc)
        jr = lax.broadcasted_iota(jnp.int32, (128, 128), 0)
        jc = lax.broadcasted_iota(jnp.int32, (128, 128), 1)
        before = ((jr > jc) if rev else (jr < jc)) & (jr < nc)
        g8 = jnp.broadcast_to(g, (8, 128))
        g_ex = _dot_hi(g8, jnp.where(before, 1.0, 0.0).astype(F32))[0:1]
        x = jnp.where(chunk_ok, m_loc - (g_ex + g), NEG)
        x8 = jnp.broadcast_to(x, (8, 128))
        if rev:
            x_prev = jnp.where(lane < nc - 1, pltpu.roll(x8, 127, 1), NEG)
        else:
            x_prev = jnp.where(lane >= 1, pltpu.roll(x8, 1, 1), NEG)
        pm = _scan_max(x_prev, 1, rev, nc)[0:1]
        m_prev = g_ex + jnp.maximum(pm, 0.0)
        m_after = jnp.maximum(g + m_prev, m_loc)
        sp = jnp.exp(g + m_prev - m_after)
        sl = jnp.exp(m_loc - m_after)
        cm = _scan_max(i_t - b_t, 0, rev, L)
        mu = jnp.maximum(m_prev, cm)
        xt_ref[d, 0 * L:1 * L] = mu
        xt_ref[d, 1 * L:2 * L] = jnp.exp(m_prev - mu) * scale
        xt_ref[d, 2 * L:3 * L] = jnp.exp(-b_t - mu)
        xt_ref[d, 3 * L:4 * L] = w_t
        xt_ref[d, 4 * L:4 * L + 8] = jnp.concatenate(
            [sp, sl, jnp.zeros((6, 128), F32)], axis=0)
        state_ref[d] = jnp.zeros((ML_HEAD_DIM, ML_AUG), F32)

    rr = lax.broadcasted_iota(jnp.int32, (L, L), 0)
    cc = lax.broadcasted_iota(jnp.int32, (L, L), 1)
    ones_col = jnp.where(lax.broadcasted_iota(jnp.int32, (L, ML_HEAD_DIM), 1) == 0, 1.0, 0.0).astype(BF16)

    def chunk_step(j, carry):
        for d in range(2):
            c = j if d == 0 else nc - 1 - j
            r0 = pl.multiple_of(c * L, L)
            qc = q_ref[pl.ds(r0, L), :]
            kc = k_ref[pl.ds(r0, L), :]
            vaug = jnp.concatenate([v_ref[pl.ds(r0, L), :], ones_col], axis=1)
            cols = jnp.sum(jnp.where(lane == c, xt_ref[d], 0.0), axis=1, keepdims=True)
            mu, iw, fl, w = (cols[i * L:(i + 1) * L] for i in range(4))
            sp = cols[4 * L:4 * L + 1]
            sl = cols[4 * L + 1:4 * L + 2]
            e_row = er_ref[d, pl.ds(c, 1), :]
            s = lax.dot_general(qc, kc, (((1,), (1,)), ((), ())), preferred_element_type=F32)
            mask = (cc >= rr) if d == 1 else (cc <= rr)
            p = jnp.where(mask, jnp.exp(e_row - mu), 0.0)
            w2 = (s * p * scale).astype(BF16)
            st = state_ref[d]
            out = (jnp.dot(w2, vaug, preferred_element_type=F32)
                   + iw * jnp.dot(qc, st.astype(BF16), preferred_element_type=F32))
            num = out[:, :ML_HEAD_DIM]
            den = out[:, ML_HEAD_DIM:ML_HEAD_DIM + 1]
            hdir_ref[d, pl.ds(r0, L), :] = num / jnp.maximum(jnp.abs(den), fl)
            wv = (w * vaug.astype(F32)).astype(BF16)
            s_loc = lax.dot_general(kc, wv, (((0,), (0,)), ((), ())), preferred_element_type=F32)
            state_ref[d] = sp * st + sl * s_loc
        return carry

    lax.fori_loop(0, nc, chunk_step, 0)

    tile = min(512, seq)

    def finish(t, carry):
        r0 = pl.multiple_of(t * tile, tile)
        rows = pl.ds(r0, tile)
        hs = hdir_ref[0, rows, :] + hdir_ref[1, rows, :]
        hs = jax.nn.sigmoid(o_ref[rows, :].astype(F32)) * hs
        mu = jnp.mean(hs, axis=-1, keepdims=True)
        hc = hs - mu
        var = jnp.mean(hc * hc, axis=-1, keepdims=True)
        hn = hc * lax.rsqrt(var + LN_EPS) * ng_ref[...]
        y_ref[rows, :] = (hn * _silu(z_ref[rows, :].astype(F32))).astype(BF16)
        return carry

    lax.fori_loop(0, seq // tile, finish, 0)


def _mlstm(u3, gates_r, gates_t, f_bias, norm_g):
    bsz, s, _ = u3.shape
    d = ML_HEAD_DIM
    nc = s // ML_CHUNK
    col = lambda base: pl.BlockSpec((None, s, d), lambda b, h: (b, 0, base // d + h))
    return pl.pallas_call(
        functools.partial(_mlstm_kernel, seq=s),
        grid=(bsz, ML_HEADS),
        in_specs=[col(B_Q), col(B_K), col(B_V), col(B_O), col(B_Z),
                  pl.BlockSpec((None, 16, nc, ML_CHUNK), lambda b, h: (b, 0, 0, 0)),
                  pl.BlockSpec((None, 16, ML_CHUNK, 128), lambda b, h: (b, 0, 0, 0)),
                  pl.BlockSpec(memory_space=pltpu.SMEM),
                  pl.BlockSpec((1, d), lambda b, h: (0, h))],
        out_specs=pl.BlockSpec((None, s, d), lambda b, h: (b, 0, h)),
        out_shape=jax.ShapeDtypeStruct((bsz, s, ML_WIDTH), BF16),
        scratch_shapes=[pltpu.VMEM((2, s, d), F32),
                        pltpu.VMEM((2, d, ML_AUG), F32),
                        pltpu.VMEM((2, nc, ML_CHUNK), F32),
                        pltpu.VMEM((2, 4 * ML_CHUNK + 8, 128), F32)],
        compiler_params=_cparams(("parallel", "parallel")),
        name="mlstm",
    )(u3, u3, u3, u3, u3, gates_r, gates_t, f_bias, norm_g.reshape(1, ML_WIDTH))


def _pair_heads(t, axis):
    shape = t.shape
    t = t.reshape(shape[:axis] + (2, 2, WA_GROUP, HEAD_DIM) + shape[axis + 1:])
    return jnp.swapaxes(t, axis + 1, axis + 2).reshape(shape)


def _reorder_in_proj(w, b):
    def cols(t):
        ax = t.ndim - 1
        return jnp.concatenate(
            [_pair_heads(t[..., 0:768], ax), _pair_heads(t[..., 1280:2048], ax), t[..., 768:1280],
             t[..., 2048:4608], t[..., 4624:7696]], axis=ax)
    g0, g1 = _SRC_GATES
    wg = jnp.pad(w[:, g0:g1], ((0, 0), (0, GATE_PAD - (g1 - g0)))).astype(BF16)
    bg = jnp.pad(b[g0:g1], (0, GATE_PAD - (g1 - g0)))
    return cols(w).astype(BF16), cols(b), wg, bg


def _reorder_out_proj(w):
    return jnp.concatenate([_pair_heads(w[:WA_WIDTH], 0), w[WA_WIDTH:]], axis=0).astype(BF16)


def kernel(x, emb_ln_g, emb_ln_b, w_in, b_in, w_out, b_out, ln_g, ln_b, t5_bias, sink, ml_f_bias,
           ml_norm_g, na_rpb):
    bsz, s, d = x.shape
    depth = w_in.shape[0]
    alpha = (2 * depth) ** 0.25
    m = bsz * s
    nc = s // ML_CHUNK
    assert d == D_MODEL and s % NA_BLOCK == 0 and s // GRID_W >= NA_KH_MAX and nc <= 128

    hf, hb = _input_norm(x.reshape(m, d), emb_ln_g, emb_ln_b)
    bias_a = _window_bias(t5_bias)
    bias_c = _na_bias(na_rpb, s // GRID_W)
    for l in range(depth):
        wm, bm, wg, bg = _reorder_in_proj(w_in[l], b_in[l])
        u3 = _projection(hb, wm, bm, BF16, 1536, "in_projection").reshape(bsz, s, U_WIDTH)
        gates = _projection(hb, wg, bg, F32, GATE_PAD, "gate_projection")[:, :16]
        gates = gates.reshape(bsz, s, 16).transpose(0, 2, 1).reshape(bsz, 16, nc, ML_CHUNK)
        gates_t = jnp.pad(gates.transpose(0, 1, 3, 2), ((0, 0), (0, 0), (0, 0), (0, 128 - nc)))
        ya = _window_attention(u3, bias_a, sink[l])
        yb = _mlstm(u3, gates, gates_t, ml_f_bias[l], ml_norm_g[l])
        yc = _neighbourhood_attention(u3, bias_c, l)
        last = l == depth - 1
        res = _out_projection(ya.reshape(m, WA_WIDTH), yb.reshape(m, ML_WIDTH), yc.reshape(m, NA_WIDTH),
                              _reorder_out_proj(w_out[l]), b_out[l], hf, ln_g[l], ln_b[l], alpha, not last)
        hf = res[0]
        hb = None if last else res[1]
    return hf.reshape(bsz, s, d)
```

```python
import functools
import math

import numpy as np
import jax
import jax.numpy as jnp
from jax import lax
from jax.experimental import pallas as pl
from jax.experimental.pallas import tpu as pltpu

F32 = jnp.float32
BF16 = jnp.bfloat16

D_MODEL = 2048
HEAD_DIM = 64
LN_EPS = 1e-5
NEG = -1e30
LOG2E = math.log2(math.e)
SM_CHUNK = 64

WA_HEADS = 12
WA_KV_HEADS = 4
WA_GROUP = WA_HEADS // WA_KV_HEADS
WA_WIDTH = WA_HEADS * HEAD_DIM
WA_KV_WIDTH = WA_KV_HEADS * HEAD_DIM
WA_BLOCK = 128
WINDOW = 128
T5_BUCKETS = 32
T5_MAX_DIST = 128
ML_HEADS = 4
ML_HEAD_DIM = 128
ML_WIDTH = ML_HEADS * ML_HEAD_DIM
ML_CHUNK = 128
ML_AUG = 2 * ML_HEAD_DIM
NA_HEADS = 12
NA_WIDTH = NA_HEADS * HEAD_DIM
GRID_W = 64
NA_KH_MAX = 8
NA_KW = 16
NA_ROWS = 4
NA_BLOCK = NA_ROWS * GRID_W

MIX_WIDTH = WA_WIDTH + ML_WIDTH + NA_WIDTH
GATE_PAD = 128

A_Q, A_Z, A_K, A_V = 0, 768, 1536, 1792
B_Q, B_K, B_V, B_O, B_Z = 2048, 2560, 3072, 3584, 4096
C_Q, C_K, C_V, C_Z = 4608, 5376, 6144, 6912
U_WIDTH = 7680
_SRC_GATES = (4608, 4624)

VMEM_LIMIT = 56 * 1024 * 1024


def _cparams(sem, flags=None):
    return pltpu.CompilerParams(dimension_semantics=sem, vmem_limit_bytes=VMEM_LIMIT, flags=flags)


def _dot_hi(a, b):
    return jnp.dot(a, b, preferred_element_type=F32, precision=lax.Precision.HIGHEST)


def _layer_norm_rows(x, g, b):
    mu = jnp.mean(x, axis=-1, keepdims=True)
    xc = x - mu
    var = jnp.mean(xc * xc, axis=-1, keepdims=True)
    return xc * lax.rsqrt(var + LN_EPS) * g + b


def _ln_kernel(x_ref, g_ref, b_ref, of_ref, ob_ref):
    y = _layer_norm_rows(x_ref[...].astype(F32), g_ref[...], b_ref[...])
    of_ref[...] = y
    ob_ref[...] = y.astype(BF16)


def _input_norm(x2, g, b):
    m, d = x2.shape
    tm = min(512, m)
    return pl.pallas_call(
        _ln_kernel,
        grid=(m // tm,),
        in_specs=[pl.BlockSpec((tm, d), lambda i: (i, 0)),
                  pl.BlockSpec((1, d), lambda i: (0, 0)),
                  pl.BlockSpec((1, d), lambda i: (0, 0))],
        out_specs=[pl.BlockSpec((tm, d), lambda i: (i, 0)),
                   pl.BlockSpec((tm, d), lambda i: (i, 0))],
        out_shape=[jax.ShapeDtypeStruct((m, d), F32), jax.ShapeDtypeStruct((m, d), BF16)],
        compiler_params=_cparams(("parallel",)),
        name="input_norm",
    )(x2, g.reshape(1, d), b.reshape(1, d))


def _proj_kernel(h_ref, w_ref, b_ref, o_ref):
    acc = jnp.dot(h_ref[...], w_ref[...], preferred_element_type=F32)
    o_ref[...] = (acc + b_ref[...]).astype(o_ref.dtype)


def _projection(hb, w, b, out_dtype, tn, name):
    m, d = hb.shape
    n = w.shape[1]
    tm = min(1024, m)
    return pl.pallas_call(
        _proj_kernel,
        grid=(n // tn, m // tm),
        in_specs=[pl.BlockSpec((tm, d), lambda j, i: (i, 0)),
                  pl.BlockSpec((d, tn), lambda j, i: (0, j)),
                  pl.BlockSpec((1, tn), lambda j, i: (0, j))],
        out_specs=pl.BlockSpec((tm, tn), lambda j, i: (i, j)),
        out_shape=jax.ShapeDtypeStruct((m, n), out_dtype),
        compiler_params=_cparams(("parallel", "parallel")),
        name=name,
    )(hb, w, b.reshape(1, n))


def _outproj_kernel(ya_ref, yb_ref, yc_ref, w_ref, b_ref, h_ref, g_ref, beta_ref, *out_refs, alpha):
    y = jnp.concatenate([ya_ref[...], yb_ref[...], yc_ref[...]], axis=1)
    out = jnp.dot(y, w_ref[...], preferred_element_type=F32) + b_ref[...]
    r = _layer_norm_rows(alpha * h_ref[...] + out, g_ref[...], beta_ref[...])
    out_refs[0][...] = r
    if len(out_refs) > 1:
        out_refs[1][...] = r.astype(BF16)


def _out_projection(ya, yb, yc, w, b, hres, g, beta, alpha, want_bf16):
    m, d = hres.shape
    tm = min(256, m)
    row = lambda i: (i, 0)
    const = lambda i: (0, 0)
    out_specs = [pl.BlockSpec((tm, d), row)]
    out_shape = [jax.ShapeDtypeStruct((m, d), F32)]
    if want_bf16:
        out_specs.append(pl.BlockSpec((tm, d), row))
        out_shape.append(jax.ShapeDtypeStruct((m, d), BF16))
    return pl.pallas_call(
        functools.partial(_outproj_kernel, alpha=alpha),
        grid=(m // tm,),
        in_specs=[pl.BlockSpec((tm, WA_WIDTH), row),
                  pl.BlockSpec((tm, ML_WIDTH), row),
                  pl.BlockSpec((tm, NA_WIDTH), row),
                  pl.BlockSpec((MIX_WIDTH, d), const),
                  pl.BlockSpec((1, d), const),
                  pl.BlockSpec((tm, d), row),
                  pl.BlockSpec((1, d), const),
                  pl.BlockSpec((1, d), const)],
        out_specs=out_specs,
        out_shape=out_shape,
        compiler_params=_cparams(("parallel",)),
        name="out_projection",
    )(ya, yb, yc, w, b.reshape(1, d), hres, g.reshape(1, d), beta.reshape(1, d))


def _silu(z):
    return z * jax.nn.sigmoid(z)


def _wattn_kernel(q_ref, z_ref, kp_ref, kc_ref, kn_ref, vp_ref, vc_ref, vn_ref, bias_ref, sink_ref,
                  o_ref, s0_ref, s1_ref, p_ref):
    n = pl.program_id(1)
    blk = WA_BLOCK
    npair = WA_KV_HEADS // 2
    rows = 2 * WA_GROUP * blk
    lane = lax.broadcasted_iota(jnp.int32, (blk, 2 * HEAD_DIM), 1)
    left = lane < HEAD_DIM

    r2 = lax.broadcasted_iota(jnp.int32, (blk, 4 * HEAD_DIM), 0)
    c2 = lax.broadcasted_iota(jnp.int32, (blk, 4 * HEAD_DIM), 1)
    sink_rows = jnp.where((r2 == 0) & (c2 == 2 * HEAD_DIM), 1.0, 0.0).astype(BF16)
    ones_col = jnp.where(lax.broadcasted_iota(jnp.int32, (3 * blk, 2 * HEAD_DIM), 1) == 0, 1.0, 0.0).astype(BF16)
    lane0 = lax.broadcasted_iota(jnp.int32, (SM_CHUNK, 2 * HEAD_DIM), 1) == 0

    def step(s_read, s_write):
        for i in range(npair):
            ks = slice(i * 2 * HEAD_DIM, (i + 1) * 2 * HEAD_DIM)
            k3 = jnp.concatenate([kp_ref[:, ks], kc_ref[:, ks], kn_ref[:, ks]], axis=0)
            tiles = [q_ref[:, (WA_GROUP * i + j) * 2 * HEAD_DIM:(WA_GROUP * i + j + 1) * 2 * HEAD_DIM]
                     * (HEAD_DIM ** -0.5 * LOG2E) for j in range(WA_GROUP)]
            zero = jnp.zeros_like(tiles[0])
            lhs = jnp.concatenate([jnp.where(left, t, zero) for t in tiles]
                                  + [jnp.where(left, zero, t) for t in tiles], axis=0)
            s_write[i] = lax.dot_general(lhs, k3, (((1,), (1,)), ((), ())), preferred_element_type=F32)
            v3 = jnp.concatenate([vp_ref[:, ks], vc_ref[:, ks], vn_ref[:, ks]], axis=0)
            vaug = jnp.concatenate([jnp.concatenate([v3, ones_col], axis=1), sink_rows], axis=0)
            for c in range(rows // SM_CHUNK):
                rs = slice(c * SM_CHUNK, (c + 1) * SM_CHUNK)
                sk = sink_ref[0, 2 * WA_GROUP * i + c * SM_CHUNK // blk] * LOG2E
                sc = s_read[i, rs, :] + bias_ref[i, rs, :]
                m = jnp.maximum(jnp.max(sc, axis=-1, keepdims=True), sk)
                p_ref[i, rs, :3 * blk] = jnp.exp2(sc - m).astype(BF16)
                p_ref[i, rs, 3 * blk:] = jnp.where(lane0, jnp.exp2(sk - m), 0.0).astype(BF16)
            out = jnp.dot(p_ref[i], vaug, preferred_element_type=F32)
            o = out[:, :2 * HEAD_DIM] / out[:, 2 * HEAD_DIM:2 * HEAD_DIM + 1]
            for j in range(WA_GROUP):
                t = WA_GROUP * i + j
                ot = jnp.where(left, o[j * blk:(j + 1) * blk], o[(WA_GROUP + j) * blk:(WA_GROUP + j + 1) * blk])
                cs = slice(t * 2 * HEAD_DIM, (t + 1) * 2 * HEAD_DIM)
                o_ref[:, cs] = (ot * _silu(z_ref[:, cs].astype(F32))).astype(BF16)

    @pl.when(n == 0)
    def _():
        s1_ref[...] = jnp.zeros(s1_ref.shape, F32)

    @pl.when(n % 2 == 0)
    def _():
        step(s1_ref, s0_ref)

    @pl.when(n % 2 == 1)
    def _():
        step(s0_ref, s1_ref)


def _window_attention(u3, bias, sink):
    bsz, s, _ = u3.shape
    nb = s // WA_BLOCK
    qw, kw = WA_WIDTH, WA_KV_WIDTH
    npair = WA_KV_HEADS // 2
    rows = 2 * WA_GROUP * WA_BLOCK
    clamp = lambda i: jnp.clip(i, 0, nb - 1)
    kspec = lambda off, c: pl.BlockSpec((None, WA_BLOCK, kw), lambda b, n: (b, clamp(n + off), c))
    btype = lambda b, n: (jnp.where(n <= 1, 0, jnp.where(n == nb, 2, 1)), 0, 0, 0)
    return pl.pallas_call(
        _wattn_kernel,
        grid=(bsz, nb + 1),
        in_specs=[pl.BlockSpec((None, WA_BLOCK, qw), lambda b, n: (b, clamp(n), A_Q // qw)),
                  pl.BlockSpec((None, WA_BLOCK, qw), lambda b, n: (b, clamp(n - 1), A_Z // qw)),
                  kspec(-1, A_K // kw), kspec(0, A_K // kw), kspec(1, A_K // kw),
                  kspec(-2, A_V // kw), kspec(-1, A_V // kw), kspec(0, A_V // kw),
                  pl.BlockSpec((None, npair, rows, 3 * WA_BLOCK), btype),
                  pl.BlockSpec(memory_space=pltpu.SMEM)],
        out_specs=pl.BlockSpec((None, WA_BLOCK, qw), lambda b, n: (b, clamp(n - 1), 0)),
        out_shape=jax.ShapeDtypeStruct((bsz, s, qw), BF16),
        scratch_shapes=[pltpu.VMEM((npair, rows, 3 * WA_BLOCK), F32),
                        pltpu.VMEM((npair, rows, 3 * WA_BLOCK), F32),
                        pltpu.VMEM((npair, rows, 4 * WA_BLOCK), BF16)],
        compiler_params=_cparams(("parallel", "arbitrary")),
        name="window_attention",
    )(u3, u3, u3, u3, u3, u3, u3, u3, bias.reshape(3, npair, rows, 3 * WA_BLOCK), sink.reshape(1, WA_HEADS))


def _t5_bucket_np(rel):
    half = T5_BUCKETS // 2
    max_exact = half // 2
    ret = np.where(rel > 0, half, 0)
    n = np.abs(rel)
    nf = np.maximum(n, 1).astype(np.float64)
    v = np.log(nf / max_exact) / math.log(T5_MAX_DIST / max_exact) * (half - max_exact)
    vr = np.round(v)
    v = np.where(np.abs(v - vr) < 1e-9, vr, v)
    large = np.minimum(max_exact + np.trunc(v).astype(np.int64), half - 1)
    return ret + np.where(n < max_exact, n, large)


def _wbias_kernel(t5t_ref, bucket_ref, o_ref):
    width = 4 * WA_BLOCK
    bk = bucket_ref[...]
    e = lax.broadcasted_iota(jnp.int32, (T5_BUCKETS, width), 0)
    onehot = jnp.where(e == bk, 1.0, 0.0).astype(F32)
    g = _dot_hi(t5t_ref[...], onehot) * LOG2E + jnp.where(bk < 0, NEG, 0.0)
    col = lax.broadcasted_iota(jnp.int32, (WA_BLOCK, 3 * WA_BLOCK), 1)
    for h in range(WA_HEADS):
        row = jnp.broadcast_to(g[h:h + 1, :], (WA_BLOCK, width))
        t = pltpu.roll(row, 3 * WA_BLOCK, 1, stride=1, stride_axis=0)[:, :3 * WA_BLOCK]
        o_ref[0, h] = jnp.where(col < WA_BLOCK, NEG, t)
        o_ref[1, h] = t
        o_ref[2, h] = jnp.where(col >= 2 * WA_BLOCK, NEG, t)


def _window_bias(t5_table):
    rel = np.arange(4 * WA_BLOCK) - 2 * WA_BLOCK
    bucket = np.where(np.abs(rel) <= WINDOW, _t5_bucket_np(rel), -1).astype(np.int32)
    t5t = jnp.pad(t5_table.astype(F32).T, ((0, 16 - WA_HEADS), (0, 0)))
    return pl.pallas_call(
        _wbias_kernel,
        out_shape=jax.ShapeDtypeStruct((3, WA_HEADS, WA_BLOCK, 3 * WA_BLOCK), F32),
        name="window_bias",
    )(t5t, jnp.asarray(bucket).reshape(1, -1))


def _natten_kernel(q_ref, z_ref, kp_ref, kc_ref, kn_ref, vp_ref, vc_ref, vn_ref, bias_ref, o_ref,
                   s0_ref, s1_ref, p_ref):
    n = pl.program_id(1)
    blk = NA_BLOCK
    lane = lax.broadcasted_iota(jnp.int32, (blk, 2 * HEAD_DIM), 1)
    left = lane < HEAD_DIM

    ones_col = jnp.where(lax.broadcasted_iota(jnp.int32, (3 * blk, 2 * HEAD_DIM), 1) == 0, 1.0, 0.0).astype(BF16)

    def step(s_read, s_write):
        for i in range(NA_HEADS // 2):
            cs = slice(i * 2 * HEAD_DIM, (i + 1) * 2 * HEAD_DIM)
            k3 = jnp.concatenate([kp_ref[:, cs], kc_ref[:, cs], kn_ref[:, cs]], axis=0)
            t = q_ref[:, cs] * (HEAD_DIM ** -0.5 * LOG2E)
            zero = jnp.zeros_like(t)
            lhs = jnp.concatenate([jnp.where(left, t, zero), jnp.where(left, zero, t)], axis=0)
            s_write[i] = lax.dot_general(lhs, k3, (((1,), (1,)), ((), ())), preferred_element_type=F32)
            pb = p_ref.at[i % 3]
            for c in range(2 * blk // SM_CHUNK):
                rs = slice(c * SM_CHUNK, (c + 1) * SM_CHUNK)
                sc = s_read[i, rs, :] + bias_ref[i, rs, :]
                m = jnp.max(sc, axis=-1, keepdims=True)
                pb[rs, :] = jnp.exp2(sc - m).astype(BF16)
            if i > 0:
                weighted_values(i - 1)
        weighted_values(NA_HEADS // 2 - 1)

    def weighted_values(i):
        cs = slice(i * 2 * HEAD_DIM, (i + 1) * 2 * HEAD_DIM)
        v3 = jnp.concatenate([vp_ref[:, cs], vc_ref[:, cs], vn_ref[:, cs]], axis=0)
        vaug = jnp.concatenate([v3, ones_col], axis=1)
        out = jnp.dot(p_ref[i % 3], vaug, preferred_element_type=F32)
        o = out[:, :2 * HEAD_DIM] / out[:, 2 * HEAD_DIM:2 * HEAD_DIM + 1]
        ot = jnp.where(left, o[:blk], o[blk:])
        o_ref[:, cs] = (ot * _silu(z_ref[:, cs].astype(F32))).astype(BF16)

    @pl.when(n == 0)
    def _():
        s1_ref[...] = jnp.zeros(s1_ref.shape, F32)

    @pl.when(n % 2 == 0)
    def _():
        step(s1_ref, s0_ref)

    @pl.when(n % 2 == 1)
    def _():
        step(s0_ref, s1_ref)


def _neighbourhood_attention(u3, bias, layer):
    bsz, s, _ = u3.shape
    nblk = s // NA_BLOCK
    w = NA_WIDTH
    npair = NA_HEADS // 2
    clamp = lambda i: jnp.clip(i, 0, nblk - 1)
    spec = lambda off, c: pl.BlockSpec((None, NA_BLOCK, w), lambda b, n: (b, clamp(n + off), c))
    btype = lambda b, n: (layer, jnp.where(n <= 1, 0, jnp.where(n == nblk, 2, 1)), 0, 0, 0)
    sshape = (npair, 2 * NA_BLOCK, 3 * NA_BLOCK)
    return pl.pallas_call(
        _natten_kernel,
        grid=(bsz, nblk + 1),
        in_specs=[spec(0, C_Q // w), spec(-1, C_Z // w),
                  spec(-1, C_K // w), spec(0, C_K // w), spec(1, C_K // w),
                  spec(-2, C_V // w), spec(-1, C_V // w), spec(0, C_V // w),
                  pl.BlockSpec((None, None) + sshape, btype)],
        out_specs=pl.BlockSpec((None, NA_BLOCK, w), lambda b, n: (b, clamp(n - 1), 0)),
        out_shape=jax.ShapeDtypeStruct((bsz, s, w), BF16),
        scratch_shapes=[pltpu.VMEM(sshape, F32), pltpu.VMEM(sshape, F32),
                        pltpu.VMEM((3,) + sshape[1:], BF16)],
        compiler_params=_cparams(("parallel", "arbitrary")),
        name="neighbourhood_attention",
    )(u3, u3, u3, u3, u3, u3, u3, u3, bias.reshape(bias.shape[:2] + sshape))


def _na_valid_rows(rows):
    kh = min(NA_KH_MAX, rows)
    nblk = rows // NA_ROWS
    out = []
    for j in (0, min(1, nblk - 1), nblk - 1):
        r = NA_ROWS * j + np.arange(NA_ROWS)[:, None]
        kr = NA_ROWS * (j - 1) + np.arange(3 * NA_ROWS)[None, :]
        rs = np.clip(r - kh // 2, 0, rows - kh)
        out.append((kr >= rs) & (kr < rs + kh))
    return np.stack(out)


def _nabias_kernel(rpb_ref, o_ref, *, valid):
    w = GRID_W
    j = lax.broadcasted_iota(jnp.int32, (32, 2 * w), 1)
    e = lax.broadcasted_iota(jnp.int32, (32, 2 * w), 0)
    dc = jnp.clip(j - w, -(NA_KW - 1), NA_KW - 1) + NA_KW - 1
    g = _dot_hi(rpb_ref[...], jnp.where(e == dc, 1.0, 0.0).astype(F32)) * LOG2E
    lane = lax.broadcasted_iota(jnp.int32, (w, 2 * w), 1)
    qc = lax.broadcasted_iota(jnp.int32, (w, 2 * w), 0)
    kc = lane & (w - 1)
    col_start = jnp.clip(qc - NA_KW // 2, 0, w - NA_KW)
    col_ok = (kc >= col_start) & (kc < col_start + NA_KW)
    left = lane < w
    neg = jnp.full((w, 2 * w), NEG, F32)

    def toeplitz(dr, shift):
        row = jnp.broadcast_to(g[dr:dr + 1, :], (w, 2 * w))
        return pltpu.roll(row, shift, 1, stride=1, stride_axis=0)

    pair = [jnp.where(col_ok, jnp.where(left, toeplitz(d, w), toeplitz(d + 1, 0)), NEG)
            for d in range(2 * NA_KH_MAX - 2)]
    for ty in range(3):
        for rl in range(NA_ROWS):
            for t in range(3 * NA_ROWS // 2):
                d = 2 * t - rl + NA_KH_MAX - 1 - NA_ROWS
                v0, v1 = bool(valid[ty, rl, 2 * t]), bool(valid[ty, rl, 2 * t + 1])
                if v0 and v1:
                    tile = pair[d]
                elif v0:
                    tile = jnp.where(left, pair[d], NEG)
                elif v1:
                    tile = jnp.where(left, NEG, pair[d])
                else:
                    tile = neg
                o_ref[ty, rl * w:(rl + 1) * w, 2 * t * w:(2 * t + 2) * w] = tile


def _na_bias(rpb_all, rows):
    depth = rpb_all.shape[0]
    rpb_p = jnp.pad(rpb_all.astype(F32), ((0, 0), (0, 0), (0, 1), (0, 1)))
    return pl.pallas_call(
        functools.partial(_nabias_kernel, valid=_na_valid_rows(rows)),
        grid=(depth, NA_HEADS),
        in_specs=[pl.BlockSpec((None, None, 16, 32), lambda l, h: (l, h, 0, 0))],
        out_specs=pl.BlockSpec((None, 3, None, NA_BLOCK, 3 * NA_BLOCK), lambda l, h: (l, 0, h, 0, 0)),
        out_shape=jax.ShapeDtypeStruct((depth, 3, NA_HEADS, NA_BLOCK, 3 * NA_BLOCK), F32),
        compiler_params=_cparams(("parallel", "parallel")),
        name="na_bias",
    )(rpb_p)


def _log_sigmoid(x):
    return jnp.minimum(x, 0.0) - jnp.log(1.0 + jnp.exp(-jnp.abs(x)))


def _tri(n, upper):
    r = lax.broadcasted_iota(jnp.int32, (n, n), 0)
    c = lax.broadcasted_iota(jnp.int32, (n, n), 1)
    return jnp.where((r <= c) if upper else (r >= c), 1.0, 0.0).astype(F32)


def _scan_max(x, axis, reverse, size):
    idx = lax.broadcasted_iota(jnp.int32, x.shape, axis)
    k = 1
    while k < size:
        if reverse:
            shifted = pltpu.roll(x, x.shape[axis] - k, axis)
            ok = idx < size - k
        else:
            shifted = pltpu.roll(x, k, axis)
            ok = idx >= k
        x = jnp.maximum(x, jnp.where(ok, shifted, NEG))
        k *= 2
    return x


def _mlstm_kernel(q_ref, k_ref, v_ref, o_ref, z_ref, gr_ref, gt_ref, fb_ref, ng_ref, y_ref,
                  hdir_ref, state_ref, sprev_ref, er_ref, xt_ref, *, seq):
    L = ML_CHUNK
    nc = seq // L
    head = pl.program_id(1)
    scale = ML_HEAD_DIM ** -0.5
    lane = lax.broadcasted_iota(jnp.int32, (1, 128), 1)
    chunk_ok = lane < nc

    for d in range(2):
        rev = d == 1
        fb = fb_ref[d, head]
        lf_r = _log_sigmoid(gr_ref[8 * d + 4 + head] + fb)
        b_r = _dot_hi(lf_r, _tri(L, upper=not rev))
        er_ref[d] = gr_ref[8 * d + head] - b_r
        i_t = gt_ref[8 * d + head]
        lf_t = _log_sigmoid(gt_ref[8 * d + 4 + head] + fb)
        b_t = _dot_hi(_tri(L, upper=rev), lf_t)
        g = jnp.sum(lf_t, axis=0, keepdims=True)
        a_t = g - b_t + i_t
        m_loc = jnp.max(a_t, axis=0, keepdims=True)
        w_t = jnp.exp(a_t - m_loc)
        jr = lax.broadcasted_iota(jnp.int32, (128, 128), 0)
        jc = lax.broadcasted_iota(jnp.int32, (128, 128), 1)
        before = ((jr > jc) if rev else (jr < jc)) & (jr < nc)
        g8 = jnp.broadcast_to(g, (8, 128))
        g_ex = _dot_hi(g8, jnp.where(before, 1.0, 0.0).astype(F32))[0:1]
        x = jnp.where(chunk_ok, m_loc - (g_ex + g), NEG)
        x8 = jnp.broadcast_to(x, (8, 128))
        if rev:
            x_prev = jnp.where(lane < nc - 1, pltpu.roll(x8, 127, 1), NEG)
        else:
            x_prev = jnp.where(lane >= 1, pltpu.roll(x8, 1, 1), NEG)
        pm = _scan_max(x_prev, 1, rev, nc)[0:1]
        m_prev = g_ex + jnp.maximum(pm, 0.0)
        m_after = jnp.maximum(g + m_prev, m_loc)
        sp = jnp.exp(g + m_prev - m_after)
        sl = jnp.exp(m_loc - m_after)
        cm = _scan_max(i_t - b_t, 0, rev, L)
        mu = jnp.maximum(m_prev, cm)
        xt_ref[d, 0 * L:1 * L] = mu
        xt_ref[d, 1 * L:2 * L] = jnp.exp(m_prev - mu) * scale
        xt_ref[d, 2 * L:3 * L] = jnp.exp(-b_t - mu)
        xt_ref[d, 3 * L:4 * L] = w_t
        xt_ref[d, 4 * L:4 * L + 8] = jnp.concatenate(
            [sp, sl, jnp.zeros((6, 128), F32)], axis=0)
        state_ref[d] = jnp.zeros((ML_HEAD_DIM, ML_AUG), F32)

    rr = lax.broadcasted_iota(jnp.int32, (L, L), 0)
    cc = lax.broadcasted_iota(jnp.int32, (L, L), 1)
    ones_col = jnp.where(lax.broadcasted_iota(jnp.int32, (L, ML_HEAD_DIM), 1) == 0, 1.0, 0.0).astype(BF16)

    def column(d, c, lo, hi):
        return jnp.sum(jnp.where(lane == c, xt_ref[d, lo:hi], 0.0), axis=1, keepdims=True)

    def state_step(j, carry):
        for d in range(2):
            c = j if d == 0 else nc - 1 - j
            r0 = pl.multiple_of(c * L, L)
            kc = k_ref[pl.ds(r0, L), :]
            vaug = jnp.concatenate([v_ref[pl.ds(r0, L), :], ones_col], axis=1)
            w = column(d, c, 3 * L, 4 * L)
            gains = column(d, c, 4 * L, 4 * L + 8)
            wv = (w * vaug.astype(F32)).astype(BF16)
            s_loc = lax.dot_general(kc, wv, (((0,), (0,)), ((), ())), preferred_element_type=F32)
            st = state_ref[d]
            sprev_ref[d, c] = st.astype(BF16)
            state_ref[d] = gains[0:1] * st + gains[1:2] * s_loc
        return carry

    lax.fori_loop(0, nc, state_step, 0, unroll=4)

    def output_step(j, carry):
        for d in range(2):
            c = j if d == 0 else nc - 1 - j
            r0 = pl.multiple_of(c * L, L)
            qc = q_ref[pl.ds(r0, L), :]
            kc = k_ref[pl.ds(r0, L), :]
            vaug = jnp.concatenate([v_ref[pl.ds(r0, L), :], ones_col], axis=1)
            cols = column(d, c, 0, 3 * L)
            mu, iw, fl = (cols[i * L:(i + 1) * L] for i in range(3))
            e_row = er_ref[d, pl.ds(c, 1), :]
            s = lax.dot_general(qc, kc, (((1,), (1,)), ((), ())), preferred_element_type=F32)
            mask = (cc >= rr) if d == 1 else (cc <= rr)
            p = jnp.where(mask, jnp.exp(e_row - mu), 0.0)
            w2 = (s * p * scale).astype(BF16)
            out = (jnp.dot(w2, vaug, preferred_element_type=F32)
                   + iw * jnp.dot(qc, sprev_ref[d, c], preferred_element_type=F32))
            num = out[:, :ML_HEAD_DIM]
            den = out[:, ML_HEAD_DIM:ML_HEAD_DIM + 1]
            hdir_ref[d, pl.ds(r0, L), :] = num / jnp.maximum(jnp.abs(den), fl)
        return carry

    lax.fori_loop(0, nc, output_step, 0, unroll=4)

    tile = min(512, seq)

    def finish(t, carry):
        r0 = pl.multiple_of(t * tile, tile)
        rows = pl.ds(r0, tile)
        hs = hdir_ref[0, rows, :] + hdir_ref[1, rows, :]
        hs = jax.nn.sigmoid(o_ref[rows, :].astype(F32)) * hs
        mu = jnp.mean(hs, axis=-1, keepdims=True)
        hc = hs - mu
        var = jnp.mean(hc * hc, axis=-1, keepdims=True)
        hn = hc * lax.rsqrt(var + LN_EPS) * ng_ref[...]
        y_ref[rows, :] = (hn * _silu(z_ref[rows, :].astype(F32))).astype(BF16)
        return carry

    lax.fori_loop(0, seq // tile, finish, 0)


def _mlstm(u3, gates_r, gates_t, f_bias, norm_g):
    bsz, s, _ = u3.shape
    d = ML_HEAD_DIM
    nc = s // ML_CHUNK
    col = lambda base: pl.BlockSpec((None, s, d), lambda b, h: (b, 0, base // d + h))
    return pl.pallas_call(
        functools.partial(_mlstm_kernel, seq=s),
        grid=(bsz, ML_HEADS),
        in_specs=[col(B_Q), col(B_K), col(B_V), col(B_O), col(B_Z),
                  pl.BlockSpec((None, 16, nc, ML_CHUNK), lambda b, h: (b, 0, 0, 0)),
                  pl.BlockSpec((None, 16, ML_CHUNK, 128), lambda b, h: (b, 0, 0, 0)),
                  pl.BlockSpec(memory_space=pltpu.SMEM),
                  pl.BlockSpec((1, d), lambda b, h: (0, h))],
        out_specs=pl.BlockSpec((None, s, d), lambda b, h: (b, 0, h)),
        out_shape=jax.ShapeDtypeStruct((bsz, s, ML_WIDTH), BF16),
        scratch_shapes=[pltpu.VMEM((2, s, d), F32),
                        pltpu.VMEM((2, d, ML_AUG), F32),
                        pltpu.VMEM((2, nc, d, ML_AUG), BF16),
                        pltpu.VMEM((2, nc, ML_CHUNK), F32),
                        pltpu.VMEM((2, 4 * ML_CHUNK + 8, 128), F32)],
        compiler_params=_cparams(("parallel", "parallel")),
        name="mlstm",
    )(u3, u3, u3, u3, u3, gates_r, gates_t, f_bias, norm_g.reshape(1, ML_WIDTH))


def _pair_heads(t, axis):
    shape = t.shape
    t = t.reshape(shape[:axis] + (2, 2, WA_GROUP, HEAD_DIM) + shape[axis + 1:])
    return jnp.swapaxes(t, axis + 1, axis + 2).reshape(shape)


def _reorder_in_proj(w, b):
    def cols(t):
        ax = t.ndim - 1
        return jnp.concatenate(
            [_pair_heads(t[..., 0:768], ax), _pair_heads(t[..., 1280:2048], ax), t[..., 768:1280],
             t[..., 2048:4608], t[..., 4624:7696]], axis=ax)
    g0, g1 = _SRC_GATES
    wg = jnp.pad(w[:, g0:g1], ((0, 0), (0, GATE_PAD - (g1 - g0)))).astype(BF16)
    bg = jnp.pad(b[g0:g1], (0, GATE_PAD - (g1 - g0)))
    return cols(w).astype(BF16), cols(b), wg, bg


def _reorder_out_proj(w):
    return jnp.concatenate([_pair_heads(w[:WA_WIDTH], 0), w[WA_WIDTH:]], axis=0).astype(BF16)


def kernel(x, emb_ln_g, emb_ln_b, w_in, b_in, w_out, b_out, ln_g, ln_b, t5_bias, sink, ml_f_bias,
           ml_norm_g, na_rpb):
    bsz, s, d = x.shape
    depth = w_in.shape[0]
    alpha = (2 * depth) ** 0.25
    m = bsz * s
    nc = s // ML_CHUNK
    assert d == D_MODEL and s % NA_BLOCK == 0 and s // GRID_W >= NA_KH_MAX and nc <= 128

    hf, hb = _input_norm(x.reshape(m, d), emb_ln_g, emb_ln_b)
    bias_a = _window_bias(t5_bias)
    bias_c = _na_bias(na_rpb, s // GRID_W)
    for l in range(depth):
        wm, bm, wg, bg = _reorder_in_proj(w_in[l], b_in[l])
        u3 = _projection(hb, wm, bm, BF16, 1536, "in_projection").reshape(bsz, s, U_WIDTH)
        gates = _projection(hb, wg, bg, F32, GATE_PAD, "gate_projection")[:, :16]
        gates = gates.reshape(bsz, s, 16).transpose(0, 2, 1).reshape(bsz, 16, nc, ML_CHUNK)
        gates_t = jnp.pad(gates.transpose(0, 1, 3, 2), ((0, 0), (0, 0), (0, 0), (0, 128 - nc)))
        ya = _window_attention(u3, bias_a, sink[l])
        yb = _mlstm(u3, gates, gates_t, ml_f_bias[l], ml_norm_g[l])
        yc = _neighbourhood_attention(u3, bias_c, l)
        last = l == depth - 1
        res = _out_projection(ya.reshape(m, WA_WIDTH), yb.reshape(m, ML_WIDTH), yc.reshape(m, NA_WIDTH),
                              _reorder_out_proj(w_out[l]), b_out[l], hf, ln_g[l], ln_b[l], alpha, not last)
        hf = res[0]
        hb = None if last else res[1]
    return hf.reshape(bsz, s, d)
```

```python
import functools
import math

import numpy as np
import jax
import jax.numpy as jnp
from jax import lax
from jax.experimental import pallas as pl
from jax.experimental.pallas import tpu as pltpu

F32 = jnp.float32
BF16 = jnp.bfloat16

D_MODEL = 2048
HEAD_DIM = 64
LN_EPS = 1e-5
NEG = -1e30
LOG2E = math.log2(math.e)
SM_CHUNK = 64

WA_HEADS = 12
WA_KV_HEADS = 4
WA_GROUP = WA_HEADS // WA_KV_HEADS
WA_WIDTH = WA_HEADS * HEAD_DIM
WA_KV_WIDTH = WA_KV_HEADS * HEAD_DIM
WA_BLOCK = 128
WINDOW = 128
T5_BUCKETS = 32
T5_MAX_DIST = 128
ML_HEADS = 4
ML_HEAD_DIM = 128
ML_WIDTH = ML_HEADS * ML_HEAD_DIM
ML_CHUNK = 128
ML_AUG = 2 * ML_HEAD_DIM
NA_HEADS = 12
NA_WIDTH = NA_HEADS * HEAD_DIM
GRID_W = 64
NA_KH_MAX = 8
NA_KW = 16
NA_ROWS = 4
NA_BLOCK = NA_ROWS * GRID_W

MIX_WIDTH = WA_WIDTH + ML_WIDTH + NA_WIDTH
GATE_PAD = 128

A_Q, A_Z, A_K, A_V = 0, 768, 1536, 1792
B_Q, B_K, B_V, B_O, B_Z = 2048, 2560, 3072, 3584, 4096
C_Q, C_K, C_V, C_Z = 4608, 5376, 6144, 6912
U_WIDTH = 7680
_SRC_GATES = (4608, 4624)

VMEM_LIMIT = 56 * 1024 * 1024
LN_TM = 512
IN_TM, IN_TN = 1024, 1536
OUT_TM = 256


def _cparams(sem, flags=None):
    return pltpu.CompilerParams(dimension_semantics=sem, vmem_limit_bytes=VMEM_LIMIT, flags=flags)


def _dot_hi(a, b):
    return jnp.dot(a, b, preferred_element_type=F32, precision=lax.Precision.HIGHEST)


def _layer_norm_rows(x, g, b):
    mu = jnp.mean(x, axis=-1, keepdims=True)
    xc = x - mu
    var = jnp.mean(xc * xc, axis=-1, keepdims=True)
    return xc * lax.rsqrt(var + LN_EPS) * g + b


def _ln_kernel(x_ref, g_ref, b_ref, ob_ref):
    ob_ref[...] = _layer_norm_rows(x_ref[...].astype(F32), g_ref[...], b_ref[...]).astype(BF16)


def _input_norm(x2, g, b):
    m, d = x2.shape
    tm = min(LN_TM, m)
    return pl.pallas_call(
        _ln_kernel,
        grid=(m // tm,),
        in_specs=[pl.BlockSpec((tm, d), lambda i: (i, 0)),
                  pl.BlockSpec((1, d), lambda i: (0, 0)),
                  pl.BlockSpec((1, d), lambda i: (0, 0))],
        out_specs=pl.BlockSpec((tm, d), lambda i: (i, 0)),
        out_shape=jax.ShapeDtypeStruct((m, d), BF16),
        compiler_params=_cparams(("parallel",)),
        name="input_norm",
    )(x2, g.reshape(1, d), b.reshape(1, d))


def _inproj_kernel(h_ref, w_ref, b_ref, wg_ref, bg_ref, u_ref, gate_ref):
    acc = jnp.dot(h_ref[...], w_ref[...], preferred_element_type=F32)
    u_ref[...] = (acc + b_ref[...]).astype(u_ref.dtype)

    @pl.when(pl.program_id(1) == 0)
    def _():
        gate_ref[...] = jnp.dot(h_ref[...], wg_ref[...], preferred_element_type=F32) + bg_ref[...]


def _in_projection(hb, w, b, wg, bg):
    m, d = hb.shape
    n = w.shape[1]
    tm, tn = min(IN_TM, m), IN_TN
    return pl.pallas_call(
        _inproj_kernel,
        grid=(m // tm, n // tn),
        in_specs=[pl.BlockSpec((tm, d), lambda i, j: (i, 0)),
                  pl.BlockSpec((d, tn), lambda i, j: (0, j)),
                  pl.BlockSpec((1, tn), lambda i, j: (0, j)),
                  pl.BlockSpec((d, GATE_PAD), lambda i, j: (0, 0)),
                  pl.BlockSpec((1, GATE_PAD), lambda i, j: (0, 0))],
        out_specs=[pl.BlockSpec((tm, tn), lambda i, j: (i, j)),
                   pl.BlockSpec((tm, GATE_PAD), lambda i, j: (i, 0))],
        out_shape=[jax.ShapeDtypeStruct((m, n), BF16), jax.ShapeDtypeStruct((m, GATE_PAD), F32)],
        compiler_params=_cparams(("parallel", "arbitrary")),
        name="in_projection",
    )(hb, w, b.reshape(1, n), wg, bg.reshape(1, GATE_PAD))


def _outproj_kernel(ya_ref, yb_ref, yc_ref, w_ref, b_ref, res_ref, rg_ref, rb_ref, g_ref, beta_ref,
                    *refs, alpha, norm_residual):
    out_refs, (acc0_ref, acc1_ref) = refs[:-2], refs[-2:]
    t = pl.program_id(0)

    def step(acc_read, acc_write):
        y = jnp.concatenate([ya_ref[...], yb_ref[...], yc_ref[...]], axis=1)
        acc_write[...] = jnp.dot(y, w_ref[...], preferred_element_type=F32)
        res = res_ref[...]
        if norm_residual:
            res = _layer_norm_rows(res, rg_ref[...], rb_ref[...])
        r = _layer_norm_rows(alpha * res + (acc_read[...] + b_ref[...]), g_ref[...], beta_ref[...])
        out_refs[0][...] = r
        if len(out_refs) > 1:
            out_refs[1][...] = r.astype(BF16)

    @pl.when(t == 0)
    def _():
        acc1_ref[...] = jnp.zeros(acc1_ref.shape, F32)

    @pl.when(t % 2 == 0)
    def _():
        step(acc1_ref, acc0_ref)

    @pl.when(t % 2 == 1)
    def _():
        step(acc0_ref, acc1_ref)


def _out_projection(ya, yb, yc, w, b, res, res_g, res_b, g, beta, alpha, norm_residual, want_bf16):
    m, d = res.shape
    tm = min(OUT_TM, m)
    nt = m // tm
    cur = lambda t: (jnp.minimum(t, nt - 1), 0)
    prev = lambda t: (jnp.maximum(t - 1, 0), 0)
    const = lambda t: (0, 0)
    out_specs = [pl.BlockSpec((tm, d), prev)]
    out_shape = [jax.ShapeDtypeStruct((m, d), F32)]
    if want_bf16:
        out_specs.append(pl.BlockSpec((tm, d), prev))
        out_shape.append(jax.ShapeDtypeStruct((m, d), BF16))
    vec = lambda v: v.reshape(1, d)
    return pl.pallas_call(
        functools.partial(_outproj_kernel, alpha=alpha, norm_residual=norm_residual),
        grid=(nt + 1,),
        in_specs=[pl.BlockSpec((tm, WA_WIDTH), cur),
                  pl.BlockSpec((tm, ML_WIDTH), cur),
                  pl.BlockSpec((tm, NA_WIDTH), cur),
                  pl.BlockSpec((MIX_WIDTH, d), const),
                  pl.BlockSpec((1, d), const),
                  pl.BlockSpec((tm, d), prev),
                  pl.BlockSpec((1, d), const),
                  pl.BlockSpec((1, d), const),
                  pl.BlockSpec((1, d), const),
                  pl.BlockSpec((1, d), const)],
        out_specs=out_specs,
        out_shape=out_shape,
        scratch_shapes=[pltpu.VMEM((tm, d), F32), pltpu.VMEM((tm, d), F32)],
        compiler_params=_cparams(("arbitrary",)),
        name="out_projection",
    )(ya, yb, yc, w, vec(b), res, vec(res_g), vec(res_b), vec(g), vec(beta))


def _silu(z):
    return z * jax.nn.sigmoid(z)


def _wattn_kernel(q_ref, z_ref, kp_ref, kc_ref, kn_ref, vp_ref, vc_ref, vn_ref, bias_ref, sink_ref,
                  o_ref, s0_ref, s1_ref, p_ref):
    n = pl.program_id(1)
    blk = WA_BLOCK
    npair = WA_KV_HEADS // 2
    rows = 2 * WA_GROUP * blk
    lane = lax.broadcasted_iota(jnp.int32, (blk, 2 * HEAD_DIM), 1)
    left = lane < HEAD_DIM

    r2 = lax.broadcasted_iota(jnp.int32, (blk, 4 * HEAD_DIM), 0)
    c2 = lax.broadcasted_iota(jnp.int32, (blk, 4 * HEAD_DIM), 1)
    sink_rows = jnp.where((r2 == 0) & (c2 == 2 * HEAD_DIM), 1.0, 0.0).astype(BF16)
    ones_col = jnp.where(lax.broadcasted_iota(jnp.int32, (3 * blk, 2 * HEAD_DIM), 1) == 0, 1.0, 0.0).astype(BF16)
    lane0 = lax.broadcasted_iota(jnp.int32, (SM_CHUNK, 2 * HEAD_DIM), 1) == 0

    def step(s_read, s_write):
        for i in range(npair):
            ks = slice(i * 2 * HEAD_DIM, (i + 1) * 2 * HEAD_DIM)
            k3 = jnp.concatenate([kp_ref[:, ks], kc_ref[:, ks], kn_ref[:, ks]], axis=0)
            tiles = [q_ref[:, (WA_GROUP * i + j) * 2 * HEAD_DIM:(WA_GROUP * i + j + 1) * 2 * HEAD_DIM]
                     * (HEAD_DIM ** -0.5 * LOG2E) for j in range(WA_GROUP)]
            zero = jnp.zeros_like(tiles[0])
            lhs = jnp.concatenate([jnp.where(left, t, zero) for t in tiles]
                                  + [jnp.where(left, zero, t) for t in tiles], axis=0)
            s_write[i] = lax.dot_general(lhs, k3, (((1,), (1,)), ((), ())), preferred_element_type=F32)
            v3 = jnp.concatenate([vp_ref[:, ks], vc_ref[:, ks], vn_ref[:, ks]], axis=0)
            vaug = jnp.concatenate([jnp.concatenate([v3, ones_col], axis=1), sink_rows], axis=0)
            for c in range(rows // SM_CHUNK):
                rs = slice(c * SM_CHUNK, (c + 1) * SM_CHUNK)
                sk = sink_ref[0, 2 * WA_GROUP * i + c * SM_CHUNK // blk] * LOG2E
                sc = s_read[i, rs, :] + bias_ref[i, rs, :]
                m = jnp.maximum(jnp.max(sc, axis=-1, keepdims=True), sk)
                p_ref[i, rs, :3 * blk] = jnp.exp2(sc - m).astype(BF16)
                p_ref[i, rs, 3 * blk:] = jnp.where(lane0, jnp.exp2(sk - m), 0.0).astype(BF16)
            out = jnp.dot(p_ref[i], vaug, preferred_element_type=F32)
            o = out[:, :2 * HEAD_DIM] / out[:, 2 * HEAD_DIM:2 * HEAD_DIM + 1]
            for j in range(WA_GROUP):
                t = WA_GROUP * i + j
                ot = jnp.where(left, o[j * blk:(j + 1) * blk], o[(WA_GROUP + j) * blk:(WA_GROUP + j + 1) * blk])
                cs = slice(t * 2 * HEAD_DIM, (t + 1) * 2 * HEAD_DIM)
                o_ref[:, cs] = (ot * _silu(z_ref[:, cs].astype(F32))).astype(BF16)

    @pl.when(n == 0)
    def _():
        s1_ref[...] = jnp.zeros(s1_ref.shape, F32)

    @pl.when(n % 2 == 0)
    def _():
        step(s1_ref, s0_ref)

    @pl.when(n % 2 == 1)
    def _():
        step(s0_ref, s1_ref)


def _window_attention(u3, bias, sink):
    bsz, s, _ = u3.shape
    nb = s // WA_BLOCK
    qw, kw = WA_WIDTH, WA_KV_WIDTH
    npair = WA_KV_HEADS // 2
    rows = 2 * WA_GROUP * WA_BLOCK
    clamp = lambda i: jnp.clip(i, 0, nb - 1)
    kspec = lambda off, c: pl.BlockSpec((None, WA_BLOCK, kw), lambda b, n: (b, clamp(n + off), c))
    btype = lambda b, n: (jnp.where(n <= 1, 0, jnp.where(n == nb, 2, 1)), 0, 0, 0)
    return pl.pallas_call(
        _wattn_kernel,
        grid=(bsz, nb + 1),
        in_specs=[pl.BlockSpec((None, WA_BLOCK, qw), lambda b, n: (b, clamp(n), A_Q // qw)),
                  pl.BlockSpec((None, WA_BLOCK, qw), lambda b, n: (b, clamp(n - 1), A_Z // qw)),
                  kspec(-1, A_K // kw), kspec(0, A_K // kw), kspec(1, A_K // kw),
                  kspec(-2, A_V // kw), kspec(-1, A_V // kw), kspec(0, A_V // kw),
                  pl.BlockSpec((None, npair, rows, 3 * WA_BLOCK), btype),
                  pl.BlockSpec(memory_space=pltpu.SMEM)],
        out_specs=pl.BlockSpec((None, WA_BLOCK, qw), lambda b, n: (b, clamp(n - 1), 0)),
        out_shape=jax.ShapeDtypeStruct((bsz, s, qw), BF16),
        scratch_shapes=[pltpu.VMEM((npair, rows, 3 * WA_BLOCK), F32),
                        pltpu.VMEM((npair, rows, 3 * WA_BLOCK), F32),
                        pltpu.VMEM((npair, rows, 4 * WA_BLOCK), BF16)],
        compiler_params=_cparams(("parallel", "arbitrary")),
        name="window_attention",
    )(u3, u3, u3, u3, u3, u3, u3, u3, bias.reshape(3, npair, rows, 3 * WA_BLOCK), sink.reshape(1, WA_HEADS))


def _t5_bucket_np(rel):
    half = T5_BUCKETS // 2
    max_exact = half // 2
    ret = np.where(rel > 0, half, 0)
    n = np.abs(rel)
    nf = np.maximum(n, 1).astype(np.float64)
    v = np.log(nf / max_exact) / math.log(T5_MAX_DIST / max_exact) * (half - max_exact)
    vr = np.round(v)
    v = np.where(np.abs(v - vr) < 1e-9, vr, v)
    large = np.minimum(max_exact + np.trunc(v).astype(np.int64), half - 1)
    return ret + np.where(n < max_exact, n, large)


def _wbias_kernel(t5t_ref, bucket_ref, o_ref):
    width = 4 * WA_BLOCK
    bk = bucket_ref[...]
    e = lax.broadcasted_iota(jnp.int32, (T5_BUCKETS, width), 0)
    onehot = jnp.where(e == bk, 1.0, 0.0).astype(F32)
    g = _dot_hi(t5t_ref[...], onehot) * LOG2E + jnp.where(bk < 0, NEG, 0.0)
    col = lax.broadcasted_iota(jnp.int32, (WA_BLOCK, 3 * WA_BLOCK), 1)
    for h in range(WA_HEADS):
        row = jnp.broadcast_to(g[h:h + 1, :], (WA_BLOCK, width))
        t = pltpu.roll(row, 3 * WA_BLOCK, 1, stride=1, stride_axis=0)[:, :3 * WA_BLOCK]
        o_ref[0, h] = jnp.where(col < WA_BLOCK, NEG, t)
        o_ref[1, h] = t
        o_ref[2, h] = jnp.where(col >= 2 * WA_BLOCK, NEG, t)


def _window_bias(t5_table):
    rel = np.arange(4 * WA_BLOCK) - 2 * WA_BLOCK
    bucket = np.where(np.abs(rel) <= WINDOW, _t5_bucket_np(rel), -1).astype(np.int32)
    t5t = jnp.pad(t5_table.astype(F32).T, ((0, 16 - WA_HEADS), (0, 0)))
    return pl.pallas_call(
        _wbias_kernel,
        out_shape=jax.ShapeDtypeStruct((3, WA_HEADS, WA_BLOCK, 3 * WA_BLOCK), F32),
        name="window_bias",
    )(t5t, jnp.asarray(bucket).reshape(1, -1))


def _natten_kernel(q_ref, z_ref, kp_ref, kc_ref, kn_ref, vp_ref, vc_ref, vn_ref, bias_ref, o_ref,
                   s0_ref, s1_ref, p_ref):
    n = pl.program_id(1)
    blk = NA_BLOCK
    lane = lax.broadcasted_iota(jnp.int32, (blk, 2 * HEAD_DIM), 1)
    left = lane < HEAD_DIM

    ones_col = jnp.where(lax.broadcasted_iota(jnp.int32, (3 * blk, 2 * HEAD_DIM), 1) == 0, 1.0, 0.0).astype(BF16)

    def step(s_read, s_write):
        for i in range(NA_HEADS // 2):
            cs = slice(i * 2 * HEAD_DIM, (i + 1) * 2 * HEAD_DIM)
            k3 = jnp.concatenate([kp_ref[:, cs], kc_ref[:, cs], kn_ref[:, cs]], axis=0)
            t = q_ref[:, cs] * (HEAD_DIM ** -0.5 * LOG2E)
            zero = jnp.zeros_like(t)
            lhs = jnp.concatenate([jnp.where(left, t, zero), jnp.where(left, zero, t)], axis=0)
            s_write[i] = lax.dot_general(lhs, k3, (((1,), (1,)), ((), ())), preferred_element_type=F32)
            pb = p_ref.at[i % 3]
            for c in range(2 * blk // SM_CHUNK):
                rs = slice(c * SM_CHUNK, (c + 1) * SM_CHUNK)
                sc = s_read[i, rs, :] + bias_ref[i, rs, :]
                m = jnp.max(sc, axis=-1, keepdims=True)
                pb[rs, :] = jnp.exp2(sc - m).astype(BF16)
            if i > 0:
                weighted_values(i - 1)
        weighted_values(NA_HEADS // 2 - 1)

    def weighted_values(i):
        cs = slice(i * 2 * HEAD_DIM, (i + 1) * 2 * HEAD_DIM)
        v3 = jnp.concatenate([vp_ref[:, cs], vc_ref[:, cs], vn_ref[:, cs]], axis=0)
        vaug = jnp.concatenate([v3, ones_col], axis=1)
        out = jnp.dot(p_ref[i % 3], vaug, preferred_element_type=F32)
        o = out[:, :2 * HEAD_DIM] / out[:, 2 * HEAD_DIM:2 * HEAD_DIM + 1]
        ot = jnp.where(left, o[:blk], o[blk:])
        o_ref[:, cs] = (ot * _silu(z_ref[:, cs].astype(F32))).astype(BF16)

    @pl.when(n == 0)
    def _():
        s1_ref[...] = jnp.zeros(s1_ref.shape, F32)

    @pl.when(n % 2 == 0)
    def _():
        step(s1_ref, s0_ref)

    @pl.when(n % 2 == 1)
    def _():
        step(s0_ref, s1_ref)


def _neighbourhood_attention(u3, bias, layer):
    bsz, s, _ = u3.shape
    nblk = s // NA_BLOCK
    w = NA_WIDTH
    npair = NA_HEADS // 2
    clamp = lambda i: jnp.clip(i, 0, nblk - 1)
    spec = lambda off, c: pl.BlockSpec((None, NA_BLOCK, w), lambda b, n: (b, clamp(n + off), c))
    btype = lambda b, n: (layer, jnp.where(n <= 1, 0, jnp.where(n == nblk, 2, 1)), 0, 0, 0)
    sshape = (npair, 2 * NA_BLOCK, 3 * NA_BLOCK)
    return pl.pallas_call(
        _natten_kernel,
        grid=(bsz, nblk + 1),
        in_specs=[spec(0, C_Q // w), spec(-1, C_Z // w),
                  spec(-1, C_K // w), spec(0, C_K // w), spec(1, C_K // w),
                  spec(-2, C_V // w), spec(-1, C_V // w), spec(0, C_V // w),
                  pl.BlockSpec((None, None) + sshape, btype)],
        out_specs=pl.BlockSpec((None, NA_BLOCK, w), lambda b, n: (b, clamp(n - 1), 0)),
        out_shape=jax.ShapeDtypeStruct((bsz, s, w), BF16),
        scratch_shapes=[pltpu.VMEM(sshape, F32), pltpu.VMEM(sshape, F32),
                        pltpu.VMEM((3,) + sshape[1:], BF16)],
        compiler_params=_cparams(("parallel", "arbitrary")),
        name="neighbourhood_attention",
    )(u3, u3, u3, u3, u3, u3, u3, u3, bias.reshape(bias.shape[:2] + sshape))


def _na_valid_rows(rows):
    kh = min(NA_KH_MAX, rows)
    nblk = rows // NA_ROWS
    out = []
    for j in (0, min(1, nblk - 1), nblk - 1):
        r = NA_ROWS * j + np.arange(NA_ROWS)[:, None]
        kr = NA_ROWS * (j - 1) + np.arange(3 * NA_ROWS)[None, :]
        rs = np.clip(r - kh // 2, 0, rows - kh)
        out.append((kr >= rs) & (kr < rs + kh))
    return np.stack(out)


def _nabias_kernel(rpb_ref, o_ref, *, valid):
    w = GRID_W
    j = lax.broadcasted_iota(jnp.int32, (32, 2 * w), 1)
    e = lax.broadcasted_iota(jnp.int32, (32, 2 * w), 0)
    dc = jnp.clip(j - w, -(NA_KW - 1), NA_KW - 1) + NA_KW - 1
    g = _dot_hi(rpb_ref[...], jnp.where(e == dc, 1.0, 0.0).astype(F32)) * LOG2E
    lane = lax.broadcasted_iota(jnp.int32, (w, 2 * w), 1)
    qc = lax.broadcasted_iota(jnp.int32, (w, 2 * w), 0)
    kc = lane & (w - 1)
    col_start = jnp.clip(qc - NA_KW // 2, 0, w - NA_KW)
    col_ok = (kc >= col_start) & (kc < col_start + NA_KW)
    left = lane < w
    neg = jnp.full((w, 2 * w), NEG, F32)

    def toeplitz(dr, shift):
        row = jnp.broadcast_to(g[dr:dr + 1, :], (w, 2 * w))
        return pltpu.roll(row, shift, 1, stride=1, stride_axis=0)

    pair = [jnp.where(col_ok, jnp.where(left, toeplitz(d, w), toeplitz(d + 1, 0)), NEG)
            for d in range(2 * NA_KH_MAX - 2)]
    for ty in range(3):
        for rl in range(NA_ROWS):
            for t in range(3 * NA_ROWS // 2):
                d = 2 * t - rl + NA_KH_MAX - 1 - NA_ROWS
                v0, v1 = bool(valid[ty, rl, 2 * t]), bool(valid[ty, rl, 2 * t + 1])
                if v0 and v1:
                    tile = pair[d]
                elif v0:
                    tile = jnp.where(left, pair[d], NEG)
                elif v1:
                    tile = jnp.where(left, NEG, pair[d])
                else:
                    tile = neg
                o_ref[ty, rl * w:(rl + 1) * w, 2 * t * w:(2 * t + 2) * w] = tile


def _na_bias(rpb_all, rows):
    depth = rpb_all.shape[0]
    rpb_p = jnp.pad(rpb_all.astype(F32), ((0, 0), (0, 0), (0, 1), (0, 1)))
    return pl.pallas_call(
        functools.partial(_nabias_kernel, valid=_na_valid_rows(rows)),
        grid=(depth, NA_HEADS),
        in_specs=[pl.BlockSpec((None, None, 16, 32), lambda l, h: (l, h, 0, 0))],
        out_specs=pl.BlockSpec((None, 3, None, NA_BLOCK, 3 * NA_BLOCK), lambda l, h: (l, 0, h, 0, 0)),
        out_shape=jax.ShapeDtypeStruct((depth, 3, NA_HEADS, NA_BLOCK, 3 * NA_BLOCK), F32),
        compiler_params=_cparams(("parallel", "parallel")),
        name="na_bias",
    )(rpb_p)


def _log_sigmoid(x):
    return jnp.minimum(x, 0.0) - jnp.log(1.0 + jnp.exp(-jnp.abs(x)))


def _tri(n, upper):
    r = lax.broadcasted_iota(jnp.int32, (n, n), 0)
    c = lax.broadcasted_iota(jnp.int32, (n, n), 1)
    return jnp.where((r <= c) if upper else (r >= c), 1.0, 0.0).astype(F32)


def _scan_max(x, axis, reverse, size):
    idx = lax.broadcasted_iota(jnp.int32, x.shape, axis)
    k = 1
    while k < size:
        if reverse:
            shifted = pltpu.roll(x, x.shape[axis] - k, axis)
            ok = idx < size - k
        else:
            shifted = pltpu.roll(x, k, axis)
            ok = idx >= k
        x = jnp.maximum(x, jnp.where(ok, shifted, NEG))
        k *= 2
    return x


def _mlstm_kernel(q_ref, k_ref, v_ref, o_ref, z_ref, gr_ref, gt_ref, fb_ref, ng_ref, y_ref,
                  hdir_ref, state_ref, sprev_ref, er_ref, xt_ref, *, seq):
    L = ML_CHUNK
    nc = seq // L
    head = pl.program_id(1)
    scale = ML_HEAD_DIM ** -0.5
    lane = lax.broadcasted_iota(jnp.int32, (1, 128), 1)
    chunk_ok = lane < nc

    for d in range(2):
        rev = d == 1
        fb = fb_ref[d, head]
        lf_r = _log_sigmoid(gr_ref[8 * d + 4 + head] + fb)
        b_r = _dot_hi(lf_r, _tri(L, upper=not rev))
        er_ref[d] = gr_ref[8 * d + head] - b_r
        i_t = gt_ref[8 * d + head]
        lf_t = _log_sigmoid(gt_ref[8 * d + 4 + head] + fb)
        b_t = _dot_hi(_tri(L, upper=rev), lf_t)
        g = jnp.sum(lf_t, axis=0, keepdims=True)
        a_t = g - b_t + i_t
        m_loc = jnp.max(a_t, axis=0, keepdims=True)
        w_t = jnp.exp(a_t - m_loc)
        jr = lax.broadcasted_iota(jnp.int32, (128, 128), 0)
        jc = lax.broadcasted_iota(jnp.int32, (128, 128), 1)
        before = ((jr > jc) if rev else (jr < jc)) & (jr < nc)
        g8 = jnp.broadcast_to(g, (8, 128))
        g_ex = _dot_hi(g8, jnp.where(before, 1.0, 0.0).astype(F32))[0:1]
        x = jnp.where(chunk_ok, m_loc - (g_ex + g), NEG)
        x8 = jnp.broadcast_to(x, (8, 128))
        if rev:
            x_prev = jnp.where(lane < nc - 1, pltpu.roll(x8, 127, 1), NEG)
        else:
            x_prev = jnp.where(lane >= 1, pltpu.roll(x8, 1, 1), NEG)
        pm = _scan_max(x_prev, 1, rev, nc)[0:1]
        m_prev = g_ex + jnp.maximum(pm, 0.0)
        m_after = jnp.maximum(g + m_prev, m_loc)
        sp = jnp.exp(g + m_prev - m_after)
        sl = jnp.exp(m_loc - m_after)
        cm = _scan_max(i_t - b_t, 0, rev, L)
        mu = jnp.maximum(m_prev, cm)
        xt_ref[d, 0 * L:1 * L] = mu
        xt_ref[d, 1 * L:2 * L] = jnp.exp(m_prev - mu) * scale
        xt_ref[d, 2 * L:3 * L] = jnp.exp(-b_t - mu)
        xt_ref[d, 3 * L:4 * L] = w_t
        xt_ref[d, 4 * L:4 * L + 8] = jnp.concatenate(
            [sp, sl, jnp.zeros((6, 128), F32)], axis=0)
        state_ref[d] = jnp.zeros((ML_HEAD_DIM, ML_AUG), F32)

    rr = lax.broadcasted_iota(jnp.int32, (L, L), 0)
    cc = lax.broadcasted_iota(jnp.int32, (L, L), 1)
    ones_col = jnp.where(lax.broadcasted_iota(jnp.int32, (L, ML_HEAD_DIM), 1) == 0, 1.0, 0.0).astype(BF16)

    def column(d, c, lo, hi):
        return jnp.sum(jnp.where(lane == c, xt_ref[d, lo:hi], 0.0), axis=1, keepdims=True)

    def state_step(j, carry):
        for d in range(2):
            c = j if d == 0 else nc - 1 - j
            r0 = pl.multiple_of(c * L, L)
            kc = k_ref[pl.ds(r0, L), :]
            vaug = jnp.concatenate([v_ref[pl.ds(r0, L), :], ones_col], axis=1)
            w = column(d, c, 3 * L, 4 * L)
            gains = column(d, c, 4 * L, 4 * L + 8)
            wv = (w * vaug.astype(F32)).astype(BF16)
            s_loc = lax.dot_general(kc, wv, (((0,), (0,)), ((), ())), preferred_element_type=F32)
            st = state_ref[d]
            sprev_ref[d, c] = st.astype(BF16)
            state_ref[d] = gains[0:1] * st + gains[1:2] * s_loc
        return carry

    lax.fori_loop(0, nc, state_step, 0, unroll=4)

    def output_step(j, carry):
        for d in range(2):
            c = j if d == 0 else nc - 1 - j
            r0 = pl.multiple_of(c * L, L)
            qc = q_ref[pl.ds(r0, L), :]
            kc = k_ref[pl.ds(r0, L), :]
            vaug = jnp.concatenate([v_ref[pl.ds(r0, L), :], ones_col], axis=1)
            cols = column(d, c, 0, 3 * L)
            mu, iw, fl = (cols[i * L:(i + 1) * L] for i in range(3))
            e_row = er_ref[d, pl.ds(c, 1), :]
            s = lax.dot_general(qc, kc, (((1,), (1,)), ((), ())), preferred_element_type=F32)
            mask = (cc >= rr) if d == 1 else (cc <= rr)
            p = jnp.where(mask, jnp.exp(e_row - mu), 0.0)
            w2 = (s * p * scale).astype(BF16)
            out = (jnp.dot(w2, vaug, preferred_element_type=F32)
                   + iw * jnp.dot(qc, sprev_ref[d, c], preferred_element_type=F32))
            num = out[:, :ML_HEAD_DIM]
            den = out[:, ML_HEAD_DIM:ML_HEAD_DIM + 1]
            hdir_ref[d, pl.ds(r0, L), :] = num / jnp.maximum(jnp.abs(den), fl)
        return carry

    lax.fori_loop(0, nc, output_step, 0, unroll=4)

    tile = min(512, seq)

    def finish(t, carry):
        r0 = pl.multiple_of(t * tile, tile)
        rows = pl.ds(r0, tile)
        hs = hdir_ref[0, rows, :] + hdir_ref[1, rows, :]
        hs = jax.nn.sigmoid(o_ref[rows, :].astype(F32)) * hs
        mu = jnp.mean(hs, axis=-1, keepdims=True)
        hc = hs - mu
        var = jnp.mean(hc * hc, axis=-1, keepdims=True)
        hn = hc * lax.rsqrt(var + LN_EPS) * ng_ref[...]
        y_ref[rows, :] = (hn * _silu(z_ref[rows, :].astype(F32))).astype(BF16)
        return carry

    lax.fori_loop(0, seq // tile, finish, 0)


def _mlstm(u3, gates_r, gates_t, f_bias, norm_g):
    bsz, s, _ = u3.shape
    d = ML_HEAD_DIM
    nc = s // ML_CHUNK
    col = lambda base: pl.BlockSpec((None, s, d), lambda b, h: (b, 0, base // d + h))
    return pl.pallas_call(
        functools.partial(_mlstm_kernel, seq=s),
        grid=(bsz, ML_HEADS),
        in_specs=[col(B_Q), col(B_K), col(B_V), col(B_O), col(B_Z),
                  pl.BlockSpec((None, 16, nc, ML_CHUNK), lambda b, h: (b, 0, 0, 0)),
                  pl.BlockSpec((None, 16, ML_CHUNK, 128), lambda b, h: (b, 0, 0, 0)),
                  pl.BlockSpec(memory_space=pltpu.SMEM),
                  pl.BlockSpec((1, d), lambda b, h: (0, h))],
        out_specs=pl.BlockSpec((None, s, d), lambda b, h: (b, 0, h)),
        out_shape=jax.ShapeDtypeStruct((bsz, s, ML_WIDTH), BF16),
        scratch_shapes=[pltpu.VMEM((2, s, d), F32),
                        pltpu.VMEM((2, d, ML_AUG), F32),
                        pltpu.VMEM((2, nc, d, ML_AUG), BF16),
                        pltpu.VMEM((2, nc, ML_CHUNK), F32),
                        pltpu.VMEM((2, 4 * ML_CHUNK + 8, 128), F32)],
        compiler_params=_cparams(("parallel", "parallel")),
        name="mlstm",
    )(u3, u3, u3, u3, u3, gates_r, gates_t, f_bias, norm_g.reshape(1, ML_WIDTH))


def _pair_heads(t, axis):
    shape = t.shape
    t = t.reshape(shape[:axis] + (2, 2, WA_GROUP, HEAD_DIM) + shape[axis + 1:])
    return jnp.swapaxes(t, axis + 1, axis + 2).reshape(shape)


def _reorder_in_proj(w, b):
    def cols(t):
        ax = t.ndim - 1
        return jnp.concatenate(
            [_pair_heads(t[..., 0:768], ax), _pair_heads(t[..., 1280:2048], ax), t[..., 768:1280],
             t[..., 2048:4608], t[..., 4624:7696]], axis=ax)
    g0, g1 = _SRC_GATES
    wg = jnp.pad(w[:, g0:g1], ((0, 0), (0, GATE_PAD - (g1 - g0)))).astype(BF16)
    bg = jnp.pad(b[g0:g1], (0, GATE_PAD - (g1 - g0)))
    return cols(w).astype(BF16), cols(b), wg, bg


def _reorder_out_proj(w):
    return jnp.concatenate([_pair_heads(w[:WA_WIDTH], 0), w[WA_WIDTH:]], axis=0).astype(BF16)


def kernel(x, emb_ln_g, emb_ln_b, w_in, b_in, w_out, b_out, ln_g, ln_b, t5_bias, sink, ml_f_bias,
           ml_norm_g, na_rpb):
    bsz, s, d = x.shape
    depth = w_in.shape[0]
    alpha = (2 * depth) ** 0.25
    m = bsz * s
    nc = s // ML_CHUNK
    assert d == D_MODEL and s % NA_BLOCK == 0 and s // GRID_W >= NA_KH_MAX and nc <= 128

    x2 = x.reshape(m, d)
    hb = _input_norm(x2, emb_ln_g, emb_ln_b)
    res = x2
    bias_a = _window_bias(t5_bias)
    bias_c = _na_bias(na_rpb, s // GRID_W)
    for l in range(depth):
        wm, bm, wg, bg = _reorder_in_proj(w_in[l], b_in[l])
        u, gates = _in_projection(hb, wm, bm, wg, bg)
        u3 = u.reshape(bsz, s, U_WIDTH)
        gates = gates[:, :16].reshape(bsz, s, 16).transpose(0, 2, 1).reshape(bsz, 16, nc, ML_CHUNK)
        gates_t = jnp.pad(gates.transpose(0, 1, 3, 2), ((0, 0), (0, 0), (0, 0), (0, 128 - nc)))
        ya = _window_attention(u3, bias_a, sink[l])
        yb = _mlstm(u3, gates, gates_t, ml_f_bias[l], ml_norm_g[l])
        yc = _neighbourhood_attention(u3, bias_c, l)
        last = l == depth - 1
        outs = _out_projection(ya.reshape(m, WA_WIDTH), yb.reshape(m, ML_WIDTH), yc.reshape(m, NA_WIDTH),
                               _reorder_out_proj(w_out[l]), b_out[l], res, emb_ln_g, emb_ln_b,
                               ln_g[l], ln_b[l], alpha, l == 0, not last)
        res = outs[0]
        hb = None if last else outs[1]
    return res.reshape(bsz, s, d)
```

```python
import functools
import math

import numpy as np
import jax
import jax.numpy as jnp
from jax import lax
from jax.experimental import pallas as pl
from jax.experimental.pallas import tpu as pltpu

F32 = jnp.float32
BF16 = jnp.bfloat16

D_MODEL = 2048
HEAD_DIM = 64
LN_EPS = 1e-5
NEG = -1e30
LOG2E = math.log2(math.e)
SM_CHUNK = 64

WA_HEADS = 12
WA_KV_HEADS = 4
WA_GROUP = WA_HEADS // WA_KV_HEADS
WA_WIDTH = WA_HEADS * HEAD_DIM
WA_KV_WIDTH = WA_KV_HEADS * HEAD_DIM
WA_BLOCK = 128
WINDOW = 128
T5_BUCKETS = 32
T5_MAX_DIST = 128
ML_HEADS = 4
ML_HEAD_DIM = 128
ML_WIDTH = ML_HEADS * ML_HEAD_DIM
ML_CHUNK = 128
ML_AUG = 2 * ML_HEAD_DIM
NA_HEADS = 12
NA_WIDTH = NA_HEADS * HEAD_DIM
GRID_W = 64
NA_KH_MAX = 8
NA_KW = 16
NA_ROWS = 4
NA_BLOCK = NA_ROWS * GRID_W

MIX_WIDTH = WA_WIDTH + ML_WIDTH + NA_WIDTH
GATE_PAD = 128

A_Q, A_Z, A_K, A_V = 0, 768, 1536, 1792
B_Q, B_K, B_V, B_O, B_Z = 2048, 2560, 3072, 3584, 4096
C_Q, C_K, C_V, C_Z = 4608, 5376, 6144, 6912
U_WIDTH = 7680
_SRC_GATES = (4608, 4624)

VMEM_LIMIT = 56 * 1024 * 1024
LN_TM = 512
IN_TM, IN_TN = 1024, 1536
OUT_TM = 512


def _cparams(sem, flags=None):
    return pltpu.CompilerParams(dimension_semantics=sem, vmem_limit_bytes=VMEM_LIMIT, flags=flags)


def _dot_hi(a, b):
    return jnp.dot(a, b, preferred_element_type=F32, precision=lax.Precision.HIGHEST)


def _layer_norm_rows(x, g, b):
    mu = jnp.mean(x, axis=-1, keepdims=True)
    xc = x - mu
    var = jnp.mean(xc * xc, axis=-1, keepdims=True)
    return xc * lax.rsqrt(var + LN_EPS) * g + b


def _ln_kernel(x_ref, g_ref, b_ref, of_ref, ob_ref):
    y = _layer_norm_rows(x_ref[...].astype(F32), g_ref[...], b_ref[...])
    of_ref[...] = y
    ob_ref[...] = y.astype(BF16)


def _input_norm(x2, g, b):
    m, d = x2.shape
    tm = min(LN_TM, m)
    row = pl.BlockSpec((tm, d), lambda i: (i, 0))
    vec = pl.BlockSpec((1, d), lambda i: (0, 0))
    return pl.pallas_call(
        _ln_kernel,
        grid=(m // tm,),
        in_specs=[row, vec, vec],
        out_specs=[row, row],
        out_shape=[jax.ShapeDtypeStruct((m, d), F32), jax.ShapeDtypeStruct((m, d), BF16)],
        compiler_params=_cparams(("parallel",)),
        name="input_norm",
    )(x2, g.reshape(1, d), b.reshape(1, d))


def _inproj_kernel(h_ref, w_ref, b_ref, wg_ref, bg_ref, u_ref, gate_ref):
    acc = jnp.dot(h_ref[...], w_ref[...], preferred_element_type=F32)
    u_ref[...] = (acc + b_ref[...]).astype(u_ref.dtype)

    @pl.when(pl.program_id(1) == 0)
    def _():
        gate_ref[...] = jnp.dot(h_ref[...], wg_ref[...], preferred_element_type=F32) + bg_ref[...]


def _in_projection(hb, w, b, wg, bg):
    m, d = hb.shape
    n = w.shape[1]
    tm, tn = min(IN_TM, m), IN_TN
    return pl.pallas_call(
        _inproj_kernel,
        grid=(m // tm, n // tn),
        in_specs=[pl.BlockSpec((tm, d), lambda i, j: (i, 0)),
                  pl.BlockSpec((d, tn), lambda i, j: (0, j)),
                  pl.BlockSpec((1, tn), lambda i, j: (0, j)),
                  pl.BlockSpec((d, GATE_PAD), lambda i, j: (0, 0)),
                  pl.BlockSpec((1, GATE_PAD), lambda i, j: (0, 0))],
        out_specs=[pl.BlockSpec((tm, tn), lambda i, j: (i, j)),
                   pl.BlockSpec((tm, GATE_PAD), lambda i, j: (i, 0))],
        out_shape=[jax.ShapeDtypeStruct((m, n), BF16), jax.ShapeDtypeStruct((m, GATE_PAD), F32)],
        compiler_params=_cparams(("parallel", "arbitrary")),
        name="in_projection",
    )(hb, w, b.reshape(1, n), wg, bg.reshape(1, GATE_PAD))


def _outproj_kernel(ya_ref, yb_ref, yc_ref, w_ref, b_ref, res_ref, g_ref, beta_ref, *out_refs, alpha):
    half = ya_ref.shape[0] // 2
    for rows in (slice(0, half), slice(half, 2 * half)):
        y = jnp.concatenate([ya_ref[rows, :], yb_ref[rows, :], yc_ref[rows, :]], axis=1)
        out = jnp.dot(y, w_ref[...], preferred_element_type=F32) + b_ref[...]
        r = _layer_norm_rows(alpha * res_ref[rows, :] + out, g_ref[...], beta_ref[...])
        out_refs[0][rows, :] = r
        if len(out_refs) > 1:
            out_refs[1][rows, :] = r.astype(BF16)


def _out_projection(ya, yb, yc, w, b, res, g, beta, alpha, want_bf16):
    m, d = res.shape
    tm = min(OUT_TM, m)
    row = lambda i: (i, 0)
    const = lambda i: (0, 0)
    out_specs = [pl.BlockSpec((tm, d), row)]
    out_shape = [jax.ShapeDtypeStruct((m, d), F32)]
    if want_bf16:
        out_specs.append(pl.BlockSpec((tm, d), row))
        out_shape.append(jax.ShapeDtypeStruct((m, d), BF16))
    vec = lambda v: v.reshape(1, d)
    return pl.pallas_call(
        functools.partial(_outproj_kernel, alpha=alpha),
        grid=(m // tm,),
        in_specs=[pl.BlockSpec((tm, WA_WIDTH), row),
                  pl.BlockSpec((tm, ML_WIDTH), row),
                  pl.BlockSpec((tm, NA_WIDTH), row),
                  pl.BlockSpec((MIX_WIDTH, d), const),
                  pl.BlockSpec((1, d), const),
                  pl.BlockSpec((tm, d), row),
                  pl.BlockSpec((1, d), const),
                  pl.BlockSpec((1, d), const)],
        out_specs=out_specs,
        out_shape=out_shape,
        compiler_params=_cparams(("parallel",)),
        name="out_projection",
    )(ya, yb, yc, w, vec(b), res, vec(g), vec(beta))


def _silu(z):
    return z * jax.nn.sigmoid(z)


def _wattn_kernel(q_ref, z_ref, kp_ref, kc_ref, kn_ref, vp_ref, vc_ref, vn_ref, bias_ref, sink_ref,
                  o_ref, s0_ref, s1_ref, p_ref):
    n = pl.program_id(1)
    blk = WA_BLOCK
    npair = WA_KV_HEADS // 2
    rows = 2 * WA_GROUP * blk
    lane = lax.broadcasted_iota(jnp.int32, (blk, 2 * HEAD_DIM), 1)
    left = lane < HEAD_DIM

    def step(s_read, s_write):
        for i in range(npair):
            ks = slice(i * 2 * HEAD_DIM, (i + 1) * 2 * HEAD_DIM)
            k3 = jnp.concatenate([kp_ref[:, ks], kc_ref[:, ks], kn_ref[:, ks]], axis=0)
            tiles = [q_ref[:, (WA_GROUP * i + j) * 2 * HEAD_DIM:(WA_GROUP * i + j + 1) * 2 * HEAD_DIM]
                     * (HEAD_DIM ** -0.5 * LOG2E) for j in range(WA_GROUP)]
            zero = jnp.zeros_like(tiles[0])
            lhs = jnp.concatenate([jnp.where(left, t, zero) for t in tiles]
                                  + [jnp.where(left, zero, t) for t in tiles], axis=0)
            s_write[i] = lax.dot_general(lhs, k3, (((1,), (1,)), ((), ())), preferred_element_type=F32)
            v3 = jnp.concatenate([vp_ref[:, ks], vc_ref[:, ks], vn_ref[:, ks]], axis=0)
            dens = []
            for c in range(rows // SM_CHUNK):
                rs = slice(c * SM_CHUNK, (c + 1) * SM_CHUNK)
                sk = sink_ref[0, 2 * WA_GROUP * i + c * SM_CHUNK // blk] * LOG2E
                sc = s_read[i, rs, :] + bias_ref[i, rs, :]
                m = jnp.maximum(jnp.max(sc, axis=-1, keepdims=True), sk)
                p = jnp.exp2(sc - m)
                dens.append(jnp.sum(p, axis=-1, keepdims=True) + jnp.exp2(sk - m))
                p_ref[i, rs, :] = p.astype(BF16)
            pv = jnp.dot(p_ref[i], v3, preferred_element_type=F32)
            o = jnp.concatenate([pv[c * SM_CHUNK:(c + 1) * SM_CHUNK] / dens[c]
                                 for c in range(rows // SM_CHUNK)], axis=0)
            for j in range(WA_GROUP):
                t = WA_GROUP * i + j
                ot = jnp.where(left, o[j * blk:(j + 1) * blk], o[(WA_GROUP + j) * blk:(WA_GROUP + j + 1) * blk])
                cs = slice(t * 2 * HEAD_DIM, (t + 1) * 2 * HEAD_DIM)
                o_ref[:, cs] = (ot * _silu(z_ref[:, cs].astype(F32))).astype(BF16)

    @pl.when(n == 0)
    def _():
        s1_ref[...] = jnp.zeros(s1_ref.shape, F32)

    @pl.when(n % 2 == 0)
    def _():
        step(s1_ref, s0_ref)

    @pl.when(n % 2 == 1)
    def _():
        step(s0_ref, s1_ref)


def _window_attention(u3, bias, sink):
    bsz, s, _ = u3.shape
    nb = s // WA_BLOCK
    qw, kw = WA_WIDTH, WA_KV_WIDTH
    npair = WA_KV_HEADS // 2
    rows = 2 * WA_GROUP * WA_BLOCK
    clamp = lambda i: jnp.clip(i, 0, nb - 1)
    kspec = lambda off, c: pl.BlockSpec((None, WA_BLOCK, kw), lambda b, n: (b, clamp(n + off), c))
    btype = lambda b, n: (jnp.where(n <= 1, 0, jnp.where(n == nb, 2, 1)), 0, 0, 0)
    return pl.pallas_call(
        _wattn_kernel,
        grid=(bsz, nb + 1),
        in_specs=[pl.BlockSpec((None, WA_BLOCK, qw), lambda b, n: (b, clamp(n), A_Q // qw)),
                  pl.BlockSpec((None, WA_BLOCK, qw), lambda b, n: (b, clamp(n - 1), A_Z // qw)),
                  kspec(-1, A_K // kw), kspec(0, A_K // kw), kspec(1, A_K // kw),
                  kspec(-2, A_V // kw), kspec(-1, A_V // kw), kspec(0, A_V // kw),
                  pl.BlockSpec((None, npair, rows, 3 * WA_BLOCK), btype),
                  pl.BlockSpec(memory_space=pltpu.SMEM)],
        out_specs=pl.BlockSpec((None, WA_BLOCK, qw), lambda b, n: (b, clamp(n - 1), 0)),
        out_shape=jax.ShapeDtypeStruct((bsz, s, qw), BF16),
        scratch_shapes=[pltpu.VMEM((npair, rows, 3 * WA_BLOCK), F32),
                        pltpu.VMEM((npair, rows, 3 * WA_BLOCK), F32),
                        pltpu.VMEM((npair, rows, 3 * WA_BLOCK), BF16)],
        compiler_params=_cparams(("parallel", "arbitrary")),
        name="window_attention",
    )(u3, u3, u3, u3, u3, u3, u3, u3, bias.reshape(3, npair, rows, 3 * WA_BLOCK), sink.reshape(1, WA_HEADS))


def _t5_bucket_np(rel):
    half = T5_BUCKETS // 2
    max_exact = half // 2
    ret = np.where(rel > 0, half, 0)
    n = np.abs(rel)
    nf = np.maximum(n, 1).astype(np.float64)
    v = np.log(nf / max_exact) / math.log(T5_MAX_DIST / max_exact) * (half - max_exact)
    vr = np.round(v)
    v = np.where(np.abs(v - vr) < 1e-9, vr, v)
    large = np.minimum(max_exact + np.trunc(v).astype(np.int64), half - 1)
    return ret + np.where(n < max_exact, n, large)


def _wbias_kernel(t5t_ref, bucket_ref, o_ref):
    width = 4 * WA_BLOCK
    bk = bucket_ref[...]
    e = lax.broadcasted_iota(jnp.int32, (T5_BUCKETS, width), 0)
    onehot = jnp.where(e == bk, 1.0, 0.0).astype(F32)
    g = _dot_hi(t5t_ref[...], onehot) * LOG2E + jnp.where(bk < 0, NEG, 0.0)
    col = lax.broadcasted_iota(jnp.int32, (WA_BLOCK, 3 * WA_BLOCK), 1)
    for h in range(WA_HEADS):
        row = jnp.broadcast_to(g[h:h + 1, :], (WA_BLOCK, width))
        t = pltpu.roll(row, 3 * WA_BLOCK, 1, stride=1, stride_axis=0)[:, :3 * WA_BLOCK]
        o_ref[0, h] = jnp.where(col < WA_BLOCK, NEG, t)
        o_ref[1, h] = t
        o_ref[2, h] = jnp.where(col >= 2 * WA_BLOCK, NEG, t)


def _window_bias(t5_table):
    rel = np.arange(4 * WA_BLOCK) - 2 * WA_BLOCK
    bucket = np.where(np.abs(rel) <= WINDOW, _t5_bucket_np(rel), -1).astype(np.int32)
    t5t = jnp.pad(t5_table.astype(F32).T, ((0, 16 - WA_HEADS), (0, 0)))
    return pl.pallas_call(
        _wbias_kernel,
        out_shape=jax.ShapeDtypeStruct((3, WA_HEADS, WA_BLOCK, 3 * WA_BLOCK), F32),
        name="window_bias",
    )(t5t, jnp.asarray(bucket).reshape(1, -1))


def _natten_kernel(q_ref, z_ref, kp_ref, kc_ref, kn_ref, vp_ref, vc_ref, vn_ref, bias_ref, o_ref,
                   s0_ref, s1_ref, p_ref):
    n = pl.program_id(1)
    blk = NA_BLOCK
    lane = lax.broadcasted_iota(jnp.int32, (blk, 2 * HEAD_DIM), 1)
    left = lane < HEAD_DIM

    ones_col = jnp.where(lax.broadcasted_iota(jnp.int32, (3 * blk, 2 * HEAD_DIM), 1) == 0, 1.0, 0.0).astype(BF16)

    def step(s_read, s_write):
        for i in range(NA_HEADS // 2):
            cs = slice(i * 2 * HEAD_DIM, (i + 1) * 2 * HEAD_DIM)
            k3 = jnp.concatenate([kp_ref[:, cs], kc_ref[:, cs], kn_ref[:, cs]], axis=0)
            t = q_ref[:, cs] * (HEAD_DIM ** -0.5 * LOG2E)
            zero = jnp.zeros_like(t)
            lhs = jnp.concatenate([jnp.where(left, t, zero), jnp.where(left, zero, t)], axis=0)
            s_write[i] = lax.dot_general(lhs, k3, (((1,), (1,)), ((), ())), preferred_element_type=F32)
            pb = p_ref.at[i % 3]
            for c in range(2 * blk // SM_CHUNK):
                rs = slice(c * SM_CHUNK, (c + 1) * SM_CHUNK)
                sc = s_read[i, rs, :] + bias_ref[i, rs, :]
                m = jnp.max(sc, axis=-1, keepdims=True)
                pb[rs, :] = jnp.exp2(sc - m).astype(BF16)
            if i > 0:
                weighted_values(i - 1)
        weighted_values(NA_HEADS // 2 - 1)

    def weighted_values(i):
        cs = slice(i * 2 * HEAD_DIM, (i + 1) * 2 * HEAD_DIM)
        v3 = jnp.concatenate([vp_ref[:, cs], vc_ref[:, cs], vn_ref[:, cs]], axis=0)
        vaug = jnp.concatenate([v3, ones_col], axis=1)
        out = jnp.dot(p_ref[i % 3], vaug, preferred_element_type=F32)
        o = out[:, :2 * HEAD_DIM] / out[:, 2 * HEAD_DIM:2 * HEAD_DIM + 1]
        ot = jnp.where(left, o[:blk], o[blk:])
        o_ref[:, cs] = (ot * _silu(z_ref[:, cs].astype(F32))).astype(BF16)

    @pl.when(n == 0)
    def _():
        s1_ref[...] = jnp.zeros(s1_ref.shape, F32)

    @pl.when(n % 2 == 0)
    def _():
        step(s1_ref, s0_ref)

    @pl.when(n % 2 == 1)
    def _():
        step(s0_ref, s1_ref)


def _neighbourhood_attention(u3, bias, layer):
    bsz, s, _ = u3.shape
    nblk = s // NA_BLOCK
    w = NA_WIDTH
    npair = NA_HEADS // 2
    clamp = lambda i: jnp.clip(i, 0, nblk - 1)
    spec = lambda off, c: pl.BlockSpec((None, NA_BLOCK, w), lambda b, n: (b, clamp(n + off), c))
    btype = lambda b, n: (layer, jnp.where(n <= 1, 0, jnp.where(n == nblk, 2, 1)), 0, 0, 0)
    sshape = (npair, 2 * NA_BLOCK, 3 * NA_BLOCK)
    return pl.pallas_call(
        _natten_kernel,
        grid=(bsz, nblk + 1),
        in_specs=[spec(0, C_Q // w), spec(-1, C_Z // w),
                  spec(-1, C_K // w), spec(0, C_K // w), spec(1, C_K // w),
                  spec(-2, C_V // w), spec(-1, C_V // w), spec(0, C_V // w),
                  pl.BlockSpec((None, None) + sshape, btype)],
        out_specs=pl.BlockSpec((None, NA_BLOCK, w), lambda b, n: (b, clamp(n - 1), 0)),
        out_shape=jax.ShapeDtypeStruct((bsz, s, w), BF16),
        scratch_shapes=[pltpu.VMEM(sshape, F32), pltpu.VMEM(sshape, F32),
                        pltpu.VMEM((3,) + sshape[1:], BF16)],
        compiler_params=_cparams(("parallel", "arbitrary")),
        name="neighbourhood_attention",
    )(u3, u3, u3, u3, u3, u3, u3, u3, bias.reshape(bias.shape[:2] + sshape))


def _na_valid_rows(rows):
    kh = min(NA_KH_MAX, rows)
    nblk = rows // NA_ROWS
    out = []
    for j in (0, min(1, nblk - 1), nblk - 1):
        r = NA_ROWS * j + np.arange(NA_ROWS)[:, None]
        kr = NA_ROWS * (j - 1) + np.arange(3 * NA_ROWS)[None, :]
        rs = np.clip(r - kh // 2, 0, rows - kh)
        out.append((kr >= rs) & (kr < rs + kh))
    return np.stack(out)


def _nabias_kernel(rpb_ref, o_ref, *, valid):
    w = GRID_W
    j = lax.broadcasted_iota(jnp.int32, (32, 2 * w), 1)
    e = lax.broadcasted_iota(jnp.int32, (32, 2 * w), 0)
    dc = jnp.clip(j - w, -(NA_KW - 1), NA_KW - 1) + NA_KW - 1
    g = _dot_hi(rpb_ref[...], jnp.where(e == dc, 1.0, 0.0).astype(F32)) * LOG2E
    lane = lax.broadcasted_iota(jnp.int32, (w, 2 * w), 1)
    qc = lax.broadcasted_iota(jnp.int32, (w, 2 * w), 0)
    kc = lane & (w - 1)
    col_start = jnp.clip(qc - NA_KW // 2, 0, w - NA_KW)
    col_ok = (kc >= col_start) & (kc < col_start + NA_KW)
    left = lane < w
    neg = jnp.full((w, 2 * w), NEG, F32)

    def toeplitz(dr, shift):
        row = jnp.broadcast_to(g[dr:dr + 1, :], (w, 2 * w))
        return pltpu.roll(row, shift, 1, stride=1, stride_axis=0)

    pair = [jnp.where(col_ok, jnp.where(left, toeplitz(d, w), toeplitz(d + 1, 0)), NEG)
            for d in range(2 * NA_KH_MAX - 2)]
    for ty in range(3):
        for rl in range(NA_ROWS):
            for t in range(3 * NA_ROWS // 2):
                d = 2 * t - rl + NA_KH_MAX - 1 - NA_ROWS
                v0, v1 = bool(valid[ty, rl, 2 * t]), bool(valid[ty, rl, 2 * t + 1])
                if v0 and v1:
                    tile = pair[d]
                elif v0:
                    tile = jnp.where(left, pair[d], NEG)
                elif v1:
                    tile = jnp.where(left, NEG, pair[d])
                else:
                    tile = neg
                o_ref[ty, rl * w:(rl + 1) * w, 2 * t * w:(2 * t + 2) * w] = tile


def _na_bias(rpb_all, rows):
    depth = rpb_all.shape[0]
    rpb_p = jnp.pad(rpb_all.astype(F32), ((0, 0), (0, 0), (0, 1), (0, 1)))
    return pl.pallas_call(
        functools.partial(_nabias_kernel, valid=_na_valid_rows(rows)),
        grid=(depth, NA_HEADS),
        in_specs=[pl.BlockSpec((None, None, 16, 32), lambda l, h: (l, h, 0, 0))],
        out_specs=pl.BlockSpec((None, 3, None, NA_BLOCK, 3 * NA_BLOCK), lambda l, h: (l, 0, h, 0, 0)),
        out_shape=jax.ShapeDtypeStruct((depth, 3, NA_HEADS, NA_BLOCK, 3 * NA_BLOCK), F32),
        compiler_params=_cparams(("parallel", "parallel")),
        name="na_bias",
    )(rpb_p)


def _log_sigmoid(x):
    return jnp.minimum(x, 0.0) - jnp.log(1.0 + jnp.exp(-jnp.abs(x)))


def _tri(n, upper):
    r = lax.broadcasted_iota(jnp.int32, (n, n), 0)
    c = lax.broadcasted_iota(jnp.int32, (n, n), 1)
    return jnp.where((r <= c) if upper else (r >= c), 1.0, 0.0).astype(F32)


def _scan_max(x, axis, reverse, size):
    idx = lax.broadcasted_iota(jnp.int32, x.shape, axis)
    k = 1
    while k < size:
        if reverse:
            shifted = pltpu.roll(x, x.shape[axis] - k, axis)
            ok = idx < size - k
        else:
            shifted = pltpu.roll(x, k, axis)
            ok = idx >= k
        x = jnp.maximum(x, jnp.where(ok, shifted, NEG))
        k *= 2
    return x


def _mlstm_kernel(q_ref, k_ref, v_ref, o_ref, z_ref, gr_ref, gt_ref, fb_ref, ng_ref, y_ref,
                  hdir_ref, state_ref, sprev_ref, er_ref, xt_ref, *, seq):
    L = ML_CHUNK
    nc = seq // L
    head = pl.program_id(1)
    scale = ML_HEAD_DIM ** -0.5
    lane = lax.broadcasted_iota(jnp.int32, (1, 128), 1)
    chunk_ok = lane < nc

    for d in range(2):
        rev = d == 1
        fb = fb_ref[d, head]
        lf_r = _log_sigmoid(gr_ref[8 * d + 4 + head] + fb)
        b_r = _dot_hi(lf_r, _tri(L, upper=not rev))
        er_ref[d] = gr_ref[8 * d + head] - b_r
        i_t = gt_ref[8 * d + head]
        lf_t = _log_sigmoid(gt_ref[8 * d + 4 + head] + fb)
        b_t = _dot_hi(_tri(L, upper=rev), lf_t)
        g = jnp.sum(lf_t, axis=0, keepdims=True)
        a_t = g - b_t + i_t
        m_loc = jnp.max(a_t, axis=0, keepdims=True)
        w_t = jnp.exp(a_t - m_loc)
        jr = lax.broadcasted_iota(jnp.int32, (128, 128), 0)
        jc = lax.broadcasted_iota(jnp.int32, (128, 128), 1)
        before = ((jr > jc) if rev else (jr < jc)) & (jr < nc)
        g8 = jnp.broadcast_to(g, (8, 128))
        g_ex = _dot_hi(g8, jnp.where(before, 1.0, 0.0).astype(F32))[0:1]
        x = jnp.where(chunk_ok, m_loc - (g_ex + g), NEG)
        x8 = jnp.broadcast_to(x, (8, 128))
        if rev:
            x_prev = jnp.where(lane < nc - 1, pltpu.roll(x8, 127, 1), NEG)
        else:
            x_prev = jnp.where(lane >= 1, pltpu.roll(x8, 1, 1), NEG)
        pm = _scan_max(x_prev, 1, rev, nc)[0:1]
        m_prev = g_ex + jnp.maximum(pm, 0.0)
        m_after = jnp.maximum(g + m_prev, m_loc)
        sp = jnp.exp(g + m_prev - m_after)
        sl = jnp.exp(m_loc - m_after)
        cm = _scan_max(i_t - b_t, 0, rev, L)
        mu = jnp.maximum(m_prev, cm)
        xt_ref[d, 0 * L:1 * L] = mu
        xt_ref[d, 1 * L:2 * L] = jnp.exp(m_prev - mu) * scale
        xt_ref[d, 2 * L:3 * L] = jnp.exp(-b_t - mu)
        xt_ref[d, 3 * L:4 * L] = w_t
        xt_ref[d, 4 * L:4 * L + 8] = jnp.concatenate(
            [sp, sl, jnp.zeros((6, 128), F32)], axis=0)
        state_ref[d] = jnp.zeros((ML_HEAD_DIM, ML_AUG), F32)

    rr = lax.broadcasted_iota(jnp.int32, (L, L), 0)
    cc = lax.broadcasted_iota(jnp.int32, (L, L), 1)
    ones_col = jnp.where(lax.broadcasted_iota(jnp.int32, (L, ML_HEAD_DIM), 1) == 0, 1.0, 0.0).astype(BF16)

    def column(d, c, lo, hi):
        return jnp.sum(jnp.where(lane == c, xt_ref[d, lo:hi], 0.0), axis=1, keepdims=True)

    def state_step(j, carry):
        for d in range(2):
            c = j if d == 0 else nc - 1 - j
            r0 = pl.multiple_of(c * L, L)
            kc = k_ref[pl.ds(r0, L), :]
            vaug = jnp.concatenate([v_ref[pl.ds(r0, L), :], ones_col], axis=1)
            w = column(d, c, 3 * L, 4 * L)
            gains = column(d, c, 4 * L, 4 * L + 8)
            wv = (w * vaug.astype(F32)).astype(BF16)
            s_loc = lax.dot_general(kc, wv, (((0,), (0,)), ((), ())), preferred_element_type=F32)
            st = state_ref[d]
            sprev_ref[d, c] = st.astype(BF16)
            state_ref[d] = gains[0:1] * st + gains[1:2] * s_loc
        return carry

    lax.fori_loop(0, nc, state_step, 0, unroll=4)

    def output_step(j, carry):
        for d in range(2):
            c = j if d == 0 else nc - 1 - j
            r0 = pl.multiple_of(c * L, L)
            qc = q_ref[pl.ds(r0, L), :]
            kc = k_ref[pl.ds(r0, L), :]
            vaug = jnp.concatenate([v_ref[pl.ds(r0, L), :], ones_col], axis=1)
            cols = column(d, c, 0, 3 * L)
            mu, iw, fl = (cols[i * L:(i + 1) * L] for i in range(3))
            e_row = er_ref[d, pl.ds(c, 1), :]
            s = lax.dot_general(qc, kc, (((1,), (1,)), ((), ())), preferred_element_type=F32)
            mask = (cc >= rr) if d == 1 else (cc <= rr)
            p = jnp.where(mask, jnp.exp(e_row - mu), 0.0)
            w2 = (s * p * scale).astype(BF16)
            out = (jnp.dot(w2, vaug, preferred_element_type=F32)
                   + iw * jnp.dot(qc, sprev_ref[d, c], preferred_element_type=F32))
            num = out[:, :ML_HEAD_DIM]
            den = out[:, ML_HEAD_DIM:ML_HEAD_DIM + 1]
            hdir_ref[d, pl.ds(r0, L), :] = num / jnp.maximum(jnp.abs(den), fl)
        return carry

    lax.fori_loop(0, nc, output_step, 0, unroll=4)

    tile = min(512, seq)

    def finish(t, carry):
        r0 = pl.multiple_of(t * tile, tile)
        rows = pl.ds(r0, tile)
        hs = hdir_ref[0, rows, :] + hdir_ref[1, rows, :]
        hs = jax.nn.sigmoid(o_ref[rows, :].astype(F32)) * hs
        mu = jnp.mean(hs, axis=-1, keepdims=True)
        hc = hs - mu
        var = jnp.mean(hc * hc, axis=-1, keepdims=True)
        hn = hc * lax.rsqrt(var + LN_EPS) * ng_ref[...]
        y_ref[rows, :] = (hn * _silu(z_ref[rows, :].astype(F32))).astype(BF16)
        return carry

    lax.fori_loop(0, seq // tile, finish, 0)


def _mlstm(u3, gates_r, gates_t, f_bias, norm_g):
    bsz, s, _ = u3.shape
    d = ML_HEAD_DIM
    nc = s // ML_CHUNK
    col = lambda base: pl.BlockSpec((None, s, d), lambda b, h: (b, 0, base // d + h))
    return pl.pallas_call(
        functools.partial(_mlstm_kernel, seq=s),
        grid=(bsz, ML_HEADS),
        in_specs=[col(B_Q), col(B_K), col(B_V), col(B_O), col(B_Z),
                  pl.BlockSpec((None, 16, nc, ML_CHUNK), lambda b, h: (b, 0, 0, 0)),
                  pl.BlockSpec((None, 16, ML_CHUNK, 128), lambda b, h: (b, 0, 0, 0)),
                  pl.BlockSpec(memory_space=pltpu.SMEM),
                  pl.BlockSpec((1, d), lambda b, h: (0, h))],
        out_specs=pl.BlockSpec((None, s, d), lambda b, h: (b, 0, h)),
        out_shape=jax.ShapeDtypeStruct((bsz, s, ML_WIDTH), BF16),
        scratch_shapes=[pltpu.VMEM((2, s, d), F32),
                        pltpu.VMEM((2, d, ML_AUG), F32),
                        pltpu.VMEM((2, nc, d, ML_AUG), BF16),
                        pltpu.VMEM((2, nc, ML_CHUNK), F32),
                        pltpu.VMEM((2, 4 * ML_CHUNK + 8, 128), F32)],
        compiler_params=_cparams(("parallel", "parallel")),
        name="mlstm",
    )(u3, u3, u3, u3, u3, gates_r, gates_t, f_bias, norm_g.reshape(1, ML_WIDTH))


def _pair_heads(t, axis):
    shape = t.shape
    t = t.reshape(shape[:axis] + (2, 2, WA_GROUP, HEAD_DIM) + shape[axis + 1:])
    return jnp.swapaxes(t, axis + 1, axis + 2).reshape(shape)


def _reorder_in_proj(w, b):
    def cols(t):
        ax = t.ndim - 1
        return jnp.concatenate(
            [_pair_heads(t[..., 0:768], ax), _pair_heads(t[..., 1280:2048], ax), t[..., 768:1280],
             t[..., 2048:4608], t[..., 4624:7696]], axis=ax)
    g0, g1 = _SRC_GATES
    wg = jnp.pad(w[:, g0:g1], ((0, 0), (0, GATE_PAD - (g1 - g0)))).astype(BF16)
    bg = jnp.pad(b[g0:g1], (0, GATE_PAD - (g1 - g0)))
    return cols(w).astype(BF16), cols(b), wg, bg


def _reorder_out_proj(w):
    return jnp.concatenate([_pair_heads(w[:WA_WIDTH], 0), w[WA_WIDTH:]], axis=0).astype(BF16)


def kernel(x, emb_ln_g, emb_ln_b, w_in, b_in, w_out, b_out, ln_g, ln_b, t5_bias, sink, ml_f_bias,
           ml_norm_g, na_rpb):
    bsz, s, d = x.shape
    depth = w_in.shape[0]
    alpha = (2 * depth) ** 0.25
    m = bsz * s
    nc = s // ML_CHUNK
    assert d == D_MODEL and s % NA_BLOCK == 0 and s // GRID_W >= NA_KH_MAX and nc <= 128

    res, hb = _input_norm(x.reshape(m, d), emb_ln_g, emb_ln_b)
    bias_a = _window_bias(t5_bias)
    bias_c = _na_bias(na_rpb, s // GRID_W)
    for l in range(depth):
        wm, bm, wg, bg = _reorder_in_proj(w_in[l], b_in[l])
        u, gates = _in_projection(hb, wm, bm, wg, bg)
        u3 = u.reshape(bsz, s, U_WIDTH)
        gates = gates[:, :16].reshape(bsz, s, 16).transpose(0, 2, 1).reshape(bsz, 16, nc, ML_CHUNK)
        gates_t = jnp.pad(gates.transpose(0, 1, 3, 2), ((0, 0), (0, 0), (0, 0), (0, 128 - nc)))
        ya = _window_attention(u3, bias_a, sink[l])
        yb = _mlstm(u3, gates, gates_t, ml_f_bias[l], ml_norm_g[l])
        yc = _neighbourhood_attention(u3, bias_c, l)
        last = l == depth - 1
        outs = _out_projection(ya.reshape(m, WA_WIDTH), yb.reshape(m, ML_WIDTH), yc.reshape(m, NA_WIDTH),
                               _reorder_out_proj(w_out[l]), b_out[l], res, ln_g[l], ln_b[l], alpha, not last)
        res = outs[0]
        hb = None if last else outs[1]
    return res.reshape(bsz, s, d)
```

```python
import functools
import math

import numpy as np
import jax
import jax.numpy as jnp
from jax import lax
from jax.experimental import pallas as pl
from jax.experimental.pallas import tpu as pltpu

F32 = jnp.float32
BF16 = jnp.bfloat16

D_MODEL = 2048
HEAD_DIM = 64
LN_EPS = 1e-5
NEG = -1e30
LOG2E = math.log2(math.e)
SM_CHUNK = 64

WA_HEADS = 12
WA_KV_HEADS = 4
WA_GROUP = WA_HEADS // WA_KV_HEADS
WA_WIDTH = WA_HEADS * HEAD_DIM
WA_KV_WIDTH = WA_KV_HEADS * HEAD_DIM
WA_BLOCK = 128
WINDOW = 128
T5_BUCKETS = 32
T5_MAX_DIST = 128
ML_HEADS = 4
ML_HEAD_DIM = 128
ML_WIDTH = ML_HEADS * ML_HEAD_DIM
ML_CHUNK = 128
ML_AUG = 2 * ML_HEAD_DIM
NA_HEADS = 12
NA_WIDTH = NA_HEADS * HEAD_DIM
GRID_W = 64
NA_KH_MAX = 8
NA_KW = 16
NA_ROWS = 4
NA_BLOCK = NA_ROWS * GRID_W

MIX_WIDTH = WA_WIDTH + ML_WIDTH + NA_WIDTH

C_Q, C_K, C_V, C_Z = 0, 768, 1536, 2304
A_Q, A_Z, A_K, A_V = 3072, 3840, 4608, 4864
B_K = 5120
U_WIDTH = 5632
T_WIDTH = 4 * ML_WIDTH

VMEM_LIMIT = 56 * 1024 * 1024
LN_TM = 512
IN_TM, IN_TN = 1024, 1408
OUT_TM = 512


def _cparams(sem):
    return pltpu.CompilerParams(dimension_semantics=sem, vmem_limit_bytes=VMEM_LIMIT)


def _dot_hi(a, b):
    return jnp.dot(a, b, preferred_element_type=F32, precision=lax.Precision.HIGHEST)


def _layer_norm_rows(x, g, b):
    mu = jnp.mean(x, axis=-1, keepdims=True)
    xc = x - mu
    var = jnp.mean(xc * xc, axis=-1, keepdims=True)
    return xc * lax.rsqrt(var + LN_EPS) * g + b


def _ln_kernel(x_ref, g_ref, b_ref, of_ref, ob_ref):
    y = _layer_norm_rows(x_ref[...].astype(F32), g_ref[...], b_ref[...])
    of_ref[...] = y
    ob_ref[...] = y.astype(BF16)


def _input_norm(x2, g, b):
    m, d = x2.shape
    tm = min(LN_TM, m)
    row = pl.BlockSpec((tm, d), lambda i: (i, 0))
    vec = pl.BlockSpec((1, d), lambda i: (0, 0))
    return pl.pallas_call(
        _ln_kernel,
        grid=(m // tm,),
        in_specs=[row, vec, vec],
        out_specs=[row, row],
        out_shape=[jax.ShapeDtypeStruct((m, d), F32), jax.ShapeDtypeStruct((m, d), BF16)],
        compiler_params=_cparams(("parallel",)),
        name="input_norm",
    )(x2, g.reshape(1, d), b.reshape(1, d))


def _inproj_kernel(h_ref, w_ref, b_ref, u_ref):
    acc = jnp.dot(h_ref[...], w_ref[...], preferred_element_type=F32)
    u_ref[...] = (acc + b_ref[...]).astype(u_ref.dtype)


def _in_projection(hb, w, b):
    m, d = hb.shape
    n = w.shape[1]
    tm, tn = min(IN_TM, m), IN_TN
    return pl.pallas_call(
        _inproj_kernel,
        grid=(n // tn, m // tm),
        in_specs=[pl.BlockSpec((tm, d), lambda j, i: (i, 0)),
                  pl.BlockSpec((d, tn), lambda j, i: (0, j)),
                  pl.BlockSpec((1, tn), lambda j, i: (0, j))],
        out_specs=pl.BlockSpec((tm, tn), lambda j, i: (i, j)),
        out_shape=jax.ShapeDtypeStruct((m, n), BF16),
        compiler_params=_cparams(("parallel", "parallel")),
        name="in_projection",
    )(hb, w, b.reshape(1, n))


def _tproj_kernel(h_ref, wt_ref, bt_ref, ut_ref, gate_ref):
    acc = lax.dot_general(wt_ref[...], h_ref[...], (((1,), (1,)), ((), ())),
                          preferred_element_type=F32) + bt_ref[...]
    for c in range(ut_ref.shape[0]):
        ut_ref[c] = acc[:T_WIDTH, c * ML_CHUNK:(c + 1) * ML_CHUNK].astype(BF16)
    gate_ref[...] = acc[T_WIDTH:]


def _t_projection(hb, wt, bt):
    m, d = hb.shape
    tm = min(IN_TM, m)
    rows = wt.shape[0]
    ngate = rows - T_WIDTH
    return pl.pallas_call(
        _tproj_kernel,
        grid=(m // tm,),
        in_specs=[pl.BlockSpec((tm, d), lambda i: (i, 0)),
                  pl.BlockSpec((rows, d), lambda i: (0, 0)),
                  pl.BlockSpec((rows, 1), lambda i: (0, 0))],
        out_specs=[pl.BlockSpec((tm // ML_CHUNK, T_WIDTH, ML_CHUNK), lambda i: (i, 0, 0)),
                   pl.BlockSpec((ngate, tm), lambda i: (0, i))],
        out_shape=[jax.ShapeDtypeStruct((m // ML_CHUNK, T_WIDTH, ML_CHUNK), BF16),
                   jax.ShapeDtypeStruct((ngate, m), F32)],
        compiler_params=_cparams(("parallel",)),
        name="t_projection",
    )(hb, wt, bt.reshape(rows, 1))


def _outproj_kernel(ya_ref, yb_ref, yc_ref, w_ref, b_ref, res_ref, g_ref, beta_ref, *out_refs, alpha):
    half = ya_ref.shape[0] // 2
    for rows in (slice(0, half), slice(half, 2 * half)):
        y = jnp.concatenate([ya_ref[rows, :], yb_ref[rows, :], yc_ref[rows, :]], axis=1)
        out = jnp.dot(y, w_ref[...], preferred_element_type=F32) + b_ref[...]
        r = _layer_norm_rows(alpha * res_ref[rows, :] + out, g_ref[...], beta_ref[...])
        out_refs[0][rows, :] = r
        if len(out_refs) > 1:
            out_refs[1][rows, :] = r.astype(BF16)


def _out_projection(ya, yb, yc, w, b, res, g, beta, alpha, want_bf16):
    m, d = res.shape
    tm = min(OUT_TM, m)
    row = lambda i: (i, 0)
    const = lambda i: (0, 0)
    out_specs = [pl.BlockSpec((tm, d), row)]
    out_shape = [jax.ShapeDtypeStruct((m, d), F32)]
    if want_bf16:
        out_specs.append(pl.BlockSpec((tm, d), row))
        out_shape.append(jax.ShapeDtypeStruct((m, d), BF16))
    vec = lambda v: v.reshape(1, d)
    return pl.pallas_call(
        functools.partial(_outproj_kernel, alpha=alpha),
        grid=(m // tm,),
        in_specs=[pl.BlockSpec((tm, WA_WIDTH), row),
                  pl.BlockSpec((tm, ML_WIDTH), row),
                  pl.BlockSpec((tm, NA_WIDTH), row),
                  pl.BlockSpec((MIX_WIDTH, d), const),
                  pl.BlockSpec((1, d), const),
                  pl.BlockSpec((tm, d), row),
                  pl.BlockSpec((1, d), const),
                  pl.BlockSpec((1, d), const)],
        out_specs=out_specs,
        out_shape=out_shape,
        compiler_params=_cparams(("parallel",)),
        name="out_projection",
    )(ya, yb, yc, w, vec(b), res, vec(g), vec(beta))


def _silu(z):
    return z * jax.nn.sigmoid(z)


def _wattn_kernel(q_ref, z_ref, kp_ref, kc_ref, kn_ref, vp_ref, vc_ref, vn_ref, bias_ref, sink_ref,
                  o_ref, s0_ref, s1_ref, p_ref):
    n = pl.program_id(1)
    blk = WA_BLOCK
    npair = WA_KV_HEADS // 2
    rows = 2 * WA_GROUP * blk
    lane = lax.broadcasted_iota(jnp.int32, (blk, 2 * HEAD_DIM), 1)
    left = lane < HEAD_DIM

    def step(s_read, s_write):
        for i in range(npair):
            ks = slice(i * 2 * HEAD_DIM, (i + 1) * 2 * HEAD_DIM)
            k3 = jnp.concatenate([kp_ref[:, ks], kc_ref[:, ks], kn_ref[:, ks]], axis=0)
            tiles = [q_ref[:, (WA_GROUP * i + j) * 2 * HEAD_DIM:(WA_GROUP * i + j + 1) * 2 * HEAD_DIM]
                     * (HEAD_DIM ** -0.5 * LOG2E) for j in range(WA_GROUP)]
            zero = jnp.zeros_like(tiles[0])
            lhs = jnp.concatenate([jnp.where(left, t, zero) for t in tiles]
                                  + [jnp.where(left, zero, t) for t in tiles], axis=0)
            s_write[i] = lax.dot_general(lhs, k3, (((1,), (1,)), ((), ())), preferred_element_type=F32)
            v3 = jnp.concatenate([vp_ref[:, ks], vc_ref[:, ks], vn_ref[:, ks]], axis=0)
            dens = []
            for c in range(rows // SM_CHUNK):
                rs = slice(c * SM_CHUNK, (c + 1) * SM_CHUNK)
                sk = sink_ref[0, 2 * WA_GROUP * i + c * SM_CHUNK // blk] * LOG2E
                sc = s_read[i, rs, :] + bias_ref[i, rs, :]
                m = jnp.maximum(jnp.max(sc, axis=-1, keepdims=True), sk)
                p = jnp.exp2(sc - m)
                dens.append(jnp.sum(p, axis=-1, keepdims=True) + jnp.exp2(sk - m))
                p_ref[i, rs, :] = p.astype(BF16)
            pv = jnp.dot(p_ref[i], v3, preferred_element_type=F32)
            o = jnp.concatenate([pv[c * SM_CHUNK:(c + 1) * SM_CHUNK] / dens[c]
                                 for c in range(rows // SM_CHUNK)], axis=0)
            for j in range(WA_GROUP):
                t = WA_GROUP * i + j
                ot = jnp.where(left, o[j * blk:(j + 1) * blk], o[(WA_GROUP + j) * blk:(WA_GROUP + j + 1) * blk])
                cs = slice(t * 2 * HEAD_DIM, (t + 1) * 2 * HEAD_DIM)
                o_ref[:, cs] = (ot * _silu(z_ref[:, cs].astype(F32))).astype(BF16)

    @pl.when(n == 0)
    def _():
        s1_ref[...] = jnp.zeros(s1_ref.shape, F32)

    @pl.when(n % 2 == 0)
    def _():
        step(s1_ref, s0_ref)

    @pl.when(n % 2 == 1)
    def _():
        step(s0_ref, s1_ref)


def _window_attention(u3, bias, sink):
    bsz, s, _ = u3.shape
    nb = s // WA_BLOCK
    qw, kw = WA_WIDTH, WA_KV_WIDTH
    npair = WA_KV_HEADS // 2
    rows = 2 * WA_GROUP * WA_BLOCK
    clamp = lambda i: jnp.clip(i, 0, nb - 1)
    kspec = lambda off, c: pl.BlockSpec((None, WA_BLOCK, kw), lambda b, n: (b, clamp(n + off), c))
    btype = lambda b, n: (jnp.where(n <= 1, 0, jnp.where(n == nb, 2, 1)), 0, 0, 0)
    return pl.pallas_call(
        _wattn_kernel,
        grid=(bsz, nb + 1),
        in_specs=[pl.BlockSpec((None, WA_BLOCK, qw), lambda b, n: (b, clamp(n), A_Q // qw)),
                  pl.BlockSpec((None, WA_BLOCK, qw), lambda b, n: (b, clamp(n - 1), A_Z // qw)),
                  kspec(-1, A_K // kw), kspec(0, A_K // kw), kspec(1, A_K // kw),
                  kspec(-2, A_V // kw), kspec(-1, A_V // kw), kspec(0, A_V // kw),
                  pl.BlockSpec((None, npair, rows, 3 * WA_BLOCK), btype),
                  pl.BlockSpec(memory_space=pltpu.SMEM)],
        out_specs=pl.BlockSpec((None, WA_BLOCK, qw), lambda b, n: (b, clamp(n - 1), 0)),
        out_shape=jax.ShapeDtypeStruct((bsz, s, qw), BF16),
        scratch_shapes=[pltpu.VMEM((npair, rows, 3 * WA_BLOCK), F32),
                        pltpu.VMEM((npair, rows, 3 * WA_BLOCK), F32),
                        pltpu.VMEM((npair, rows, 3 * WA_BLOCK), BF16)],
        compiler_params=_cparams(("parallel", "arbitrary")),
        name="window_attention",
    )(u3, u3, u3, u3, u3, u3, u3, u3, bias.reshape(3, npair, rows, 3 * WA_BLOCK), sink.reshape(1, WA_HEADS))


def _t5_bucket_np(rel):
    half = T5_BUCKETS // 2
    max_exact = half // 2
    ret = np.where(rel > 0, half, 0)
    n = np.abs(rel)
    nf = np.maximum(n, 1).astype(np.float64)
    v = np.log(nf / max_exact) / math.log(T5_MAX_DIST / max_exact) * (half - max_exact)
    vr = np.round(v)
    v = np.where(np.abs(v - vr) < 1e-9, vr, v)
    large = np.minimum(max_exact + np.trunc(v).astype(np.int64), half - 1)
    return ret + np.where(n < max_exact, n, large)


def _wbias_kernel(t5t_ref, bucket_ref, o_ref):
    width = 4 * WA_BLOCK
    bk = bucket_ref[...]
    e = lax.broadcasted_iota(jnp.int32, (T5_BUCKETS, width), 0)
    onehot = jnp.where(e == bk, 1.0, 0.0).astype(F32)
    g = _dot_hi(t5t_ref[...], onehot) * LOG2E + jnp.where(bk < 0, NEG, 0.0)
    col = lax.broadcasted_iota(jnp.int32, (WA_BLOCK, 3 * WA_BLOCK), 1)
    for h in range(WA_HEADS):
        row = jnp.broadcast_to(g[h:h + 1, :], (WA_BLOCK, width))
        t = pltpu.roll(row, 3 * WA_BLOCK, 1, stride=1, stride_axis=0)[:, :3 * WA_BLOCK]
        o_ref[0, h] = jnp.where(col < WA_BLOCK, NEG, t)
        o_ref[1, h] = t
        o_ref[2, h] = jnp.where(col >= 2 * WA_BLOCK, NEG, t)


def _window_bias(t5_table):
    rel = np.arange(4 * WA_BLOCK) - 2 * WA_BLOCK
    bucket = np.where(np.abs(rel) <= WINDOW, _t5_bucket_np(rel), -1).astype(np.int32)
    t5t = jnp.pad(t5_table.astype(F32).T, ((0, 16 - WA_HEADS), (0, 0)))
    return pl.pallas_call(
        _wbias_kernel,
        out_shape=jax.ShapeDtypeStruct((3, WA_HEADS, WA_BLOCK, 3 * WA_BLOCK), F32),
        name="window_bias",
    )(t5t, jnp.asarray(bucket).reshape(1, -1))


def _natten_kernel(q_ref, z_ref, kp_ref, kc_ref, kn_ref, vp_ref, vc_ref, vn_ref, bias_ref, o_ref,
                   s0_ref, s1_ref, p_ref):
    n = pl.program_id(1)
    blk = NA_BLOCK
    lane = lax.broadcasted_iota(jnp.int32, (blk, 2 * HEAD_DIM), 1)
    left = lane < HEAD_DIM

    ones_col = jnp.where(lax.broadcasted_iota(jnp.int32, (3 * blk, 2 * HEAD_DIM), 1) == 0, 1.0, 0.0).astype(BF16)

    def step(s_read, s_write):
        for i in range(NA_HEADS // 2):
            cs = slice(i * 2 * HEAD_DIM, (i + 1) * 2 * HEAD_DIM)
            k3 = jnp.concatenate([kp_ref[:, cs], kc_ref[:, cs], kn_ref[:, cs]], axis=0)
            t = q_ref[:, cs] * (HEAD_DIM ** -0.5 * LOG2E)
            zero = jnp.zeros_like(t)
            lhs = jnp.concatenate([jnp.where(left, t, zero), jnp.where(left, zero, t)], axis=0)
            s_write[i] = lax.dot_general(lhs, k3, (((1,), (1,)), ((), ())), preferred_element_type=F32)
            pb = p_ref.at[i % 3]
            for c in range(2 * blk // SM_CHUNK):
                rs = slice(c * SM_CHUNK, (c + 1) * SM_CHUNK)
                sc = s_read[i, rs, :] + bias_ref[i, rs, :]
                m = jnp.max(sc, axis=-1, keepdims=True)
                pb[rs, :] = jnp.exp2(sc - m).astype(BF16)
            if i > 0:
                weighted_values(i - 1)
        weighted_values(NA_HEADS // 2 - 1)

    def weighted_values(i):
        cs = slice(i * 2 * HEAD_DIM, (i + 1) * 2 * HEAD_DIM)
        v3 = jnp.concatenate([vp_ref[:, cs], vc_ref[:, cs], vn_ref[:, cs]], axis=0)
        vaug = jnp.concatenate([v3, ones_col], axis=1)
        out = jnp.dot(p_ref[i % 3], vaug, preferred_element_type=F32)
        o = out[:, :2 * HEAD_DIM] / out[:, 2 * HEAD_DIM:2 * HEAD_DIM + 1]
        ot = jnp.where(left, o[:blk], o[blk:])
        o_ref[:, cs] = (ot * _silu(z_ref[:, cs].astype(F32))).astype(BF16)

    @pl.when(n == 0)
    def _():
        s1_ref[...] = jnp.zeros(s1_ref.shape, F32)

    @pl.when(n % 2 == 0)
    def _():
        step(s1_ref, s0_ref)

    @pl.when(n % 2 == 1)
    def _():
        step(s0_ref, s1_ref)


def _neighbourhood_attention(u3, bias, layer):
    bsz, s, _ = u3.shape
    nblk = s // NA_BLOCK
    w = NA_WIDTH
    npair = NA_HEADS // 2
    clamp = lambda i: jnp.clip(i, 0, nblk - 1)
    spec = lambda off, c: pl.BlockSpec((None, NA_BLOCK, w), lambda b, n: (b, clamp(n + off), c))
    btype = lambda b, n: (layer, jnp.where(n <= 1, 0, jnp.where(n == nblk, 2, 1)), 0, 0, 0)
    sshape = (npair, 2 * NA_BLOCK, 3 * NA_BLOCK)
    return pl.pallas_call(
        _natten_kernel,
        grid=(bsz, nblk + 1),
        in_specs=[spec(0, C_Q // w), spec(-1, C_Z // w),
                  spec(-1, C_K // w), spec(0, C_K // w), spec(1, C_K // w),
                  spec(-2, C_V // w), spec(-1, C_V // w), spec(0, C_V // w),
                  pl.BlockSpec((None, None) + sshape, btype)],
        out_specs=pl.BlockSpec((None, NA_BLOCK, w), lambda b, n: (b, clamp(n - 1), 0)),
        out_shape=jax.ShapeDtypeStruct((bsz, s, w), BF16),
        scratch_shapes=[pltpu.VMEM(sshape, F32), pltpu.VMEM(sshape, F32),
                        pltpu.VMEM((3,) + sshape[1:], BF16)],
        compiler_params=_cparams(("parallel", "arbitrary")),
        name="neighbourhood_attention",
    )(u3, u3, u3, u3, u3, u3, u3, u3, bias.reshape(bias.shape[:2] + sshape))


def _na_valid_rows(rows):
    kh = min(NA_KH_MAX, rows)
    nblk = rows // NA_ROWS
    out = []
    for j in (0, min(1, nblk - 1), nblk - 1):
        r = NA_ROWS * j + np.arange(NA_ROWS)[:, None]
        kr = NA_ROWS * (j - 1) + np.arange(3 * NA_ROWS)[None, :]
        rs = np.clip(r - kh // 2, 0, rows - kh)
        out.append((kr >= rs) & (kr < rs + kh))
    return np.stack(out)


def _nabias_kernel(rpb_ref, o_ref, *, valid):
    w = GRID_W
    j = lax.broadcasted_iota(jnp.int32, (32, 2 * w), 1)
    e = lax.broadcasted_iota(jnp.int32, (32, 2 * w), 0)
    dc = jnp.clip(j - w, -(NA_KW - 1), NA_KW - 1) + NA_KW - 1
    g = _dot_hi(rpb_ref[...], jnp.where(e == dc, 1.0, 0.0).astype(F32)) * LOG2E
    lane = lax.broadcasted_iota(jnp.int32, (w, 2 * w), 1)
    qc = lax.broadcasted_iota(jnp.int32, (w, 2 * w), 0)
    kc = lane & (w - 1)
    col_start = jnp.clip(qc - NA_KW // 2, 0, w - NA_KW)
    col_ok = (kc >= col_start) & (kc < col_start + NA_KW)
    left = lane < w
    neg = jnp.full((w, 2 * w), NEG, F32)

    def toeplitz(dr, shift):
        row = jnp.broadcast_to(g[dr:dr + 1, :], (w, 2 * w))
        return pltpu.roll(row, shift, 1, stride=1, stride_axis=0)

    pair = [jnp.where(col_ok, jnp.where(left, toeplitz(d, w), toeplitz(d + 1, 0)), NEG)
            for d in range(2 * NA_KH_MAX - 2)]
    for ty in range(3):
        for rl in range(NA_ROWS):
            for t in range(3 * NA_ROWS // 2):
                d = 2 * t - rl + NA_KH_MAX - 1 - NA_ROWS
                v0, v1 = bool(valid[ty, rl, 2 * t]), bool(valid[ty, rl, 2 * t + 1])
                if v0 and v1:
                    tile = pair[d]
                elif v0:
                    tile = jnp.where(left, pair[d], NEG)
                elif v1:
                    tile = jnp.where(left, NEG, pair[d])
                else:
                    tile = neg
                o_ref[ty, rl * w:(rl + 1) * w, 2 * t * w:(2 * t + 2) * w] = tile


def _na_bias(rpb_all, rows):
    depth = rpb_all.shape[0]
    rpb_p = jnp.pad(rpb_all.astype(F32), ((0, 0), (0, 0), (0, 1), (0, 1)))
    return pl.pallas_call(
        functools.partial(_nabias_kernel, valid=_na_valid_rows(rows)),
        grid=(depth, NA_HEADS),
        in_specs=[pl.BlockSpec((None, None, 16, 32), lambda l, h: (l, h, 0, 0))],
        out_specs=pl.BlockSpec((None, 3, None, NA_BLOCK, 3 * NA_BLOCK), lambda l, h: (l, 0, h, 0, 0)),
        out_shape=jax.ShapeDtypeStruct((depth, 3, NA_HEADS, NA_BLOCK, 3 * NA_BLOCK), F32),
        compiler_params=_cparams(("parallel", "parallel")),
        name="na_bias",
    )(rpb_p)


def _log_sigmoid(x):
    return jnp.minimum(x, 0.0) - jnp.log(1.0 + jnp.exp(-jnp.abs(x)))


def _tri(n, upper):
    r = lax.broadcasted_iota(jnp.int32, (n, n), 0)
    c = lax.broadcasted_iota(jnp.int32, (n, n), 1)
    return jnp.where((r <= c) if upper else (r >= c), 1.0, 0.0).astype(F32)


def _scan_max(x, axis, reverse, size):
    idx = lax.broadcasted_iota(jnp.int32, x.shape, axis)
    k = 1
    while k < size:
        if reverse:
            shifted = pltpu.roll(x, x.shape[axis] - k, axis)
            ok = idx < size - k
        else:
            shifted = pltpu.roll(x, k, axis)
            ok = idx >= k
        x = jnp.maximum(x, jnp.where(ok, shifted, NEG))
        k *= 2
    return x


def _mlstm_kernel(k_ref, qt_ref, vt_ref, ot_ref, zt_ref, gr_ref, gt_ref, fb_ref, ng_ref, y_ref,
                  hdir_ref, state_ref, sprev_ref, rows_ref, gain_ref, et_ref, *, seq):
    L = ML_CHUNK
    nc = seq // L
    head = pl.program_id(1)
    scale = ML_HEAD_DIM ** -0.5
    lane = lax.broadcasted_iota(jnp.int32, (1, 128), 1)

    for d in range(2):
        rev = d == 1
        fb = fb_ref[d, head]
        i_r = gr_ref[8 * d + head]
        lf_r = _log_sigmoid(gr_ref[8 * d + 4 + head] + fb)
        b_r = _dot_hi(lf_r, _tri(L, upper=not rev))
        g = jnp.broadcast_to(jnp.sum(lf_r, axis=1, keepdims=True), (nc, L))
        a_r = g - b_r + i_r
        m_loc = jnp.broadcast_to(jnp.max(a_r, axis=1, keepdims=True), (nc, L))
        jr = lax.broadcasted_iota(jnp.int32, (nc, nc), 0)
        jc = lax.broadcasted_iota(jnp.int32, (nc, nc), 1)
        before = jnp.where((jc > jr) if rev else (jc < jr), 1.0, 0.0).astype(F32)
        g_ex = _dot_hi(before, g)
        x = m_loc - (g_ex + g)
        row = lax.broadcasted_iota(jnp.int32, (nc, L), 0)
        if rev:
            x_prev = jnp.where(row < nc - 1, pltpu.roll(x, nc - 1, 0), NEG)
        else:
            x_prev = jnp.where(row >= 1, pltpu.roll(x, 1, 0), NEG)
        m_prev = g_ex + jnp.maximum(_scan_max(x_prev, 0, rev, nc), 0.0)
        m_after = jnp.maximum(g + m_prev, m_loc)
        gain_ref[d, 0] = jnp.exp(g + m_prev - m_after)
        gain_ref[d, 1] = jnp.exp(m_loc - m_after)
        mu = jnp.maximum(m_prev, _scan_max(i_r - b_r, 1, rev, L))
        rows_ref[d, 0] = mu
        rows_ref[d, 1] = jnp.exp(m_prev - mu) * scale
        rows_ref[d, 2] = jnp.exp(-b_r - mu)
        rows_ref[d, 3] = jnp.exp(a_r - m_loc)
        lf_t = _log_sigmoid(gt_ref[8 * d + 4 + head] + fb)
        et_ref[d] = gt_ref[8 * d + head] - _dot_hi(_tri(L, upper=rev), lf_t)
        state_ref[d] = jnp.zeros((ML_AUG, ML_HEAD_DIM), F32)

    rr = lax.broadcasted_iota(jnp.int32, (L, L), 0)
    cc = lax.broadcasted_iota(jnp.int32, (L, L), 1)
    ones_row = jnp.where(lax.broadcasted_iota(jnp.int32, (ML_HEAD_DIM, L), 0) == 0, 1.0, 0.0).astype(BF16)

    def state_step(j, carry):
        for d in range(2):
            c = j if d == 0 else nc - 1 - j
            kc = k_ref[pl.ds(pl.multiple_of(c * L, L), L), :]
            vaug_t = jnp.concatenate([vt_ref[c], ones_row], axis=0)
            wv_t = (rows_ref[d, 3, pl.ds(c, 1), :] * vaug_t.astype(F32)).astype(BF16)
            s_loc = jnp.dot(wv_t, kc, preferred_element_type=F32)
            st = state_ref[d]
            sprev_ref[d, c] = st.astype(BF16)
            state_ref[d] = gain_ref[d, 0, pl.ds(c, 1), :] * st + gain_ref[d, 1, pl.ds(c, 1), :] * s_loc
        return carry

    lax.fori_loop(0, nc, state_step, 0, unroll=4)

    def output_step(j, carry):
        for d in range(2):
            c = j if d == 0 else nc - 1 - j
            kc = k_ref[pl.ds(pl.multiple_of(c * L, L), L), :]
            q_t = qt_ref[c]
            vaug_t = jnp.concatenate([vt_ref[c], ones_row], axis=0)
            e_col = jnp.sum(jnp.where(lane == c, et_ref[d], 0.0), axis=1, keepdims=True)
            mu = rows_ref[d, 0, pl.ds(c, 1), :]
            s_t = jnp.dot(kc, q_t, preferred_element_type=F32)
            mask = (rr >= cc) if d == 1 else (rr <= cc)
            p_t = jnp.where(mask, jnp.exp(e_col - mu), 0.0)
            w2_t = (s_t * p_t * scale).astype(BF16)
            out_t = (jnp.dot(vaug_t, w2_t, preferred_element_type=F32)
                     + rows_ref[d, 1, pl.ds(c, 1), :]
                     * jnp.dot(sprev_ref[d, c], q_t, preferred_element_type=F32))
            den = out_t[ML_HEAD_DIM:ML_HEAD_DIM + 1]
            hdir_ref[d, c] = out_t[:ML_HEAD_DIM] / jnp.maximum(jnp.abs(den), rows_ref[d, 2, pl.ds(c, 1), :])
        return carry

    lax.fori_loop(0, nc, output_step, 0, unroll=4)

    ng_col = jnp.broadcast_to(ng_ref[...], (ML_HEAD_DIM, L))

    def finish(c, carry):
        hs = hdir_ref[0, c] + hdir_ref[1, c]
        hs = jax.nn.sigmoid(ot_ref[c].astype(F32)) * hs
        mu = jnp.mean(hs, axis=0, keepdims=True)
        hc = hs - mu
        var = jnp.mean(hc * hc, axis=0, keepdims=True)
        y_t = hc * lax.rsqrt(var + LN_EPS) * ng_col * _silu(zt_ref[c].astype(F32))
        y_ref[pl.ds(pl.multiple_of(c * L, L), L), :] = y_t.T.astype(BF16)
        return carry

    lax.fori_loop(0, nc, finish, 0, unroll=2)


def _mlstm(u3, ut, gates_r, gates_t, f_bias, norm_g):
    bsz, s, _ = u3.shape
    d = ML_HEAD_DIM
    nc = s // ML_CHUNK
    tcol = lambda part: pl.BlockSpec((nc, d, ML_CHUNK), lambda b, h: (b, part * ML_HEADS + h, 0))
    return pl.pallas_call(
        functools.partial(_mlstm_kernel, seq=s),
        grid=(bsz, ML_HEADS),
        in_specs=[pl.BlockSpec((None, s, d), lambda b, h: (b, 0, B_K // d + h)),
                  tcol(0), tcol(1), tcol(2), tcol(3),
                  pl.BlockSpec((None, 16, nc, ML_CHUNK), lambda b, h: (b, 0, 0, 0)),
                  pl.BlockSpec((None, 16, ML_CHUNK, 128), lambda b, h: (b, 0, 0, 0)),
                  pl.BlockSpec(memory_space=pltpu.SMEM),
                  pl.BlockSpec((d, 1), lambda b, h: (h, 0))],
        out_specs=pl.BlockSpec((None, s, d), lambda b, h: (b, 0, h)),
        out_shape=jax.ShapeDtypeStruct((bsz, s, ML_WIDTH), BF16),
        scratch_shapes=[pltpu.VMEM((2, nc, d, ML_CHUNK), F32),
                        pltpu.VMEM((2, ML_AUG, d), F32),
                        pltpu.VMEM((2, nc, ML_AUG, d), BF16),
                        pltpu.VMEM((2, 4, nc, ML_CHUNK), F32),
                        pltpu.VMEM((2, 2, nc, ML_CHUNK), F32),
                        pltpu.VMEM((2, ML_CHUNK, 128), F32)],
        compiler_params=_cparams(("parallel", "parallel")),
        name="mlstm",
    )(u3, ut, ut, ut, ut, gates_r, gates_t, f_bias, norm_g.reshape(ML_WIDTH, 1))


def _pair_heads(t, axis):
    shape = t.shape
    t = t.reshape(shape[:axis] + (2, 2, WA_GROUP, HEAD_DIM) + shape[axis + 1:])
    return jnp.swapaxes(t, axis + 1, axis + 2).reshape(shape)


def _reorder_in_proj(w, b):
    def natural(t):
        ax = t.ndim - 1
        return jnp.concatenate(
            [t[..., 4624:7696],
             _pair_heads(t[..., 0:768], ax), _pair_heads(t[..., 1280:2048], ax), t[..., 768:1280],
             t[..., 2560:3072]], axis=ax)

    def feature_major(t):
        ax = t.ndim - 1
        return jnp.concatenate([t[..., 2048:2560], t[..., 3072:4608], t[..., 4608:4624]], axis=ax)

    return natural(w).astype(BF16), natural(b), feature_major(w).T.astype(BF16), feature_major(b)


def _reorder_out_proj(w):
    return jnp.concatenate([_pair_heads(w[:WA_WIDTH], 0), w[WA_WIDTH:]], axis=0).astype(BF16)


def kernel(x, emb_ln_g, emb_ln_b, w_in, b_in, w_out, b_out, ln_g, ln_b, t5_bias, sink, ml_f_bias,
           ml_norm_g, na_rpb):
    bsz, s, d = x.shape
    depth = w_in.shape[0]
    alpha = (2 * depth) ** 0.25
    m = bsz * s
    nc = s // ML_CHUNK
    assert d == D_MODEL and s % NA_BLOCK == 0 and s // GRID_W >= NA_KH_MAX and nc <= 128

    res, hb = _input_norm(x.reshape(m, d), emb_ln_g, emb_ln_b)
    bias_a = _window_bias(t5_bias)
    bias_c = _na_bias(na_rpb, s // GRID_W)
    for l in range(depth):
        wm, bm, wt, bt = _reorder_in_proj(w_in[l], b_in[l])
        u3 = _in_projection(hb, wm, bm).reshape(bsz, s, U_WIDTH)
        ut, gates = _t_projection(hb, wt, bt)
        gates = gates.reshape(16, bsz, nc, ML_CHUNK).transpose(1, 0, 2, 3)
        gates_t = jnp.pad(gates.transpose(0, 1, 3, 2), ((0, 0), (0, 0), (0, 0), (0, 128 - nc)))
        ya = _window_attention(u3, bias_a, sink[l])
        yb = _mlstm(u3, ut, gates, gates_t, ml_f_bias[l], ml_norm_g[l])
        yc = _neighbourhood_attention(u3, bias_c, l)
        last = l == depth - 1
        outs = _out_projection(ya.reshape(m, WA_WIDTH), yb.reshape(m, ML_WIDTH), yc.reshape(m, NA_WIDTH),
                               _reorder_out_proj(w_out[l]), b_out[l], res, ln_g[l], ln_b[l], alpha, not last)
        res = outs[0]
        hb = None if last else outs[1]
    return res.reshape(bsz, s, d)
```

```python
import functools
import math

import numpy as np
import jax
import jax.numpy as jnp
from jax import lax
from jax.experimental import pallas as pl
from jax.experimental.pallas import tpu as pltpu

F32 = jnp.float32
BF16 = jnp.bfloat16

D_MODEL = 2048
HEAD_DIM = 64
LN_EPS = 1e-5
NEG = -1e30
LOG2E = math.log2(math.e)
SM_CHUNK = 64

WA_HEADS = 12
WA_KV_HEADS = 4
WA_GROUP = WA_HEADS // WA_KV_HEADS
WA_WIDTH = WA_HEADS * HEAD_DIM
WA_KV_WIDTH = WA_KV_HEADS * HEAD_DIM
WA_BLOCK = 128
WINDOW = 128
T5_BUCKETS = 32
T5_MAX_DIST = 128
ML_HEADS = 4
ML_HEAD_DIM = 128
ML_WIDTH = ML_HEADS * ML_HEAD_DIM
ML_CHUNK = 128
ML_AUG = 2 * ML_HEAD_DIM
NA_HEADS = 12
NA_WIDTH = NA_HEADS * HEAD_DIM
GRID_W = 64
NA_KH_MAX = 8
NA_KW = 16
NA_ROWS = 4
NA_BLOCK = NA_ROWS * GRID_W

MIX_WIDTH = WA_WIDTH + ML_WIDTH + NA_WIDTH

C_Q, C_K, C_V, C_Z = 0, 768, 1536, 2304
A_Q, A_Z, A_K, A_V = 3072, 3840, 4608, 4864
B_K = 5120
U_WIDTH = 5632
T_WIDTH = 4 * ML_WIDTH

VMEM_LIMIT = 56 * 1024 * 1024
LN_TM = 512
IN_TM, IN_TN = 1024, 1408
OUT_TM = 512


def _cparams(sem):
    return pltpu.CompilerParams(dimension_semantics=sem, vmem_limit_bytes=VMEM_LIMIT)


def _dot_hi(a, b):
    return jnp.dot(a, b, preferred_element_type=F32, precision=lax.Precision.HIGHEST)


def _layer_norm_rows(x, g, b):
    mu = jnp.mean(x, axis=-1, keepdims=True)
    xc = x - mu
    var = jnp.mean(xc * xc, axis=-1, keepdims=True)
    return xc * lax.rsqrt(var + LN_EPS) * g + b


def _ln_kernel(x_ref, g_ref, b_ref, of_ref, ob_ref):
    y = _layer_norm_rows(x_ref[...].astype(F32), g_ref[...], b_ref[...])
    of_ref[...] = y
    ob_ref[...] = y.astype(BF16)


def _input_norm(x2, g, b):
    m, d = x2.shape
    tm = min(LN_TM, m)
    row = pl.BlockSpec((tm, d), lambda i: (i, 0))
    vec = pl.BlockSpec((1, d), lambda i: (0, 0))
    return pl.pallas_call(
        _ln_kernel,
        grid=(m // tm,),
        in_specs=[row, vec, vec],
        out_specs=[row, row],
        out_shape=[jax.ShapeDtypeStruct((m, d), F32), jax.ShapeDtypeStruct((m, d), BF16)],
        compiler_params=_cparams(("parallel",)),
        name="input_norm",
    )(x2, g.reshape(1, d), b.reshape(1, d))


def _inproj_kernel(h_ref, w_ref, b_ref, u_ref):
    acc = jnp.dot(h_ref[...], w_ref[...], preferred_element_type=F32)
    u_ref[...] = (acc + b_ref[...]).astype(u_ref.dtype)


def _in_projection(hb, w, b):
    m, d = hb.shape
    n = w.shape[1]
    tm, tn = min(IN_TM, m), IN_TN
    return pl.pallas_call(
        _inproj_kernel,
        grid=(n // tn, m // tm),
        in_specs=[pl.BlockSpec((tm, d), lambda j, i: (i, 0)),
                  pl.BlockSpec((d, tn), lambda j, i: (0, j)),
                  pl.BlockSpec((1, tn), lambda j, i: (0, j))],
        out_specs=pl.BlockSpec((tm, tn), lambda j, i: (i, j)),
        out_shape=jax.ShapeDtypeStruct((m, n), BF16),
        compiler_params=_cparams(("parallel", "parallel")),
        name="in_projection",
    )(hb, w, b.reshape(1, n))


def _tproj_kernel(h_ref, wt_ref, bt_ref, ut_ref, gate_ref):
    acc = lax.dot_general(wt_ref[...], h_ref[...], (((1,), (1,)), ((), ())),
                          preferred_element_type=F32) + bt_ref[...]
    for c in range(ut_ref.shape[0]):
        ut_ref[c] = acc[:T_WIDTH, c * ML_CHUNK:(c + 1) * ML_CHUNK].astype(BF16)
    gate_ref[...] = acc[T_WIDTH:]


def _t_projection(hb, wt, bt):
    m, d = hb.shape
    tm = min(IN_TM, m)
    rows = wt.shape[0]
    ngate = rows - T_WIDTH
    return pl.pallas_call(
        _tproj_kernel,
        grid=(m // tm,),
        in_specs=[pl.BlockSpec((tm, d), lambda i: (i, 0)),
                  pl.BlockSpec((rows, d), lambda i: (0, 0)),
                  pl.BlockSpec((rows, 1), lambda i: (0, 0))],
        out_specs=[pl.BlockSpec((tm // ML_CHUNK, T_WIDTH, ML_CHUNK), lambda i: (i, 0, 0)),
                   pl.BlockSpec((ngate, tm), lambda i: (0, i))],
        out_shape=[jax.ShapeDtypeStruct((m // ML_CHUNK, T_WIDTH, ML_CHUNK), BF16),
                   jax.ShapeDtypeStruct((ngate, m), F32)],
        compiler_params=_cparams(("parallel",)),
        name="t_projection",
    )(hb, wt, bt.reshape(rows, 1))


def _outproj_kernel(ya_ref, yb_ref, yc_ref, w_ref, b_ref, res_ref, g_ref, beta_ref, *out_refs, alpha):
    half = ya_ref.shape[0] // 2
    for rows in (slice(0, half), slice(half, 2 * half)):
        y = jnp.concatenate([ya_ref[rows, :], yb_ref[rows, :], yc_ref[rows, :]], axis=1)
        out = jnp.dot(y, w_ref[...], preferred_element_type=F32) + b_ref[...]
        r = _layer_norm_rows(alpha * res_ref[rows, :] + out, g_ref[...], beta_ref[...])
        out_refs[0][rows, :] = r
        if len(out_refs) > 1:
            out_refs[1][rows, :] = r.astype(BF16)


def _out_projection(ya, yb, yc, w, b, res, g, beta, alpha, want_bf16):
    m, d = res.shape
    tm = min(OUT_TM, m)
    row = lambda i: (i, 0)
    const = lambda i: (0, 0)
    out_specs = [pl.BlockSpec((tm, d), row)]
    out_shape = [jax.ShapeDtypeStruct((m, d), F32)]
    if want_bf16:
        out_specs.append(pl.BlockSpec((tm, d), row))
        out_shape.append(jax.ShapeDtypeStruct((m, d), BF16))
    vec = lambda v: v.reshape(1, d)
    return pl.pallas_call(
        functools.partial(_outproj_kernel, alpha=alpha),
        grid=(m // tm,),
        in_specs=[pl.BlockSpec((tm, WA_WIDTH), row),
                  pl.BlockSpec((tm, ML_WIDTH), row),
                  pl.BlockSpec((tm, NA_WIDTH), row),
                  pl.BlockSpec((MIX_WIDTH, d), const),
                  pl.BlockSpec((1, d), const),
                  pl.BlockSpec((tm, d), row),
                  pl.BlockSpec((1, d), const),
                  pl.BlockSpec((1, d), const)],
        out_specs=out_specs,
        out_shape=out_shape,
        compiler_params=_cparams(("parallel",)),
        name="out_projection",
    )(ya, yb, yc, w, vec(b), res, vec(g), vec(beta))


def _silu(z):
    return z * jax.nn.sigmoid(z)


def _scale_q(q):
    return (q.astype(F32) * (HEAD_DIM ** -0.5 * LOG2E)).astype(BF16)


def _wattn_kernel(q_ref, z_ref, kp_ref, kc_ref, kn_ref, vp_ref, vc_ref, vn_ref, bias_ref, sink_ref,
                  o_ref, s0_ref, s1_ref, p_ref):
    n = pl.program_id(1)
    blk = WA_BLOCK
    npair = WA_KV_HEADS // 2
    rows = 2 * WA_GROUP * blk
    lane = lax.broadcasted_iota(jnp.int32, (blk, 2 * HEAD_DIM), 1)
    left = lane < HEAD_DIM

    def step(s_read, s_write):
        for i in range(npair):
            ks = slice(i * 2 * HEAD_DIM, (i + 1) * 2 * HEAD_DIM)
            k3 = jnp.concatenate([kp_ref[:, ks], kc_ref[:, ks], kn_ref[:, ks]], axis=0)
            tiles = [_scale_q(q_ref[:, (WA_GROUP * i + j) * 2 * HEAD_DIM:(WA_GROUP * i + j + 1) * 2 * HEAD_DIM])
                     for j in range(WA_GROUP)]
            zero = jnp.zeros_like(tiles[0])
            lhs = jnp.concatenate([jnp.where(left, t, zero) for t in tiles]
                                  + [jnp.where(left, zero, t) for t in tiles], axis=0)
            s_write[i] = lax.dot_general(lhs, k3, (((1,), (1,)), ((), ())), preferred_element_type=F32)
            v3 = jnp.concatenate([vp_ref[:, ks], vc_ref[:, ks], vn_ref[:, ks]], axis=0)
            dens = []
            for c in range(rows // SM_CHUNK):
                rs = slice(c * SM_CHUNK, (c + 1) * SM_CHUNK)
                sk = sink_ref[0, 2 * WA_GROUP * i + c * SM_CHUNK // blk] * LOG2E
                sc = s_read[i, rs, :] + bias_ref[i, rs, :]
                m = jnp.maximum(jnp.max(sc, axis=-1, keepdims=True), sk)
                p = jnp.exp2(sc - m)
                dens.append(jnp.sum(p, axis=-1, keepdims=True) + jnp.exp2(sk - m))
                p_ref[i, rs, :] = p.astype(BF16)
            pv = jnp.dot(p_ref[i], v3, preferred_element_type=F32)
            o = jnp.concatenate([pv[c * SM_CHUNK:(c + 1) * SM_CHUNK] / dens[c]
                                 for c in range(rows // SM_CHUNK)], axis=0)
            for j in range(WA_GROUP):
                t = WA_GROUP * i + j
                ot = jnp.where(left, o[j * blk:(j + 1) * blk], o[(WA_GROUP + j) * blk:(WA_GROUP + j + 1) * blk])
                cs = slice(t * 2 * HEAD_DIM, (t + 1) * 2 * HEAD_DIM)
                o_ref[:, cs] = (ot * _silu(z_ref[:, cs].astype(F32))).astype(BF16)

    @pl.when(n == 0)
    def _():
        s1_ref[...] = jnp.zeros(s1_ref.shape, F32)

    @pl.when(n % 2 == 0)
    def _():
        step(s1_ref, s0_ref)

    @pl.when(n % 2 == 1)
    def _():
        step(s0_ref, s1_ref)


def _window_attention(u3, bias, sink):
    bsz, s, _ = u3.shape
    nb = s // WA_BLOCK
    qw, kw = WA_WIDTH, WA_KV_WIDTH
    npair = WA_KV_HEADS // 2
    rows = 2 * WA_GROUP * WA_BLOCK
    clamp = lambda i: jnp.clip(i, 0, nb - 1)
    kspec = lambda off, c: pl.BlockSpec((None, WA_BLOCK, kw), lambda b, n: (b, clamp(n + off), c))
    btype = lambda b, n: (jnp.where(n <= 1, 0, jnp.where(n == nb, 2, 1)), 0, 0, 0)
    return pl.pallas_call(
        _wattn_kernel,
        grid=(bsz, nb + 1),
        in_specs=[pl.BlockSpec((None, WA_BLOCK, qw), lambda b, n: (b, clamp(n), A_Q // qw)),
                  pl.BlockSpec((None, WA_BLOCK, qw), lambda b, n: (b, clamp(n - 1), A_Z // qw)),
                  kspec(-1, A_K // kw), kspec(0, A_K // kw), kspec(1, A_K // kw),
                  kspec(-2, A_V // kw), kspec(-1, A_V // kw), kspec(0, A_V // kw),
                  pl.BlockSpec((None, npair, rows, 3 * WA_BLOCK), btype),
                  pl.BlockSpec(memory_space=pltpu.SMEM)],
        out_specs=pl.BlockSpec((None, WA_BLOCK, qw), lambda b, n: (b, clamp(n - 1), 0)),
        out_shape=jax.ShapeDtypeStruct((bsz, s, qw), BF16),
        scratch_shapes=[pltpu.VMEM((npair, rows, 3 * WA_BLOCK), F32),
                        pltpu.VMEM((npair, rows, 3 * WA_BLOCK), F32),
                        pltpu.VMEM((npair, rows, 3 * WA_BLOCK), BF16)],
        compiler_params=_cparams(("parallel", "arbitrary")),
        name="window_attention",
    )(u3, u3, u3, u3, u3, u3, u3, u3, bias.reshape(3, npair, rows, 3 * WA_BLOCK), sink.reshape(1, WA_HEADS))


def _t5_bucket_np(rel):
    half = T5_BUCKETS // 2
    max_exact = half // 2
    ret = np.where(rel > 0, half, 0)
    n = np.abs(rel)
    nf = np.maximum(n, 1).astype(np.float64)
    v = np.log(nf / max_exact) / math.log(T5_MAX_DIST / max_exact) * (half - max_exact)
    vr = np.round(v)
    v = np.where(np.abs(v - vr) < 1e-9, vr, v)
    large = np.minimum(max_exact + np.trunc(v).astype(np.int64), half - 1)
    return ret + np.where(n < max_exact, n, large)


def _wbias_kernel(t5t_ref, bucket_ref, o_ref):
    width = 4 * WA_BLOCK
    bk = bucket_ref[...]
    e = lax.broadcasted_iota(jnp.int32, (T5_BUCKETS, width), 0)
    onehot = jnp.where(e == bk, 1.0, 0.0).astype(F32)
    g = _dot_hi(t5t_ref[...], onehot) * LOG2E + jnp.where(bk < 0, NEG, 0.0)
    col = lax.broadcasted_iota(jnp.int32, (WA_BLOCK, 3 * WA_BLOCK), 1)
    for h in range(WA_HEADS):
        row = jnp.broadcast_to(g[h:h + 1, :], (WA_BLOCK, width))
        t = pltpu.roll(row, 3 * WA_BLOCK, 1, stride=1, stride_axis=0)[:, :3 * WA_BLOCK]
        o_ref[0, h] = jnp.where(col < WA_BLOCK, NEG, t)
        o_ref[1, h] = t
        o_ref[2, h] = jnp.where(col >= 2 * WA_BLOCK, NEG, t)


def _window_bias(t5_table):
    rel = np.arange(4 * WA_BLOCK) - 2 * WA_BLOCK
    bucket = np.where(np.abs(rel) <= WINDOW, _t5_bucket_np(rel), -1).astype(np.int32)
    t5t = jnp.pad(t5_table.astype(F32).T, ((0, 16 - WA_HEADS), (0, 0)))
    return pl.pallas_call(
        _wbias_kernel,
        out_shape=jax.ShapeDtypeStruct((3, WA_HEADS, WA_BLOCK, 3 * WA_BLOCK), F32),
        name="window_bias",
    )(t5t, jnp.asarray(bucket).reshape(1, -1))


def _natten_kernel(q_ref, z_ref, kp_ref, kc_ref, kn_ref, vp_ref, vc_ref, vn_ref, bias_ref, o_ref,
                   s0_ref, s1_ref, p_ref):
    n = pl.program_id(1)
    blk = NA_BLOCK
    lane = lax.broadcasted_iota(jnp.int32, (blk, 2 * HEAD_DIM), 1)
    left = lane < HEAD_DIM

    ones_col = jnp.where(lax.broadcasted_iota(jnp.int32, (3 * blk, 2 * HEAD_DIM), 1) == 0, 1.0, 0.0).astype(BF16)

    def step(s_read, s_write):
        for i in range(NA_HEADS // 2):
            cs = slice(i * 2 * HEAD_DIM, (i + 1) * 2 * HEAD_DIM)
            k3 = jnp.concatenate([kp_ref[:, cs], kc_ref[:, cs], kn_ref[:, cs]], axis=0)
            t = _scale_q(q_ref[:, cs])
            zero = jnp.zeros_like(t)
            lhs = jnp.concatenate([jnp.where(left, t, zero), jnp.where(left, zero, t)], axis=0)
            s_write[i] = lax.dot_general(lhs, k3, (((1,), (1,)), ((), ())), preferred_element_type=F32)
            pb = p_ref.at[i % 3]
            for c in range(2 * blk // SM_CHUNK):
                rs = slice(c * SM_CHUNK, (c + 1) * SM_CHUNK)
                sc = s_read[i, rs, :] + bias_ref[i, rs, :]
                m = jnp.max(sc, axis=-1, keepdims=True)
                pb[rs, :] = jnp.exp2(sc - m).astype(BF16)
            if i > 0:
                weighted_values(i - 1)
        weighted_values(NA_HEADS // 2 - 1)

    def weighted_values(i):
        cs = slice(i * 2 * HEAD_DIM, (i + 1) * 2 * HEAD_DIM)
        v3 = jnp.concatenate([vp_ref[:, cs], vc_ref[:, cs], vn_ref[:, cs]], axis=0)
        vaug = jnp.concatenate([v3, ones_col], axis=1)
        out = jnp.dot(p_ref[i % 3], vaug, preferred_element_type=F32)
        o = out[:, :2 * HEAD_DIM] / out[:, 2 * HEAD_DIM:2 * HEAD_DIM + 1]
        ot = jnp.where(left, o[:blk], o[blk:])
        o_ref[:, cs] = (ot * _silu(z_ref[:, cs].astype(F32))).astype(BF16)

    @pl.when(n == 0)
    def _():
        s1_ref[...] = jnp.zeros(s1_ref.shape, F32)

    @pl.when(n % 2 == 0)
    def _():
        step(s1_ref, s0_ref)

    @pl.when(n % 2 == 1)
    def _():
        step(s0_ref, s1_ref)


def _neighbourhood_attention(u3, bias, layer):
    bsz, s, _ = u3.shape
    nblk = s // NA_BLOCK
    w = NA_WIDTH
    npair = NA_HEADS // 2
    clamp = lambda i: jnp.clip(i, 0, nblk - 1)
    spec = lambda off, c: pl.BlockSpec((None, NA_BLOCK, w), lambda b, n: (b, clamp(n + off), c))
    btype = lambda b, n: (layer, jnp.where(n <= 1, 0, jnp.where(n == nblk, 2, 1)), 0, 0, 0)
    sshape = (npair, 2 * NA_BLOCK, 3 * NA_BLOCK)
    return pl.pallas_call(
        _natten_kernel,
        grid=(bsz, nblk + 1),
        in_specs=[spec(0, C_Q // w), spec(-1, C_Z // w),
                  spec(-1, C_K // w), spec(0, C_K // w), spec(1, C_K // w),
                  spec(-2, C_V // w), spec(-1, C_V // w), spec(0, C_V // w),
                  pl.BlockSpec((None, None) + sshape, btype)],
        out_specs=pl.BlockSpec((None, NA_BLOCK, w), lambda b, n: (b, clamp(n - 1), 0)),
        out_shape=jax.ShapeDtypeStruct((bsz, s, w), BF16),
        scratch_shapes=[pltpu.VMEM(sshape, F32), pltpu.VMEM(sshape, F32),
                        pltpu.VMEM((3,) + sshape[1:], BF16)],
        compiler_params=_cparams(("parallel", "arbitrary")),
        name="neighbourhood_attention",
    )(u3, u3, u3, u3, u3, u3, u3, u3, bias.reshape(bias.shape[:2] + sshape))


def _na_valid_rows(rows):
    kh = min(NA_KH_MAX, rows)
    nblk = rows // NA_ROWS
    out = []
    for j in (0, min(1, nblk - 1), nblk - 1):
        r = NA_ROWS * j + np.arange(NA_ROWS)[:, None]
        kr = NA_ROWS * (j - 1) + np.arange(3 * NA_ROWS)[None, :]
        rs = np.clip(r - kh // 2, 0, rows - kh)
        out.append((kr >= rs) & (kr < rs + kh))
    return np.stack(out)


def _nabias_kernel(rpb_ref, o_ref, *, valid):
    w = GRID_W
    j = lax.broadcasted_iota(jnp.int32, (32, 2 * w), 1)
    e = lax.broadcasted_iota(jnp.int32, (32, 2 * w), 0)
    dc = jnp.clip(j - w, -(NA_KW - 1), NA_KW - 1) + NA_KW - 1
    g = _dot_hi(rpb_ref[...], jnp.where(e == dc, 1.0, 0.0).astype(F32)) * LOG2E
    lane = lax.broadcasted_iota(jnp.int32, (w, 2 * w), 1)
    qc = lax.broadcasted_iota(jnp.int32, (w, 2 * w), 0)
    kc = lane & (w - 1)
    col_start = jnp.clip(qc - NA_KW // 2, 0, w - NA_KW)
    col_ok = (kc >= col_start) & (kc < col_start + NA_KW)
    left = lane < w
    neg = jnp.full((w, 2 * w), NEG, F32)

    def toeplitz(dr, shift):
        row = jnp.broadcast_to(g[dr:dr + 1, :], (w, 2 * w))
        return pltpu.roll(row, shift, 1, stride=1, stride_axis=0)

    pair = [jnp.where(col_ok, jnp.where(left, toeplitz(d, w), toeplitz(d + 1, 0)), NEG)
            for d in range(2 * NA_KH_MAX - 2)]
    for ty in range(3):
        for rl in range(NA_ROWS):
            for t in range(3 * NA_ROWS // 2):
                d = 2 * t - rl + NA_KH_MAX - 1 - NA_ROWS
                v0, v1 = bool(valid[ty, rl, 2 * t]), bool(valid[ty, rl, 2 * t + 1])
                if v0 and v1:
                    tile = pair[d]
                elif v0:
                    tile = jnp.where(left, pair[d], NEG)
                elif v1:
                    tile = jnp.where(left, NEG, pair[d])
                else:
                    tile = neg
                o_ref[ty, rl * w:(rl + 1) * w, 2 * t * w:(2 * t + 2) * w] = tile


def _na_bias(rpb_all, rows):
    depth = rpb_all.shape[0]
    rpb_p = jnp.pad(rpb_all.astype(F32), ((0, 0), (0, 0), (0, 1), (0, 1)))
    return pl.pallas_call(
        functools.partial(_nabias_kernel, valid=_na_valid_rows(rows)),
        grid=(depth, NA_HEADS),
        in_specs=[pl.BlockSpec((None, None, 16, 32), lambda l, h: (l, h, 0, 0))],
        out_specs=pl.BlockSpec((None, 3, None, NA_BLOCK, 3 * NA_BLOCK), lambda l, h: (l, 0, h, 0, 0)),
        out_shape=jax.ShapeDtypeStruct((depth, 3, NA_HEADS, NA_BLOCK, 3 * NA_BLOCK), F32),
        compiler_params=_cparams(("parallel", "parallel")),
        name="na_bias",
    )(rpb_p)


def _log_sigmoid(x):
    return jnp.minimum(x, 0.0) - jnp.log(1.0 + jnp.exp(-jnp.abs(x)))


def _tri(n, upper):
    r = lax.broadcasted_iota(jnp.int32, (n, n), 0)
    c = lax.broadcasted_iota(jnp.int32, (n, n), 1)
    return jnp.where((r <= c) if upper else (r >= c), 1.0, 0.0).astype(F32)


def _scan_max(x, axis, reverse, size):
    idx = lax.broadcasted_iota(jnp.int32, x.shape, axis)
    k = 1
    while k < size:
        if reverse:
            shifted = pltpu.roll(x, x.shape[axis] - k, axis)
            ok = idx < size - k
        else:
            shifted = pltpu.roll(x, k, axis)
            ok = idx >= k
        x = jnp.maximum(x, jnp.where(ok, shifted, NEG))
        k *= 2
    return x


def _mlstm_kernel(k_ref, qt_ref, vt_ref, ot_ref, zt_ref, gr_ref, gt_ref, fb_ref, ng_ref, y_ref,
                  state_ref, sprev_ref, rows_ref, gain_ref, et_ref, w2a_ref, w2b_ref, *, seq):
    L = ML_CHUNK
    nc = seq // L
    head = pl.program_id(1)
    scale = ML_HEAD_DIM ** -0.5
    lane = lax.broadcasted_iota(jnp.int32, (1, 128), 1)

    for d in range(2):
        rev = d == 1
        fb = fb_ref[d, head]
        i_r = gr_ref[8 * d + head]
        lf_r = _log_sigmoid(gr_ref[8 * d + 4 + head] + fb)
        b_r = _dot_hi(lf_r, _tri(L, upper=not rev))
        g = jnp.broadcast_to(jnp.sum(lf_r, axis=1, keepdims=True), (nc, L))
        a_r = g - b_r + i_r
        m_loc = jnp.broadcast_to(jnp.max(a_r, axis=1, keepdims=True), (nc, L))
        jr = lax.broadcasted_iota(jnp.int32, (nc, nc), 0)
        jc = lax.broadcasted_iota(jnp.int32, (nc, nc), 1)
        before = jnp.where((jc > jr) if rev else (jc < jr), 1.0, 0.0).astype(F32)
        g_ex = _dot_hi(before, g)
        x = m_loc - (g_ex + g)
        row = lax.broadcasted_iota(jnp.int32, (nc, L), 0)
        if rev:
            x_prev = jnp.where(row < nc - 1, pltpu.roll(x, nc - 1, 0), NEG)
        else:
            x_prev = jnp.where(row >= 1, pltpu.roll(x, 1, 0), NEG)
        m_prev = g_ex + jnp.maximum(_scan_max(x_prev, 0, rev, nc), 0.0)
        m_after = jnp.maximum(g + m_prev, m_loc)
        gain_ref[d, 0] = jnp.exp(g + m_prev - m_after)
        gain_ref[d, 1] = jnp.exp(m_loc - m_after)
        mu = jnp.maximum(m_prev, _scan_max(i_r - b_r, 1, rev, L))
        rows_ref[d, 0] = mu
        rows_ref[d, 1] = jnp.exp(m_prev - mu) * scale
        rows_ref[d, 2] = jnp.exp(-b_r - mu)
        rows_ref[d, 3] = jnp.exp(a_r - m_loc)
        lf_t = _log_sigmoid(gt_ref[8 * d + 4 + head] + fb)
        et_ref[d] = gt_ref[8 * d + head] - _dot_hi(_tri(L, upper=rev), lf_t)
        state_ref[d] = jnp.zeros((ML_AUG, ML_HEAD_DIM), F32)

    rr = lax.broadcasted_iota(jnp.int32, (L, L), 0)
    cc = lax.broadcasted_iota(jnp.int32, (L, L), 1)
    ones_row = jnp.where(lax.broadcasted_iota(jnp.int32, (ML_HEAD_DIM, L), 0) == 0, 1.0, 0.0).astype(BF16)

    def state_step(j, carry):
        for d in range(2):
            c = j if d == 0 else nc - 1 - j
            kc = k_ref[pl.ds(pl.multiple_of(c * L, L), L), :]
            vaug_t = jnp.concatenate([vt_ref[c], ones_row], axis=0)
            wv_t = (rows_ref[d, 3, pl.ds(c, 1), :] * vaug_t.astype(F32)).astype(BF16)
            s_loc = jnp.dot(wv_t, kc, preferred_element_type=F32)
            st = state_ref[d]
            sprev_ref[d, c] = st.astype(BF16)
            state_ref[d] = gain_ref[d, 0, pl.ds(c, 1), :] * st + gain_ref[d, 1, pl.ds(c, 1), :] * s_loc
        return carry

    lax.fori_loop(0, nc, state_step, 0, unroll=4)

    ng_col = jnp.broadcast_to(ng_ref[...], (ML_HEAD_DIM, L))
    group = min(4, nc // 2)
    ngroups = nc // group

    def score_stage(g, w2_ref):
        for jj in range(group):
            c = g * group + jj
            kc = k_ref[pl.ds(pl.multiple_of(c * L, L), L), :]
            s_t = jnp.dot(kc, qt_ref[c], preferred_element_type=F32)
            for d in range(2):
                e_col = jnp.sum(jnp.where(lane == c, et_ref[d], 0.0), axis=1, keepdims=True)
                mask = (rr >= cc) if d == 1 else (rr <= cc)
                p_t = jnp.where(mask, jnp.exp(e_col - rows_ref[d, 0, pl.ds(c, 1), :]), 0.0)
                w2_ref[2 * jj + d] = (s_t * p_t * scale).astype(BF16)

    def value_stage(g, w2_ref):
        for jj in range(group):
            c = g * group + jj
            q_t = qt_ref[c]
            vaug_t = jnp.concatenate([vt_ref[c], ones_row], axis=0)
            hs = None
            for d in range(2):
                out_t = (jnp.dot(vaug_t, w2_ref[2 * jj + d], preferred_element_type=F32)
                         + rows_ref[d, 1, pl.ds(c, 1), :]
                         * jnp.dot(sprev_ref[d, c], q_t, preferred_element_type=F32))
                den = out_t[ML_HEAD_DIM:ML_HEAD_DIM + 1]
                h_d = out_t[:ML_HEAD_DIM] / jnp.maximum(jnp.abs(den), rows_ref[d, 2, pl.ds(c, 1), :])
                hs = h_d if hs is None else hs + h_d
            hs = jax.nn.sigmoid(ot_ref[c].astype(F32)) * hs
            mu = jnp.mean(hs, axis=0, keepdims=True)
            hc = hs - mu
            var = jnp.mean(hc * hc, axis=0, keepdims=True)
            y_t = hc * lax.rsqrt(var + LN_EPS) * ng_col * _silu(zt_ref[c].astype(F32))
            y_ref[pl.ds(pl.multiple_of(c * L, L), L), :] = y_t.T.astype(BF16)

    w2b_ref[...] = jnp.zeros(w2b_ref.shape, BF16)

    def pipeline_step(g, carry):
        @pl.when(g % 2 == 0)
        def _():
            score_stage(g, w2a_ref)
            value_stage(jnp.maximum(g - 1, 0), w2b_ref)

        @pl.when(g % 2 == 1)
        def _():
            score_stage(g, w2b_ref)
            value_stage(g - 1, w2a_ref)
        return carry

    lax.fori_loop(0, ngroups, pipeline_step, 0)
    value_stage(ngroups - 1, w2b_ref if (ngroups - 1) % 2 else w2a_ref)


def _mlstm(u3, ut, gates_r, gates_t, f_bias, norm_g):
    bsz, s, _ = u3.shape
    d = ML_HEAD_DIM
    nc = s // ML_CHUNK
    tcol = lambda part: pl.BlockSpec((nc, d, ML_CHUNK), lambda b, h: (b, part * ML_HEADS + h, 0))
    return pl.pallas_call(
        functools.partial(_mlstm_kernel, seq=s),
        grid=(bsz, ML_HEADS),
        in_specs=[pl.BlockSpec((None, s, d), lambda b, h: (b, 0, B_K // d + h)),
                  tcol(0), tcol(1), tcol(2), tcol(3),
                  pl.BlockSpec((None, 16, nc, ML_CHUNK), lambda b, h: (b, 0, 0, 0)),
                  pl.BlockSpec((None, 16, ML_CHUNK, 128), lambda b, h: (b, 0, 0, 0)),
                  pl.BlockSpec(memory_space=pltpu.SMEM),
                  pl.BlockSpec((d, 1), lambda b, h: (h, 0))],
        out_specs=pl.BlockSpec((None, s, d), lambda b, h: (b, 0, h)),
        out_shape=jax.ShapeDtypeStruct((bsz, s, ML_WIDTH), BF16),
        scratch_shapes=[pltpu.VMEM((2, ML_AUG, d), F32),
                        pltpu.VMEM((2, nc, ML_AUG, d), BF16),
                        pltpu.VMEM((2, 4, nc, ML_CHUNK), F32),
                        pltpu.VMEM((2, 2, nc, ML_CHUNK), F32),
                        pltpu.VMEM((2, ML_CHUNK, 128), F32),
                        pltpu.VMEM((2 * min(4, nc // 2), ML_CHUNK, ML_CHUNK), BF16),
                        pltpu.VMEM((2 * min(4, nc // 2), ML_CHUNK, ML_CHUNK), BF16)],
        compiler_params=_cparams(("parallel", "parallel")),
        name="mlstm",
    )(u3, ut, ut, ut, ut, gates_r, gates_t, f_bias, norm_g.reshape(ML_WIDTH, 1))


def _pair_heads(t, axis):
    shape = t.shape
    t = t.reshape(shape[:axis] + (2, 2, WA_GROUP, HEAD_DIM) + shape[axis + 1:])
    return jnp.swapaxes(t, axis + 1, axis + 2).reshape(shape)


def _reorder_in_proj(w, b):
    def natural(t):
        ax = t.ndim - 1
        return jnp.concatenate(
            [t[..., 4624:7696],
             _pair_heads(t[..., 0:768], ax), _pair_heads(t[..., 1280:2048], ax), t[..., 768:1280],
             t[..., 2560:3072]], axis=ax)

    def feature_major(t):
        ax = t.ndim - 1
        return jnp.concatenate([t[..., 2048:2560], t[..., 3072:4608], t[..., 4608:4624]], axis=ax)

    return natural(w).astype(BF16), natural(b), feature_major(w).T.astype(BF16), feature_major(b)


def _reorder_out_proj(w):
    return jnp.concatenate([_pair_heads(w[:WA_WIDTH], 0), w[WA_WIDTH:]], axis=0).astype(BF16)


def kernel(x, emb_ln_g, emb_ln_b, w_in, b_in, w_out, b_out, ln_g, ln_b, t5_bias, sink, ml_f_bias,
           ml_norm_g, na_rpb):
    bsz, s, d = x.shape
    depth = w_in.shape[0]
    alpha = (2 * depth) ** 0.25
    m = bsz * s
    nc = s // ML_CHUNK
    assert d == D_MODEL and s % NA_BLOCK == 0 and s // GRID_W >= NA_KH_MAX and nc <= 128

    res, hb = _input_norm(x.reshape(m, d), emb_ln_g, emb_ln_b)
    bias_a = _window_bias(t5_bias)
    bias_c = _na_bias(na_rpb, s // GRID_W)
    for l in range(depth):
        wm, bm, wt, bt = _reorder_in_proj(w_in[l], b_in[l])
        u3 = _in_projection(hb, wm, bm).reshape(bsz, s, U_WIDTH)
        ut, gates = _t_projection(hb, wt, bt)
        gates = gates.reshape(16, bsz, nc, ML_CHUNK).transpose(1, 0, 2, 3)
        gates_t = jnp.pad(gates.transpose(0, 1, 3, 2), ((0, 0), (0, 0), (0, 0), (0, 128 - nc)))
        ya = _window_attention(u3, bias_a, sink[l])
        yb = _mlstm(u3, ut, gates, gates_t, ml_f_bias[l], ml_norm_g[l])
        yc = _neighbourhood_attention(u3, bias_c, l)
        last = l == depth - 1
        outs = _out_projection(ya.reshape(m, WA_WIDTH), yb.reshape(m, ML_WIDTH), yc.reshape(m, NA_WIDTH),
                               _reorder_out_proj(w_out[l]), b_out[l], res, ln_g[l], ln_b[l], alpha, not last)
        res = outs[0]
        hb = None if last else outs[1]
    return res.reshape(bsz, s, d)
```

```python
import functools
import math

import numpy as np
import jax
import jax.numpy as jnp
from jax import lax
from jax.experimental import pallas as pl
from jax.experimental.pallas import tpu as pltpu

F32 = jnp.float32
BF16 = jnp.bfloat16

D_MODEL = 2048
HEAD_DIM = 64
LN_EPS = 1e-5
NEG = -1e30
LOG2E = math.log2(math.e)
SM_CHUNK = 64

WA_HEADS = 12
WA_KV_HEADS = 4
WA_GROUP = WA_HEADS // WA_KV_HEADS
WA_WIDTH = WA_HEADS * HEAD_DIM
WA_KV_WIDTH = WA_KV_HEADS * HEAD_DIM
WA_BLOCK = 128
WINDOW = 128
T5_BUCKETS = 32
T5_MAX_DIST = 128
ML_HEADS = 4
ML_HEAD_DIM = 128
ML_WIDTH = ML_HEADS * ML_HEAD_DIM
ML_CHUNK = 128
ML_AUG = 2 * ML_HEAD_DIM
NA_HEADS = 12
NA_WIDTH = NA_HEADS * HEAD_DIM
GRID_W = 64
NA_KH_MAX = 8
NA_KW = 16
NA_ROWS = 4
NA_BLOCK = NA_ROWS * GRID_W

MIX_WIDTH = WA_WIDTH + ML_WIDTH + NA_WIDTH

C_Q, C_K, C_V, C_Z = 0, 768, 1536, 2304
A_Q, A_Z, A_K, A_V = 3072, 3840, 4608, 4864
B_K = 5120
U_WIDTH = 5632
T_WIDTH = 4 * ML_WIDTH

VMEM_LIMIT = 56 * 1024 * 1024
LN_TM = 512
IN_TM, IN_TN = 1024, 1408
OUT_TM = 512


def _cparams(sem):
    return pltpu.CompilerParams(dimension_semantics=sem, vmem_limit_bytes=VMEM_LIMIT)


def _dot_hi(a, b):
    return jnp.dot(a, b, preferred_element_type=F32, precision=lax.Precision.HIGHEST)


def _layer_norm_rows(x, g, b):
    mu = jnp.mean(x, axis=-1, keepdims=True)
    xc = x - mu
    var = jnp.mean(xc * xc, axis=-1, keepdims=True)
    return xc * lax.rsqrt(var + LN_EPS) * g + b


def _ln_kernel(x_ref, g_ref, b_ref, of_ref, ob_ref):
    y = _layer_norm_rows(x_ref[...].astype(F32), g_ref[...], b_ref[...])
    of_ref[...] = y
    ob_ref[...] = y.astype(BF16)


def _input_norm(x2, g, b):
    m, d = x2.shape
    tm = min(LN_TM, m)
    row = pl.BlockSpec((tm, d), lambda i: (i, 0))
    vec = pl.BlockSpec((1, d), lambda i: (0, 0))
    return pl.pallas_call(
        _ln_kernel,
        grid=(m // tm,),
        in_specs=[row, vec, vec],
        out_specs=[row, row],
        out_shape=[jax.ShapeDtypeStruct((m, d), F32), jax.ShapeDtypeStruct((m, d), BF16)],
        compiler_params=_cparams(("parallel",)),
        name="input_norm",
    )(x2, g.reshape(1, d), b.reshape(1, d))


def _inproj_kernel(h_ref, w_ref, b_ref, u_ref):
    acc = jnp.dot(h_ref[...], w_ref[...], preferred_element_type=F32)
    u_ref[...] = (acc + b_ref[...]).astype(u_ref.dtype)


def _in_projection(hb, w, b):
    m, d = hb.shape
    n = w.shape[1]
    tm, tn = min(IN_TM, m), IN_TN
    return pl.pallas_call(
        _inproj_kernel,
        grid=(n // tn, m // tm),
        in_specs=[pl.BlockSpec((tm, d), lambda j, i: (i, 0)),
                  pl.BlockSpec((d, tn), lambda j, i: (0, j)),
                  pl.BlockSpec((1, tn), lambda j, i: (0, j))],
        out_specs=pl.BlockSpec((tm, tn), lambda j, i: (i, j)),
        out_shape=jax.ShapeDtypeStruct((m, n), BF16),
        compiler_params=_cparams(("parallel", "parallel")),
        name="in_projection",
    )(hb, w, b.reshape(1, n))


def _tproj_kernel(h_ref, wt_ref, bt_ref, ut_ref, gate_ref):
    acc = lax.dot_general(wt_ref[...], h_ref[...], (((1,), (1,)), ((), ())),
                          preferred_element_type=F32) + bt_ref[...]
    for c in range(ut_ref.shape[0]):
        ut_ref[c] = acc[:T_WIDTH, c * ML_CHUNK:(c + 1) * ML_CHUNK].astype(BF16)
    gate_ref[...] = acc[T_WIDTH:]


def _t_projection(hb, wt, bt):
    m, d = hb.shape
    tm = min(IN_TM, m)
    rows = wt.shape[0]
    ngate = rows - T_WIDTH
    return pl.pallas_call(
        _tproj_kernel,
        grid=(m // tm,),
        in_specs=[pl.BlockSpec((tm, d), lambda i: (i, 0)),
                  pl.BlockSpec((rows, d), lambda i: (0, 0)),
                  pl.BlockSpec((rows, 1), lambda i: (0, 0))],
        out_specs=[pl.BlockSpec((tm // ML_CHUNK, T_WIDTH, ML_CHUNK), lambda i: (i, 0, 0)),
                   pl.BlockSpec((ngate, tm), lambda i: (0, i))],
        out_shape=[jax.ShapeDtypeStruct((m // ML_CHUNK, T_WIDTH, ML_CHUNK), BF16),
                   jax.ShapeDtypeStruct((ngate, m), F32)],
        compiler_params=_cparams(("parallel",)),
        name="t_projection",
    )(hb, wt, bt.reshape(rows, 1))


def _outproj_kernel(ya_ref, yb_ref, yc_ref, w_ref, b_ref, res_ref, g_ref, beta_ref, *out_refs, alpha):
    half = ya_ref.shape[0] // 2
    for rows in (slice(0, half), slice(half, 2 * half)):
        y = jnp.concatenate([ya_ref[rows, :], yb_ref[rows, :], yc_ref[rows, :]], axis=1)
        out = jnp.dot(y, w_ref[...], preferred_element_type=F32) + b_ref[...]
        r = _layer_norm_rows(alpha * res_ref[rows, :] + out, g_ref[...], beta_ref[...])
        out_refs[0][rows, :] = r
        if len(out_refs) > 1:
            out_refs[1][rows, :] = r.astype(BF16)


def _out_projection(ya, yb, yc, w, b, res, g, beta, alpha, want_bf16):
    m, d = res.shape
    tm = min(OUT_TM, m)
    row = lambda i: (i, 0)
    const = lambda i: (0, 0)
    out_specs = [pl.BlockSpec((tm, d), row)]
    out_shape = [jax.ShapeDtypeStruct((m, d), F32)]
    if want_bf16:
        out_specs.append(pl.BlockSpec((tm, d), row))
        out_shape.append(jax.ShapeDtypeStruct((m, d), BF16))
    vec = lambda v: v.reshape(1, d)
    return pl.pallas_call(
        functools.partial(_outproj_kernel, alpha=alpha),
        grid=(m // tm,),
        in_specs=[pl.BlockSpec((tm, WA_WIDTH), row),
                  pl.BlockSpec((tm, ML_WIDTH), row),
                  pl.BlockSpec((tm, NA_WIDTH), row),
                  pl.BlockSpec((MIX_WIDTH, d), const),
                  pl.BlockSpec((1, d), const),
                  pl.BlockSpec((tm, d), row),
                  pl.BlockSpec((1, d), const),
                  pl.BlockSpec((1, d), const)],
        out_specs=out_specs,
        out_shape=out_shape,
        compiler_params=_cparams(("parallel",)),
        name="out_projection",
    )(ya, yb, yc, w, vec(b), res, vec(g), vec(beta))


def _silu(z):
    return z * jax.nn.sigmoid(z)


def _scale_q(q):
    return (q.astype(F32) * (HEAD_DIM ** -0.5 * LOG2E)).astype(BF16)


def _wattn_kernel(q_ref, z_ref, k0_ref, k1_ref, k2_ref, k3_ref, v0_ref, v1_ref, v2_ref, v3_ref,
                  bias_e_ref, bias_o_ref, sink_ref, o_ref, s0_ref, s1_ref, p_ref):
    n = pl.program_id(1)
    blk = WA_BLOCK
    npair = WA_KV_HEADS // 2
    rows = 2 * WA_GROUP * blk
    lane = lax.broadcasted_iota(jnp.int32, (blk, 2 * HEAD_DIM), 1)
    left = lane < HEAD_DIM
    k_refs = (k0_ref, k1_ref, k2_ref, k3_ref)
    v_refs = (v0_ref, v1_ref, v2_ref, v3_ref)

    def step(s_read, s_write):
        for half in range(2):
            qrows = slice(half * blk, (half + 1) * blk)
            bias_ref = (bias_e_ref, bias_o_ref)[half]
            for i in range(npair):
                ks = slice(i * 2 * HEAD_DIM, (i + 1) * 2 * HEAD_DIM)
                k3 = jnp.concatenate([r[:, ks] for r in k_refs[half:half + 3]], axis=0)
                tiles = [_scale_q(q_ref[qrows, (WA_GROUP * i + j) * 2 * HEAD_DIM:(WA_GROUP * i + j + 1) * 2 * HEAD_DIM])
                         for j in range(WA_GROUP)]
                zero = jnp.zeros_like(tiles[0])
                lhs = jnp.concatenate([jnp.where(left, t, zero) for t in tiles]
                                      + [jnp.where(left, zero, t) for t in tiles], axis=0)
                s_write[half, i] = lax.dot_general(lhs, k3, (((1,), (1,)), ((), ())),
                                                   preferred_element_type=F32)
                v3 = jnp.concatenate([r[:, ks] for r in v_refs[half:half + 3]], axis=0)
                dens = []
                for c in range(rows // SM_CHUNK):
                    rs = slice(c * SM_CHUNK, (c + 1) * SM_CHUNK)
                    sk = sink_ref[0, 2 * WA_GROUP * i + c * SM_CHUNK // blk] * LOG2E
                    sc = s_read[half, i, rs, :] + bias_ref[i, rs, :]
                    m = jnp.maximum(jnp.max(sc, axis=-1, keepdims=True), sk)
                    p = jnp.exp2(sc - m)
                    dens.append(jnp.sum(p, axis=-1, keepdims=True) + jnp.exp2(sk - m))
                    p_ref[half, i, rs, :] = p.astype(BF16)
                pv = jnp.dot(p_ref[half, i], v3, preferred_element_type=F32)
                o = jnp.concatenate([pv[c * SM_CHUNK:(c + 1) * SM_CHUNK] / dens[c]
                                     for c in range(rows // SM_CHUNK)], axis=0)
                for j in range(WA_GROUP):
                    t = WA_GROUP * i + j
                    ot = jnp.where(left, o[j * blk:(j + 1) * blk],
                                   o[(WA_GROUP + j) * blk:(WA_GROUP + j + 1) * blk])
                    cs = slice(t * 2 * HEAD_DIM, (t + 1) * 2 * HEAD_DIM)
                    o_ref[qrows, cs] = (ot * _silu(z_ref[qrows, cs].astype(F32))).astype(BF16)

    @pl.when(n == 0)
    def _():
        s1_ref[...] = jnp.zeros(s1_ref.shape, F32)

    @pl.when(n % 2 == 0)
    def _():
        step(s1_ref, s0_ref)

    @pl.when(n % 2 == 1)
    def _():
        step(s0_ref, s1_ref)


def _window_attention(u3, bias, sink):
    bsz, s, _ = u3.shape
    nb = s // WA_BLOCK
    npr = nb // 2
    qw, kw = WA_WIDTH, WA_KV_WIDTH
    npair = WA_KV_HEADS // 2
    rows = 2 * WA_GROUP * WA_BLOCK
    clamp = lambda i: jnp.clip(i, 0, nb - 1)
    pair = lambda i: jnp.clip(i, 0, npr - 1)
    kspec = lambda off, c: pl.BlockSpec((None, WA_BLOCK, kw), lambda b, n: (b, clamp(2 * n + off), c))
    vspec = lambda off, c: pl.BlockSpec((None, WA_BLOCK, kw), lambda b, n: (b, clamp(2 * n - 2 + off), c))
    even_type = lambda b, n: (jnp.where(n <= 1, 0, 1), 0, 0, 0)
    odd_type = lambda b, n: (jnp.where(n == npr, 2, 1), 0, 0, 0)
    sshape = (2, npair, rows, 3 * WA_BLOCK)
    bias = bias.reshape(3, npair, rows, 3 * WA_BLOCK)
    return pl.pallas_call(
        _wattn_kernel,
        grid=(bsz, npr + 1),
        in_specs=[pl.BlockSpec((None, 2 * WA_BLOCK, qw), lambda b, n: (b, pair(n), A_Q // qw)),
                  pl.BlockSpec((None, 2 * WA_BLOCK, qw), lambda b, n: (b, pair(n - 1), A_Z // qw)),
                  kspec(-1, A_K // kw), kspec(0, A_K // kw), kspec(1, A_K // kw), kspec(2, A_K // kw),
                  vspec(-1, A_V // kw), vspec(0, A_V // kw), vspec(1, A_V // kw), vspec(2, A_V // kw),
                  pl.BlockSpec((None, npair, rows, 3 * WA_BLOCK), even_type),
                  pl.BlockSpec((None, npair, rows, 3 * WA_BLOCK), odd_type),
                  pl.BlockSpec(memory_space=pltpu.SMEM)],
        out_specs=pl.BlockSpec((None, 2 * WA_BLOCK, qw), lambda b, n: (b, pair(n - 1), 0)),
        out_shape=jax.ShapeDtypeStruct((bsz, s, qw), BF16),
        scratch_shapes=[pltpu.VMEM(sshape, F32), pltpu.VMEM(sshape, F32), pltpu.VMEM(sshape, BF16)],
        compiler_params=_cparams(("parallel", "arbitrary")),
        name="window_attention",
    )(u3, u3, u3, u3, u3, u3, u3, u3, u3, u3, bias, bias, sink.reshape(1, WA_HEADS))


def _t5_bucket_np(rel):
    half = T5_BUCKETS // 2
    max_exact = half // 2
    ret = np.where(rel > 0, half, 0)
    n = np.abs(rel)
    nf = np.maximum(n, 1).astype(np.float64)
    v = np.log(nf / max_exact) / math.log(T5_MAX_DIST / max_exact) * (half - max_exact)
    vr = np.round(v)
    v = np.where(np.abs(v - vr) < 1e-9, vr, v)
    large = np.minimum(max_exact + np.trunc(v).astype(np.int64), half - 1)
    return ret + np.where(n < max_exact, n, large)


def _wbias_kernel(t5t_ref, bucket_ref, o_ref):
    width = 4 * WA_BLOCK
    bk = bucket_ref[...]
    e = lax.broadcasted_iota(jnp.int32, (T5_BUCKETS, width), 0)
    onehot = jnp.where(e == bk, 1.0, 0.0).astype(F32)
    g = _dot_hi(t5t_ref[...], onehot) * LOG2E + jnp.where(bk < 0, NEG, 0.0)
    col = lax.broadcasted_iota(jnp.int32, (WA_BLOCK, 3 * WA_BLOCK), 1)
    for h in range(WA_HEADS):
        row = jnp.broadcast_to(g[h:h + 1, :], (WA_BLOCK, width))
        t = pltpu.roll(row, 3 * WA_BLOCK, 1, stride=1, stride_axis=0)[:, :3 * WA_BLOCK]
        o_ref[0, h] = jnp.where(col < WA_BLOCK, NEG, t)
        o_ref[1, h] = t
        o_ref[2, h] = jnp.where(col >= 2 * WA_BLOCK, NEG, t)


def _window_bias(t5_table):
    rel = np.arange(4 * WA_BLOCK) - 2 * WA_BLOCK
    bucket = np.where(np.abs(rel) <= WINDOW, _t5_bucket_np(rel), -1).astype(np.int32)
    t5t = jnp.pad(t5_table.astype(F32).T, ((0, 16 - WA_HEADS), (0, 0)))
    return pl.pallas_call(
        _wbias_kernel,
        out_shape=jax.ShapeDtypeStruct((3, WA_HEADS, WA_BLOCK, 3 * WA_BLOCK), F32),
        name="window_bias",
    )(t5t, jnp.asarray(bucket).reshape(1, -1))


def _natten_kernel(q_ref, z_ref, kp_ref, kc_ref, kn_ref, vp_ref, vc_ref, vn_ref, bias_ref, o_ref,
                   s0_ref, s1_ref, p_ref):
    n = pl.program_id(1)
    blk = NA_BLOCK
    lane = lax.broadcasted_iota(jnp.int32, (blk, 2 * HEAD_DIM), 1)
    left = lane < HEAD_DIM

    ones_col = jnp.where(lax.broadcasted_iota(jnp.int32, (3 * blk, 2 * HEAD_DIM), 1) == 0, 1.0, 0.0).astype(BF16)

    def step(s_read, s_write):
        for i in range(NA_HEADS // 2):
            cs = slice(i * 2 * HEAD_DIM, (i + 1) * 2 * HEAD_DIM)
            k3 = jnp.concatenate([kp_ref[:, cs], kc_ref[:, cs], kn_ref[:, cs]], axis=0)
            t = _scale_q(q_ref[:, cs])
            zero = jnp.zeros_like(t)
            lhs = jnp.concatenate([jnp.where(left, t, zero), jnp.where(left, zero, t)], axis=0)
            s_write[i] = lax.dot_general(lhs, k3, (((1,), (1,)), ((), ())), preferred_element_type=F32)
            pb = p_ref.at[i % 3]
            for c in range(2 * blk // SM_CHUNK):
                rs = slice(c * SM_CHUNK, (c + 1) * SM_CHUNK)
                sc = s_read[i, rs, :] + bias_ref[i, rs, :]
                m = jnp.max(sc, axis=-1, keepdims=True)
                pb[rs, :] = jnp.exp2(sc - m).astype(BF16)
            if i > 0:
                weighted_values(i - 1)
        weighted_values(NA_HEADS // 2 - 1)

    def weighted_values(i):
        cs = slice(i * 2 * HEAD_DIM, (i + 1) * 2 * HEAD_DIM)
        v3 = jnp.concatenate([vp_ref[:, cs], vc_ref[:, cs], vn_ref[:, cs]], axis=0)
        vaug = jnp.concatenate([v3, ones_col], axis=1)
        out = jnp.dot(p_ref[i % 3], vaug, preferred_element_type=F32)
        o = out[:, :2 * HEAD_DIM] / out[:, 2 * HEAD_DIM:2 * HEAD_DIM + 1]
        ot = jnp.where(left, o[:blk], o[blk:])
        o_ref[:, cs] = (ot * _silu(z_ref[:, cs].astype(F32))).astype(BF16)

    @pl.when(n == 0)
    def _():
        s1_ref[...] = jnp.zeros(s1_ref.shape, F32)

    @pl.when(n % 2 == 0)
    def _():
        step(s1_ref, s0_ref)

    @pl.when(n % 2 == 1)
    def _():
        step(s0_ref, s1_ref)


def _neighbourhood_attention(u3, bias, layer):
    bsz, s, _ = u3.shape
    nblk = s // NA_BLOCK
    w = NA_WIDTH
    npair = NA_HEADS // 2
    clamp = lambda i: jnp.clip(i, 0, nblk - 1)
    spec = lambda off, c: pl.BlockSpec((None, NA_BLOCK, w), lambda b, n: (b, clamp(n + off), c))
    btype = lambda b, n: (layer, jnp.where(n <= 1, 0, jnp.where(n == nblk, 2, 1)), 0, 0, 0)
    sshape = (npair, 2 * NA_BLOCK, 3 * NA_BLOCK)
    return pl.pallas_call(
        _natten_kernel,
        grid=(bsz, nblk + 1),
        in_specs=[spec(0, C_Q // w), spec(-1, C_Z // w),
                  spec(-1, C_K // w), spec(0, C_K // w), spec(1, C_K // w),
                  spec(-2, C_V // w), spec(-1, C_V // w), spec(0, C_V // w),
                  pl.BlockSpec((None, None) + sshape, btype)],
        out_specs=pl.BlockSpec((None, NA_BLOCK, w), lambda b, n: (b, clamp(n - 1), 0)),
        out_shape=jax.ShapeDtypeStruct((bsz, s, w), BF16),
        scratch_shapes=[pltpu.VMEM(sshape, F32), pltpu.VMEM(sshape, F32),
                        pltpu.VMEM((3,) + sshape[1:], BF16)],
        compiler_params=_cparams(("parallel", "arbitrary")),
        name="neighbourhood_attention",
    )(u3, u3, u3, u3, u3, u3, u3, u3, bias.reshape(bias.shape[:2] + sshape))


def _na_valid_rows(rows):
    kh = min(NA_KH_MAX, rows)
    nblk = rows // NA_ROWS
    out = []
    for j in (0, min(1, nblk - 1), nblk - 1):
        r = NA_ROWS * j + np.arange(NA_ROWS)[:, None]
        kr = NA_ROWS * (j - 1) + np.arange(3 * NA_ROWS)[None, :]
        rs = np.clip(r - kh // 2, 0, rows - kh)
        out.append((kr >= rs) & (kr < rs + kh))
    return np.stack(out)


def _nabias_kernel(rpb_ref, o_ref, *, valid):
    w = GRID_W
    j = lax.broadcasted_iota(jnp.int32, (32, 2 * w), 1)
    e = lax.broadcasted_iota(jnp.int32, (32, 2 * w), 0)
    dc = jnp.clip(j - w, -(NA_KW - 1), NA_KW - 1) + NA_KW - 1
    g = _dot_hi(rpb_ref[...], jnp.where(e == dc, 1.0, 0.0).astype(F32)) * LOG2E
    lane = lax.broadcasted_iota(jnp.int32, (w, 2 * w), 1)
    qc = lax.broadcasted_iota(jnp.int32, (w, 2 * w), 0)
    kc = lane & (w - 1)
    col_start = jnp.clip(qc - NA_KW // 2, 0, w - NA_KW)
    col_ok = (kc >= col_start) & (kc < col_start + NA_KW)
    left = lane < w
    neg = jnp.full((w, 2 * w), NEG, F32)

    def toeplitz(dr, shift):
        row = jnp.broadcast_to(g[dr:dr + 1, :], (w, 2 * w))
        return pltpu.roll(row, shift, 1, stride=1, stride_axis=0)

    pair = [jnp.where(col_ok, jnp.where(left, toeplitz(d, w), toeplitz(d + 1, 0)), NEG)
            for d in range(2 * NA_KH_MAX - 2)]
    for ty in range(3):
        for rl in range(NA_ROWS):
            for t in range(3 * NA_ROWS // 2):
                d = 2 * t - rl + NA_KH_MAX - 1 - NA_ROWS
                v0, v1 = bool(valid[ty, rl, 2 * t]), bool(valid[ty, rl, 2 * t + 1])
                if v0 and v1:
                    tile = pair[d]
                elif v0:
                    tile = jnp.where(left, pair[d], NEG)
                elif v1:
                    tile = jnp.where(left, NEG, pair[d])
                else:
                    tile = neg
                o_ref[ty, rl * w:(rl + 1) * w, 2 * t * w:(2 * t + 2) * w] = tile


def _na_bias(rpb_all, rows):
    depth = rpb_all.shape[0]
    rpb_p = jnp.pad(rpb_all.astype(F32), ((0, 0), (0, 0), (0, 1), (0, 1)))
    return pl.pallas_call(
        functools.partial(_nabias_kernel, valid=_na_valid_rows(rows)),
        grid=(depth, NA_HEADS),
        in_specs=[pl.BlockSpec((None, None, 16, 32), lambda l, h: (l, h, 0, 0))],
        out_specs=pl.BlockSpec((None, 3, None, NA_BLOCK, 3 * NA_BLOCK), lambda l, h: (l, 0, h, 0, 0)),
        out_shape=jax.ShapeDtypeStruct((depth, 3, NA_HEADS, NA_BLOCK, 3 * NA_BLOCK), F32),
        compiler_params=_cparams(("parallel", "parallel")),
        name="na_bias",
    )(rpb_p)


def _log_sigmoid(x):
    return jnp.minimum(x, 0.0) - jnp.log(1.0 + jnp.exp(-jnp.abs(x)))


def _tri(n, upper):
    r = lax.broadcasted_iota(jnp.int32, (n, n), 0)
    c = lax.broadcasted_iota(jnp.int32, (n, n), 1)
    return jnp.where((r <= c) if upper else (r >= c), 1.0, 0.0).astype(F32)


def _scan_max(x, axis, reverse, size):
    idx = lax.broadcasted_iota(jnp.int32, x.shape, axis)
    k = 1
    while k < size:
        if reverse:
            shifted = pltpu.roll(x, x.shape[axis] - k, axis)
            ok = idx < size - k
        else:
            shifted = pltpu.roll(x, k, axis)
            ok = idx >= k
        x = jnp.maximum(x, jnp.where(ok, shifted, NEG))
        k *= 2
    return x


def _mlstm_kernel(k_ref, qt_ref, vt_ref, ot_ref, zt_ref, gr_ref, gt_ref, fb_ref, ng_ref, y_ref,
                  state_ref, sprev_ref, rows_ref, gain_ref, et_ref, w2a_ref, w2b_ref, *, seq):
    L = ML_CHUNK
    nc = seq // L
    head = pl.program_id(1)
    scale = ML_HEAD_DIM ** -0.5
    lane = lax.broadcasted_iota(jnp.int32, (1, 128), 1)

    for d in range(2):
        rev = d == 1
        fb = fb_ref[d, head]
        i_r = gr_ref[8 * d + head]
        lf_r = _log_sigmoid(gr_ref[8 * d + 4 + head] + fb)
        b_r = _dot_hi(lf_r, _tri(L, upper=not rev))
        g = jnp.broadcast_to(jnp.sum(lf_r, axis=1, keepdims=True), (nc, L))
        a_r = g - b_r + i_r
        m_loc = jnp.broadcast_to(jnp.max(a_r, axis=1, keepdims=True), (nc, L))
        jr = lax.broadcasted_iota(jnp.int32, (nc, nc), 0)
        jc = lax.broadcasted_iota(jnp.int32, (nc, nc), 1)
        before = jnp.where((jc > jr) if rev else (jc < jr), 1.0, 0.0).astype(F32)
        g_ex = _dot_hi(before, g)
        x = m_loc - (g_ex + g)
        row = lax.broadcasted_iota(jnp.int32, (nc, L), 0)
        if rev:
            x_prev = jnp.where(row < nc - 1, pltpu.roll(x, nc - 1, 0), NEG)
        else:
            x_prev = jnp.where(row >= 1, pltpu.roll(x, 1, 0), NEG)
        m_prev = g_ex + jnp.maximum(_scan_max(x_prev, 0, rev, nc), 0.0)
        m_after = jnp.maximum(g + m_prev, m_loc)
        gain_ref[d, 0] = jnp.exp(g + m_prev - m_after)
        gain_ref[d, 1] = jnp.exp(m_loc - m_after)
        mu = jnp.maximum(m_prev, _scan_max(i_r - b_r, 1, rev, L))
        rows_ref[d, 0] = mu
        rows_ref[d, 1] = jnp.exp(m_prev - mu) * scale
        rows_ref[d, 2] = jnp.exp(-b_r - mu)
        rows_ref[d, 3] = jnp.exp(a_r - m_loc)
        lf_t = _log_sigmoid(gt_ref[8 * d + 4 + head] + fb)
        et_ref[d] = gt_ref[8 * d + head] - _dot_hi(_tri(L, upper=rev), lf_t)
        state_ref[d] = jnp.zeros((ML_AUG, ML_HEAD_DIM), F32)

    rr = lax.broadcasted_iota(jnp.int32, (L, L), 0)
    cc = lax.broadcasted_iota(jnp.int32, (L, L), 1)
    ones_row = jnp.where(lax.broadcasted_iota(jnp.int32, (ML_HEAD_DIM, L), 0) == 0, 1.0, 0.0).astype(BF16)

    def state_step(j, carry):
        for d in range(2):
            c = j if d == 0 else nc - 1 - j
            kc = k_ref[pl.ds(pl.multiple_of(c * L, L), L), :]
            vaug_t = jnp.concatenate([vt_ref[c], ones_row], axis=0)
            wv_t = (rows_ref[d, 3, pl.ds(c, 1), :] * vaug_t.astype(F32)).astype(BF16)
            s_loc = jnp.dot(wv_t, kc, preferred_element_type=F32)
            st = state_ref[d]
            sprev_ref[d, c] = st.astype(BF16)
            state_ref[d] = gain_ref[d, 0, pl.ds(c, 1), :] * st + gain_ref[d, 1, pl.ds(c, 1), :] * s_loc
        return carry

    lax.fori_loop(0, nc, state_step, 0, unroll=4)

    ng_col = jnp.broadcast_to(ng_ref[...], (ML_HEAD_DIM, L))
    group = min(4, nc // 2)
    ngroups = nc // group

    def score_stage(g, w2_ref):
        for jj in range(group):
            c = g * group + jj
            kc = k_ref[pl.ds(pl.multiple_of(c * L, L), L), :]
            s_t = jnp.dot(kc, qt_ref[c], preferred_element_type=F32)
            for d in range(2):
                e_col = jnp.sum(jnp.where(lane == c, et_ref[d], 0.0), axis=1, keepdims=True)
                mask = (rr >= cc) if d == 1 else (rr <= cc)
                p_t = jnp.where(mask, jnp.exp(e_col - rows_ref[d, 0, pl.ds(c, 1), :]), 0.0)
                w2_ref[jj, :, d * L:(d + 1) * L] = (s_t * p_t * scale).astype(BF16)

    def value_stage(g, w2_ref):
        for jj in range(group):
            c = g * group + jj
            q_t = qt_ref[c]
            vaug_t = jnp.concatenate([vt_ref[c], ones_row], axis=0)
            intra = jnp.dot(vaug_t, w2_ref[jj], preferred_element_type=F32)
            hs = None
            for d in range(2):
                out_t = (intra[:, d * L:(d + 1) * L]
                         + rows_ref[d, 1, pl.ds(c, 1), :]
                         * jnp.dot(sprev_ref[d, c], q_t, preferred_element_type=F32))
                den = out_t[ML_HEAD_DIM:ML_HEAD_DIM + 1]
                h_d = out_t[:ML_HEAD_DIM] / jnp.maximum(jnp.abs(den), rows_ref[d, 2, pl.ds(c, 1), :])
                hs = h_d if hs is None else hs + h_d
            hs = jax.nn.sigmoid(ot_ref[c].astype(F32)) * hs
            mu = jnp.mean(hs, axis=0, keepdims=True)
            hc = hs - mu
            var = jnp.mean(hc * hc, axis=0, keepdims=True)
            y_t = hc * lax.rsqrt(var + LN_EPS) * ng_col * _silu(zt_ref[c].astype(F32))
            y_ref[pl.ds(pl.multiple_of(c * L, L), L), :] = y_t.T.astype(BF16)

    w2b_ref[...] = jnp.zeros(w2b_ref.shape, BF16)

    def pipeline_step(g, carry):
        @pl.when(g % 2 == 0)
        def _():
            score_stage(g, w2a_ref)
            value_stage(jnp.maximum(g - 1, 0), w2b_ref)

        @pl.when(g % 2 == 1)
        def _():
            score_stage(g, w2b_ref)
            value_stage(g - 1, w2a_ref)
        return carry

    lax.fori_loop(0, ngroups, pipeline_step, 0)
    value_stage(ngroups - 1, w2b_ref if (ngroups - 1) % 2 else w2a_ref)


def _mlstm(u3, ut, gates_r, gates_t, f_bias, norm_g):
    bsz, s, _ = u3.shape
    d = ML_HEAD_DIM
    nc = s // ML_CHUNK
    tcol = lambda part: pl.BlockSpec((nc, d, ML_CHUNK), lambda b, h: (b, part * ML_HEADS + h, 0))
    return pl.pallas_call(
        functools.partial(_mlstm_kernel, seq=s),
        grid=(bsz, ML_HEADS),
        in_specs=[pl.BlockSpec((None, s, d), lambda b, h: (b, 0, B_K // d + h)),
                  tcol(0), tcol(1), tcol(2), tcol(3),
                  pl.BlockSpec((None, 16, nc, ML_CHUNK), lambda b, h: (b, 0, 0, 0)),
                  pl.BlockSpec((None, 16, ML_CHUNK, 128), lambda b, h: (b, 0, 0, 0)),
                  pl.BlockSpec(memory_space=pltpu.SMEM),
                  pl.BlockSpec((d, 1), lambda b, h: (h, 0))],
        out_specs=pl.BlockSpec((None, s, d), lambda b, h: (b, 0, h)),
        out_shape=jax.ShapeDtypeStruct((bsz, s, ML_WIDTH), BF16),
        scratch_shapes=[pltpu.VMEM((2, ML_AUG, d), F32),
                        pltpu.VMEM((2, nc, ML_AUG, d), BF16),
                        pltpu.VMEM((2, 4, nc, ML_CHUNK), F32),
                        pltpu.VMEM((2, 2, nc, ML_CHUNK), F32),
                        pltpu.VMEM((2, ML_CHUNK, 128), F32),
                        pltpu.VMEM((min(4, nc // 2), ML_CHUNK, 2 * ML_CHUNK), BF16),
                        pltpu.VMEM((min(4, nc // 2), ML_CHUNK, 2 * ML_CHUNK), BF16)],
        compiler_params=_cparams(("parallel", "parallel")),
        name="mlstm",
    )(u3, ut, ut, ut, ut, gates_r, gates_t, f_bias, norm_g.reshape(ML_WIDTH, 1))


def _pair_heads(t, axis):
    shape = t.shape
    t = t.reshape(shape[:axis] + (2, 2, WA_GROUP, HEAD_DIM) + shape[axis + 1:])
    return jnp.swapaxes(t, axis + 1, axis + 2).reshape(shape)


def _reorder_in_proj(w, b):
    def natural(t):
        ax = t.ndim - 1
        return jnp.concatenate(
            [t[..., 4624:7696],
             _pair_heads(t[..., 0:768], ax), _pair_heads(t[..., 1280:2048], ax), t[..., 768:1280],
             t[..., 2560:3072]], axis=ax)

    def feature_major(t):
        ax = t.ndim - 1
        return jnp.concatenate([t[..., 2048:2560], t[..., 3072:4608], t[..., 4608:4624]], axis=ax)

    return natural(w).astype(BF16), natural(b), feature_major(w).astype(BF16).T, feature_major(b)


def _reorder_out_proj(w):
    return jnp.concatenate([_pair_heads(w[:WA_WIDTH], 0), w[WA_WIDTH:]], axis=0).astype(BF16)


def kernel(x, emb_ln_g, emb_ln_b, w_in, b_in, w_out, b_out, ln_g, ln_b, t5_bias, sink, ml_f_bias,
           ml_norm_g, na_rpb):
    bsz, s, d = x.shape
    depth = w_in.shape[0]
    alpha = (2 * depth) ** 0.25
    m = bsz * s
    nc = s // ML_CHUNK
    assert d == D_MODEL and s % NA_BLOCK == 0 and s // GRID_W >= NA_KH_MAX and nc <= 128

    res, hb = _input_norm(x.reshape(m, d), emb_ln_g, emb_ln_b)
    bias_a = _window_bias(t5_bias)
    bias_c = _na_bias(na_rpb, s // GRID_W)
    for l in range(depth):
        wm, bm, wt, bt = _reorder_in_proj(w_in[l], b_in[l])
        u3 = _in_projection(hb, wm, bm).reshape(bsz, s, U_WIDTH)
        ut, gates = _t_projection(hb, wt, bt)
        gates = gates.reshape(16, bsz, nc, ML_CHUNK).transpose(1, 0, 2, 3)
        gates_t = jnp.pad(gates.transpose(0, 1, 3, 2), ((0, 0), (0, 0), (0, 0), (0, 128 - nc)))
        ya = _window_attention(u3, bias_a, sink[l])
        yb = _mlstm(u3, ut, gates, gates_t, ml_f_bias[l], ml_norm_g[l])
        yc = _neighbourhood_attention(u3, bias_c, l)
        last = l == depth - 1
        outs = _out_projection(ya.reshape(m, WA_WIDTH), yb.reshape(m, ML_WIDTH), yc.reshape(m, NA_WIDTH),
                               _reorder_out_proj(w_out[l]), b_out[l], res, ln_g[l], ln_b[l], alpha, not last)
        res = outs[0]
        hb = None if last else outs[1]
    return res.reshape(bsz, s, d)
```

```python
import functools
import math

import numpy as np
import jax
import jax.numpy as jnp
from jax import lax
from jax.experimental import pallas as pl
from jax.experimental.pallas import tpu as pltpu

F32 = jnp.float32
BF16 = jnp.bfloat16

D_MODEL = 2048
HEAD_DIM = 64
LN_EPS = 1e-5
NEG = -1e30
LOG2E = math.log2(math.e)
SM_CHUNK = 64

WA_HEADS = 12
WA_KV_HEADS = 4
WA_GROUP = WA_HEADS // WA_KV_HEADS
WA_WIDTH = WA_HEADS * HEAD_DIM
WA_KV_WIDTH = WA_KV_HEADS * HEAD_DIM
WA_BLOCK = 128
WINDOW = 128
T5_BUCKETS = 32
T5_MAX_DIST = 128
ML_HEADS = 4
ML_HEAD_DIM = 128
ML_WIDTH = ML_HEADS * ML_HEAD_DIM
ML_CHUNK = 128
ML_AUG = 2 * ML_HEAD_DIM
ML_GATES = 4 * ML_HEADS
NA_HEADS = 12
NA_WIDTH = NA_HEADS * HEAD_DIM
GRID_W = 64
NA_KH_MAX = 8
NA_KW = 16
NA_ROWS = 4
NA_BLOCK = NA_ROWS * GRID_W

MIX_WIDTH = WA_WIDTH + ML_WIDTH + NA_WIDTH

C_Q, C_K, C_V, C_Z = 0, 768, 1536, 2304
A_Q, A_Z, A_K, A_V = 3072, 3840, 4608, 4864
B_K = 5120
U_WIDTH = 5632
T_WIDTH = 4 * ML_WIDTH

VMEM_LIMIT = 56 * 1024 * 1024
LN_TM = 512
IN_TM, IN_TN = 1024, 1408
OUT_TM = 512


def _cparams(sem):
    return pltpu.CompilerParams(dimension_semantics=sem, vmem_limit_bytes=VMEM_LIMIT)


def _dot_hi(a, b):
    return jnp.dot(a, b, preferred_element_type=F32, precision=lax.Precision.HIGHEST)


def _layer_norm_rows(x, g, b):
    mu = jnp.mean(x, axis=-1, keepdims=True)
    xc = x - mu
    var = jnp.mean(xc * xc, axis=-1, keepdims=True)
    return xc * lax.rsqrt(var + LN_EPS) * g + b


def _ln_kernel(x_ref, g_ref, b_ref, of_ref, ob_ref):
    y = _layer_norm_rows(x_ref[...].astype(F32), g_ref[...], b_ref[...])
    of_ref[...] = y
    ob_ref[...] = y.astype(BF16)


def _input_norm(x2, g, b):
    m, d = x2.shape
    tm = min(LN_TM, m)
    row = pl.BlockSpec((tm, d), lambda i: (i, 0))
    vec = pl.BlockSpec((1, d), lambda i: (0, 0))
    return pl.pallas_call(
        _ln_kernel,
        grid=(m // tm,),
        in_specs=[row, vec, vec],
        out_specs=[row, row],
        out_shape=[jax.ShapeDtypeStruct((m, d), F32), jax.ShapeDtypeStruct((m, d), BF16)],
        compiler_params=_cparams(("parallel",)),
        name="input_norm",
    )(x2, g.reshape(1, d), b.reshape(1, d))


def _inproj_kernel(h_ref, w_ref, b_ref, u_ref):
    acc = jnp.dot(h_ref[...], w_ref[...], preferred_element_type=F32)
    u_ref[...] = (acc + b_ref[...]).astype(u_ref.dtype)


def _in_projection(hb, w, b):
    m, d = hb.shape
    n = w.shape[1]
    tm, tn = min(IN_TM, m), IN_TN
    return pl.pallas_call(
        _inproj_kernel,
        grid=(n // tn, m // tm),
        in_specs=[pl.BlockSpec((tm, d), lambda j, i: (i, 0)),
                  pl.BlockSpec((d, tn), lambda j, i: (0, j)),
                  pl.BlockSpec((1, tn), lambda j, i: (0, j))],
        out_specs=pl.BlockSpec((tm, tn), lambda j, i: (i, j)),
        out_shape=jax.ShapeDtypeStruct((m, n), BF16),
        compiler_params=_cparams(("parallel", "parallel")),
        name="in_projection",
    )(hb, w, b.reshape(1, n))


def _tproj_kernel(h_ref, w_ref, b_ref, wg_ref, bg_ref, ut_ref, gate_ref):
    h = h_ref[...]
    acc = jnp.dot(h, w_ref[...], preferred_element_type=F32) + b_ref[...]
    for c in range(ut_ref.shape[0]):
        ut_ref[c] = acc[c * ML_CHUNK:(c + 1) * ML_CHUNK].T.astype(BF16)
    gate_ref[...] = lax.dot_general(wg_ref[...], h, (((1,), (1,)), ((), ())),
                                    preferred_element_type=F32) + bg_ref[...]


def _t_projection(hb, w, b, wg, bg):
    m, d = hb.shape
    tm = min(IN_TM, m)
    ngate = wg.shape[0]
    return pl.pallas_call(
        _tproj_kernel,
        grid=(m // tm,),
        in_specs=[pl.BlockSpec((tm, d), lambda i: (i, 0)),
                  pl.BlockSpec((d, T_WIDTH), lambda i: (0, 0)),
                  pl.BlockSpec((1, T_WIDTH), lambda i: (0, 0)),
                  pl.BlockSpec((ngate, d), lambda i: (0, 0)),
                  pl.BlockSpec((ngate, 1), lambda i: (0, 0))],
        out_specs=[pl.BlockSpec((tm // ML_CHUNK, T_WIDTH, ML_CHUNK), lambda i: (i, 0, 0)),
                   pl.BlockSpec((ngate, tm), lambda i: (0, i))],
        out_shape=[jax.ShapeDtypeStruct((m // ML_CHUNK, T_WIDTH, ML_CHUNK), BF16),
                   jax.ShapeDtypeStruct((ngate, m), F32)],
        compiler_params=_cparams(("parallel",)),
        name="t_projection",
    )(hb, w, b.reshape(1, T_WIDTH), wg, bg.reshape(ngate, 1))


def _outproj_kernel(ya_ref, yb_ref, yc_ref, w_ref, b_ref, res_ref, g_ref, beta_ref, *out_refs, alpha):
    half = ya_ref.shape[0] // 2
    for rows in (slice(0, half), slice(half, 2 * half)):
        y = jnp.concatenate([ya_ref[rows, :], yb_ref[rows, :], yc_ref[rows, :]], axis=1)
        out = jnp.dot(y, w_ref[...], preferred_element_type=F32) + b_ref[...]
        r = _layer_norm_rows(alpha * res_ref[rows, :] + out, g_ref[...], beta_ref[...])
        out_refs[0][rows, :] = r
        if len(out_refs) > 1:
            out_refs[1][rows, :] = r.astype(BF16)


def _out_projection(ya, yb, yc, w, b, res, g, beta, alpha, want_bf16):
    m, d = res.shape
    tm = min(OUT_TM, m)
    row = lambda i: (i, 0)
    const = lambda i: (0, 0)
    out_specs = [pl.BlockSpec((tm, d), row)]
    out_shape = [jax.ShapeDtypeStruct((m, d), F32)]
    if want_bf16:
        out_specs.append(pl.BlockSpec((tm, d), row))
        out_shape.append(jax.ShapeDtypeStruct((m, d), BF16))
    vec = lambda v: v.reshape(1, d)
    return pl.pallas_call(
        functools.partial(_outproj_kernel, alpha=alpha),
        grid=(m // tm,),
        in_specs=[pl.BlockSpec((tm, WA_WIDTH), row),
                  pl.BlockSpec((tm, ML_WIDTH), row),
                  pl.BlockSpec((tm, NA_WIDTH), row),
                  pl.BlockSpec((MIX_WIDTH, d), const),
                  pl.BlockSpec((1, d), const),
                  pl.BlockSpec((tm, d), row),
                  pl.BlockSpec((1, d), const),
                  pl.BlockSpec((1, d), const)],
        out_specs=out_specs,
        out_shape=out_shape,
        compiler_params=_cparams(("parallel",)),
        name="out_projection",
    )(ya, yb, yc, w, vec(b), res, vec(g), vec(beta))


def _silu(z):
    return z * jax.nn.sigmoid(z)


def _scale_q(q):
    return (q.astype(F32) * (HEAD_DIM ** -0.5 * LOG2E)).astype(BF16)


def _wattn_kernel(q_ref, z_ref, k0_ref, k1_ref, k2_ref, k3_ref, v0_ref, v1_ref, v2_ref, v3_ref,
                  bias_e_ref, bias_o_ref, sink_ref, o_ref, s0_ref, s1_ref, p_ref):
    n = pl.program_id(1)
    blk = WA_BLOCK
    npair = WA_KV_HEADS // 2
    rows = 2 * WA_GROUP * blk
    lane = lax.broadcasted_iota(jnp.int32, (blk, 2 * HEAD_DIM), 1)
    left = lane < HEAD_DIM
    k_refs = (k0_ref, k1_ref, k2_ref, k3_ref)
    v_refs = (v0_ref, v1_ref, v2_ref, v3_ref)

    def step(s_read, s_write):
        for half in range(2):
            qrows = slice(half * blk, (half + 1) * blk)
            bias_ref = (bias_e_ref, bias_o_ref)[half]
            for i in range(npair):
                ks = slice(i * 2 * HEAD_DIM, (i + 1) * 2 * HEAD_DIM)
                k3 = jnp.concatenate([r[:, ks] for r in k_refs[half:half + 3]], axis=0)
                tiles = [_scale_q(q_ref[qrows, (WA_GROUP * i + j) * 2 * HEAD_DIM:(WA_GROUP * i + j + 1) * 2 * HEAD_DIM])
                         for j in range(WA_GROUP)]
                zero = jnp.zeros_like(tiles[0])
                lhs = jnp.concatenate([jnp.where(left, t, zero) for t in tiles]
                                      + [jnp.where(left, zero, t) for t in tiles], axis=0)
                s_write[half, i] = lax.dot_general(lhs, k3, (((1,), (1,)), ((), ())),
                                                   preferred_element_type=F32)
                v3 = jnp.concatenate([r[:, ks] for r in v_refs[half:half + 3]], axis=0)
                dens = []
                for c in range(rows // SM_CHUNK):
                    rs = slice(c * SM_CHUNK, (c + 1) * SM_CHUNK)
                    sk = sink_ref[0, 2 * WA_GROUP * i + c * SM_CHUNK // blk] * LOG2E
                    sc = s_read[half, i, rs, :] + bias_ref[i, rs, :]
                    m = jnp.maximum(jnp.max(sc, axis=-1, keepdims=True), sk)
                    p = jnp.exp2(sc - m)
                    dens.append(jnp.sum(p, axis=-1, keepdims=True) + jnp.exp2(sk - m))
                    p_ref[half, i, rs, :] = p.astype(BF16)
                pv = jnp.dot(p_ref[half, i], v3, preferred_element_type=F32)
                o = jnp.concatenate([pv[c * SM_CHUNK:(c + 1) * SM_CHUNK] / dens[c]
                                     for c in range(rows // SM_CHUNK)], axis=0)
                for j in range(WA_GROUP):
                    t = WA_GROUP * i + j
                    ot = jnp.where(left, o[j * blk:(j + 1) * blk],
                                   o[(WA_GROUP + j) * blk:(WA_GROUP + j + 1) * blk])
                    cs = slice(t * 2 * HEAD_DIM, (t + 1) * 2 * HEAD_DIM)
                    o_ref[qrows, cs] = (ot * _silu(z_ref[qrows, cs].astype(F32))).astype(BF16)

    @pl.when(n == 0)
    def _():
        s1_ref[...] = jnp.zeros(s1_ref.shape, F32)

    @pl.when(n % 2 == 0)
    def _():
        step(s1_ref, s0_ref)

    @pl.when(n % 2 == 1)
    def _():
        step(s0_ref, s1_ref)


def _window_attention(u3, bias, sink):
    bsz, s, _ = u3.shape
    nb = s // WA_BLOCK
    npr = nb // 2
    qw, kw = WA_WIDTH, WA_KV_WIDTH
    npair = WA_KV_HEADS // 2
    rows = 2 * WA_GROUP * WA_BLOCK
    clamp = lambda i: jnp.clip(i, 0, nb - 1)
    pair = lambda i: jnp.clip(i, 0, npr - 1)
    kspec = lambda off, c: pl.BlockSpec((None, WA_BLOCK, kw), lambda b, n: (b, clamp(2 * n + off), c))
    vspec = lambda off, c: pl.BlockSpec((None, WA_BLOCK, kw), lambda b, n: (b, clamp(2 * n - 2 + off), c))
    even_type = lambda b, n: (jnp.where(n <= 1, 0, 1), 0, 0, 0)
    odd_type = lambda b, n: (jnp.where(n == npr, 2, 1), 0, 0, 0)
    sshape = (2, npair, rows, 3 * WA_BLOCK)
    bias = bias.reshape(3, npair, rows, 3 * WA_BLOCK)
    return pl.pallas_call(
        _wattn_kernel,
        grid=(bsz, npr + 1),
        in_specs=[pl.BlockSpec((None, 2 * WA_BLOCK, qw), lambda b, n: (b, pair(n), A_Q // qw)),
                  pl.BlockSpec((None, 2 * WA_BLOCK, qw), lambda b, n: (b, pair(n - 1), A_Z // qw)),
                  kspec(-1, A_K // kw), kspec(0, A_K // kw), kspec(1, A_K // kw), kspec(2, A_K // kw),
                  vspec(-1, A_V // kw), vspec(0, A_V // kw), vspec(1, A_V // kw), vspec(2, A_V // kw),
                  pl.BlockSpec((None, npair, rows, 3 * WA_BLOCK), even_type),
                  pl.BlockSpec((None, npair, rows, 3 * WA_BLOCK), odd_type),
                  pl.BlockSpec(memory_space=pltpu.SMEM)],
        out_specs=pl.BlockSpec((None, 2 * WA_BLOCK, qw), lambda b, n: (b, pair(n - 1), 0)),
        out_shape=jax.ShapeDtypeStruct((bsz, s, qw), BF16),
        scratch_shapes=[pltpu.VMEM(sshape, F32), pltpu.VMEM(sshape, F32), pltpu.VMEM(sshape, BF16)],
        compiler_params=_cparams(("parallel", "arbitrary")),
        name="window_attention",
    )(u3, u3, u3, u3, u3, u3, u3, u3, u3, u3, bias, bias, sink.reshape(1, WA_HEADS))


def _t5_bucket_np(rel):
    half = T5_BUCKETS // 2
    max_exact = half // 2
    ret = np.where(rel > 0, half, 0)
    n = np.abs(rel)
    nf = np.maximum(n, 1).astype(np.float64)
    v = np.log(nf / max_exact) / math.log(T5_MAX_DIST / max_exact) * (half - max_exact)
    vr = np.round(v)
    v = np.where(np.abs(v - vr) < 1e-9, vr, v)
    large = np.minimum(max_exact + np.trunc(v).astype(np.int64), half - 1)
    return ret + np.where(n < max_exact, n, large)


def _wbias_kernel(t5t_ref, bucket_ref, o_ref):
    width = 4 * WA_BLOCK
    bk = bucket_ref[...]
    e = lax.broadcasted_iota(jnp.int32, (T5_BUCKETS, width), 0)
    onehot = jnp.where(e == bk, 1.0, 0.0).astype(F32)
    g = _dot_hi(t5t_ref[...], onehot) * LOG2E + jnp.where(bk < 0, NEG, 0.0)
    col = lax.broadcasted_iota(jnp.int32, (WA_BLOCK, 3 * WA_BLOCK), 1)
    for h in range(WA_HEADS):
        row = jnp.broadcast_to(g[h:h + 1, :], (WA_BLOCK, width))
        t = pltpu.roll(row, 3 * WA_BLOCK, 1, stride=1, stride_axis=0)[:, :3 * WA_BLOCK]
        o_ref[0, h] = jnp.where(col < WA_BLOCK, NEG, t)
        o_ref[1, h] = t
        o_ref[2, h] = jnp.where(col >= 2 * WA_BLOCK, NEG, t)


def _window_bias(t5_table):
    rel = np.arange(4 * WA_BLOCK) - 2 * WA_BLOCK
    bucket = np.where(np.abs(rel) <= WINDOW, _t5_bucket_np(rel), -1).astype(np.int32)
    t5t = jnp.pad(t5_table.astype(F32).T, ((0, 16 - WA_HEADS), (0, 0)))
    return pl.pallas_call(
        _wbias_kernel,
        out_shape=jax.ShapeDtypeStruct((3, WA_HEADS, WA_BLOCK, 3 * WA_BLOCK), F32),
        name="window_bias",
    )(t5t, jnp.asarray(bucket).reshape(1, -1))


def _natten_kernel(q_ref, z_ref, kp_ref, kc_ref, kn_ref, vp_ref, vc_ref, vn_ref, bias_ref, o_ref,
                   s0_ref, s1_ref, p_ref):
    n = pl.program_id(1)
    blk = NA_BLOCK
    lane = lax.broadcasted_iota(jnp.int32, (blk, 2 * HEAD_DIM), 1)
    left = lane < HEAD_DIM

    ones_col = jnp.where(lax.broadcasted_iota(jnp.int32, (3 * blk, 2 * HEAD_DIM), 1) == 0, 1.0, 0.0).astype(BF16)

    def step(s_read, s_write):
        for i in range(NA_HEADS // 2):
            cs = slice(i * 2 * HEAD_DIM, (i + 1) * 2 * HEAD_DIM)
            k3 = jnp.concatenate([kp_ref[:, cs], kc_ref[:, cs], kn_ref[:, cs]], axis=0)
            t = _scale_q(q_ref[:, cs])
            zero = jnp.zeros_like(t)
            lhs = jnp.concatenate([jnp.where(left, t, zero), jnp.where(left, zero, t)], axis=0)
            s_write[i] = lax.dot_general(lhs, k3, (((1,), (1,)), ((), ())), preferred_element_type=F32)
            pb = p_ref.at[i % 3]
            for c in range(2 * blk // SM_CHUNK):
                rs = slice(c * SM_CHUNK, (c + 1) * SM_CHUNK)
                sc = s_read[i, rs, :] + bias_ref[i, rs, :]
                m = jnp.max(sc, axis=-1, keepdims=True)
                pb[rs, :] = jnp.exp2(sc - m).astype(BF16)
            if i > 0:
                weighted_values(i - 1)
        weighted_values(NA_HEADS // 2 - 1)

    def weighted_values(i):
        cs = slice(i * 2 * HEAD_DIM, (i + 1) * 2 * HEAD_DIM)
        v3 = jnp.concatenate([vp_ref[:, cs], vc_ref[:, cs], vn_ref[:, cs]], axis=0)
        vaug = jnp.concatenate([v3, ones_col], axis=1)
        out = jnp.dot(p_ref[i % 3], vaug, preferred_element_type=F32)
        o = out[:, :2 * HEAD_DIM] / out[:, 2 * HEAD_DIM:2 * HEAD_DIM + 1]
        ot = jnp.where(left, o[:blk], o[blk:])
        o_ref[:, cs] = (ot * _silu(z_ref[:, cs].astype(F32))).astype(BF16)

    @pl.when(n == 0)
    def _():
        s1_ref[...] = jnp.zeros(s1_ref.shape, F32)

    @pl.when(n % 2 == 0)
    def _():
        step(s1_ref, s0_ref)

    @pl.when(n % 2 == 1)
    def _():
        step(s0_ref, s1_ref)


def _neighbourhood_attention(u3, bias, layer):
    bsz, s, _ = u3.shape
    nblk = s // NA_BLOCK
    w = NA_WIDTH
    npair = NA_HEADS // 2
    clamp = lambda i: jnp.clip(i, 0, nblk - 1)
    spec = lambda off, c: pl.BlockSpec((None, NA_BLOCK, w), lambda b, n: (b, clamp(n + off), c))
    btype = lambda b, n: (layer, jnp.where(n <= 1, 0, jnp.where(n == nblk, 2, 1)), 0, 0, 0)
    sshape = (npair, 2 * NA_BLOCK, 3 * NA_BLOCK)
    return pl.pallas_call(
        _natten_kernel,
        grid=(bsz, nblk + 1),
        in_specs=[spec(0, C_Q // w), spec(-1, C_Z // w),
                  spec(-1, C_K // w), spec(0, C_K // w), spec(1, C_K // w),
                  spec(-2, C_V // w), spec(-1, C_V // w), spec(0, C_V // w),
                  pl.BlockSpec((None, None) + sshape, btype)],
        out_specs=pl.BlockSpec((None, NA_BLOCK, w), lambda b, n: (b, clamp(n - 1), 0)),
        out_shape=jax.ShapeDtypeStruct((bsz, s, w), BF16),
        scratch_shapes=[pltpu.VMEM(sshape, F32), pltpu.VMEM(sshape, F32),
                        pltpu.VMEM((3,) + sshape[1:], BF16)],
        compiler_params=_cparams(("parallel", "arbitrary")),
        name="neighbourhood_attention",
    )(u3, u3, u3, u3, u3, u3, u3, u3, bias.reshape(bias.shape[:2] + sshape))


def _na_valid_rows(rows):
    kh = min(NA_KH_MAX, rows)
    nblk = rows // NA_ROWS
    out = []
    for j in (0, min(1, nblk - 1), nblk - 1):
        r = NA_ROWS * j + np.arange(NA_ROWS)[:, None]
        kr = NA_ROWS * (j - 1) + np.arange(3 * NA_ROWS)[None, :]
        rs = np.clip(r - kh // 2, 0, rows - kh)
        out.append((kr >= rs) & (kr < rs + kh))
    return np.stack(out)


def _nabias_kernel(rpb_ref, o_ref, *, valid):
    w = GRID_W
    j = lax.broadcasted_iota(jnp.int32, (32, 2 * w), 1)
    e = lax.broadcasted_iota(jnp.int32, (32, 2 * w), 0)
    dc = jnp.clip(j - w, -(NA_KW - 1), NA_KW - 1) + NA_KW - 1
    g = _dot_hi(rpb_ref[...], jnp.where(e == dc, 1.0, 0.0).astype(F32)) * LOG2E
    lane = lax.broadcasted_iota(jnp.int32, (w, 2 * w), 1)
    qc = lax.broadcasted_iota(jnp.int32, (w, 2 * w), 0)
    kc = lane & (w - 1)
    col_start = jnp.clip(qc - NA_KW // 2, 0, w - NA_KW)
    col_ok = (kc >= col_start) & (kc < col_start + NA_KW)
    left = lane < w
    neg = jnp.full((w, 2 * w), NEG, F32)

    def toeplitz(dr, shift):
        row = jnp.broadcast_to(g[dr:dr + 1, :], (w, 2 * w))
        return pltpu.roll(row, shift, 1, stride=1, stride_axis=0)

    pair = [jnp.where(col_ok, jnp.where(left, toeplitz(d, w), toeplitz(d + 1, 0)), NEG)
            for d in range(2 * NA_KH_MAX - 2)]
    for ty in range(3):
        for rl in range(NA_ROWS):
            for t in range(3 * NA_ROWS // 2):
                d = 2 * t - rl + NA_KH_MAX - 1 - NA_ROWS
                v0, v1 = bool(valid[ty, rl, 2 * t]), bool(valid[ty, rl, 2 * t + 1])
                if v0 and v1:
                    tile = pair[d]
                elif v0:
                    tile = jnp.where(left, pair[d], NEG)
                elif v1:
                    tile = jnp.where(left, NEG, pair[d])
                else:
                    tile = neg
                o_ref[ty, rl * w:(rl + 1) * w, 2 * t * w:(2 * t + 2) * w] = tile


def _na_bias(rpb_all, rows):
    depth = rpb_all.shape[0]
    rpb_p = jnp.pad(rpb_all.astype(F32), ((0, 0), (0, 0), (0, 1), (0, 1)))
    return pl.pallas_call(
        functools.partial(_nabias_kernel, valid=_na_valid_rows(rows)),
        grid=(depth, NA_HEADS),
        in_specs=[pl.BlockSpec((None, None, 16, 32), lambda l, h: (l, h, 0, 0))],
        out_specs=pl.BlockSpec((None, 3, None, NA_BLOCK, 3 * NA_BLOCK), lambda l, h: (l, 0, h, 0, 0)),
        out_shape=jax.ShapeDtypeStruct((depth, 3, NA_HEADS, NA_BLOCK, 3 * NA_BLOCK), F32),
        compiler_params=_cparams(("parallel", "parallel")),
        name="na_bias",
    )(rpb_p)


def _log_sigmoid(x):
    return jnp.minimum(x, 0.0) - jnp.log(1.0 + jnp.exp(-jnp.abs(x)))


def _tri(n, upper):
    r = lax.broadcasted_iota(jnp.int32, (n, n), 0)
    c = lax.broadcasted_iota(jnp.int32, (n, n), 1)
    return jnp.where((r <= c) if upper else (r >= c), 1.0, 0.0).astype(F32)


def _scan_max(x, axis, reverse, size):
    idx = lax.broadcasted_iota(jnp.int32, x.shape, axis)
    k = 1
    while k < size:
        if reverse:
            shifted = pltpu.roll(x, x.shape[axis] - k, axis)
            ok = idx < size - k
        else:
            shifted = pltpu.roll(x, k, axis)
            ok = idx >= k
        x = jnp.maximum(x, jnp.where(ok, shifted, NEG))
        k *= 2
    return x


def _mlstm_kernel(k_ref, qt_ref, vt_ref, ot_ref, zt_ref, gr_ref, fb_ref, ng_ref, y_ref,
                  state_ref, sprev_ref, rows_ref, gain_ref, et_ref, w2a_ref, w2b_ref, *, seq):
    L = ML_CHUNK
    nc = seq // L
    head = pl.program_id(1)
    scale = ML_HEAD_DIM ** -0.5
    lane = lax.broadcasted_iota(jnp.int32, (1, 128), 1)

    for d in range(2):
        rev = d == 1
        fb = fb_ref[d, head]
        i_r = gr_ref[8 * d + head]
        lf_r = _log_sigmoid(gr_ref[8 * d + 4 + head] + fb)
        b_r = _dot_hi(lf_r, _tri(L, upper=not rev))
        g = jnp.broadcast_to(jnp.sum(lf_r, axis=1, keepdims=True), (nc, L))
        a_r = g - b_r + i_r
        m_loc = jnp.broadcast_to(jnp.max(a_r, axis=1, keepdims=True), (nc, L))
        jr = lax.broadcasted_iota(jnp.int32, (nc, nc), 0)
        jc = lax.broadcasted_iota(jnp.int32, (nc, nc), 1)
        before = jnp.where((jc > jr) if rev else (jc < jr), 1.0, 0.0).astype(F32)
        g_ex = _dot_hi(before, g)
        x = m_loc - (g_ex + g)
        row = lax.broadcasted_iota(jnp.int32, (nc, L), 0)
        if rev:
            x_prev = jnp.where(row < nc - 1, pltpu.roll(x, nc - 1, 0), NEG)
        else:
            x_prev = jnp.where(row >= 1, pltpu.roll(x, 1, 0), NEG)
        m_prev = g_ex + jnp.maximum(_scan_max(x_prev, 0, rev, nc), 0.0)
        m_after = jnp.maximum(g + m_prev, m_loc)
        gain_ref[d, 0] = jnp.exp(g + m_prev - m_after)
        gain_ref[d, 1] = jnp.exp(m_loc - m_after)
        e_r = i_r - b_r
        mu = jnp.maximum(m_prev, _scan_max(e_r, 1, rev, L))
        rows_ref[d, 0] = mu
        rows_ref[d, 1] = jnp.exp(m_prev - mu) * scale
        rows_ref[d, 2] = jnp.exp(-b_r - mu)
        rows_ref[d, 3] = jnp.exp(a_r - m_loc)
        e_pad = jnp.concatenate([e_r, jnp.zeros((128 - nc, L), F32)], axis=0) if nc < 128 else e_r
        et_ref[d] = e_pad.T
        state_ref[d] = jnp.zeros((ML_AUG, ML_HEAD_DIM), F32)

    rr = lax.broadcasted_iota(jnp.int32, (L, L), 0)
    cc = lax.broadcasted_iota(jnp.int32, (L, L), 1)
    ones_row = jnp.where(lax.broadcasted_iota(jnp.int32, (ML_HEAD_DIM, L), 0) == 0, 1.0, 0.0).astype(BF16)

    def state_step(j, carry):
        for d in range(2):
            c = j if d == 0 else nc - 1 - j
            kc = k_ref[pl.ds(pl.multiple_of(c * L, L), L), :]
            vaug_t = jnp.concatenate([vt_ref[c], ones_row], axis=0)
            wv_t = (rows_ref[d, 3, pl.ds(c, 1), :] * vaug_t.astype(F32)).astype(BF16)
            s_loc = jnp.dot(wv_t, kc, preferred_element_type=F32)
            st = state_ref[d]
            sprev_ref[d, c] = st.astype(BF16)
            state_ref[d] = gain_ref[d, 0, pl.ds(c, 1), :] * st + gain_ref[d, 1, pl.ds(c, 1), :] * s_loc
        return carry

    lax.fori_loop(0, nc, state_step, 0, unroll=4)

    ng_col = jnp.broadcast_to(ng_ref[...], (ML_HEAD_DIM, L))
    group = min(4, nc // 2)
    ngroups = nc // group

    def score_stage(g, w2_ref):
        for jj in range(group):
            c = g * group + jj
            kc = k_ref[pl.ds(pl.multiple_of(c * L, L), L), :]
            s_t = jnp.dot(kc, qt_ref[c], preferred_element_type=F32)
            for d in range(2):
                e_col = jnp.sum(jnp.where(lane == c, et_ref[d], 0.0), axis=1, keepdims=True)
                mask = (rr >= cc) if d == 1 else (rr <= cc)
                p_t = jnp.where(mask, jnp.exp(e_col - rows_ref[d, 0, pl.ds(c, 1), :]), 0.0)
                w2_ref[2 * jj + d] = (s_t * p_t * scale).astype(BF16)

    def value_stage(g, w2_ref):
        for jj in range(group):
            c = g * group + jj
            q_t = qt_ref[c]
            vaug_t = jnp.concatenate([vt_ref[c], ones_row], axis=0)
            hs = None
            for d in range(2):
                out_t = (jnp.dot(vaug_t, w2_ref[2 * jj + d], preferred_element_type=F32)
                         + rows_ref[d, 1, pl.ds(c, 1), :]
                         * jnp.dot(sprev_ref[d, c], q_t, preferred_element_type=F32))
                den = out_t[ML_HEAD_DIM:ML_HEAD_DIM + 1]
                h_d = out_t[:ML_HEAD_DIM] / jnp.maximum(jnp.abs(den), rows_ref[d, 2, pl.ds(c, 1), :])
                hs = h_d if hs is None else hs + h_d
            hs = jax.nn.sigmoid(ot_ref[c].astype(F32)) * hs
            mu = jnp.mean(hs, axis=0, keepdims=True)
            hc = hs - mu
            var = jnp.mean(hc * hc, axis=0, keepdims=True)
            y_t = hc * lax.rsqrt(var + LN_EPS) * ng_col * _silu(zt_ref[c].astype(F32))
            y_ref[pl.ds(pl.multiple_of(c * L, L), L), :] = y_t.T.astype(BF16)

    w2b_ref[...] = jnp.zeros(w2b_ref.shape, BF16)

    def pipeline_step(g, carry):
        @pl.when(g % 2 == 0)
        def _():
            score_stage(g, w2a_ref)
            value_stage(jnp.maximum(g - 1, 0), w2b_ref)

        @pl.when(g % 2 == 1)
        def _():
            score_stage(g, w2b_ref)
            value_stage(g - 1, w2a_ref)
        return carry

    lax.fori_loop(0, ngroups, pipeline_step, 0)
    value_stage(ngroups - 1, w2b_ref if (ngroups - 1) % 2 else w2a_ref)


def _mlstm(u3, ut, gates, f_bias, norm_g):
    bsz, s, _ = u3.shape
    d = ML_HEAD_DIM
    nc = s // ML_CHUNK
    tcol = lambda part: pl.BlockSpec((nc, d, ML_CHUNK), lambda b, h: (b, part * ML_HEADS + h, 0))
    return pl.pallas_call(
        functools.partial(_mlstm_kernel, seq=s),
        grid=(bsz, ML_HEADS),
        in_specs=[pl.BlockSpec((None, s, d), lambda b, h: (b, 0, B_K // d + h)),
                  tcol(0), tcol(1), tcol(2), tcol(3),
                  pl.BlockSpec((ML_GATES, None, nc, ML_CHUNK), lambda b, h: (0, b, 0, 0)),
                  pl.BlockSpec(memory_space=pltpu.SMEM),
                  pl.BlockSpec((d, 1), lambda b, h: (h, 0))],
        out_specs=pl.BlockSpec((None, s, d), lambda b, h: (b, 0, h)),
        out_shape=jax.ShapeDtypeStruct((bsz, s, ML_WIDTH), BF16),
        scratch_shapes=[pltpu.VMEM((2, ML_AUG, d), F32),
                        pltpu.VMEM((2, nc, ML_AUG, d), BF16),
                        pltpu.VMEM((2, 4, nc, ML_CHUNK), F32),
                        pltpu.VMEM((2, 2, nc, ML_CHUNK), F32),
                        pltpu.VMEM((2, ML_CHUNK, 128), F32),
                        pltpu.VMEM((2 * min(4, nc // 2), ML_CHUNK, ML_CHUNK), BF16),
                        pltpu.VMEM((2 * min(4, nc // 2), ML_CHUNK, ML_CHUNK), BF16)],
        compiler_params=_cparams(("parallel", "parallel")),
        name="mlstm",
    )(u3, ut, ut, ut, ut, gates, f_bias, norm_g.reshape(ML_WIDTH, 1))


def _pair_heads(t, axis):
    shape = t.shape
    t = t.reshape(shape[:axis] + (2, 2, WA_GROUP, HEAD_DIM) + shape[axis + 1:])
    return jnp.swapaxes(t, axis + 1, axis + 2).reshape(shape)


def _reorder_in_proj(w, b):
    def natural(t):
        ax = t.ndim - 1
        return jnp.concatenate(
            [t[..., 4624:7696],
             _pair_heads(t[..., 0:768], ax), _pair_heads(t[..., 1280:2048], ax), t[..., 768:1280],
             t[..., 2560:3072]], axis=ax)

    def feature_major(t):
        ax = t.ndim - 1
        return jnp.concatenate([t[..., 2048:2560], t[..., 3072:4608]], axis=ax)

    gates = slice(4608, 4608 + ML_GATES)
    return (natural(w).astype(BF16), natural(b), feature_major(w).astype(BF16), feature_major(b),
            w[:, gates].T.astype(BF16), b[gates])


def _reorder_out_proj(w):
    return jnp.concatenate([_pair_heads(w[:WA_WIDTH], 0), w[WA_WIDTH:]], axis=0).astype(BF16)


def kernel(x, emb_ln_g, emb_ln_b, w_in, b_in, w_out, b_out, ln_g, ln_b, t5_bias, sink, ml_f_bias,
           ml_norm_g, na_rpb):
    bsz, s, d = x.shape
    depth = w_in.shape[0]
    alpha = (2 * depth) ** 0.25
    m = bsz * s
    nc = s // ML_CHUNK
    assert d == D_MODEL and s % NA_BLOCK == 0 and s // GRID_W >= NA_KH_MAX and nc <= 128

    res, hb = _input_norm(x.reshape(m, d), emb_ln_g, emb_ln_b)
    bias_a = _window_bias(t5_bias)
    bias_c = _na_bias(na_rpb, s // GRID_W)
    for l in range(depth):
        wm, bm, wt, bt, wg, bg = _reorder_in_proj(w_in[l], b_in[l])
        u3 = _in_projection(hb, wm, bm).reshape(bsz, s, U_WIDTH)
        ut, gates = _t_projection(hb, wt, bt, wg, bg)
        gates = gates.reshape(ML_GATES, bsz, nc, ML_CHUNK)
        ya = _window_attention(u3, bias_a, sink[l])
        yb = _mlstm(u3, ut, gates, ml_f_bias[l], ml_norm_g[l])
        yc = _neighbourhood_attention(u3, bias_c, l)
        last = l == depth - 1
        outs = _out_projection(ya.reshape(m, WA_WIDTH), yb.reshape(m, ML_WIDTH), yc.reshape(m, NA_WIDTH),
                               _reorder_out_proj(w_out[l]), b_out[l], res, ln_g[l], ln_b[l], alpha, not last)
        res = outs[0]
        hb = None if last else outs[1]
    return res.reshape(bsz, s, d)
```

```python
import functools
import math

import numpy as np
import jax
import jax.numpy as jnp
from jax import lax
from jax.experimental import pallas as pl
from jax.experimental.pallas import tpu as pltpu

F32 = jnp.float32
BF16 = jnp.bfloat16

D_MODEL = 2048
HEAD_DIM = 64
LN_EPS = 1e-5
NEG = -1e30
LOG2E = math.log2(math.e)
SM_CHUNK = 64

WA_HEADS = 12
WA_KV_HEADS = 4
WA_GROUP = WA_HEADS // WA_KV_HEADS
WA_WIDTH = WA_HEADS * HEAD_DIM
WA_KV_WIDTH = WA_KV_HEADS * HEAD_DIM
WA_BLOCK = 128
WINDOW = 128
T5_BUCKETS = 32
T5_MAX_DIST = 128
ML_HEADS = 4
ML_HEAD_DIM = 128
ML_WIDTH = ML_HEADS * ML_HEAD_DIM
ML_CHUNK = 128
ML_AUG = 2 * ML_HEAD_DIM
ML_GATES = 4 * ML_HEADS
NA_HEADS = 12
NA_WIDTH = NA_HEADS * HEAD_DIM
GRID_W = 64
NA_KH_MAX = 8
NA_KW = 16
NA_ROWS = 4
NA_BLOCK = NA_ROWS * GRID_W

MIX_WIDTH = WA_WIDTH + ML_WIDTH + NA_WIDTH

C_Q, C_K, C_V, C_Z = 0, 768, 1536, 2304
A_Q, A_Z, A_K, A_V = 3072, 3840, 4608, 4864
B_K = 5120
U_WIDTH = 5632
T_WIDTH = 4 * ML_WIDTH

VMEM_LIMIT = 56 * 1024 * 1024
LN_TM = 512
IN_TM, IN_TN = 1024, 1408
OUT_TM = 512


def _cparams(sem):
    return pltpu.CompilerParams(dimension_semantics=sem, vmem_limit_bytes=VMEM_LIMIT)


def _dot_hi(a, b):
    return jnp.dot(a, b, preferred_element_type=F32, precision=lax.Precision.HIGHEST)


def _layer_norm_rows(x, g, b):
    mu = jnp.mean(x, axis=-1, keepdims=True)
    xc = x - mu
    var = jnp.mean(xc * xc, axis=-1, keepdims=True)
    return xc * lax.rsqrt(var + LN_EPS) * g + b


def _ln_kernel(x_ref, g_ref, b_ref, of_ref, ob_ref):
    y = _layer_norm_rows(x_ref[...].astype(F32), g_ref[...], b_ref[...])
    of_ref[...] = y
    ob_ref[...] = y.astype(BF16)


def _input_norm(x2, g, b):
    m, d = x2.shape
    tm = min(LN_TM, m)
    row = pl.BlockSpec((tm, d), lambda i: (i, 0))
    vec = pl.BlockSpec((1, d), lambda i: (0, 0))
    return pl.pallas_call(
        _ln_kernel,
        grid=(m // tm,),
        in_specs=[row, vec, vec],
        out_specs=[row, row],
        out_shape=[jax.ShapeDtypeStruct((m, d), F32), jax.ShapeDtypeStruct((m, d), BF16)],
        compiler_params=_cparams(("parallel",)),
        name="input_norm",
    )(x2, g.reshape(1, d), b.reshape(1, d))


def _inproj_kernel(h_ref, w_ref, b_ref, u_ref):
    acc = jnp.dot(h_ref[...], w_ref[...], preferred_element_type=F32)
    u_ref[...] = (acc + b_ref[...]).astype(u_ref.dtype)


def _in_projection(hb, w, b):
    m, d = hb.shape
    n = w.shape[1]
    tm, tn = min(IN_TM, m), IN_TN
    return pl.pallas_call(
        _inproj_kernel,
        grid=(n // tn, m // tm),
        in_specs=[pl.BlockSpec((tm, d), lambda j, i: (i, 0)),
                  pl.BlockSpec((d, tn), lambda j, i: (0, j)),
                  pl.BlockSpec((1, tn), lambda j, i: (0, j))],
        out_specs=pl.BlockSpec((tm, tn), lambda j, i: (i, j)),
        out_shape=jax.ShapeDtypeStruct((m, n), BF16),
        compiler_params=_cparams(("parallel", "parallel")),
        name="in_projection",
    )(hb, w, b.reshape(1, n))


def _tproj_kernel(h_ref, w_ref, b_ref, wg_ref, bg_ref, ut_ref, gate_ref):
    h = h_ref[...]
    acc = jnp.dot(h, w_ref[...], preferred_element_type=F32) + b_ref[...]
    for c in range(ut_ref.shape[0]):
        ut_ref[c] = acc[c * ML_CHUNK:(c + 1) * ML_CHUNK].T.astype(BF16)
    gate_ref[...] = lax.dot_general(wg_ref[...], h, (((1,), (1,)), ((), ())),
                                    preferred_element_type=F32) + bg_ref[...]


def _t_projection(hb, w, b, wg, bg):
    m, d = hb.shape
    tm = min(IN_TM, m)
    ngate = wg.shape[0]
    return pl.pallas_call(
        _tproj_kernel,
        grid=(m // tm,),
        in_specs=[pl.BlockSpec((tm, d), lambda i: (i, 0)),
                  pl.BlockSpec((d, T_WIDTH), lambda i: (0, 0)),
                  pl.BlockSpec((1, T_WIDTH), lambda i: (0, 0)),
                  pl.BlockSpec((ngate, d), lambda i: (0, 0)),
                  pl.BlockSpec((ngate, 1), lambda i: (0, 0))],
        out_specs=[pl.BlockSpec((tm // ML_CHUNK, T_WIDTH, ML_CHUNK), lambda i: (i, 0, 0)),
                   pl.BlockSpec((ngate, tm), lambda i: (0, i))],
        out_shape=[jax.ShapeDtypeStruct((m // ML_CHUNK, T_WIDTH, ML_CHUNK), BF16),
                   jax.ShapeDtypeStruct((ngate, m), F32)],
        compiler_params=_cparams(("parallel",)),
        name="t_projection",
    )(hb, w, b.reshape(1, T_WIDTH), wg, bg.reshape(ngate, 1))


def _outproj_kernel(ya_ref, yb_ref, yc_ref, w_ref, b_ref, res_ref, g_ref, beta_ref, *out_refs, alpha):
    half = ya_ref.shape[0] // 2
    for rows in (slice(0, half), slice(half, 2 * half)):
        y = jnp.concatenate([ya_ref[rows, :], yb_ref[rows, :], yc_ref[rows, :]], axis=1)
        out = jnp.dot(y, w_ref[...], preferred_element_type=F32) + b_ref[...]
        r = _layer_norm_rows(alpha * res_ref[rows, :] + out, g_ref[...], beta_ref[...])
        out_refs[0][rows, :] = r
        if len(out_refs) > 1:
            out_refs[1][rows, :] = r.astype(BF16)


def _out_projection(ya, yb, yc, w, b, res, g, beta, alpha, want_bf16):
    m, d = res.shape
    tm = min(OUT_TM, m)
    row = lambda i: (i, 0)
    const = lambda i: (0, 0)
    out_specs = [pl.BlockSpec((tm, d), row)]
    out_shape = [jax.ShapeDtypeStruct((m, d), F32)]
    if want_bf16:
        out_specs.append(pl.BlockSpec((tm, d), row))
        out_shape.append(jax.ShapeDtypeStruct((m, d), BF16))
    vec = lambda v: v.reshape(1, d)
    return pl.pallas_call(
        functools.partial(_outproj_kernel, alpha=alpha),
        grid=(m // tm,),
        in_specs=[pl.BlockSpec((tm, WA_WIDTH), row),
                  pl.BlockSpec((tm, ML_WIDTH), row),
                  pl.BlockSpec((tm, NA_WIDTH), row),
                  pl.BlockSpec((MIX_WIDTH, d), const),
                  pl.BlockSpec((1, d), const),
                  pl.BlockSpec((tm, d), row),
                  pl.BlockSpec((1, d), const),
                  pl.BlockSpec((1, d), const)],
        out_specs=out_specs,
        out_shape=out_shape,
        compiler_params=_cparams(("parallel",)),
        name="out_projection",
    )(ya, yb, yc, w, vec(b), res, vec(g), vec(beta))


def _silu(z):
    return z * jax.nn.sigmoid(z)


def _scale_q(q):
    return (q.astype(F32) * (HEAD_DIM ** -0.5 * LOG2E)).astype(BF16)


def _wattn_kernel(q_ref, z_ref, k0_ref, k1_ref, k2_ref, k3_ref, v0_ref, v1_ref, v2_ref, v3_ref,
                  bias_e_ref, bias_o_ref, sink_ref, o_ref, s0_ref, s1_ref, p_ref):
    n = pl.program_id(1)
    blk = WA_BLOCK
    npair = WA_KV_HEADS // 2
    rows = 2 * WA_GROUP * blk
    lane = lax.broadcasted_iota(jnp.int32, (blk, 2 * HEAD_DIM), 1)
    left = lane < HEAD_DIM
    k_refs = (k0_ref, k1_ref, k2_ref, k3_ref)
    v_refs = (v0_ref, v1_ref, v2_ref, v3_ref)

    def step(s_read, s_write):
        for half in range(2):
            qrows = slice(half * blk, (half + 1) * blk)
            bias_ref = (bias_e_ref, bias_o_ref)[half]
            for i in range(npair):
                ks = slice(i * 2 * HEAD_DIM, (i + 1) * 2 * HEAD_DIM)
                k3 = jnp.concatenate([r[:, ks] for r in k_refs[half:half + 3]], axis=0)
                tiles = [_scale_q(q_ref[qrows, (WA_GROUP * i + j) * 2 * HEAD_DIM:(WA_GROUP * i + j + 1) * 2 * HEAD_DIM])
                         for j in range(WA_GROUP)]
                zero = jnp.zeros_like(tiles[0])
                lhs = jnp.concatenate([jnp.where(left, t, zero) for t in tiles]
                                      + [jnp.where(left, zero, t) for t in tiles], axis=0)
                s_write[half, i] = lax.dot_general(lhs, k3, (((1,), (1,)), ((), ())),
                                                   preferred_element_type=F32)
                v3 = jnp.concatenate([r[:, ks] for r in v_refs[half:half + 3]], axis=0)
                dens = []
                for c in range(rows // SM_CHUNK):
                    rs = slice(c * SM_CHUNK, (c + 1) * SM_CHUNK)
                    sk = sink_ref[0, 2 * WA_GROUP * i + c * SM_CHUNK // blk] * LOG2E
                    sc = s_read[half, i, rs, :] + bias_ref[i, rs, :]
                    m = jnp.maximum(jnp.max(sc, axis=-1, keepdims=True), sk)
                    p = jnp.exp2(sc - m)
                    dens.append(jnp.sum(p, axis=-1, keepdims=True) + jnp.exp2(sk - m))
                    p_ref[half, i, rs, :] = p.astype(BF16)
                pv = jnp.dot(p_ref[half, i], v3, preferred_element_type=F32)
                o = jnp.concatenate([pv[c * SM_CHUNK:(c + 1) * SM_CHUNK] / dens[c]
                                     for c in range(rows // SM_CHUNK)], axis=0)
                for j in range(WA_GROUP):
                    t = WA_GROUP * i + j
                    ot = jnp.where(left, o[j * blk:(j + 1) * blk],
                                   o[(WA_GROUP + j) * blk:(WA_GROUP + j + 1) * blk])
                    cs = slice(t * 2 * HEAD_DIM, (t + 1) * 2 * HEAD_DIM)
                    o_ref[qrows, cs] = (ot * _silu(z_ref[qrows, cs].astype(F32))).astype(BF16)

    @pl.when(n == 0)
    def _():
        s1_ref[...] = jnp.zeros(s1_ref.shape, F32)

    @pl.when(n % 2 == 0)
    def _():
        step(s1_ref, s0_ref)

    @pl.when(n % 2 == 1)
    def _():
        step(s0_ref, s1_ref)


def _window_attention(u3, bias, sink):
    bsz, s, _ = u3.shape
    nb = s // WA_BLOCK
    npr = nb // 2
    qw, kw = WA_WIDTH, WA_KV_WIDTH
    npair = WA_KV_HEADS // 2
    rows = 2 * WA_GROUP * WA_BLOCK
    clamp = lambda i: jnp.clip(i, 0, nb - 1)
    pair = lambda i: jnp.clip(i, 0, npr - 1)
    kspec = lambda off, c: pl.BlockSpec((None, WA_BLOCK, kw), lambda b, n: (b, clamp(2 * n + off), c))
    vspec = lambda off, c: pl.BlockSpec((None, WA_BLOCK, kw), lambda b, n: (b, clamp(2 * n - 2 + off), c))
    even_type = lambda b, n: (jnp.where(n <= 1, 0, 1), 0, 0, 0)
    odd_type = lambda b, n: (jnp.where(n == npr, 2, 1), 0, 0, 0)
    sshape = (2, npair, rows, 3 * WA_BLOCK)
    bias = bias.reshape(3, npair, rows, 3 * WA_BLOCK)
    return pl.pallas_call(
        _wattn_kernel,
        grid=(bsz, npr + 1),
        in_specs=[pl.BlockSpec((None, 2 * WA_BLOCK, qw), lambda b, n: (b, pair(n), A_Q // qw)),
                  pl.BlockSpec((None, 2 * WA_BLOCK, qw), lambda b, n: (b, pair(n - 1), A_Z // qw)),
                  kspec(-1, A_K // kw), kspec(0, A_K // kw), kspec(1, A_K // kw), kspec(2, A_K // kw),
                  vspec(-1, A_V // kw), vspec(0, A_V // kw), vspec(1, A_V // kw), vspec(2, A_V // kw),
                  pl.BlockSpec((None, npair, rows, 3 * WA_BLOCK), even_type),
                  pl.BlockSpec((None, npair, rows, 3 * WA_BLOCK), odd_type),
                  pl.BlockSpec(memory_space=pltpu.SMEM)],
        out_specs=pl.BlockSpec((None, 2 * WA_BLOCK, qw), lambda b, n: (b, pair(n - 1), 0)),
        out_shape=jax.ShapeDtypeStruct((bsz, s, qw), BF16),
        scratch_shapes=[pltpu.VMEM(sshape, F32), pltpu.VMEM(sshape, F32), pltpu.VMEM(sshape, BF16)],
        compiler_params=_cparams(("parallel", "arbitrary")),
        name="window_attention",
    )(u3, u3, u3, u3, u3, u3, u3, u3, u3, u3, bias, bias, sink.reshape(1, WA_HEADS))


def _t5_bucket_np(rel):
    half = T5_BUCKETS // 2
    max_exact = half // 2
    ret = np.where(rel > 0, half, 0)
    n = np.abs(rel)
    nf = np.maximum(n, 1).astype(np.float64)
    v = np.log(nf / max_exact) / math.log(T5_MAX_DIST / max_exact) * (half - max_exact)
    vr = np.round(v)
    v = np.where(np.abs(v - vr) < 1e-9, vr, v)
    large = np.minimum(max_exact + np.trunc(v).astype(np.int64), half - 1)
    return ret + np.where(n < max_exact, n, large)


def _wbias_kernel(t5t_ref, bucket_ref, o_ref):
    width = 4 * WA_BLOCK
    bk = bucket_ref[...]
    e = lax.broadcasted_iota(jnp.int32, (T5_BUCKETS, width), 0)
    onehot = jnp.where(e == bk, 1.0, 0.0).astype(F32)
    g = _dot_hi(t5t_ref[...], onehot) * LOG2E + jnp.where(bk < 0, NEG, 0.0)
    col = lax.broadcasted_iota(jnp.int32, (WA_BLOCK, 3 * WA_BLOCK), 1)
    for h in range(WA_HEADS):
        row = jnp.broadcast_to(g[h:h + 1, :], (WA_BLOCK, width))
        t = pltpu.roll(row, 3 * WA_BLOCK, 1, stride=1, stride_axis=0)[:, :3 * WA_BLOCK]
        o_ref[0, h] = jnp.where(col < WA_BLOCK, NEG, t)
        o_ref[1, h] = t
        o_ref[2, h] = jnp.where(col >= 2 * WA_BLOCK, NEG, t)


def _window_bias(t5_table):
    rel = np.arange(4 * WA_BLOCK) - 2 * WA_BLOCK
    bucket = np.where(np.abs(rel) <= WINDOW, _t5_bucket_np(rel), -1).astype(np.int32)
    t5t = jnp.pad(t5_table.astype(F32).T, ((0, 16 - WA_HEADS), (0, 0)))
    return pl.pallas_call(
        _wbias_kernel,
        out_shape=jax.ShapeDtypeStruct((3, WA_HEADS, WA_BLOCK, 3 * WA_BLOCK), F32),
        name="window_bias",
    )(t5t, jnp.asarray(bucket).reshape(1, -1))


def _natten_kernel(q_ref, z_ref, kp_ref, kc_ref, kn_ref, vp_ref, vc_ref, vn_ref, bias_ref, o_ref,
                   s0_ref, s1_ref, p_ref):
    n = pl.program_id(1)
    blk = NA_BLOCK
    lane = lax.broadcasted_iota(jnp.int32, (blk, 2 * HEAD_DIM), 1)
    left = lane < HEAD_DIM

    ones_col = jnp.where(lax.broadcasted_iota(jnp.int32, (3 * blk, 2 * HEAD_DIM), 1) == 0, 1.0, 0.0).astype(BF16)

    def step(s_read, s_write):
        for i in range(NA_HEADS // 2):
            cs = slice(i * 2 * HEAD_DIM, (i + 1) * 2 * HEAD_DIM)
            k3 = jnp.concatenate([kp_ref[:, cs], kc_ref[:, cs], kn_ref[:, cs]], axis=0)
            t = _scale_q(q_ref[:, cs])
            zero = jnp.zeros_like(t)
            lhs = jnp.concatenate([jnp.where(left, t, zero), jnp.where(left, zero, t)], axis=0)
            s_write[i] = lax.dot_general(lhs, k3, (((1,), (1,)), ((), ())), preferred_element_type=F32)
            pb = p_ref.at[i % 3]
            for c in range(2 * blk // SM_CHUNK):
                rs = slice(c * SM_CHUNK, (c + 1) * SM_CHUNK)
                sc = s_read[i, rs, :] + bias_ref[i, rs, :]
                m = jnp.max(sc, axis=-1, keepdims=True)
                pb[rs, :] = jnp.exp2(sc - m).astype(BF16)
            if i > 0:
                weighted_values(i - 1)
        weighted_values(NA_HEADS // 2 - 1)

    def weighted_values(i):
        cs = slice(i * 2 * HEAD_DIM, (i + 1) * 2 * HEAD_DIM)
        v3 = jnp.concatenate([vp_ref[:, cs], vc_ref[:, cs], vn_ref[:, cs]], axis=0)
        vaug = jnp.concatenate([v3, ones_col], axis=1)
        out = jnp.dot(p_ref[i % 3], vaug, preferred_element_type=F32)
        o = out[:, :2 * HEAD_DIM] / out[:, 2 * HEAD_DIM:2 * HEAD_DIM + 1]
        ot = jnp.where(left, o[:blk], o[blk:])
        o_ref[:, cs] = (ot * _silu(z_ref[:, cs].astype(F32))).astype(BF16)

    @pl.when(n == 0)
    def _():
        s1_ref[...] = jnp.zeros(s1_ref.shape, F32)

    @pl.when(n % 2 == 0)
    def _():
        step(s1_ref, s0_ref)

    @pl.when(n % 2 == 1)
    def _():
        step(s0_ref, s1_ref)


def _neighbourhood_attention(u3, bias, layer):
    bsz, s, _ = u3.shape
    nblk = s // NA_BLOCK
    w = NA_WIDTH
    npair = NA_HEADS // 2
    clamp = lambda i: jnp.clip(i, 0, nblk - 1)
    spec = lambda off, c: pl.BlockSpec((None, NA_BLOCK, w), lambda b, n: (b, clamp(n + off), c))
    btype = lambda b, n: (layer, jnp.where(n <= 1, 0, jnp.where(n == nblk, 2, 1)), 0, 0, 0)
    sshape = (npair, 2 * NA_BLOCK, 3 * NA_BLOCK)
    return pl.pallas_call(
        _natten_kernel,
        grid=(bsz, nblk + 1),
        in_specs=[spec(0, C_Q // w), spec(-1, C_Z // w),
                  spec(-1, C_K // w), spec(0, C_K // w), spec(1, C_K // w),
                  spec(-2, C_V // w), spec(-1, C_V // w), spec(0, C_V // w),
                  pl.BlockSpec((None, None) + sshape, btype)],
        out_specs=pl.BlockSpec((None, NA_BLOCK, w), lambda b, n: (b, clamp(n - 1), 0)),
        out_shape=jax.ShapeDtypeStruct((bsz, s, w), BF16),
        scratch_shapes=[pltpu.VMEM(sshape, F32), pltpu.VMEM(sshape, F32),
                        pltpu.VMEM((3,) + sshape[1:], BF16)],
        compiler_params=_cparams(("parallel", "arbitrary")),
        name="neighbourhood_attention",
    )(u3, u3, u3, u3, u3, u3, u3, u3, bias.reshape(bias.shape[:2] + sshape))


def _na_valid_rows(rows):
    kh = min(NA_KH_MAX, rows)
    nblk = rows // NA_ROWS
    out = []
    for j in (0, min(1, nblk - 1), nblk - 1):
        r = NA_ROWS * j + np.arange(NA_ROWS)[:, None]
        kr = NA_ROWS * (j - 1) + np.arange(3 * NA_ROWS)[None, :]
        rs = np.clip(r - kh // 2, 0, rows - kh)
        out.append((kr >= rs) & (kr < rs + kh))
    return np.stack(out)


def _nabias_kernel(rpb_ref, o_ref, *, valid):
    w = GRID_W
    j = lax.broadcasted_iota(jnp.int32, (32, 2 * w), 1)
    e = lax.broadcasted_iota(jnp.int32, (32, 2 * w), 0)
    dc = jnp.clip(j - w, -(NA_KW - 1), NA_KW - 1) + NA_KW - 1
    g = _dot_hi(rpb_ref[...], jnp.where(e == dc, 1.0, 0.0).astype(F32)) * LOG2E
    lane = lax.broadcasted_iota(jnp.int32, (w, 2 * w), 1)
    qc = lax.broadcasted_iota(jnp.int32, (w, 2 * w), 0)
    kc = lane & (w - 1)
    col_start = jnp.clip(qc - NA_KW // 2, 0, w - NA_KW)
    col_ok = (kc >= col_start) & (kc < col_start + NA_KW)
    left = lane < w
    neg = jnp.full((w, 2 * w), NEG, F32)

    def toeplitz(dr, shift):
        row = jnp.broadcast_to(g[dr:dr + 1, :], (w, 2 * w))
        return pltpu.roll(row, shift, 1, stride=1, stride_axis=0)

    pair = [jnp.where(col_ok, jnp.where(left, toeplitz(d, w), toeplitz(d + 1, 0)), NEG)
            for d in range(2 * NA_KH_MAX - 2)]
    for ty in range(3):
        for rl in range(NA_ROWS):
            for t in range(3 * NA_ROWS // 2):
                d = 2 * t - rl + NA_KH_MAX - 1 - NA_ROWS
                v0, v1 = bool(valid[ty, rl, 2 * t]), bool(valid[ty, rl, 2 * t + 1])
                if v0 and v1:
                    tile = pair[d]
                elif v0:
                    tile = jnp.where(left, pair[d], NEG)
                elif v1:
                    tile = jnp.where(left, NEG, pair[d])
                else:
                    tile = neg
                o_ref[ty, rl * w:(rl + 1) * w, 2 * t * w:(2 * t + 2) * w] = tile


def _na_bias(rpb_all, rows):
    depth = rpb_all.shape[0]
    rpb_p = jnp.pad(rpb_all.astype(F32), ((0, 0), (0, 0), (0, 1), (0, 1)))
    return pl.pallas_call(
        functools.partial(_nabias_kernel, valid=_na_valid_rows(rows)),
        grid=(depth, NA_HEADS),
        in_specs=[pl.BlockSpec((None, None, 16, 32), lambda l, h: (l, h, 0, 0))],
        out_specs=pl.BlockSpec((None, 3, None, NA_BLOCK, 3 * NA_BLOCK), lambda l, h: (l, 0, h, 0, 0)),
        out_shape=jax.ShapeDtypeStruct((depth, 3, NA_HEADS, NA_BLOCK, 3 * NA_BLOCK), F32),
        compiler_params=_cparams(("parallel", "parallel")),
        name="na_bias",
    )(rpb_p)


def _log_sigmoid(x):
    return jnp.minimum(x, 0.0) - jnp.log(1.0 + jnp.exp(-jnp.abs(x)))


def _tri(n, upper):
    r = lax.broadcasted_iota(jnp.int32, (n, n), 0)
    c = lax.broadcasted_iota(jnp.int32, (n, n), 1)
    return jnp.where((r <= c) if upper else (r >= c), 1.0, 0.0).astype(F32)


def _scan_max(x, axis, reverse, size):
    idx = lax.broadcasted_iota(jnp.int32, x.shape, axis)
    k = 1
    while k < size:
        if reverse:
            shifted = pltpu.roll(x, x.shape[axis] - k, axis)
            ok = idx < size - k
        else:
            shifted = pltpu.roll(x, k, axis)
            ok = idx >= k
        x = jnp.maximum(x, jnp.where(ok, shifted, NEG))
        k *= 2
    return x


def _mlstm_kernel(k_ref, qt_ref, vt_ref, ot_ref, zt_ref, gr_ref, fb_ref, ng_ref, y_ref,
                  state_ref, sprev_ref, rows_ref, gain_ref, et_ref, w2a_ref, w2b_ref, *, seq):
    L = ML_CHUNK
    nc = seq // L
    head = pl.program_id(1)
    scale = ML_HEAD_DIM ** -0.5
    lane = lax.broadcasted_iota(jnp.int32, (1, 128), 1)

    for d in range(2):
        rev = d == 1
        fb = fb_ref[d, head]
        i_r = gr_ref[8 * d + head]
        lf_r = _log_sigmoid(gr_ref[8 * d + 4 + head] + fb)
        b_r = _dot_hi(lf_r, _tri(L, upper=not rev))
        g = jnp.broadcast_to(jnp.sum(lf_r, axis=1, keepdims=True), (nc, L))
        a_r = g - b_r + i_r
        m_loc = jnp.broadcast_to(jnp.max(a_r, axis=1, keepdims=True), (nc, L))
        jr = lax.broadcasted_iota(jnp.int32, (nc, nc), 0)
        jc = lax.broadcasted_iota(jnp.int32, (nc, nc), 1)
        before = jnp.where((jc > jr) if rev else (jc < jr), 1.0, 0.0).astype(F32)
        g_ex = _dot_hi(before, g)
        x = m_loc - (g_ex + g)
        row = lax.broadcasted_iota(jnp.int32, (nc, L), 0)
        if rev:
            x_prev = jnp.where(row < nc - 1, pltpu.roll(x, nc - 1, 0), NEG)
        else:
            x_prev = jnp.where(row >= 1, pltpu.roll(x, 1, 0), NEG)
        m_prev = g_ex + jnp.maximum(_scan_max(x_prev, 0, rev, nc), 0.0)
        m_after = jnp.maximum(g + m_prev, m_loc)
        gain_ref[d, 0] = jnp.exp(g + m_prev - m_after)
        gain_ref[d, 1] = jnp.exp(m_loc - m_after)
        e_r = i_r - b_r
        mu = jnp.maximum(m_prev, _scan_max(e_r, 1, rev, L))
        rows_ref[d, 0] = mu
        rows_ref[d, 1] = jnp.exp(m_prev - mu) * scale
        rows_ref[d, 2] = jnp.exp(-b_r - mu)
        rows_ref[d, 3] = jnp.exp(a_r - m_loc)
        e_pad = jnp.concatenate([e_r, jnp.zeros((128 - nc, L), F32)], axis=0) if nc < 128 else e_r
        et_ref[d] = e_pad.T
        state_ref[d] = jnp.zeros((ML_AUG, ML_HEAD_DIM), F32)

    rr = lax.broadcasted_iota(jnp.int32, (L, L), 0)
    cc = lax.broadcasted_iota(jnp.int32, (L, L), 1)
    ones_row = jnp.where(lax.broadcasted_iota(jnp.int32, (ML_HEAD_DIM, L), 0) == 0, 1.0, 0.0).astype(BF16)

    def state_step(j, carry):
        for d in range(2):
            c = j if d == 0 else nc - 1 - j
            kc = k_ref[pl.ds(pl.multiple_of(c * L, L), L), :]
            vaug_t = jnp.concatenate([vt_ref[c], ones_row], axis=0)
            wv_t = (rows_ref[d, 3, pl.ds(c, 1), :] * vaug_t.astype(F32)).astype(BF16)
            s_loc = jnp.dot(wv_t, kc, preferred_element_type=F32)
            st = state_ref[d]
            sprev_ref[d, c] = st.astype(BF16)
            state_ref[d] = gain_ref[d, 0, pl.ds(c, 1), :] * st + gain_ref[d, 1, pl.ds(c, 1), :] * s_loc
        return carry

    lax.fori_loop(0, nc, state_step, 0, unroll=4)

    ng_col = jnp.broadcast_to(ng_ref[...], (ML_HEAD_DIM, L))
    group = min(4, nc // 2)
    ngroups = nc // group

    def score_stage(g, w2_ref):
        for jj in range(group):
            c = g * group + jj
            kc = k_ref[pl.ds(pl.multiple_of(c * L, L), L), :]
            s_t = jnp.dot(kc, qt_ref[c], preferred_element_type=F32)
            for d in range(2):
                e_col = jnp.sum(jnp.where(lane == c, et_ref[d], 0.0), axis=1, keepdims=True)
                mask = (rr >= cc) if d == 1 else (rr <= cc)
                p_t = jnp.where(mask, jnp.exp(e_col - rows_ref[d, 0, pl.ds(c, 1), :]), 0.0)
                w2_ref[2 * jj + d] = (s_t * p_t * scale).astype(BF16)

    def value_stage(g, w2_ref):
        for jj in range(group):
            c = g * group + jj
            q_t = qt_ref[c]
            vaug_t = jnp.concatenate([vt_ref[c], ones_row], axis=0)
            hs = None
            for d in range(2):
                out_t = (jnp.dot(vaug_t, w2_ref[2 * jj + d], preferred_element_type=F32)
                         + rows_ref[d, 1, pl.ds(c, 1), :]
                         * jnp.dot(sprev_ref[d, c], q_t, preferred_element_type=F32))
                den = out_t[ML_HEAD_DIM:ML_HEAD_DIM + 1]
                h_d = out_t[:ML_HEAD_DIM] / jnp.maximum(jnp.abs(den), rows_ref[d, 2, pl.ds(c, 1), :])
                hs = h_d if hs is None else hs + h_d
            hs = jax.nn.sigmoid(ot_ref[c].astype(F32)) * hs
            mu = jnp.mean(hs, axis=0, keepdims=True)
            hc = hs - mu
            var = jnp.mean(hc * hc, axis=0, keepdims=True)
            y_t = hc * lax.rsqrt(var + LN_EPS) * ng_col * _silu(zt_ref[c].astype(F32))
            y_ref[pl.ds(pl.multiple_of(c * L, L), L), :] = y_t.T.astype(BF16)

    w2b_ref[...] = jnp.zeros(w2b_ref.shape, BF16)

    def pipeline_step(g, carry):
        @pl.when(g % 2 == 0)
        def _():
            score_stage(g, w2a_ref)
            value_stage(jnp.maximum(g - 1, 0), w2b_ref)

        @pl.when(g % 2 == 1)
        def _():
            score_stage(g, w2b_ref)
            value_stage(g - 1, w2a_ref)
        return carry

    lax.fori_loop(0, ngroups, pipeline_step, 0)
    value_stage(ngroups - 1, w2b_ref if (ngroups - 1) % 2 else w2a_ref)


def _mlstm(u3, ut, gates, f_bias, norm_g):
    bsz, s, _ = u3.shape
    d = ML_HEAD_DIM
    nc = s // ML_CHUNK
    tcol = lambda part: pl.BlockSpec((nc, d, ML_CHUNK), lambda b, h: (b, part * ML_HEADS + h, 0))
    return pl.pallas_call(
        functools.partial(_mlstm_kernel, seq=s),
        grid=(bsz, ML_HEADS),
        in_specs=[pl.BlockSpec((None, s, d), lambda b, h: (b, 0, B_K // d + h)),
                  tcol(0), tcol(1), tcol(2), tcol(3),
                  pl.BlockSpec((ML_GATES, None, nc, ML_CHUNK), lambda b, h: (0, b, 0, 0)),
                  pl.BlockSpec(memory_space=pltpu.SMEM),
                  pl.BlockSpec((d, 1), lambda b, h: (h, 0))],
        out_specs=pl.BlockSpec((None, s, d), lambda b, h: (b, 0, h)),
        out_shape=jax.ShapeDtypeStruct((bsz, s, ML_WIDTH), BF16),
        scratch_shapes=[pltpu.VMEM((2, ML_AUG, d), F32),
                        pltpu.VMEM((2, nc, ML_AUG, d), BF16),
                        pltpu.VMEM((2, 4, nc, ML_CHUNK), F32),
                        pltpu.VMEM((2, 2, nc, ML_CHUNK), F32),
                        pltpu.VMEM((2, ML_CHUNK, 128), F32),
                        pltpu.VMEM((2 * min(4, nc // 2), ML_CHUNK, ML_CHUNK), BF16),
                        pltpu.VMEM((2 * min(4, nc // 2), ML_CHUNK, ML_CHUNK), BF16)],
        compiler_params=_cparams(("parallel", "parallel")),
        name="mlstm",
    )(u3, ut, ut, ut, ut, gates, f_bias, norm_g.reshape(ML_WIDTH, 1))


def _a_head_copies(src, dst):
    out = []
    for i in range(2):
        for j in range(WA_GROUP):
            for half in range(2):
                h = 2 * WA_GROUP * i + WA_GROUP * half + j
                out.append((src + h * HEAD_DIM, HEAD_DIM, dst + ((WA_GROUP * i + j) * 2 + half) * HEAD_DIM))
    return out


_NATURAL_COPIES = ([(4624, 3072, C_Q)] + _a_head_copies(0, A_Q) + _a_head_copies(1280, A_Z)
                   + [(768, 512, A_K), (2560, 512, B_K)])
_FEATURE_MAJOR_COPIES = [(2048, 512, 0), (3072, 1536, 512)]
_OUT_ROW_COPIES = [(s0, n, d0) for s0, n, d0 in _a_head_copies(0, 0)] + [(WA_WIDTH, MIX_WIDTH - WA_WIDTH, WA_WIDTH)]


def _inprep_kernel(w_ref, wn_ref, wt_ref):
    for src, width, dst in _NATURAL_COPIES:
        wn_ref[:, dst:dst + width] = w_ref[:, src:src + width].astype(BF16)
    for src, width, dst in _FEATURE_MAJOR_COPIES:
        wt_ref[:, dst:dst + width] = w_ref[:, src:src + width].astype(BF16)


def _prep_in_weights(w_in):
    depth, d, n = w_in.shape
    tr = 256
    return pl.pallas_call(
        _inprep_kernel,
        grid=(depth, d // tr),
        in_specs=[pl.BlockSpec((None, tr, n), lambda l, i: (l, i, 0))],
        out_specs=[pl.BlockSpec((None, tr, U_WIDTH), lambda l, i: (l, i, 0)),
                   pl.BlockSpec((None, tr, T_WIDTH), lambda l, i: (l, i, 0))],
        out_shape=[jax.ShapeDtypeStruct((depth, d, U_WIDTH), BF16),
                   jax.ShapeDtypeStruct((depth, d, T_WIDTH), BF16)],
        compiler_params=_cparams(("parallel", "parallel")),
        name="prep_in_weights",
    )(w_in)


def _outprep_kernel(w_ref, o_ref):
    for src, rows, dst in _OUT_ROW_COPIES:
        o_ref[dst:dst + rows, :] = w_ref[src:src + rows, :].astype(BF16)


def _prep_out_weights(w_out):
    depth, k, n = w_out.shape
    return pl.pallas_call(
        _outprep_kernel,
        grid=(depth,),
        in_specs=[pl.BlockSpec((None, k, n), lambda l: (l, 0, 0))],
        out_specs=pl.BlockSpec((None, k, n), lambda l: (l, 0, 0)),
        out_shape=jax.ShapeDtypeStruct((depth, k, n), BF16),
        compiler_params=_cparams(("parallel",)),
        name="prep_out_weights",
    )(w_out)


def _gather_columns(v, copies, width):
    out = jnp.zeros(v.shape[:-1] + (width,), v.dtype)
    for src, n, dst in copies:
        out = out.at[..., dst:dst + n].set(v[..., src:src + n])
    return out


def kernel(x, emb_ln_g, emb_ln_b, w_in, b_in, w_out, b_out, ln_g, ln_b, t5_bias, sink, ml_f_bias,
           ml_norm_g, na_rpb):
    bsz, s, d = x.shape
    depth = w_in.shape[0]
    alpha = (2 * depth) ** 0.25
    m = bsz * s
    nc = s // ML_CHUNK
    assert d == D_MODEL and s % NA_BLOCK == 0 and s // GRID_W >= NA_KH_MAX and nc <= 128

    res, hb = _input_norm(x.reshape(m, d), emb_ln_g, emb_ln_b)
    bias_a = _window_bias(t5_bias)
    bias_c = _na_bias(na_rpb, s // GRID_W)
    wn_all, wt_all = _prep_in_weights(w_in)
    wo_all = _prep_out_weights(w_out)
    gates = slice(4608, 4608 + ML_GATES)
    for l in range(depth):
        bn = _gather_columns(b_in[l], _NATURAL_COPIES, U_WIDTH)
        bt = _gather_columns(b_in[l], _FEATURE_MAJOR_COPIES, T_WIDTH)
        u3 = _in_projection(hb, wn_all[l], bn).reshape(bsz, s, U_WIDTH)
        ut, g = _t_projection(hb, wt_all[l], bt, w_in[l][:, gates].T.astype(BF16), b_in[l][gates])
        g = g.reshape(ML_GATES, bsz, nc, ML_CHUNK)
        ya = _window_attention(u3, bias_a, sink[l])
        yb = _mlstm(u3, ut, g, ml_f_bias[l], ml_norm_g[l])
        yc = _neighbourhood_attention(u3, bias_c, l)
        last = l == depth - 1
        outs = _out_projection(ya.reshape(m, WA_WIDTH), yb.reshape(m, ML_WIDTH), yc.reshape(m, NA_WIDTH),
                               wo_all[l], b_out[l], res, ln_g[l], ln_b[l], alpha, not last)
        res = outs[0]
        hb = None if last else outs[1]
    return res.reshape(bsz, s, d)
```

```python
import functools
import math

import numpy as np
import jax
import jax.numpy as jnp
from jax import lax
from jax.experimental import pallas as pl
from jax.experimental.pallas import tpu as pltpu

F32 = jnp.float32
BF16 = jnp.bfloat16

D_MODEL = 2048
HEAD_DIM = 64
LN_EPS = 1e-5
NEG = -1e30
LOG2E = math.log2(math.e)
SM_CHUNK = 64

WA_HEADS = 12
WA_KV_HEADS = 4
WA_GROUP = WA_HEADS // WA_KV_HEADS
WA_WIDTH = WA_HEADS * HEAD_DIM
WA_KV_WIDTH = WA_KV_HEADS * HEAD_DIM
WA_BLOCK = 128
WINDOW = 128
T5_BUCKETS = 32
T5_MAX_DIST = 128
ML_HEADS = 4
ML_HEAD_DIM = 128
ML_WIDTH = ML_HEADS * ML_HEAD_DIM
ML_CHUNK = 128
ML_AUG = 2 * ML_HEAD_DIM
ML_GATES = 4 * ML_HEADS
NA_HEADS = 12
NA_WIDTH = NA_HEADS * HEAD_DIM
GRID_W = 64
NA_KH_MAX = 8
NA_KW = 16
NA_ROWS = 4
NA_BLOCK = NA_ROWS * GRID_W

MIX_WIDTH = WA_WIDTH + ML_WIDTH + NA_WIDTH

C_Q, C_K, C_V, C_Z = 0, 768, 1536, 2304
A_Q, A_Z, A_K, A_V = 3072, 3840, 4608, 4864
B_K = 5120
U_WIDTH = 5632
T_WIDTH = 4 * ML_WIDTH

VMEM_LIMIT = 56 * 1024 * 1024
LN_TM = 512
IN_TM, IN_TN = 1024, 1408
OUT_TM = 512


def _cparams(sem):
    return pltpu.CompilerParams(dimension_semantics=sem, vmem_limit_bytes=VMEM_LIMIT)


def _dot_hi(a, b):
    return jnp.dot(a, b, preferred_element_type=F32, precision=lax.Precision.HIGHEST)


def _layer_norm_rows(x, g, b):
    mu = jnp.mean(x, axis=-1, keepdims=True)
    xc = x - mu
    var = jnp.mean(xc * xc, axis=-1, keepdims=True)
    return xc * lax.rsqrt(var + LN_EPS) * g + b


def _ln_kernel(x_ref, g_ref, b_ref, of_ref, ob_ref):
    y = _layer_norm_rows(x_ref[...].astype(F32), g_ref[...], b_ref[...])
    of_ref[...] = y
    ob_ref[...] = y.astype(BF16)


def _input_norm(x2, g, b):
    m, d = x2.shape
    tm = min(LN_TM, m)
    row = pl.BlockSpec((tm, d), lambda i: (i, 0))
    vec = pl.BlockSpec((1, d), lambda i: (0, 0))
    return pl.pallas_call(
        _ln_kernel,
        grid=(m // tm,),
        in_specs=[row, vec, vec],
        out_specs=[row, row],
        out_shape=[jax.ShapeDtypeStruct((m, d), F32), jax.ShapeDtypeStruct((m, d), BF16)],
        compiler_params=_cparams(("parallel",)),
        name="input_norm",
    )(x2, g.reshape(1, d), b.reshape(1, d))


_NT = (((1,), (1,)), ((), ()))


def _inproj_kernel(h_ref, wt_ref, b_ref, u_ref):
    acc = lax.dot_general(h_ref[...], wt_ref[...], _NT, preferred_element_type=F32)
    u_ref[...] = (acc + b_ref[...]).astype(u_ref.dtype)


def _in_projection(hb, wt_all, layer, b):
    m, d = hb.shape
    n = wt_all.shape[1]
    tm, tn = min(IN_TM, m), IN_TN
    return pl.pallas_call(
        _inproj_kernel,
        grid=(n // tn, m // tm),
        in_specs=[pl.BlockSpec((tm, d), lambda j, i: (i, 0)),
                  pl.BlockSpec((None, tn, d), lambda j, i: (layer, j, 0)),
                  pl.BlockSpec((1, tn), lambda j, i: (0, j))],
        out_specs=pl.BlockSpec((tm, tn), lambda j, i: (i, j)),
        out_shape=jax.ShapeDtypeStruct((m, n), BF16),
        compiler_params=_cparams(("parallel", "parallel")),
        name="in_projection",
    )(hb, wt_all, b.reshape(1, n))


def _tproj_kernel(h_ref, wt_ref, bt_ref, ut_ref, gate_ref):
    acc = lax.dot_general(wt_ref[...], h_ref[...], _NT, preferred_element_type=F32) + bt_ref[...]
    for c in range(ut_ref.shape[0]):
        ut_ref[c] = acc[:T_WIDTH, c * ML_CHUNK:(c + 1) * ML_CHUNK].astype(BF16)
    gate_ref[...] = acc[T_WIDTH:]


def _t_projection(hb, wt_all, layer, bt):
    m, d = hb.shape
    tm = min(IN_TM, m)
    rows = wt_all.shape[1]
    return pl.pallas_call(
        _tproj_kernel,
        grid=(m // tm,),
        in_specs=[pl.BlockSpec((tm, d), lambda i: (i, 0)),
                  pl.BlockSpec((None, rows, d), lambda i: (layer, 0, 0)),
                  pl.BlockSpec((rows, 1), lambda i: (0, 0))],
        out_specs=[pl.BlockSpec((tm // ML_CHUNK, T_WIDTH, ML_CHUNK), lambda i: (i, 0, 0)),
                   pl.BlockSpec((ML_GATES, tm), lambda i: (0, i))],
        out_shape=[jax.ShapeDtypeStruct((m // ML_CHUNK, T_WIDTH, ML_CHUNK), BF16),
                   jax.ShapeDtypeStruct((ML_GATES, m), F32)],
        compiler_params=_cparams(("parallel",)),
        name="t_projection",
    )(hb, wt_all, bt.reshape(rows, 1))


def _outproj_kernel(ya_ref, yb_ref, yc_ref, w_ref, b_ref, res_ref, g_ref, beta_ref, *out_refs, alpha):
    half = ya_ref.shape[0] // 2
    for rows in (slice(0, half), slice(half, 2 * half)):
        y = jnp.concatenate([ya_ref[rows, :], yb_ref[rows, :], yc_ref[rows, :]], axis=1)
        out = jnp.dot(y, w_ref[...], preferred_element_type=F32) + b_ref[...]
        r = _layer_norm_rows(alpha * res_ref[rows, :] + out, g_ref[...], beta_ref[...])
        out_refs[0][rows, :] = r
        if len(out_refs) > 1:
            out_refs[1][rows, :] = r.astype(BF16)


def _out_projection(ya, yb, yc, w_all, layer, b, res, g, beta, alpha, want_bf16):
    m, d = res.shape
    tm = min(OUT_TM, m)
    row = lambda i: (i, 0)
    const = lambda i: (0, 0)
    out_specs = [pl.BlockSpec((tm, d), row)]
    out_shape = [jax.ShapeDtypeStruct((m, d), F32)]
    if want_bf16:
        out_specs.append(pl.BlockSpec((tm, d), row))
        out_shape.append(jax.ShapeDtypeStruct((m, d), BF16))
    vec = lambda v: v.reshape(1, d)
    return pl.pallas_call(
        functools.partial(_outproj_kernel, alpha=alpha),
        grid=(m // tm,),
        in_specs=[pl.BlockSpec((tm, WA_WIDTH), row),
                  pl.BlockSpec((tm, ML_WIDTH), row),
                  pl.BlockSpec((tm, NA_WIDTH), row),
                  pl.BlockSpec((None, MIX_WIDTH, d), lambda i: (layer, 0, 0)),
                  pl.BlockSpec((1, d), const),
                  pl.BlockSpec((tm, d), row),
                  pl.BlockSpec((1, d), const),
                  pl.BlockSpec((1, d), const)],
        out_specs=out_specs,
        out_shape=out_shape,
        compiler_params=_cparams(("parallel",)),
        name="out_projection",
    )(ya, yb, yc, w_all, vec(b), res, vec(g), vec(beta))


def _silu(z):
    return z * jax.nn.sigmoid(z)


def _scale_q(q):
    return (q.astype(F32) * (HEAD_DIM ** -0.5 * LOG2E)).astype(BF16)


def _wattn_kernel(q_ref, z_ref, k0_ref, k1_ref, k2_ref, k3_ref, v0_ref, v1_ref, v2_ref, v3_ref,
                  bias_e_ref, bias_o_ref, sink_ref, o_ref, s0_ref, s1_ref, p_ref):
    n = pl.program_id(1)
    blk = WA_BLOCK
    npair = WA_KV_HEADS // 2
    rows = 2 * WA_GROUP * blk
    lane = lax.broadcasted_iota(jnp.int32, (blk, 2 * HEAD_DIM), 1)
    left = lane < HEAD_DIM
    k_refs = (k0_ref, k1_ref, k2_ref, k3_ref)
    v_refs = (v0_ref, v1_ref, v2_ref, v3_ref)

    def step(s_read, s_write):
        for half in range(2):
            qrows = slice(half * blk, (half + 1) * blk)
            bias_ref = (bias_e_ref, bias_o_ref)[half]
            for i in range(npair):
                ks = slice(i * 2 * HEAD_DIM, (i + 1) * 2 * HEAD_DIM)
                k3 = jnp.concatenate([r[:, ks] for r in k_refs[half:half + 3]], axis=0)
                tiles = [_scale_q(q_ref[qrows, (WA_GROUP * i + j) * 2 * HEAD_DIM:(WA_GROUP * i + j + 1) * 2 * HEAD_DIM])
                         for j in range(WA_GROUP)]
                zero = jnp.zeros_like(tiles[0])
                lhs = jnp.concatenate([jnp.where(left, t, zero) for t in tiles]
                                      + [jnp.where(left, zero, t) for t in tiles], axis=0)
                s_write[half, i] = lax.dot_general(lhs, k3, (((1,), (1,)), ((), ())),
                                                   preferred_element_type=F32)
                v3 = jnp.concatenate([r[:, ks] for r in v_refs[half:half + 3]], axis=0)
                dens = []
                for c in range(rows // SM_CHUNK):
                    rs = slice(c * SM_CHUNK, (c + 1) * SM_CHUNK)
                    sk = sink_ref[0, 2 * WA_GROUP * i + c * SM_CHUNK // blk] * LOG2E
                    sc = s_read[half, i, rs, :] + bias_ref[i, rs, :]
                    m = jnp.maximum(jnp.max(sc, axis=-1, keepdims=True), sk)
                    p = jnp.exp2(sc - m)
                    dens.append(jnp.sum(p, axis=-1, keepdims=True) + jnp.exp2(sk - m))
                    p_ref[half, i, rs, :] = p.astype(BF16)
                pv = jnp.dot(p_ref[half, i], v3, preferred_element_type=F32)
                o = jnp.concatenate([pv[c * SM_CHUNK:(c + 1) * SM_CHUNK] / dens[c]
                                     for c in range(rows // SM_CHUNK)], axis=0)
                for j in range(WA_GROUP):
                    t = WA_GROUP * i + j
                    ot = jnp.where(left, o[j * blk:(j + 1) * blk],
                                   o[(WA_GROUP + j) * blk:(WA_GROUP + j + 1) * blk])
                    cs = slice(t * 2 * HEAD_DIM, (t + 1) * 2 * HEAD_DIM)
                    o_ref[qrows, cs] = (ot * _silu(z_ref[qrows, cs].astype(F32))).astype(BF16)

    @pl.when(n == 0)
    def _():
        s1_ref[...] = jnp.zeros(s1_ref.shape, F32)

    @pl.when(n % 2 == 0)
    def _():
        step(s1_ref, s0_ref)

    @pl.when(n % 2 == 1)
    def _():
        step(s0_ref, s1_ref)


def _window_attention(u3, bias, sink):
    bsz, s, _ = u3.shape
    nb = s // WA_BLOCK
    npr = nb // 2
    qw, kw = WA_WIDTH, WA_KV_WIDTH
    npair = WA_KV_HEADS // 2
    rows = 2 * WA_GROUP * WA_BLOCK
    clamp = lambda i: jnp.clip(i, 0, nb - 1)
    pair = lambda i: jnp.clip(i, 0, npr - 1)
    kspec = lambda off, c: pl.BlockSpec((None, WA_BLOCK, kw), lambda b, n: (b, clamp(2 * n + off), c))
    vspec = lambda off, c: pl.BlockSpec((None, WA_BLOCK, kw), lambda b, n: (b, clamp(2 * n - 2 + off), c))
    even_type = lambda b, n: (jnp.where(n <= 1, 0, 1), 0, 0, 0)
    odd_type = lambda b, n: (jnp.where(n == npr, 2, 1), 0, 0, 0)
    sshape = (2, npair, rows, 3 * WA_BLOCK)
    bias = bias.reshape(3, npair, rows, 3 * WA_BLOCK)
    return pl.pallas_call(
        _wattn_kernel,
        grid=(bsz, npr + 1),
        in_specs=[pl.BlockSpec((None, 2 * WA_BLOCK, qw), lambda b, n: (b, pair(n), A_Q // qw)),
                  pl.BlockSpec((None, 2 * WA_BLOCK, qw), lambda b, n: (b, pair(n - 1), A_Z // qw)),
                  kspec(-1, A_K // kw), kspec(0, A_K // kw), kspec(1, A_K // kw), kspec(2, A_K // kw),
                  vspec(-1, A_V // kw), vspec(0, A_V // kw), vspec(1, A_V // kw), vspec(2, A_V // kw),
                  pl.BlockSpec((None, npair, rows, 3 * WA_BLOCK), even_type),
                  pl.BlockSpec((None, npair, rows, 3 * WA_BLOCK), odd_type),
                  pl.BlockSpec(memory_space=pltpu.SMEM)],
        out_specs=pl.BlockSpec((None, 2 * WA_BLOCK, qw), lambda b, n: (b, pair(n - 1), 0)),
        out_shape=jax.ShapeDtypeStruct((bsz, s, qw), BF16),
        scratch_shapes=[pltpu.VMEM(sshape, F32), pltpu.VMEM(sshape, F32), pltpu.VMEM(sshape, BF16)],
        compiler_params=_cparams(("parallel", "arbitrary")),
        name="window_attention",
    )(u3, u3, u3, u3, u3, u3, u3, u3, u3, u3, bias, bias, sink.reshape(1, WA_HEADS))


def _t5_bucket_np(rel):
    half = T5_BUCKETS // 2
    max_exact = half // 2
    ret = np.where(rel > 0, half, 0)
    n = np.abs(rel)
    nf = np.maximum(n, 1).astype(np.float64)
    v = np.log(nf / max_exact) / math.log(T5_MAX_DIST / max_exact) * (half - max_exact)
    vr = np.round(v)
    v = np.where(np.abs(v - vr) < 1e-9, vr, v)
    large = np.minimum(max_exact + np.trunc(v).astype(np.int64), half - 1)
    return ret + np.where(n < max_exact, n, large)


def _wbias_kernel(t5t_ref, bucket_ref, o_ref):
    width = 4 * WA_BLOCK
    bk = bucket_ref[...]
    e = lax.broadcasted_iota(jnp.int32, (T5_BUCKETS, width), 0)
    onehot = jnp.where(e == bk, 1.0, 0.0).astype(F32)
    g = _dot_hi(t5t_ref[...], onehot) * LOG2E + jnp.where(bk < 0, NEG, 0.0)
    col = lax.broadcasted_iota(jnp.int32, (WA_BLOCK, 3 * WA_BLOCK), 1)
    for h in range(WA_HEADS):
        row = jnp.broadcast_to(g[h:h + 1, :], (WA_BLOCK, width))
        t = pltpu.roll(row, 3 * WA_BLOCK, 1, stride=1, stride_axis=0)[:, :3 * WA_BLOCK]
        o_ref[0, h] = jnp.where(col < WA_BLOCK, NEG, t)
        o_ref[1, h] = t
        o_ref[2, h] = jnp.where(col >= 2 * WA_BLOCK, NEG, t)


def _window_bias(t5_table):
    rel = np.arange(4 * WA_BLOCK) - 2 * WA_BLOCK
    bucket = np.where(np.abs(rel) <= WINDOW, _t5_bucket_np(rel), -1).astype(np.int32)
    t5t = jnp.pad(t5_table.astype(F32).T, ((0, 16 - WA_HEADS), (0, 0)))
    return pl.pallas_call(
        _wbias_kernel,
        out_shape=jax.ShapeDtypeStruct((3, WA_HEADS, WA_BLOCK, 3 * WA_BLOCK), F32),
        name="window_bias",
    )(t5t, jnp.asarray(bucket).reshape(1, -1))


def _natten_kernel(q_ref, z_ref, kp_ref, kc_ref, kn_ref, vp_ref, vc_ref, vn_ref, bias_ref, o_ref,
                   s0_ref, s1_ref, p_ref):
    n = pl.program_id(1)
    blk = NA_BLOCK
    lane = lax.broadcasted_iota(jnp.int32, (blk, 2 * HEAD_DIM), 1)
    left = lane < HEAD_DIM

    ones_col = jnp.where(lax.broadcasted_iota(jnp.int32, (3 * blk, 2 * HEAD_DIM), 1) == 0, 1.0, 0.0).astype(BF16)

    def step(s_read, s_write):
        for i in range(NA_HEADS // 2):
            cs = slice(i * 2 * HEAD_DIM, (i + 1) * 2 * HEAD_DIM)
            k3 = jnp.concatenate([kp_ref[:, cs], kc_ref[:, cs], kn_ref[:, cs]], axis=0)
            t = _scale_q(q_ref[:, cs])
            zero = jnp.zeros_like(t)
            lhs = jnp.concatenate([jnp.where(left, t, zero), jnp.where(left, zero, t)], axis=0)
            s_write[i] = lax.dot_general(lhs, k3, (((1,), (1,)), ((), ())), preferred_element_type=F32)
            pb = p_ref.at[i % 3]
            for c in range(2 * blk // SM_CHUNK):
                rs = slice(c * SM_CHUNK, (c + 1) * SM_CHUNK)
                sc = s_read[i, rs, :] + bias_ref[i, rs, :]
                m = jnp.max(sc, axis=-1, keepdims=True)
                pb[rs, :] = jnp.exp2(sc - m).astype(BF16)
            if i > 0:
                weighted_values(i - 1)
        weighted_values(NA_HEADS // 2 - 1)

    def weighted_values(i):
        cs = slice(i * 2 * HEAD_DIM, (i + 1) * 2 * HEAD_DIM)
        v3 = jnp.concatenate([vp_ref[:, cs], vc_ref[:, cs], vn_ref[:, cs]], axis=0)
        vaug = jnp.concatenate([v3, ones_col], axis=1)
        out = jnp.dot(p_ref[i % 3], vaug, preferred_element_type=F32)
        o = out[:, :2 * HEAD_DIM] / out[:, 2 * HEAD_DIM:2 * HEAD_DIM + 1]
        ot = jnp.where(left, o[:blk], o[blk:])
        o_ref[:, cs] = (ot * _silu(z_ref[:, cs].astype(F32))).astype(BF16)

    @pl.when(n == 0)
    def _():
        s1_ref[...] = jnp.zeros(s1_ref.shape, F32)

    @pl.when(n % 2 == 0)
    def _():
        step(s1_ref, s0_ref)

    @pl.when(n % 2 == 1)
    def _():
        step(s0_ref, s1_ref)


def _neighbourhood_attention(u3, bias, layer):
    bsz, s, _ = u3.shape
    nblk = s // NA_BLOCK
    w = NA_WIDTH
    npair = NA_HEADS // 2
    clamp = lambda i: jnp.clip(i, 0, nblk - 1)
    spec = lambda off, c: pl.BlockSpec((None, NA_BLOCK, w), lambda b, n: (b, clamp(n + off), c))
    btype = lambda b, n: (layer, jnp.where(n <= 1, 0, jnp.where(n == nblk, 2, 1)), 0, 0, 0)
    sshape = (npair, 2 * NA_BLOCK, 3 * NA_BLOCK)
    return pl.pallas_call(
        _natten_kernel,
        grid=(bsz, nblk + 1),
        in_specs=[spec(0, C_Q // w), spec(-1, C_Z // w),
                  spec(-1, C_K // w), spec(0, C_K // w), spec(1, C_K // w),
                  spec(-2, C_V // w), spec(-1, C_V // w), spec(0, C_V // w),
                  pl.BlockSpec((None, None) + sshape, btype)],
        out_specs=pl.BlockSpec((None, NA_BLOCK, w), lambda b, n: (b, clamp(n - 1), 0)),
        out_shape=jax.ShapeDtypeStruct((bsz, s, w), BF16),
        scratch_shapes=[pltpu.VMEM(sshape, F32), pltpu.VMEM(sshape, F32),
                        pltpu.VMEM((3,) + sshape[1:], BF16)],
        compiler_params=_cparams(("parallel", "arbitrary")),
        name="neighbourhood_attention",
    )(u3, u3, u3, u3, u3, u3, u3, u3, bias.reshape(bias.shape[:2] + sshape))


def _na_valid_rows(rows):
    kh = min(NA_KH_MAX, rows)
    nblk = rows // NA_ROWS
    out = []
    for j in (0, min(1, nblk - 1), nblk - 1):
        r = NA_ROWS * j + np.arange(NA_ROWS)[:, None]
        kr = NA_ROWS * (j - 1) + np.arange(3 * NA_ROWS)[None, :]
        rs = np.clip(r - kh // 2, 0, rows - kh)
        out.append((kr >= rs) & (kr < rs + kh))
    return np.stack(out)


def _nabias_kernel(rpb_ref, o_ref, *, valid):
    w = GRID_W
    j = lax.broadcasted_iota(jnp.int32, (32, 2 * w), 1)
    e = lax.broadcasted_iota(jnp.int32, (32, 2 * w), 0)
    dc = jnp.clip(j - w, -(NA_KW - 1), NA_KW - 1) + NA_KW - 1
    g = _dot_hi(rpb_ref[...], jnp.where(e == dc, 1.0, 0.0).astype(F32)) * LOG2E
    lane = lax.broadcasted_iota(jnp.int32, (w, 2 * w), 1)
    qc = lax.broadcasted_iota(jnp.int32, (w, 2 * w), 0)
    kc = lane & (w - 1)
    col_start = jnp.clip(qc - NA_KW // 2, 0, w - NA_KW)
    col_ok = (kc >= col_start) & (kc < col_start + NA_KW)
    left = lane < w
    neg = jnp.full((w, 2 * w), NEG, F32)

    def toeplitz(dr, shift):
        row = jnp.broadcast_to(g[dr:dr + 1, :], (w, 2 * w))
        return pltpu.roll(row, shift, 1, stride=1, stride_axis=0)

    pair = [jnp.where(col_ok, jnp.where(left, toeplitz(d, w), toeplitz(d + 1, 0)), NEG)
            for d in range(2 * NA_KH_MAX - 2)]
    for ty in range(3):
        for rl in range(NA_ROWS):
            for t in range(3 * NA_ROWS // 2):
                d = 2 * t - rl + NA_KH_MAX - 1 - NA_ROWS
                v0, v1 = bool(valid[ty, rl, 2 * t]), bool(valid[ty, rl, 2 * t + 1])
                if v0 and v1:
                    tile = pair[d]
                elif v0:
                    tile = jnp.where(left, pair[d], NEG)
                elif v1:
                    tile = jnp.where(left, NEG, pair[d])
                else:
                    tile = neg
                o_ref[ty, rl * w:(rl + 1) * w, 2 * t * w:(2 * t + 2) * w] = tile


def _na_bias(rpb_all, rows):
    depth = rpb_all.shape[0]
    rpb_p = jnp.pad(rpb_all.astype(F32), ((0, 0), (0, 0), (0, 1), (0, 1)))
    return pl.pallas_call(
        functools.partial(_nabias_kernel, valid=_na_valid_rows(rows)),
        grid=(depth, NA_HEADS),
        in_specs=[pl.BlockSpec((None, None, 16, 32), lambda l, h: (l, h, 0, 0))],
        out_specs=pl.BlockSpec((None, 3, None, NA_BLOCK, 3 * NA_BLOCK), lambda l, h: (l, 0, h, 0, 0)),
        out_shape=jax.ShapeDtypeStruct((depth, 3, NA_HEADS, NA_BLOCK, 3 * NA_BLOCK), F32),
        compiler_params=_cparams(("parallel", "parallel")),
        name="na_bias",
    )(rpb_p)


def _log_sigmoid(x):
    return jnp.minimum(x, 0.0) - jnp.log(1.0 + jnp.exp(-jnp.abs(x)))


def _tri(n, upper):
    r = lax.broadcasted_iota(jnp.int32, (n, n), 0)
    c = lax.broadcasted_iota(jnp.int32, (n, n), 1)
    return jnp.where((r <= c) if upper else (r >= c), 1.0, 0.0).astype(F32)


def _scan_max(x, axis, reverse, size):
    idx = lax.broadcasted_iota(jnp.int32, x.shape, axis)
    k = 1
    while k < size:
        if reverse:
            shifted = pltpu.roll(x, x.shape[axis] - k, axis)
            ok = idx < size - k
        else:
            shifted = pltpu.roll(x, k, axis)
            ok = idx >= k
        x = jnp.maximum(x, jnp.where(ok, shifted, NEG))
        k *= 2
    return x


def _mlstm_kernel(k_ref, qt_ref, vt_ref, ot_ref, zt_ref, gr_ref, fb_ref, ng_ref, y_ref,
                  state_ref, sprev_ref, rows_ref, gain_ref, et_ref, w2a_ref, w2b_ref, *, seq):
    L = ML_CHUNK
    nc = seq // L
    head = pl.program_id(1)
    scale = ML_HEAD_DIM ** -0.5
    lane = lax.broadcasted_iota(jnp.int32, (1, 128), 1)

    for d in range(2):
        rev = d == 1
        fb = fb_ref[d, head]
        i_r = gr_ref[8 * d + head]
        lf_r = _log_sigmoid(gr_ref[8 * d + 4 + head] + fb)
        b_r = _dot_hi(lf_r, _tri(L, upper=not rev))
        g = jnp.broadcast_to(jnp.sum(lf_r, axis=1, keepdims=True), (nc, L))
        a_r = g - b_r + i_r
        m_loc = jnp.broadcast_to(jnp.max(a_r, axis=1, keepdims=True), (nc, L))
        jr = lax.broadcasted_iota(jnp.int32, (nc, nc), 0)
        jc = lax.broadcasted_iota(jnp.int32, (nc, nc), 1)
        before = jnp.where((jc > jr) if rev else (jc < jr), 1.0, 0.0).astype(F32)
        g_ex = _dot_hi(before, g)
        x = m_loc - (g_ex + g)
        row = lax.broadcasted_iota(jnp.int32, (nc, L), 0)
        if rev:
            x_prev = jnp.where(row < nc - 1, pltpu.roll(x, nc - 1, 0), NEG)
        else:
            x_prev = jnp.where(row >= 1, pltpu.roll(x, 1, 0), NEG)
        m_prev = g_ex + jnp.maximum(_scan_max(x_prev, 0, rev, nc), 0.0)
        m_after = jnp.maximum(g + m_prev, m_loc)
        gain_ref[d, 0] = jnp.exp(g + m_prev - m_after)
        gain_ref[d, 1] = jnp.exp(m_loc - m_after)
        e_r = i_r - b_r
        mu = jnp.maximum(m_prev, _scan_max(e_r, 1, rev, L))
        rows_ref[d, 0] = mu
        rows_ref[d, 1] = jnp.exp(m_prev - mu) * scale
        rows_ref[d, 2] = jnp.exp(-b_r - mu)
        rows_ref[d, 3] = jnp.exp(a_r - m_loc)
        e_pad = jnp.concatenate([e_r, jnp.zeros((128 - nc, L), F32)], axis=0) if nc < 128 else e_r
        et_ref[d] = e_pad.T
        state_ref[d] = jnp.zeros((ML_AUG, ML_HEAD_DIM), F32)

    rr = lax.broadcasted_iota(jnp.int32, (L, L), 0)
    cc = lax.broadcasted_iota(jnp.int32, (L, L), 1)
    ones_row = jnp.where(lax.broadcasted_iota(jnp.int32, (ML_HEAD_DIM, L), 0) == 0, 1.0, 0.0).astype(BF16)

    def state_step(j, carry):
        for d in range(2):
            c = j if d == 0 else nc - 1 - j
            kc = k_ref[pl.ds(pl.multiple_of(c * L, L), L), :]
            vaug_t = jnp.concatenate([vt_ref[c], ones_row], axis=0)
            wv_t = (rows_ref[d, 3, pl.ds(c, 1), :] * vaug_t.astype(F32)).astype(BF16)
            s_loc = jnp.dot(wv_t, kc, preferred_element_type=F32)
            st = state_ref[d]
            sprev_ref[d, c] = st.astype(BF16)
            state_ref[d] = gain_ref[d, 0, pl.ds(c, 1), :] * st + gain_ref[d, 1, pl.ds(c, 1), :] * s_loc
        return carry

    lax.fori_loop(0, nc, state_step, 0, unroll=4)

    ng_col = jnp.broadcast_to(ng_ref[...], (ML_HEAD_DIM, L))
    group = min(4, nc // 2)
    ngroups = nc // group

    def score_stage(g, w2_ref):
        for jj in range(group):
            c = g * group + jj
            kc = k_ref[pl.ds(pl.multiple_of(c * L, L), L), :]
            s_t = jnp.dot(kc, qt_ref[c], preferred_element_type=F32)
            for d in range(2):
                e_col = jnp.sum(jnp.where(lane == c, et_ref[d], 0.0), axis=1, keepdims=True)
                mask = (rr >= cc) if d == 1 else (rr <= cc)
                p_t = jnp.where(mask, jnp.exp(e_col - rows_ref[d, 0, pl.ds(c, 1), :]), 0.0)
                w2_ref[2 * jj + d] = (s_t * p_t * scale).astype(BF16)

    def value_stage(g, w2_ref):
        for jj in range(group):
            c = g * group + jj
            q_t = qt_ref[c]
            vaug_t = jnp.concatenate([vt_ref[c], ones_row], axis=0)
            hs = None
            for d in range(2):
                out_t = (jnp.dot(vaug_t, w2_ref[2 * jj + d], preferred_element_type=F32)
                         + rows_ref[d, 1, pl.ds(c, 1), :]
                         * jnp.dot(sprev_ref[d, c], q_t, preferred_element_type=F32))
                den = out_t[ML_HEAD_DIM:ML_HEAD_DIM + 1]
                h_d = out_t[:ML_HEAD_DIM] / jnp.maximum(jnp.abs(den), rows_ref[d, 2, pl.ds(c, 1), :])
                hs = h_d if hs is None else hs + h_d
            hs = jax.nn.sigmoid(ot_ref[c].astype(F32)) * hs
            mu = jnp.mean(hs, axis=0, keepdims=True)
            hc = hs - mu
            var = jnp.mean(hc * hc, axis=0, keepdims=True)
            y_t = hc * lax.rsqrt(var + LN_EPS) * ng_col * _silu(zt_ref[c].astype(F32))
            y_ref[pl.ds(pl.multiple_of(c * L, L), L), :] = y_t.T.astype(BF16)

    w2b_ref[...] = jnp.zeros(w2b_ref.shape, BF16)

    def pipeline_step(g, carry):
        @pl.when(g % 2 == 0)
        def _():
            score_stage(g, w2a_ref)
            value_stage(jnp.maximum(g - 1, 0), w2b_ref)

        @pl.when(g % 2 == 1)
        def _():
            score_stage(g, w2b_ref)
            value_stage(g - 1, w2a_ref)
        return carry

    lax.fori_loop(0, ngroups, pipeline_step, 0)
    value_stage(ngroups - 1, w2b_ref if (ngroups - 1) % 2 else w2a_ref)


def _mlstm(u3, ut, gates, f_bias, norm_g):
    bsz, s, _ = u3.shape
    d = ML_HEAD_DIM
    nc = s // ML_CHUNK
    tcol = lambda part: pl.BlockSpec((nc, d, ML_CHUNK), lambda b, h: (b, part * ML_HEADS + h, 0))
    return pl.pallas_call(
        functools.partial(_mlstm_kernel, seq=s),
        grid=(bsz, ML_HEADS),
        in_specs=[pl.BlockSpec((None, s, d), lambda b, h: (b, 0, B_K // d + h)),
                  tcol(0), tcol(1), tcol(2), tcol(3),
                  pl.BlockSpec((ML_GATES, None, nc, ML_CHUNK), lambda b, h: (0, b, 0, 0)),
                  pl.BlockSpec(memory_space=pltpu.SMEM),
                  pl.BlockSpec((d, 1), lambda b, h: (h, 0))],
        out_specs=pl.BlockSpec((None, s, d), lambda b, h: (b, 0, h)),
        out_shape=jax.ShapeDtypeStruct((bsz, s, ML_WIDTH), BF16),
        scratch_shapes=[pltpu.VMEM((2, ML_AUG, d), F32),
                        pltpu.VMEM((2, nc, ML_AUG, d), BF16),
                        pltpu.VMEM((2, 4, nc, ML_CHUNK), F32),
                        pltpu.VMEM((2, 2, nc, ML_CHUNK), F32),
                        pltpu.VMEM((2, ML_CHUNK, 128), F32),
                        pltpu.VMEM((2 * min(4, nc // 2), ML_CHUNK, ML_CHUNK), BF16),
                        pltpu.VMEM((2 * min(4, nc // 2), ML_CHUNK, ML_CHUNK), BF16)],
        compiler_params=_cparams(("parallel", "parallel")),
        name="mlstm",
    )(u3, ut, ut, ut, ut, gates, f_bias, norm_g.reshape(ML_WIDTH, 1))


def _a_head_copies(src, dst):
    out = []
    for i in range(2):
        for j in range(WA_GROUP):
            for half in range(2):
                h = 2 * WA_GROUP * i + WA_GROUP * half + j
                out.append((src + h * HEAD_DIM, HEAD_DIM, dst + ((WA_GROUP * i + j) * 2 + half) * HEAD_DIM))
    return out


_NATURAL_COPIES = ([(4624, 3072, C_Q)] + _a_head_copies(0, A_Q) + _a_head_copies(1280, A_Z)
                   + [(768, 512, A_K), (2560, 512, B_K)])
_FEATURE_MAJOR_COPIES = [(2048, 512, 0), (3072, 1536, 512), (4608, ML_GATES, T_WIDTH)]
_OUT_ROW_COPIES = [(s0, n, d0) for s0, n, d0 in _a_head_copies(0, 0)] + [(WA_WIDTH, MIX_WIDTH - WA_WIDTH, WA_WIDTH)]


def _inprep_kernel(wt_ref, wn_ref, wf_ref):
    for src, width, dst in _NATURAL_COPIES:
        wn_ref[dst:dst + width, :] = wt_ref[src:src + width, :].astype(BF16)
    for src, width, dst in _FEATURE_MAJOR_COPIES:
        wf_ref[dst:dst + width, :] = wt_ref[src:src + width, :].astype(BF16)


def _prep_in_weights(w_in):
    depth, d, n = w_in.shape
    tk = 256
    return pl.pallas_call(
        _inprep_kernel,
        grid=(depth, d // tk),
        in_specs=[pl.BlockSpec((None, n, tk), lambda l, i: (l, 0, i))],
        out_specs=[pl.BlockSpec((None, U_WIDTH, tk), lambda l, i: (l, 0, i)),
                   pl.BlockSpec((None, T_WIDTH + ML_GATES, tk), lambda l, i: (l, 0, i))],
        out_shape=[jax.ShapeDtypeStruct((depth, U_WIDTH, d), BF16),
                   jax.ShapeDtypeStruct((depth, T_WIDTH + ML_GATES, d), BF16)],
        compiler_params=_cparams(("parallel", "parallel")),
        name="prep_in_weights",
    )(jnp.swapaxes(w_in, 1, 2))


def _outprep_kernel(w_ref, o_ref):
    for src, rows, dst in _OUT_ROW_COPIES:
        o_ref[dst:dst + rows, :] = w_ref[src:src + rows, :].astype(BF16)


def _prep_out_weights(w_out):
    depth, k, n = w_out.shape
    return pl.pallas_call(
        _outprep_kernel,
        grid=(depth,),
        in_specs=[pl.BlockSpec((None, k, n), lambda l: (l, 0, 0))],
        out_specs=pl.BlockSpec((None, k, n), lambda l: (l, 0, 0)),
        out_shape=jax.ShapeDtypeStruct((depth, k, n), BF16),
        compiler_params=_cparams(("parallel",)),
        name="prep_out_weights",
    )(w_out)


def _gather_columns(v, copies, width):
    out = jnp.zeros(v.shape[:-1] + (width,), v.dtype)
    for src, n, dst in copies:
        out = out.at[..., dst:dst + n].set(v[..., src:src + n])
    return out


def kernel(x, emb_ln_g, emb_ln_b, w_in, b_in, w_out, b_out, ln_g, ln_b, t5_bias, sink, ml_f_bias,
           ml_norm_g, na_rpb):
    bsz, s, d = x.shape
    depth = w_in.shape[0]
    alpha = (2 * depth) ** 0.25
    m = bsz * s
    nc = s // ML_CHUNK
    assert d == D_MODEL and s % NA_BLOCK == 0 and s // GRID_W >= NA_KH_MAX and nc <= 128

    res, hb = _input_norm(x.reshape(m, d), emb_ln_g, emb_ln_b)
    bias_a = _window_bias(t5_bias)
    bias_c = _na_bias(na_rpb, s // GRID_W)
    wn_all, wf_all = _prep_in_weights(w_in)
    wo_all = _prep_out_weights(w_out)
    for l in range(depth):
        bn = _gather_columns(b_in[l], _NATURAL_COPIES, U_WIDTH)
        bf = _gather_columns(b_in[l], _FEATURE_MAJOR_COPIES, T_WIDTH + ML_GATES)
        u3 = _in_projection(hb, wn_all, l, bn).reshape(bsz, s, U_WIDTH)
        ut, g = _t_projection(hb, wf_all, l, bf)
        g = g.reshape(ML_GATES, bsz, nc, ML_CHUNK)
        ya = _window_attention(u3, bias_a, sink[l])
        yb = _mlstm(u3, ut, g, ml_f_bias[l], ml_norm_g[l])
        yc = _neighbourhood_attention(u3, bias_c, l)
        last = l == depth - 1
        outs = _out_projection(ya.reshape(m, WA_WIDTH), yb.reshape(m, ML_WIDTH), yc.reshape(m, NA_WIDTH),
                               wo_all, l, b_out[l], res, ln_g[l], ln_b[l], alpha, not last)
        res = outs[0]
        hb = None if last else outs[1]
    return res.reshape(bsz, s, d)
```

```python
import functools
import math

import numpy as np
import jax
import jax.numpy as jnp
from jax import lax
from jax.experimental import pallas as pl
from jax.experimental.pallas import tpu as pltpu

F32 = jnp.float32
BF16 = jnp.bfloat16

D_MODEL = 2048
HEAD_DIM = 64
LN_EPS = 1e-5
NEG = -1e30
LOG2E = math.log2(math.e)
SM_CHUNK = 64

WA_HEADS = 12
WA_KV_HEADS = 4
WA_GROUP = WA_HEADS // WA_KV_HEADS
WA_WIDTH = WA_HEADS * HEAD_DIM
WA_KV_WIDTH = WA_KV_HEADS * HEAD_DIM
WA_BLOCK = 128
WINDOW = 128
T5_BUCKETS = 32
T5_MAX_DIST = 128
ML_HEADS = 4
ML_HEAD_DIM = 128
ML_WIDTH = ML_HEADS * ML_HEAD_DIM
ML_CHUNK = 128
ML_AUG = 2 * ML_HEAD_DIM
ML_GATES = 4 * ML_HEADS
NA_HEADS = 12
NA_WIDTH = NA_HEADS * HEAD_DIM
GRID_W = 64
NA_KH_MAX = 8
NA_KW = 16
NA_ROWS = 4
NA_BLOCK = NA_ROWS * GRID_W

MIX_WIDTH = WA_WIDTH + ML_WIDTH + NA_WIDTH

C_Q, C_K, C_V, C_Z = 0, 768, 1536, 2304
A_Q, A_Z, A_K, A_V = 3072, 3840, 4608, 4864
B_K = 5120
U_WIDTH = 5632
T_WIDTH = 4 * ML_WIDTH

VMEM_LIMIT = 56 * 1024 * 1024
LN_TM = 512
IN_TM, IN_TN = 512, 2816
T_TM = 1024
OUT_TM = 512


def _cparams(sem):
    return pltpu.CompilerParams(dimension_semantics=sem, vmem_limit_bytes=VMEM_LIMIT)


def _dot_hi(a, b):
    return jnp.dot(a, b, preferred_element_type=F32, precision=lax.Precision.HIGHEST)


def _layer_norm_rows(x, g, b):
    mu = jnp.mean(x, axis=-1, keepdims=True)
    xc = x - mu
    var = jnp.mean(xc * xc, axis=-1, keepdims=True)
    return xc * lax.rsqrt(var + LN_EPS) * g + b


def _ln_kernel(x_ref, g_ref, b_ref, of_ref, ob_ref):
    y = _layer_norm_rows(x_ref[...].astype(F32), g_ref[...], b_ref[...])
    of_ref[...] = y
    ob_ref[...] = y.astype(BF16)


def _input_norm(x2, g, b):
    m, d = x2.shape
    tm = min(LN_TM, m)
    row = pl.BlockSpec((tm, d), lambda i: (i, 0))
    vec = pl.BlockSpec((1, d), lambda i: (0, 0))
    return pl.pallas_call(
        _ln_kernel,
        grid=(m // tm,),
        in_specs=[row, vec, vec],
        out_specs=[row, row],
        out_shape=[jax.ShapeDtypeStruct((m, d), F32), jax.ShapeDtypeStruct((m, d), BF16)],
        compiler_params=_cparams(("parallel",)),
        name="input_norm",
    )(x2, g.reshape(1, d), b.reshape(1, d))


_NT = (((1,), (1,)), ((), ()))


def _inproj_kernel(h_ref, wt_ref, b_ref, u_ref):
    acc = lax.dot_general(h_ref[...], wt_ref[...], _NT, preferred_element_type=F32)
    u_ref[...] = (acc + b_ref[...]).astype(u_ref.dtype)


def _in_projection(hb, wt_all, layer, b):
    m, d = hb.shape
    n = wt_all.shape[1]
    tm, tn = min(IN_TM, m), IN_TN
    return pl.pallas_call(
        _inproj_kernel,
        grid=(n // tn, m // tm),
        in_specs=[pl.BlockSpec((tm, d), lambda j, i: (i, 0)),
                  pl.BlockSpec((None, tn, d), lambda j, i: (layer, j, 0)),
                  pl.BlockSpec((1, tn), lambda j, i: (0, j))],
        out_specs=pl.BlockSpec((tm, tn), lambda j, i: (i, j)),
        out_shape=jax.ShapeDtypeStruct((m, n), BF16),
        compiler_params=_cparams(("parallel", "parallel")),
        name="in_projection",
    )(hb, wt_all, b.reshape(1, n))


def _tproj_kernel(h_ref, wt_ref, bt_ref, ut_ref, gate_ref):
    acc = lax.dot_general(wt_ref[...], h_ref[...], _NT, preferred_element_type=F32) + bt_ref[...]
    for c in range(ut_ref.shape[0]):
        ut_ref[c] = acc[:T_WIDTH, c * ML_CHUNK:(c + 1) * ML_CHUNK].astype(BF16)
    gate_ref[...] = acc[T_WIDTH:]


def _t_projection(hb, wt_all, layer, bt):
    m, d = hb.shape
    tm = min(T_TM, m)
    rows = wt_all.shape[1]
    return pl.pallas_call(
        _tproj_kernel,
        grid=(m // tm,),
        in_specs=[pl.BlockSpec((tm, d), lambda i: (i, 0)),
                  pl.BlockSpec((None, rows, d), lambda i: (layer, 0, 0)),
                  pl.BlockSpec((rows, 1), lambda i: (0, 0))],
        out_specs=[pl.BlockSpec((tm // ML_CHUNK, T_WIDTH, ML_CHUNK), lambda i: (i, 0, 0)),
                   pl.BlockSpec((ML_GATES, tm), lambda i: (0, i))],
        out_shape=[jax.ShapeDtypeStruct((m // ML_CHUNK, T_WIDTH, ML_CHUNK), BF16),
                   jax.ShapeDtypeStruct((ML_GATES, m), F32)],
        compiler_params=_cparams(("parallel",)),
        name="t_projection",
    )(hb, wt_all, bt.reshape(rows, 1))


def _outproj_kernel(ya_ref, yb_ref, yc_ref, w_ref, b_ref, res_ref, g_ref, beta_ref, *out_refs, alpha):
    half = ya_ref.shape[0] // 2
    for rows in (slice(0, half), slice(half, 2 * half)):
        y = jnp.concatenate([ya_ref[rows, :], yb_ref[rows, :], yc_ref[rows, :]], axis=1)
        out = jnp.dot(y, w_ref[...], preferred_element_type=F32) + b_ref[...]
        r = _layer_norm_rows(alpha * res_ref[rows, :] + out, g_ref[...], beta_ref[...])
        out_refs[0][rows, :] = r
        if len(out_refs) > 1:
            out_refs[1][rows, :] = r.astype(BF16)


def _out_projection(ya, yb, yc, w_all, layer, b, res, g, beta, alpha, want_bf16):
    m, d = res.shape
    tm = min(OUT_TM, m)
    row = lambda i: (i, 0)
    const = lambda i: (0, 0)
    out_specs = [pl.BlockSpec((tm, d), row)]
    out_shape = [jax.ShapeDtypeStruct((m, d), F32)]
    if want_bf16:
        out_specs.append(pl.BlockSpec((tm, d), row))
        out_shape.append(jax.ShapeDtypeStruct((m, d), BF16))
    vec = lambda v: v.reshape(1, d)
    return pl.pallas_call(
        functools.partial(_outproj_kernel, alpha=alpha),
        grid=(m // tm,),
        in_specs=[pl.BlockSpec((tm, WA_WIDTH), row),
                  pl.BlockSpec((tm, ML_WIDTH), row),
                  pl.BlockSpec((tm, NA_WIDTH), row),
                  pl.BlockSpec((None, MIX_WIDTH, d), lambda i: (layer, 0, 0)),
                  pl.BlockSpec((1, d), const),
                  pl.BlockSpec((tm, d), row),
                  pl.BlockSpec((1, d), const),
                  pl.BlockSpec((1, d), const)],
        out_specs=out_specs,
        out_shape=out_shape,
        compiler_params=_cparams(("parallel",)),
        name="out_projection",
    )(ya, yb, yc, w_all, vec(b), res, vec(g), vec(beta))


def _silu(z):
    return z * jax.nn.sigmoid(z)


def _scale_q(q):
    return (q.astype(F32) * (HEAD_DIM ** -0.5 * LOG2E)).astype(BF16)


def _wattn_kernel(q_ref, z_ref, k0_ref, k1_ref, k2_ref, k3_ref, v0_ref, v1_ref, v2_ref, v3_ref,
                  bias_e_ref, bias_o_ref, sink_ref, o_ref, s0_ref, s1_ref, p_ref):
    n = pl.program_id(1)
    blk = WA_BLOCK
    npair = WA_KV_HEADS // 2
    rows = 2 * WA_GROUP * blk
    lane = lax.broadcasted_iota(jnp.int32, (blk, 2 * HEAD_DIM), 1)
    left = lane < HEAD_DIM
    k_refs = (k0_ref, k1_ref, k2_ref, k3_ref)
    v_refs = (v0_ref, v1_ref, v2_ref, v3_ref)

    def step(s_read, s_write):
        for half in range(2):
            qrows = slice(half * blk, (half + 1) * blk)
            bias_ref = (bias_e_ref, bias_o_ref)[half]
            for i in range(npair):
                ks = slice(i * 2 * HEAD_DIM, (i + 1) * 2 * HEAD_DIM)
                k3 = jnp.concatenate([r[:, ks] for r in k_refs[half:half + 3]], axis=0)
                tiles = [_scale_q(q_ref[qrows, (WA_GROUP * i + j) * 2 * HEAD_DIM:(WA_GROUP * i + j + 1) * 2 * HEAD_DIM])
                         for j in range(WA_GROUP)]
                zero = jnp.zeros_like(tiles[0])
                lhs = jnp.concatenate([jnp.where(left, t, zero) for t in tiles]
                                      + [jnp.where(left, zero, t) for t in tiles], axis=0)
                s_write[half, i] = lax.dot_general(lhs, k3, (((1,), (1,)), ((), ())),
                                                   preferred_element_type=F32)
                v3 = jnp.concatenate([r[:, ks] for r in v_refs[half:half + 3]], axis=0)
                dens = []
                for c in range(rows // SM_CHUNK):
                    rs = slice(c * SM_CHUNK, (c + 1) * SM_CHUNK)
                    sk = sink_ref[0, 2 * WA_GROUP * i + c * SM_CHUNK // blk] * LOG2E
                    sc = s_read[half, i, rs, :] + bias_ref[i, rs, :]
                    m = jnp.maximum(jnp.max(sc, axis=-1, keepdims=True), sk)
                    p = jnp.exp2(sc - m)
                    dens.append(jnp.sum(p, axis=-1, keepdims=True) + jnp.exp2(sk - m))
                    p_ref[half, i, rs, :] = p.astype(BF16)
                pv = jnp.dot(p_ref[half, i], v3, preferred_element_type=F32)
                o = jnp.concatenate([pv[c * SM_CHUNK:(c + 1) * SM_CHUNK] / dens[c]
                                     for c in range(rows // SM_CHUNK)], axis=0)
                for j in range(WA_GROUP):
                    t = WA_GROUP * i + j
                    ot = jnp.where(left, o[j * blk:(j + 1) * blk],
                                   o[(WA_GROUP + j) * blk:(WA_GROUP + j + 1) * blk])
                    cs = slice(t * 2 * HEAD_DIM, (t + 1) * 2 * HEAD_DIM)
                    o_ref[qrows, cs] = (ot * _silu(z_ref[qrows, cs].astype(F32))).astype(BF16)

    @pl.when(n == 0)
    def _():
        s1_ref[...] = jnp.zeros(s1_ref.shape, F32)

    @pl.when(n % 2 == 0)
    def _():
        step(s1_ref, s0_ref)

    @pl.when(n % 2 == 1)
    def _():
        step(s0_ref, s1_ref)


def _window_attention(u3, bias, sink):
    bsz, s, _ = u3.shape
    nb = s // WA_BLOCK
    npr = nb // 2
    qw, kw = WA_WIDTH, WA_KV_WIDTH
    npair = WA_KV_HEADS // 2
    rows = 2 * WA_GROUP * WA_BLOCK
    clamp = lambda i: jnp.clip(i, 0, nb - 1)
    pair = lambda i: jnp.clip(i, 0, npr - 1)
    kspec = lambda off, c: pl.BlockSpec((None, WA_BLOCK, kw), lambda b, n: (b, clamp(2 * n + off), c))
    vspec = lambda off, c: pl.BlockSpec((None, WA_BLOCK, kw), lambda b, n: (b, clamp(2 * n - 2 + off), c))
    even_type = lambda b, n: (jnp.where(n <= 1, 0, 1), 0, 0, 0)
    odd_type = lambda b, n: (jnp.where(n == npr, 2, 1), 0, 0, 0)
    sshape = (2, npair, rows, 3 * WA_BLOCK)
    bias = bias.reshape(3, npair, rows, 3 * WA_BLOCK)
    return pl.pallas_call(
        _wattn_kernel,
        grid=(bsz, npr + 1),
        in_specs=[pl.BlockSpec((None, 2 * WA_BLOCK, qw), lambda b, n: (b, pair(n), A_Q // qw)),
                  pl.BlockSpec((None, 2 * WA_BLOCK, qw), lambda b, n: (b, pair(n - 1), A_Z // qw)),
                  kspec(-1, A_K // kw), kspec(0, A_K // kw), kspec(1, A_K // kw), kspec(2, A_K // kw),
                  vspec(-1, A_V // kw), vspec(0, A_V // kw), vspec(1, A_V // kw), vspec(2, A_V // kw),
                  pl.BlockSpec((None, npair, rows, 3 * WA_BLOCK), even_type),
                  pl.BlockSpec((None, npair, rows, 3 * WA_BLOCK), odd_type),
                  pl.BlockSpec(memory_space=pltpu.SMEM)],
        out_specs=pl.BlockSpec((None, 2 * WA_BLOCK, qw), lambda b, n: (b, pair(n - 1), 0)),
        out_shape=jax.ShapeDtypeStruct((bsz, s, qw), BF16),
        scratch_shapes=[pltpu.VMEM(sshape, F32), pltpu.VMEM(sshape, F32), pltpu.VMEM(sshape, BF16)],
        compiler_params=_cparams(("parallel", "arbitrary")),
        name="window_attention",
    )(u3, u3, u3, u3, u3, u3, u3, u3, u3, u3, bias, bias, sink.reshape(1, WA_HEADS))


def _t5_bucket_np(rel):
    half = T5_BUCKETS // 2
    max_exact = half // 2
    ret = np.where(rel > 0, half, 0)
    n = np.abs(rel)
    nf = np.maximum(n, 1).astype(np.float64)
    v = np.log(nf / max_exact) / math.log(T5_MAX_DIST / max_exact) * (half - max_exact)
    vr = np.round(v)
    v = np.where(np.abs(v - vr) < 1e-9, vr, v)
    large = np.minimum(max_exact + np.trunc(v).astype(np.int64), half - 1)
    return ret + np.where(n < max_exact, n, large)


def _wbias_kernel(t5t_ref, bucket_ref, o_ref):
    width = 4 * WA_BLOCK
    bk = bucket_ref[...]
    e = lax.broadcasted_iota(jnp.int32, (T5_BUCKETS, width), 0)
    onehot = jnp.where(e == bk, 1.0, 0.0).astype(F32)
    g = _dot_hi(t5t_ref[...], onehot) * LOG2E + jnp.where(bk < 0, NEG, 0.0)
    col = lax.broadcasted_iota(jnp.int32, (WA_BLOCK, 3 * WA_BLOCK), 1)
    for h in range(WA_HEADS):
        row = jnp.broadcast_to(g[h:h + 1, :], (WA_BLOCK, width))
        t = pltpu.roll(row, 3 * WA_BLOCK, 1, stride=1, stride_axis=0)[:, :3 * WA_BLOCK]
        o_ref[0, h] = jnp.where(col < WA_BLOCK, NEG, t)
        o_ref[1, h] = t
        o_ref[2, h] = jnp.where(col >= 2 * WA_BLOCK, NEG, t)


def _window_bias(t5_table):
    rel = np.arange(4 * WA_BLOCK) - 2 * WA_BLOCK
    bucket = np.where(np.abs(rel) <= WINDOW, _t5_bucket_np(rel), -1).astype(np.int32)
    t5t = jnp.pad(t5_table.astype(F32).T, ((0, 16 - WA_HEADS), (0, 0)))
    return pl.pallas_call(
        _wbias_kernel,
        out_shape=jax.ShapeDtypeStruct((3, WA_HEADS, WA_BLOCK, 3 * WA_BLOCK), F32),
        name="window_bias",
    )(t5t, jnp.asarray(bucket).reshape(1, -1))


def _natten_kernel(q_ref, z_ref, kp_ref, kc_ref, kn_ref, vp_ref, vc_ref, vn_ref, bias_ref, o_ref,
                   s0_ref, s1_ref, p_ref):
    n = pl.program_id(1)
    blk = NA_BLOCK
    lane = lax.broadcasted_iota(jnp.int32, (blk, 2 * HEAD_DIM), 1)
    left = lane < HEAD_DIM

    ones_col = jnp.where(lax.broadcasted_iota(jnp.int32, (3 * blk, 2 * HEAD_DIM), 1) == 0, 1.0, 0.0).astype(BF16)

    def step(s_read, s_write):
        for i in range(NA_HEADS // 2):
            cs = slice(i * 2 * HEAD_DIM, (i + 1) * 2 * HEAD_DIM)
            k3 = jnp.concatenate([kp_ref[:, cs], kc_ref[:, cs], kn_ref[:, cs]], axis=0)
            t = _scale_q(q_ref[:, cs])
            zero = jnp.zeros_like(t)
            lhs = jnp.concatenate([jnp.where(left, t, zero), jnp.where(left, zero, t)], axis=0)
            s_write[i] = lax.dot_general(lhs, k3, (((1,), (1,)), ((), ())), preferred_element_type=F32)
            pb = p_ref.at[i % 3]
            for c in range(2 * blk // SM_CHUNK):
                rs = slice(c * SM_CHUNK, (c + 1) * SM_CHUNK)
                sc = s_read[i, rs, :] + bias_ref[i, rs, :]
                m = jnp.max(sc, axis=-1, keepdims=True)
                pb[rs, :] = jnp.exp2(sc - m).astype(BF16)
            if i > 0:
                weighted_values(i - 1)
        weighted_values(NA_HEADS // 2 - 1)

    def weighted_values(i):
        cs = slice(i * 2 * HEAD_DIM, (i + 1) * 2 * HEAD_DIM)
        v3 = jnp.concatenate([vp_ref[:, cs], vc_ref[:, cs], vn_ref[:, cs]], axis=0)
        vaug = jnp.concatenate([v3, ones_col], axis=1)
        out = jnp.dot(p_ref[i % 3], vaug, preferred_element_type=F32)
        o = out[:, :2 * HEAD_DIM] / out[:, 2 * HEAD_DIM:2 * HEAD_DIM + 1]
        ot = jnp.where(left, o[:blk], o[blk:])
        o_ref[:, cs] = (ot * _silu(z_ref[:, cs].astype(F32))).astype(BF16)

    @pl.when(n == 0)
    def _():
        s1_ref[...] = jnp.zeros(s1_ref.shape, F32)

    @pl.when(n % 2 == 0)
    def _():
        step(s1_ref, s0_ref)

    @pl.when(n % 2 == 1)
    def _():
        step(s0_ref, s1_ref)


def _neighbourhood_attention(u3, bias, layer):
    bsz, s, _ = u3.shape
    nblk = s // NA_BLOCK
    w = NA_WIDTH
    npair = NA_HEADS // 2
    clamp = lambda i: jnp.clip(i, 0, nblk - 1)
    spec = lambda off, c: pl.BlockSpec((None, NA_BLOCK, w), lambda b, n: (b, clamp(n + off), c))
    btype = lambda b, n: (layer, jnp.where(n <= 1, 0, jnp.where(n == nblk, 2, 1)), 0, 0, 0)
    sshape = (npair, 2 * NA_BLOCK, 3 * NA_BLOCK)
    return pl.pallas_call(
        _natten_kernel,
        grid=(bsz, nblk + 1),
        in_specs=[spec(0, C_Q // w), spec(-1, C_Z // w),
                  spec(-1, C_K // w), spec(0, C_K // w), spec(1, C_K // w),
                  spec(-2, C_V // w), spec(-1, C_V // w), spec(0, C_V // w),
                  pl.BlockSpec((None, None) + sshape, btype)],
        out_specs=pl.BlockSpec((None, NA_BLOCK, w), lambda b, n: (b, clamp(n - 1), 0)),
        out_shape=jax.ShapeDtypeStruct((bsz, s, w), BF16),
        scratch_shapes=[pltpu.VMEM(sshape, F32), pltpu.VMEM(sshape, F32),
                        pltpu.VMEM((3,) + sshape[1:], BF16)],
        compiler_params=_cparams(("parallel", "arbitrary")),
        name="neighbourhood_attention",
    )(u3, u3, u3, u3, u3, u3, u3, u3, bias.reshape(bias.shape[:2] + sshape))


def _na_valid_rows(rows):
    kh = min(NA_KH_MAX, rows)
    nblk = rows // NA_ROWS
    out = []
    for j in (0, min(1, nblk - 1), nblk - 1):
        r = NA_ROWS * j + np.arange(NA_ROWS)[:, None]
        kr = NA_ROWS * (j - 1) + np.arange(3 * NA_ROWS)[None, :]
        rs = np.clip(r - kh // 2, 0, rows - kh)
        out.append((kr >= rs) & (kr < rs + kh))
    return np.stack(out)


def _nabias_kernel(rpb_ref, o_ref, *, valid):
    w = GRID_W
    j = lax.broadcasted_iota(jnp.int32, (32, 2 * w), 1)
    e = lax.broadcasted_iota(jnp.int32, (32, 2 * w), 0)
    dc = jnp.clip(j - w, -(NA_KW - 1), NA_KW - 1) + NA_KW - 1
    g = _dot_hi(rpb_ref[...], jnp.where(e == dc, 1.0, 0.0).astype(F32)) * LOG2E
    lane = lax.broadcasted_iota(jnp.int32, (w, 2 * w), 1)
    qc = lax.broadcasted_iota(jnp.int32, (w, 2 * w), 0)
    kc = lane & (w - 1)
    col_start = jnp.clip(qc - NA_KW // 2, 0, w - NA_KW)
    col_ok = (kc >= col_start) & (kc < col_start + NA_KW)
    left = lane < w
    neg = jnp.full((w, 2 * w), NEG, F32)

    def toeplitz(dr, shift):
        row = jnp.broadcast_to(g[dr:dr + 1, :], (w, 2 * w))
        return pltpu.roll(row, shift, 1, stride=1, stride_axis=0)

    pair = [jnp.where(col_ok, jnp.where(left, toeplitz(d, w), toeplitz(d + 1, 0)), NEG)
            for d in range(2 * NA_KH_MAX - 2)]
    for ty in range(3):
        for rl in range(NA_ROWS):
            for t in range(3 * NA_ROWS // 2):
                d = 2 * t - rl + NA_KH_MAX - 1 - NA_ROWS
                v0, v1 = bool(valid[ty, rl, 2 * t]), bool(valid[ty, rl, 2 * t + 1])
                if v0 and v1:
                    tile = pair[d]
                elif v0:
                    tile = jnp.where(left, pair[d], NEG)
                elif v1:
                    tile = jnp.where(left, NEG, pair[d])
                else:
                    tile = neg
                o_ref[ty, rl * w:(rl + 1) * w, 2 * t * w:(2 * t + 2) * w] = tile


def _na_bias(rpb_all, rows):
    depth = rpb_all.shape[0]
    rpb_p = jnp.pad(rpb_all.astype(F32), ((0, 0), (0, 0), (0, 1), (0, 1)))
    return pl.pallas_call(
        functools.partial(_nabias_kernel, valid=_na_valid_rows(rows)),
        grid=(depth, NA_HEADS),
        in_specs=[pl.BlockSpec((None, None, 16, 32), lambda l, h: (l, h, 0, 0))],
        out_specs=pl.BlockSpec((None, 3, None, NA_BLOCK, 3 * NA_BLOCK), lambda l, h: (l, 0, h, 0, 0)),
        out_shape=jax.ShapeDtypeStruct((depth, 3, NA_HEADS, NA_BLOCK, 3 * NA_BLOCK), F32),
        compiler_params=_cparams(("parallel", "parallel")),
        name="na_bias",
    )(rpb_p)


def _log_sigmoid(x):
    return jnp.minimum(x, 0.0) - jnp.log(1.0 + jnp.exp(-jnp.abs(x)))


def _tri(n, upper):
    r = lax.broadcasted_iota(jnp.int32, (n, n), 0)
    c = lax.broadcasted_iota(jnp.int32, (n, n), 1)
    return jnp.where((r <= c) if upper else (r >= c), 1.0, 0.0).astype(F32)


def _scan_max(x, axis, reverse, size):
    idx = lax.broadcasted_iota(jnp.int32, x.shape, axis)
    k = 1
    while k < size:
        if reverse:
            shifted = pltpu.roll(x, x.shape[axis] - k, axis)
            ok = idx < size - k
        else:
            shifted = pltpu.roll(x, k, axis)
            ok = idx >= k
        x = jnp.maximum(x, jnp.where(ok, shifted, NEG))
        k *= 2
    return x


def _mlstm_kernel(k_ref, qt_ref, vt_ref, ot_ref, zt_ref, gr_ref, fb_ref, ng_ref, y_ref,
                  state_ref, sprev_ref, rows_ref, gain_ref, et_ref, w2a_ref, w2b_ref, *, seq):
    L = ML_CHUNK
    nc = seq // L
    head = pl.program_id(1)
    scale = ML_HEAD_DIM ** -0.5
    lane = lax.broadcasted_iota(jnp.int32, (1, 128), 1)

    for d in range(2):
        rev = d == 1
        fb = fb_ref[d, head]
        i_r = gr_ref[8 * d + head]
        lf_r = _log_sigmoid(gr_ref[8 * d + 4 + head] + fb)
        b_r = _dot_hi(lf_r, _tri(L, upper=not rev))
        g = jnp.broadcast_to(jnp.sum(lf_r, axis=1, keepdims=True), (nc, L))
        a_r = g - b_r + i_r
        m_loc = jnp.broadcast_to(jnp.max(a_r, axis=1, keepdims=True), (nc, L))
        jr = lax.broadcasted_iota(jnp.int32, (nc, nc), 0)
        jc = lax.broadcasted_iota(jnp.int32, (nc, nc), 1)
        before = jnp.where((jc > jr) if rev else (jc < jr), 1.0, 0.0).astype(F32)
        g_ex = _dot_hi(before, g)
        x = m_loc - (g_ex + g)
        row = lax.broadcasted_iota(jnp.int32, (nc, L), 0)
        if rev:
            x_prev = jnp.where(row < nc - 1, pltpu.roll(x, nc - 1, 0), NEG)
        else:
            x_prev = jnp.where(row >= 1, pltpu.roll(x, 1, 0), NEG)
        m_prev = g_ex + jnp.maximum(_scan_max(x_prev, 0, rev, nc), 0.0)
        m_after = jnp.maximum(g + m_prev, m_loc)
        gain_ref[d, 0] = jnp.exp(g + m_prev - m_after)
        gain_ref[d, 1] = jnp.exp(m_loc - m_after)
        e_r = i_r - b_r
        mu = jnp.maximum(m_prev, _scan_max(e_r, 1, rev, L))
        rows_ref[d, 0] = mu
        rows_ref[d, 1] = jnp.exp(m_prev - mu) * scale
        rows_ref[d, 2] = jnp.exp(-b_r - mu)
        rows_ref[d, 3] = jnp.exp(a_r - m_loc)
        e_pad = jnp.concatenate([e_r, jnp.zeros((128 - nc, L), F32)], axis=0) if nc < 128 else e_r
        et_ref[d] = e_pad.T
        state_ref[d] = jnp.zeros((ML_AUG, ML_HEAD_DIM), F32)

    rr = lax.broadcasted_iota(jnp.int32, (L, L), 0)
    cc = lax.broadcasted_iota(jnp.int32, (L, L), 1)
    ones_row = jnp.where(lax.broadcasted_iota(jnp.int32, (ML_HEAD_DIM, L), 0) == 0, 1.0, 0.0).astype(BF16)

    def state_step(j, carry):
        for d in range(2):
            c = j if d == 0 else nc - 1 - j
            kc = k_ref[pl.ds(pl.multiple_of(c * L, L), L), :]
            vaug_t = jnp.concatenate([vt_ref[c], ones_row], axis=0)
            wv_t = (rows_ref[d, 3, pl.ds(c, 1), :] * vaug_t.astype(F32)).astype(BF16)
            s_loc = jnp.dot(wv_t, kc, preferred_element_type=F32)
            st = state_ref[d]
            sprev_ref[d, c] = st.astype(BF16)
            state_ref[d] = gain_ref[d, 0, pl.ds(c, 1), :] * st + gain_ref[d, 1, pl.ds(c, 1), :] * s_loc
        return carry

    lax.fori_loop(0, nc, state_step, 0, unroll=4)

    ng_col = jnp.broadcast_to(ng_ref[...], (ML_HEAD_DIM, L))
    group = min(4, nc // 2)
    ngroups = nc // group

    def score_stage(g, w2_ref):
        for jj in range(group):
            c = g * group + jj
            kc = k_ref[pl.ds(pl.multiple_of(c * L, L), L), :]
            s_t = jnp.dot(kc, qt_ref[c], preferred_element_type=F32)
            for d in range(2):
                e_col = jnp.sum(jnp.where(lane == c, et_ref[d], 0.0), axis=1, keepdims=True)
                mask = (rr >= cc) if d == 1 else (rr <= cc)
                p_t = jnp.where(mask, jnp.exp(e_col - rows_ref[d, 0, pl.ds(c, 1), :]), 0.0)
                w2_ref[2 * jj + d] = (s_t * p_t * scale).astype(BF16)

    def value_stage(g, w2_ref):
        for jj in range(group):
            c = g * group + jj
            q_t = qt_ref[c]
            vaug_t = jnp.concatenate([vt_ref[c], ones_row], axis=0)
            hs = None
            for d in range(2):
                out_t = (jnp.dot(vaug_t, w2_ref[2 * jj + d], preferred_element_type=F32)
                         + rows_ref[d, 1, pl.ds(c, 1), :]
                         * jnp.dot(sprev_ref[d, c], q_t, preferred_element_type=F32))
                den = out_t[ML_HEAD_DIM:ML_HEAD_DIM + 1]
                h_d = out_t[:ML_HEAD_DIM] / jnp.maximum(jnp.abs(den), rows_ref[d, 2, pl.ds(c, 1), :])
                hs = h_d if hs is None else hs + h_d
            hs = jax.nn.sigmoid(ot_ref[c].astype(F32)) * hs
            mu = jnp.mean(hs, axis=0, keepdims=True)
            hc = hs - mu
            var = jnp.mean(hc * hc, axis=0, keepdims=True)
            y_t = hc * lax.rsqrt(var + LN_EPS) * ng_col * _silu(zt_ref[c].astype(F32))
            y_ref[pl.ds(pl.multiple_of(c * L, L), L), :] = y_t.T.astype(BF16)

    w2b_ref[...] = jnp.zeros(w2b_ref.shape, BF16)

    def pipeline_step(g, carry):
        @pl.when(g % 2 == 0)
        def _():
            score_stage(g, w2a_ref)
            value_stage(jnp.maximum(g - 1, 0), w2b_ref)

        @pl.when(g % 2 == 1)
        def _():
            score_stage(g, w2b_ref)
            value_stage(g - 1, w2a_ref)
        return carry

    lax.fori_loop(0, ngroups, pipeline_step, 0)
    value_stage(ngroups - 1, w2b_ref if (ngroups - 1) % 2 else w2a_ref)


def _mlstm(u3, ut, gates, f_bias, norm_g):
    bsz, s, _ = u3.shape
    d = ML_HEAD_DIM
    nc = s // ML_CHUNK
    tcol = lambda part: pl.BlockSpec((nc, d, ML_CHUNK), lambda b, h: (b, part * ML_HEADS + h, 0))
    return pl.pallas_call(
        functools.partial(_mlstm_kernel, seq=s),
        grid=(bsz, ML_HEADS),
        in_specs=[pl.BlockSpec((None, s, d), lambda b, h: (b, 0, B_K // d + h)),
                  tcol(0), tcol(1), tcol(2), tcol(3),
                  pl.BlockSpec((ML_GATES, None, nc, ML_CHUNK), lambda b, h: (0, b, 0, 0)),
                  pl.BlockSpec(memory_space=pltpu.SMEM),
                  pl.BlockSpec((d, 1), lambda b, h: (h, 0))],
        out_specs=pl.BlockSpec((None, s, d), lambda b, h: (b, 0, h)),
        out_shape=jax.ShapeDtypeStruct((bsz, s, ML_WIDTH), BF16),
        scratch_shapes=[pltpu.VMEM((2, ML_AUG, d), F32),
                        pltpu.VMEM((2, nc, ML_AUG, d), BF16),
                        pltpu.VMEM((2, 4, nc, ML_CHUNK), F32),
                        pltpu.VMEM((2, 2, nc, ML_CHUNK), F32),
                        pltpu.VMEM((2, ML_CHUNK, 128), F32),
                        pltpu.VMEM((2 * min(4, nc // 2), ML_CHUNK, ML_CHUNK), BF16),
                        pltpu.VMEM((2 * min(4, nc // 2), ML_CHUNK, ML_CHUNK), BF16)],
        compiler_params=_cparams(("parallel", "parallel")),
        name="mlstm",
    )(u3, ut, ut, ut, ut, gates, f_bias, norm_g.reshape(ML_WIDTH, 1))


def _a_head_copies(src, dst):
    out = []
    for i in range(2):
        for j in range(WA_GROUP):
            for half in range(2):
                h = 2 * WA_GROUP * i + WA_GROUP * half + j
                out.append((src + h * HEAD_DIM, HEAD_DIM, dst + ((WA_GROUP * i + j) * 2 + half) * HEAD_DIM))
    return out


_NATURAL_COPIES = ([(4624, 3072, C_Q)] + _a_head_copies(0, A_Q) + _a_head_copies(1280, A_Z)
                   + [(768, 512, A_K), (2560, 512, B_K)])
_FEATURE_MAJOR_COPIES = [(2048, 512, 0), (3072, 1536, 512), (4608, ML_GATES, T_WIDTH)]
_OUT_ROW_COPIES = [(s0, n, d0) for s0, n, d0 in _a_head_copies(0, 0)] + [(WA_WIDTH, MIX_WIDTH - WA_WIDTH, WA_WIDTH)]


def _inprep_kernel(wt_ref, wn_ref, wf_ref):
    for src, width, dst in _NATURAL_COPIES:
        wn_ref[dst:dst + width, :] = wt_ref[src:src + width, :].astype(BF16)
    for src, width, dst in _FEATURE_MAJOR_COPIES:
        wf_ref[dst:dst + width, :] = wt_ref[src:src + width, :].astype(BF16)


def _prep_in_weights(w_in):
    depth, d, n = w_in.shape
    tk = 256
    return pl.pallas_call(
        _inprep_kernel,
        grid=(depth, d // tk),
        in_specs=[pl.BlockSpec((None, n, tk), lambda l, i: (l, 0, i))],
        out_specs=[pl.BlockSpec((None, U_WIDTH, tk), lambda l, i: (l, 0, i)),
                   pl.BlockSpec((None, T_WIDTH + ML_GATES, tk), lambda l, i: (l, 0, i))],
        out_shape=[jax.ShapeDtypeStruct((depth, U_WIDTH, d), BF16),
                   jax.ShapeDtypeStruct((depth, T_WIDTH + ML_GATES, d), BF16)],
        compiler_params=_cparams(("parallel", "parallel")),
        name="prep_in_weights",
    )(jnp.swapaxes(w_in, 1, 2))


def _outprep_kernel(w_ref, o_ref):
    for src, rows, dst in _OUT_ROW_COPIES:
        o_ref[dst:dst + rows, :] = w_ref[src:src + rows, :].astype(BF16)


def _prep_out_weights(w_out):
    depth, k, n = w_out.shape
    return pl.pallas_call(
        _outprep_kernel,
        grid=(depth,),
        in_specs=[pl.BlockSpec((None, k, n), lambda l: (l, 0, 0))],
        out_specs=pl.BlockSpec((None, k, n), lambda l: (l, 0, 0)),
        out_shape=jax.ShapeDtypeStruct((depth, k, n), BF16),
        compiler_params=_cparams(("parallel",)),
        name="prep_out_weights",
    )(w_out)


def _gather_columns(v, copies, width):
    out = jnp.zeros(v.shape[:-1] + (width,), v.dtype)
    for src, n, dst in copies:
        out = out.at[..., dst:dst + n].set(v[..., src:src + n])
    return out


def kernel(x, emb_ln_g, emb_ln_b, w_in, b_in, w_out, b_out, ln_g, ln_b, t5_bias, sink, ml_f_bias,
           ml_norm_g, na_rpb):
    bsz, s, d = x.shape
    depth = w_in.shape[0]
    alpha = (2 * depth) ** 0.25
    m = bsz * s
    nc = s // ML_CHUNK
    assert d == D_MODEL and s % NA_BLOCK == 0 and s // GRID_W >= NA_KH_MAX and nc <= 128

    res, hb = _input_norm(x.reshape(m, d), emb_ln_g, emb_ln_b)
    bias_a = _window_bias(t5_bias)
    bias_c = _na_bias(na_rpb, s // GRID_W)
    wn_all, wf_all = _prep_in_weights(w_in)
    wo_all = _prep_out_weights(w_out)
    for l in range(depth):
        bn = _gather_columns(b_in[l], _NATURAL_COPIES, U_WIDTH)
        bf = _gather_columns(b_in[l], _FEATURE_MAJOR_COPIES, T_WIDTH + ML_GATES)
        u3 = _in_projection(hb, wn_all, l, bn).reshape(bsz, s, U_WIDTH)
        ut, g = _t_projection(hb, wf_all, l, bf)
        g = g.reshape(ML_GATES, bsz, nc, ML_CHUNK)
        ya = _window_attention(u3, bias_a, sink[l])
        yb = _mlstm(u3, ut, g, ml_f_bias[l], ml_norm_g[l])
        yc = _neighbourhood_attention(u3, bias_c, l)
        last = l == depth - 1
        outs = _out_projection(ya.reshape(m, WA_WIDTH), yb.reshape(m, ML_WIDTH), yc.reshape(m, NA_WIDTH),
                               wo_all, l, b_out[l], res, ln_g[l], ln_b[l], alpha, not last)
        res = outs[0]
        hb = None if last else outs[1]
    return res.reshape(bsz, s, d)
```

```python
import functools
import math

import numpy as np
import jax
import jax.numpy as jnp
from jax import lax
from jax.experimental import pallas as pl
from jax.experimental.pallas import tpu as pltpu

F32 = jnp.float32
BF16 = jnp.bfloat16

D_MODEL = 2048
HEAD_DIM = 64
LN_EPS = 1e-5
NEG = -1e30
LOG2E = math.log2(math.e)
SM_CHUNK = 64

WA_HEADS = 12
WA_KV_HEADS = 4
WA_GROUP = WA_HEADS // WA_KV_HEADS
WA_WIDTH = WA_HEADS * HEAD_DIM
WA_KV_WIDTH = WA_KV_HEADS * HEAD_DIM
WA_BLOCK = 128
WINDOW = 128
T5_BUCKETS = 32
T5_MAX_DIST = 128
ML_HEADS = 4
ML_HEAD_DIM = 128
ML_WIDTH = ML_HEADS * ML_HEAD_DIM
ML_CHUNK = 128
ML_AUG = 2 * ML_HEAD_DIM
ML_GATES = 4 * ML_HEADS
ML_GROUP = 8
NA_HEADS = 12
NA_WIDTH = NA_HEADS * HEAD_DIM
GRID_W = 64
NA_KH_MAX = 8
NA_KW = 16
NA_ROWS = 4
NA_BLOCK = NA_ROWS * GRID_W

MIX_WIDTH = WA_WIDTH + ML_WIDTH + NA_WIDTH

C_Q, C_K, C_V, C_Z = 0, 768, 1536, 2304
A_Q, A_Z, A_K, A_V = 3072, 3840, 4608, 4864
B_K = 5120
U_WIDTH = 5632
T_WIDTH = 4 * ML_WIDTH

VMEM_LIMIT = 56 * 1024 * 1024
LN_TM = 512
IN_TM, IN_TN = 512, 2816
T_TM = 1024
OUT_TM = 512


def _cparams(sem):
    return pltpu.CompilerParams(dimension_semantics=sem, vmem_limit_bytes=VMEM_LIMIT)


def _dot_hi(a, b):
    return jnp.dot(a, b, preferred_element_type=F32, precision=lax.Precision.HIGHEST)


def _layer_norm_rows(x, g, b):
    mu = jnp.mean(x, axis=-1, keepdims=True)
    xc = x - mu
    var = jnp.mean(xc * xc, axis=-1, keepdims=True)
    return xc * lax.rsqrt(var + LN_EPS) * g + b


def _ln_kernel(x_ref, g_ref, b_ref, of_ref, ob_ref):
    y = _layer_norm_rows(x_ref[...].astype(F32), g_ref[...], b_ref[...])
    of_ref[...] = y
    ob_ref[...] = y.astype(BF16)


def _input_norm(x2, g, b):
    m, d = x2.shape
    tm = min(LN_TM, m)
    row = pl.BlockSpec((tm, d), lambda i: (i, 0))
    vec = pl.BlockSpec((1, d), lambda i: (0, 0))
    return pl.pallas_call(
        _ln_kernel,
        grid=(m // tm,),
        in_specs=[row, vec, vec],
        out_specs=[row, row],
        out_shape=[jax.ShapeDtypeStruct((m, d), F32), jax.ShapeDtypeStruct((m, d), BF16)],
        compiler_params=_cparams(("parallel",)),
        name="input_norm",
    )(x2, g.reshape(1, d), b.reshape(1, d))


_NT = (((1,), (1,)), ((), ()))


def _inproj_kernel(h_ref, wt_ref, b_ref, u_ref):
    acc = lax.dot_general(h_ref[...], wt_ref[...], _NT, preferred_element_type=F32)
    u_ref[...] = (acc + b_ref[...]).astype(u_ref.dtype)


def _in_projection(hb, wt_all, layer, b):
    m, d = hb.shape
    n = wt_all.shape[1]
    tm, tn = min(IN_TM, m), IN_TN
    return pl.pallas_call(
        _inproj_kernel,
        grid=(n // tn, m // tm),
        in_specs=[pl.BlockSpec((tm, d), lambda j, i: (i, 0)),
                  pl.BlockSpec((None, tn, d), lambda j, i: (layer, j, 0)),
                  pl.BlockSpec((1, tn), lambda j, i: (0, j))],
        out_specs=pl.BlockSpec((tm, tn), lambda j, i: (i, j)),
        out_shape=jax.ShapeDtypeStruct((m, n), BF16),
        compiler_params=_cparams(("parallel", "parallel")),
        name="in_projection",
    )(hb, wt_all, b.reshape(1, n))


def _tproj_kernel(h_ref, wt_ref, bt_ref, ut_ref, gate_ref):
    acc = lax.dot_general(wt_ref[...], h_ref[...], _NT, preferred_element_type=F32) + bt_ref[...]
    for c in range(ut_ref.shape[0]):
        ut_ref[c] = acc[:T_WIDTH, c * ML_CHUNK:(c + 1) * ML_CHUNK].astype(BF16)
    gate_ref[...] = acc[T_WIDTH:]


def _t_projection(hb, wt_all, layer, bt):
    m, d = hb.shape
    tm = min(T_TM, m)
    rows = wt_all.shape[1]
    return pl.pallas_call(
        _tproj_kernel,
        grid=(m // tm,),
        in_specs=[pl.BlockSpec((tm, d), lambda i: (i, 0)),
                  pl.BlockSpec((None, rows, d), lambda i: (layer, 0, 0)),
                  pl.BlockSpec((rows, 1), lambda i: (0, 0))],
        out_specs=[pl.BlockSpec((tm // ML_CHUNK, T_WIDTH, ML_CHUNK), lambda i: (i, 0, 0)),
                   pl.BlockSpec((ML_GATES, tm), lambda i: (0, i))],
        out_shape=[jax.ShapeDtypeStruct((m // ML_CHUNK, T_WIDTH, ML_CHUNK), BF16),
                   jax.ShapeDtypeStruct((ML_GATES, m), F32)],
        compiler_params=_cparams(("parallel",)),
        name="t_projection",
    )(hb, wt_all, bt.reshape(rows, 1))


def _outproj_kernel(ya_ref, yb_ref, yc_ref, w_ref, b_ref, res_ref, g_ref, beta_ref, *out_refs, alpha):
    half = ya_ref.shape[0] // 2
    for rows in (slice(0, half), slice(half, 2 * half)):
        y = jnp.concatenate([ya_ref[rows, :], yb_ref[rows, :], yc_ref[rows, :]], axis=1)
        out = jnp.dot(y, w_ref[...], preferred_element_type=F32) + b_ref[...]
        r = _layer_norm_rows(alpha * res_ref[rows, :] + out, g_ref[...], beta_ref[...])
        out_refs[0][rows, :] = r
        if len(out_refs) > 1:
            out_refs[1][rows, :] = r.astype(BF16)


def _out_projection(ya, yb, yc, w_all, layer, b, res, g, beta, alpha, want_bf16):
    m, d = res.shape
    tm = min(OUT_TM, m)
    row = lambda i: (i, 0)
    const = lambda i: (0, 0)
    out_specs = [pl.BlockSpec((tm, d), row)]
    out_shape = [jax.ShapeDtypeStruct((m, d), F32)]
    if want_bf16:
        out_specs.append(pl.BlockSpec((tm, d), row))
        out_shape.append(jax.ShapeDtypeStruct((m, d), BF16))
    vec = lambda v: v.reshape(1, d)
    return pl.pallas_call(
        functools.partial(_outproj_kernel, alpha=alpha),
        grid=(m // tm,),
        in_specs=[pl.BlockSpec((tm, WA_WIDTH), row),
                  pl.BlockSpec((tm, ML_WIDTH), row),
                  pl.BlockSpec((tm, NA_WIDTH), row),
                  pl.BlockSpec((None, MIX_WIDTH, d), lambda i: (layer, 0, 0)),
                  pl.BlockSpec((1, d), const),
                  pl.BlockSpec((tm, d), row),
                  pl.BlockSpec((1, d), const),
                  pl.BlockSpec((1, d), const)],
        out_specs=out_specs,
        out_shape=out_shape,
        compiler_params=_cparams(("parallel",)),
        name="out_projection",
    )(ya, yb, yc, w_all, vec(b), res, vec(g), vec(beta))


def _silu(z):
    return z * jax.nn.sigmoid(z)


def _scale_q(q):
    return (q.astype(F32) * (HEAD_DIM ** -0.5 * LOG2E)).astype(BF16)


def _wattn_kernel(q_ref, z_ref, k0_ref, k1_ref, k2_ref, k3_ref, v0_ref, v1_ref, v2_ref, v3_ref,
                  bias_e_ref, bias_o_ref, sink_ref, o_ref, s0_ref, s1_ref, p_ref):
    n = pl.program_id(1)
    blk = WA_BLOCK
    npair = WA_KV_HEADS // 2
    rows = 2 * WA_GROUP * blk
    lane = lax.broadcasted_iota(jnp.int32, (blk, 2 * HEAD_DIM), 1)
    left = lane < HEAD_DIM
    k_refs = (k0_ref, k1_ref, k2_ref, k3_ref)
    v_refs = (v0_ref, v1_ref, v2_ref, v3_ref)

    def step(s_read, s_write):
        for half in range(2):
            qrows = slice(half * blk, (half + 1) * blk)
            bias_ref = (bias_e_ref, bias_o_ref)[half]
            for i in range(npair):
                ks = slice(i * 2 * HEAD_DIM, (i + 1) * 2 * HEAD_DIM)
                k3 = jnp.concatenate([r[:, ks] for r in k_refs[half:half + 3]], axis=0)
                tiles = [_scale_q(q_ref[qrows, (WA_GROUP * i + j) * 2 * HEAD_DIM:(WA_GROUP * i + j + 1) * 2 * HEAD_DIM])
                         for j in range(WA_GROUP)]
                zero = jnp.zeros_like(tiles[0])
                lhs = jnp.concatenate([jnp.where(left, t, zero) for t in tiles]
                                      + [jnp.where(left, zero, t) for t in tiles], axis=0)
                s_write[half, i] = lax.dot_general(lhs, k3, (((1,), (1,)), ((), ())),
                                                   preferred_element_type=F32)
                v3 = jnp.concatenate([r[:, ks] for r in v_refs[half:half + 3]], axis=0)
                dens = []
                for c in range(rows // SM_CHUNK):
                    rs = slice(c * SM_CHUNK, (c + 1) * SM_CHUNK)
                    sk = sink_ref[0, 2 * WA_GROUP * i + c * SM_CHUNK // blk] * LOG2E
                    sc = s_read[half, i, rs, :] + bias_ref[i, rs, :]
                    m = jnp.maximum(jnp.max(sc, axis=-1, keepdims=True), sk)
                    p = jnp.exp2(sc - m)
                    dens.append(jnp.sum(p, axis=-1, keepdims=True) + jnp.exp2(sk - m))
                    p_ref[half, i, rs, :] = p.astype(BF16)
                pv = jnp.dot(p_ref[half, i], v3, preferred_element_type=F32)
                o = jnp.concatenate([pv[c * SM_CHUNK:(c + 1) * SM_CHUNK] / dens[c]
                                     for c in range(rows // SM_CHUNK)], axis=0)
                for j in range(WA_GROUP):
                    t = WA_GROUP * i + j
                    ot = jnp.where(left, o[j * blk:(j + 1) * blk],
                                   o[(WA_GROUP + j) * blk:(WA_GROUP + j + 1) * blk])
                    cs = slice(t * 2 * HEAD_DIM, (t + 1) * 2 * HEAD_DIM)
                    o_ref[qrows, cs] = (ot * _silu(z_ref[qrows, cs].astype(F32))).astype(BF16)

    @pl.when(n == 0)
    def _():
        s1_ref[...] = jnp.zeros(s1_ref.shape, F32)

    @pl.when(n % 2 == 0)
    def _():
        step(s1_ref, s0_ref)

    @pl.when(n % 2 == 1)
    def _():
        step(s0_ref, s1_ref)


def _window_attention(u3, bias, sink):
    bsz, s, _ = u3.shape
    nb = s // WA_BLOCK
    npr = nb // 2
    qw, kw = WA_WIDTH, WA_KV_WIDTH
    npair = WA_KV_HEADS // 2
    rows = 2 * WA_GROUP * WA_BLOCK
    clamp = lambda i: jnp.clip(i, 0, nb - 1)
    pair = lambda i: jnp.clip(i, 0, npr - 1)
    kspec = lambda off, c: pl.BlockSpec((None, WA_BLOCK, kw), lambda b, n: (b, clamp(2 * n + off), c))
    vspec = lambda off, c: pl.BlockSpec((None, WA_BLOCK, kw), lambda b, n: (b, clamp(2 * n - 2 + off), c))
    even_type = lambda b, n: (jnp.where(n <= 1, 0, 1), 0, 0, 0)
    odd_type = lambda b, n: (jnp.where(n == npr, 2, 1), 0, 0, 0)
    sshape = (2, npair, rows, 3 * WA_BLOCK)
    bias = bias.reshape(3, npair, rows, 3 * WA_BLOCK)
    return pl.pallas_call(
        _wattn_kernel,
        grid=(bsz, npr + 1),
        in_specs=[pl.BlockSpec((None, 2 * WA_BLOCK, qw), lambda b, n: (b, pair(n), A_Q // qw)),
                  pl.BlockSpec((None, 2 * WA_BLOCK, qw), lambda b, n: (b, pair(n - 1), A_Z // qw)),
                  kspec(-1, A_K // kw), kspec(0, A_K // kw), kspec(1, A_K // kw), kspec(2, A_K // kw),
                  vspec(-1, A_V // kw), vspec(0, A_V // kw), vspec(1, A_V // kw), vspec(2, A_V // kw),
                  pl.BlockSpec((None, npair, rows, 3 * WA_BLOCK), even_type),
                  pl.BlockSpec((None, npair, rows, 3 * WA_BLOCK), odd_type),
                  pl.BlockSpec(memory_space=pltpu.SMEM)],
        out_specs=pl.BlockSpec((None, 2 * WA_BLOCK, qw), lambda b, n: (b, pair(n - 1), 0)),
        out_shape=jax.ShapeDtypeStruct((bsz, s, qw), BF16),
        scratch_shapes=[pltpu.VMEM(sshape, F32), pltpu.VMEM(sshape, F32), pltpu.VMEM(sshape, BF16)],
        compiler_params=_cparams(("parallel", "arbitrary")),
        name="window_attention",
    )(u3, u3, u3, u3, u3, u3, u3, u3, u3, u3, bias, bias, sink.reshape(1, WA_HEADS))


def _t5_bucket_np(rel):
    half = T5_BUCKETS // 2
    max_exact = half // 2
    ret = np.where(rel > 0, half, 0)
    n = np.abs(rel)
    nf = np.maximum(n, 1).astype(np.float64)
    v = np.log(nf / max_exact) / math.log(T5_MAX_DIST / max_exact) * (half - max_exact)
    vr = np.round(v)
    v = np.where(np.abs(v - vr) < 1e-9, vr, v)
    large = np.minimum(max_exact + np.trunc(v).astype(np.int64), half - 1)
    return ret + np.where(n < max_exact, n, large)


def _wbias_kernel(t5t_ref, bucket_ref, o_ref):
    width = 4 * WA_BLOCK
    bk = bucket_ref[...]
    e = lax.broadcasted_iota(jnp.int32, (T5_BUCKETS, width), 0)
    onehot = jnp.where(e == bk, 1.0, 0.0).astype(F32)
    g = _dot_hi(t5t_ref[...], onehot) * LOG2E + jnp.where(bk < 0, NEG, 0.0)
    col = lax.broadcasted_iota(jnp.int32, (WA_BLOCK, 3 * WA_BLOCK), 1)
    for h in range(WA_HEADS):
        row = jnp.broadcast_to(g[h:h + 1, :], (WA_BLOCK, width))
        t = pltpu.roll(row, 3 * WA_BLOCK, 1, stride=1, stride_axis=0)[:, :3 * WA_BLOCK]
        o_ref[0, h] = jnp.where(col < WA_BLOCK, NEG, t)
        o_ref[1, h] = t
        o_ref[2, h] = jnp.where(col >= 2 * WA_BLOCK, NEG, t)


def _window_bias(t5_table):
    rel = np.arange(4 * WA_BLOCK) - 2 * WA_BLOCK
    bucket = np.where(np.abs(rel) <= WINDOW, _t5_bucket_np(rel), -1).astype(np.int32)
    t5t = jnp.pad(t5_table.astype(F32).T, ((0, 16 - WA_HEADS), (0, 0)))
    return pl.pallas_call(
        _wbias_kernel,
        out_shape=jax.ShapeDtypeStruct((3, WA_HEADS, WA_BLOCK, 3 * WA_BLOCK), F32),
        name="window_bias",
    )(t5t, jnp.asarray(bucket).reshape(1, -1))


def _natten_kernel(q_ref, z_ref, kp_ref, kc_ref, kn_ref, vp_ref, vc_ref, vn_ref, bias_ref, o_ref,
                   s0_ref, s1_ref, p_ref):
    n = pl.program_id(1)
    blk = NA_BLOCK
    lane = lax.broadcasted_iota(jnp.int32, (blk, 2 * HEAD_DIM), 1)
    left = lane < HEAD_DIM

    ones_col = jnp.where(lax.broadcasted_iota(jnp.int32, (3 * blk, 2 * HEAD_DIM), 1) == 0, 1.0, 0.0).astype(BF16)

    def step(s_read, s_write):
        for i in range(NA_HEADS // 2):
            cs = slice(i * 2 * HEAD_DIM, (i + 1) * 2 * HEAD_DIM)
            k3 = jnp.concatenate([kp_ref[:, cs], kc_ref[:, cs], kn_ref[:, cs]], axis=0)
            t = _scale_q(q_ref[:, cs])
            zero = jnp.zeros_like(t)
            lhs = jnp.concatenate([jnp.where(left, t, zero), jnp.where(left, zero, t)], axis=0)
            s_write[i] = lax.dot_general(lhs, k3, (((1,), (1,)), ((), ())), preferred_element_type=F32)
            pb = p_ref.at[i % 3]
            for c in range(2 * blk // SM_CHUNK):
                rs = slice(c * SM_CHUNK, (c + 1) * SM_CHUNK)
                sc = s_read[i, rs, :] + bias_ref[i, rs, :]
                m = jnp.max(sc, axis=-1, keepdims=True)
                pb[rs, :] = jnp.exp2(sc - m).astype(BF16)
            if i > 0:
                weighted_values(i - 1)
        weighted_values(NA_HEADS // 2 - 1)

    def weighted_values(i):
        cs = slice(i * 2 * HEAD_DIM, (i + 1) * 2 * HEAD_DIM)
        v3 = jnp.concatenate([vp_ref[:, cs], vc_ref[:, cs], vn_ref[:, cs]], axis=0)
        vaug = jnp.concatenate([v3, ones_col], axis=1)
        out = jnp.dot(p_ref[i % 3], vaug, preferred_element_type=F32)
        o = out[:, :2 * HEAD_DIM] / out[:, 2 * HEAD_DIM:2 * HEAD_DIM + 1]
        ot = jnp.where(left, o[:blk], o[blk:])
        o_ref[:, cs] = (ot * _silu(z_ref[:, cs].astype(F32))).astype(BF16)

    @pl.when(n == 0)
    def _():
        s1_ref[...] = jnp.zeros(s1_ref.shape, F32)

    @pl.when(n % 2 == 0)
    def _():
        step(s1_ref, s0_ref)

    @pl.when(n % 2 == 1)
    def _():
        step(s0_ref, s1_ref)


def _neighbourhood_attention(u3, bias, layer):
    bsz, s, _ = u3.shape
    nblk = s // NA_BLOCK
    w = NA_WIDTH
    npair = NA_HEADS // 2
    clamp = lambda i: jnp.clip(i, 0, nblk - 1)
    spec = lambda off, c: pl.BlockSpec((None, NA_BLOCK, w), lambda b, n: (b, clamp(n + off), c))
    btype = lambda b, n: (layer, jnp.where(n <= 1, 0, jnp.where(n == nblk, 2, 1)), 0, 0, 0)
    sshape = (npair, 2 * NA_BLOCK, 3 * NA_BLOCK)
    return pl.pallas_call(
        _natten_kernel,
        grid=(bsz, nblk + 1),
        in_specs=[spec(0, C_Q // w), spec(-1, C_Z // w),
                  spec(-1, C_K // w), spec(0, C_K // w), spec(1, C_K // w),
                  spec(-2, C_V // w), spec(-1, C_V // w), spec(0, C_V // w),
                  pl.BlockSpec((None, None) + sshape, btype)],
        out_specs=pl.BlockSpec((None, NA_BLOCK, w), lambda b, n: (b, clamp(n - 1), 0)),
        out_shape=jax.ShapeDtypeStruct((bsz, s, w), BF16),
        scratch_shapes=[pltpu.VMEM(sshape, F32), pltpu.VMEM(sshape, F32),
                        pltpu.VMEM((3,) + sshape[1:], BF16)],
        compiler_params=_cparams(("parallel", "arbitrary")),
        name="neighbourhood_attention",
    )(u3, u3, u3, u3, u3, u3, u3, u3, bias.reshape(bias.shape[:2] + sshape))


def _na_valid_rows(rows):
    kh = min(NA_KH_MAX, rows)
    nblk = rows // NA_ROWS
    out = []
    for j in (0, min(1, nblk - 1), nblk - 1):
        r = NA_ROWS * j + np.arange(NA_ROWS)[:, None]
        kr = NA_ROWS * (j - 1) + np.arange(3 * NA_ROWS)[None, :]
        rs = np.clip(r - kh // 2, 0, rows - kh)
        out.append((kr >= rs) & (kr < rs + kh))
    return np.stack(out)


def _nabias_kernel(rpb_ref, o_ref, *, valid):
    w = GRID_W
    j = lax.broadcasted_iota(jnp.int32, (32, 2 * w), 1)
    e = lax.broadcasted_iota(jnp.int32, (32, 2 * w), 0)
    dc = jnp.clip(j - w, -(NA_KW - 1), NA_KW - 1) + NA_KW - 1
    g = _dot_hi(rpb_ref[...], jnp.where(e == dc, 1.0, 0.0).astype(F32)) * LOG2E
    lane = lax.broadcasted_iota(jnp.int32, (w, 2 * w), 1)
    qc = lax.broadcasted_iota(jnp.int32, (w, 2 * w), 0)
    kc = lane & (w - 1)
    col_start = jnp.clip(qc - NA_KW // 2, 0, w - NA_KW)
    col_ok = (kc >= col_start) & (kc < col_start + NA_KW)
    left = lane < w
    neg = jnp.full((w, 2 * w), NEG, F32)

    def toeplitz(dr, shift):
        row = jnp.broadcast_to(g[dr:dr + 1, :], (w, 2 * w))
        return pltpu.roll(row, shift, 1, stride=1, stride_axis=0)

    pair = [jnp.where(col_ok, jnp.where(left, toeplitz(d, w), toeplitz(d + 1, 0)), NEG)
            for d in range(2 * NA_KH_MAX - 2)]
    for ty in range(3):
        for rl in range(NA_ROWS):
            for t in range(3 * NA_ROWS // 2):
                d = 2 * t - rl + NA_KH_MAX - 1 - NA_ROWS
                v0, v1 = bool(valid[ty, rl, 2 * t]), bool(valid[ty, rl, 2 * t + 1])
                if v0 and v1:
                    tile = pair[d]
                elif v0:
                    tile = jnp.where(left, pair[d], NEG)
                elif v1:
                    tile = jnp.where(left, NEG, pair[d])
                else:
                    tile = neg
                o_ref[ty, rl * w:(rl + 1) * w, 2 * t * w:(2 * t + 2) * w] = tile


def _na_bias(rpb_all, rows):
    depth = rpb_all.shape[0]
    rpb_p = jnp.pad(rpb_all.astype(F32), ((0, 0), (0, 0), (0, 1), (0, 1)))
    return pl.pallas_call(
        functools.partial(_nabias_kernel, valid=_na_valid_rows(rows)),
        grid=(depth, NA_HEADS),
        in_specs=[pl.BlockSpec((None, None, 16, 32), lambda l, h: (l, h, 0, 0))],
        out_specs=pl.BlockSpec((None, 3, None, NA_BLOCK, 3 * NA_BLOCK), lambda l, h: (l, 0, h, 0, 0)),
        out_shape=jax.ShapeDtypeStruct((depth, 3, NA_HEADS, NA_BLOCK, 3 * NA_BLOCK), F32),
        compiler_params=_cparams(("parallel", "parallel")),
        name="na_bias",
    )(rpb_p)


def _log_sigmoid(x):
    return jnp.minimum(x, 0.0) - jnp.log(1.0 + jnp.exp(-jnp.abs(x)))


def _tri(n, upper):
    r = lax.broadcasted_iota(jnp.int32, (n, n), 0)
    c = lax.broadcasted_iota(jnp.int32, (n, n), 1)
    return jnp.where((r <= c) if upper else (r >= c), 1.0, 0.0).astype(F32)


def _scan_max(x, axis, reverse, size):
    idx = lax.broadcasted_iota(jnp.int32, x.shape, axis)
    k = 1
    while k < size:
        if reverse:
            shifted = pltpu.roll(x, x.shape[axis] - k, axis)
            ok = idx < size - k
        else:
            shifted = pltpu.roll(x, k, axis)
            ok = idx >= k
        x = jnp.maximum(x, jnp.where(ok, shifted, NEG))
        k *= 2
    return x


def _mlstm_kernel(k_ref, qt_ref, vt_ref, ot_ref, zt_ref, gr_ref, fb_ref, ng_ref, y_ref,
                  state_ref, sprev_ref, rows_ref, gain_ref, et_ref, w2a_ref, w2b_ref, *, seq):
    L = ML_CHUNK
    nc = seq // L
    head = pl.program_id(1)
    scale = ML_HEAD_DIM ** -0.5
    lane = lax.broadcasted_iota(jnp.int32, (1, 128), 1)

    for d in range(2):
        rev = d == 1
        fb = fb_ref[d, head]
        i_r = gr_ref[8 * d + head]
        lf_r = _log_sigmoid(gr_ref[8 * d + 4 + head] + fb)
        b_r = _dot_hi(lf_r, _tri(L, upper=not rev))
        g = jnp.broadcast_to(jnp.sum(lf_r, axis=1, keepdims=True), (nc, L))
        a_r = g - b_r + i_r
        m_loc = jnp.broadcast_to(jnp.max(a_r, axis=1, keepdims=True), (nc, L))
        jr = lax.broadcasted_iota(jnp.int32, (nc, nc), 0)
        jc = lax.broadcasted_iota(jnp.int32, (nc, nc), 1)
        before = jnp.where((jc > jr) if rev else (jc < jr), 1.0, 0.0).astype(F32)
        g_ex = _dot_hi(before, g)
        x = m_loc - (g_ex + g)
        row = lax.broadcasted_iota(jnp.int32, (nc, L), 0)
        if rev:
            x_prev = jnp.where(row < nc - 1, pltpu.roll(x, nc - 1, 0), NEG)
        else:
            x_prev = jnp.where(row >= 1, pltpu.roll(x, 1, 0), NEG)
        m_prev = g_ex + jnp.maximum(_scan_max(x_prev, 0, rev, nc), 0.0)
        m_after = jnp.maximum(g + m_prev, m_loc)
        gain_ref[d, 0] = jnp.exp(g + m_prev - m_after)
        gain_ref[d, 1] = jnp.exp(m_loc - m_after)
        e_r = i_r - b_r
        mu = jnp.maximum(m_prev, _scan_max(e_r, 1, rev, L))
        rows_ref[d, 0] = mu
        rows_ref[d, 1] = jnp.exp(m_prev - mu) * scale
        rows_ref[d, 2] = jnp.exp(-b_r - mu)
        rows_ref[d, 3] = jnp.exp(a_r - m_loc)
        e_pad = jnp.concatenate([e_r, jnp.zeros((128 - nc, L), F32)], axis=0) if nc < 128 else e_r
        et_ref[d] = e_pad.T
        state_ref[d] = jnp.zeros((ML_AUG, ML_HEAD_DIM), F32)

    rr = lax.broadcasted_iota(jnp.int32, (L, L), 0)
    cc = lax.broadcasted_iota(jnp.int32, (L, L), 1)
    ones_row = jnp.where(lax.broadcasted_iota(jnp.int32, (ML_HEAD_DIM, L), 0) == 0, 1.0, 0.0).astype(BF16)

    def state_step(j, carry):
        for d in range(2):
            c = j if d == 0 else nc - 1 - j
            kc = k_ref[pl.ds(pl.multiple_of(c * L, L), L), :]
            vaug_t = jnp.concatenate([vt_ref[c], ones_row], axis=0)
            wv_t = (rows_ref[d, 3, pl.ds(c, 1), :] * vaug_t.astype(F32)).astype(BF16)
            s_loc = jnp.dot(wv_t, kc, preferred_element_type=F32)
            st = state_ref[d]
            sprev_ref[d, c] = st.astype(BF16)
            state_ref[d] = gain_ref[d, 0, pl.ds(c, 1), :] * st + gain_ref[d, 1, pl.ds(c, 1), :] * s_loc
        return carry

    lax.fori_loop(0, nc, state_step, 0, unroll=4)

    ng_col = jnp.broadcast_to(ng_ref[...], (ML_HEAD_DIM, L))
    group = min(ML_GROUP, nc // 2)
    ngroups = nc // group

    def score_stage(g, w2_ref):
        for jj in range(group):
            c = g * group + jj
            kc = k_ref[pl.ds(pl.multiple_of(c * L, L), L), :]
            s_t = jnp.dot(kc, qt_ref[c], preferred_element_type=F32)
            for d in range(2):
                e_col = jnp.sum(jnp.where(lane == c, et_ref[d], 0.0), axis=1, keepdims=True)
                mask = (rr >= cc) if d == 1 else (rr <= cc)
                p_t = jnp.where(mask, jnp.exp(e_col - rows_ref[d, 0, pl.ds(c, 1), :]), 0.0)
                w2_ref[2 * jj + d] = (s_t * p_t * scale).astype(BF16)

    def value_stage(g, w2_ref):
        for jj in range(group):
            c = g * group + jj
            q_t = qt_ref[c]
            vaug_t = jnp.concatenate([vt_ref[c], ones_row], axis=0)
            hs = None
            for d in range(2):
                out_t = (jnp.dot(vaug_t, w2_ref[2 * jj + d], preferred_element_type=F32)
                         + rows_ref[d, 1, pl.ds(c, 1), :]
                         * jnp.dot(sprev_ref[d, c], q_t, preferred_element_type=F32))
                den = out_t[ML_HEAD_DIM:ML_HEAD_DIM + 1]
                h_d = out_t[:ML_HEAD_DIM] / jnp.maximum(jnp.abs(den), rows_ref[d, 2, pl.ds(c, 1), :])
                hs = h_d if hs is None else hs + h_d
            hs = jax.nn.sigmoid(ot_ref[c].astype(F32)) * hs
            mu = jnp.mean(hs, axis=0, keepdims=True)
            hc = hs - mu
            var = jnp.mean(hc * hc, axis=0, keepdims=True)
            y_t = hc * lax.rsqrt(var + LN_EPS) * ng_col * _silu(zt_ref[c].astype(F32))
            y_ref[pl.ds(pl.multiple_of(c * L, L), L), :] = y_t.T.astype(BF16)

    w2b_ref[...] = jnp.zeros(w2b_ref.shape, BF16)

    def pipeline_step(g, carry):
        @pl.when(g % 2 == 0)
        def _():
            score_stage(g, w2a_ref)
            value_stage(jnp.maximum(g - 1, 0), w2b_ref)

        @pl.when(g % 2 == 1)
        def _():
            score_stage(g, w2b_ref)
            value_stage(g - 1, w2a_ref)
        return carry

    lax.fori_loop(0, ngroups, pipeline_step, 0)
    value_stage(ngroups - 1, w2b_ref if (ngroups - 1) % 2 else w2a_ref)


def _mlstm(u3, ut, gates, f_bias, norm_g):
    bsz, s, _ = u3.shape
    d = ML_HEAD_DIM
    nc = s // ML_CHUNK
    tcol = lambda part: pl.BlockSpec((nc, d, ML_CHUNK), lambda b, h: (b, part * ML_HEADS + h, 0))
    return pl.pallas_call(
        functools.partial(_mlstm_kernel, seq=s),
        grid=(bsz, ML_HEADS),
        in_specs=[pl.BlockSpec((None, s, d), lambda b, h: (b, 0, B_K // d + h)),
                  tcol(0), tcol(1), tcol(2), tcol(3),
                  pl.BlockSpec((ML_GATES, None, nc, ML_CHUNK), lambda b, h: (0, b, 0, 0)),
                  pl.BlockSpec(memory_space=pltpu.SMEM),
                  pl.BlockSpec((d, 1), lambda b, h: (h, 0))],
        out_specs=pl.BlockSpec((None, s, d), lambda b, h: (b, 0, h)),
        out_shape=jax.ShapeDtypeStruct((bsz, s, ML_WIDTH), BF16),
        scratch_shapes=[pltpu.VMEM((2, ML_AUG, d), F32),
                        pltpu.VMEM((2, nc, ML_AUG, d), BF16),
                        pltpu.VMEM((2, 4, nc, ML_CHUNK), F32),
                        pltpu.VMEM((2, 2, nc, ML_CHUNK), F32),
                        pltpu.VMEM((2, ML_CHUNK, 128), F32),
                        pltpu.VMEM((2 * min(ML_GROUP, nc // 2), ML_CHUNK, ML_CHUNK), BF16),
                        pltpu.VMEM((2 * min(ML_GROUP, nc // 2), ML_CHUNK, ML_CHUNK), BF16)],
        compiler_params=_cparams(("parallel", "parallel")),
        name="mlstm",
    )(u3, ut, ut, ut, ut, gates, f_bias, norm_g.reshape(ML_WIDTH, 1))


def _a_head_copies(src, dst):
    out = []
    for i in range(2):
        for j in range(WA_GROUP):
            for half in range(2):
                h = 2 * WA_GROUP * i + WA_GROUP * half + j
                out.append((src + h * HEAD_DIM, HEAD_DIM, dst + ((WA_GROUP * i + j) * 2 + half) * HEAD_DIM))
    return out


_NATURAL_COPIES = ([(4624, 3072, C_Q)] + _a_head_copies(0, A_Q) + _a_head_copies(1280, A_Z)
                   + [(768, 512, A_K), (2560, 512, B_K)])
_FEATURE_MAJOR_COPIES = [(2048, 512, 0), (3072, 1536, 512), (4608, ML_GATES, T_WIDTH)]
_OUT_ROW_COPIES = [(s0, n, d0) for s0, n, d0 in _a_head_copies(0, 0)] + [(WA_WIDTH, MIX_WIDTH - WA_WIDTH, WA_WIDTH)]


def _inprep_kernel(wt_ref, wn_ref, wf_ref):
    for src, width, dst in _NATURAL_COPIES:
        wn_ref[dst:dst + width, :] = wt_ref[src:src + width, :].astype(BF16)
    for src, width, dst in _FEATURE_MAJOR_COPIES:
        wf_ref[dst:dst + width, :] = wt_ref[src:src + width, :].astype(BF16)


def _prep_in_weights(w_in):
    depth, d, n = w_in.shape
    tk = 256
    return pl.pallas_call(
        _inprep_kernel,
        grid=(depth, d // tk),
        in_specs=[pl.BlockSpec((None, n, tk), lambda l, i: (l, 0, i))],
        out_specs=[pl.BlockSpec((None, U_WIDTH, tk), lambda l, i: (l, 0, i)),
                   pl.BlockSpec((None, T_WIDTH + ML_GATES, tk), lambda l, i: (l, 0, i))],
        out_shape=[jax.ShapeDtypeStruct((depth, U_WIDTH, d), BF16),
                   jax.ShapeDtypeStruct((depth, T_WIDTH + ML_GATES, d), BF16)],
        compiler_params=_cparams(("parallel", "parallel")),
        name="prep_in_weights",
    )(jnp.swapaxes(w_in, 1, 2))


def _outprep_kernel(w_ref, o_ref):
    for src, rows, dst in _OUT_ROW_COPIES:
        o_ref[dst:dst + rows, :] = w_ref[src:src + rows, :].astype(BF16)


def _prep_out_weights(w_out):
    depth, k, n = w_out.shape
    return pl.pallas_call(
        _outprep_kernel,
        grid=(depth,),
        in_specs=[pl.BlockSpec((None, k, n), lambda l: (l, 0, 0))],
        out_specs=pl.BlockSpec((None, k, n), lambda l: (l, 0, 0)),
        out_shape=jax.ShapeDtypeStruct((depth, k, n), BF16),
        compiler_params=_cparams(("parallel",)),
        name="prep_out_weights",
    )(w_out)


def _gather_columns(v, copies, width):
    out = jnp.zeros(v.shape[:-1] + (width,), v.dtype)
    for src, n, dst in copies:
        out = out.at[..., dst:dst + n].set(v[..., src:src + n])
    return out


def kernel(x, emb_ln_g, emb_ln_b, w_in, b_in, w_out, b_out, ln_g, ln_b, t5_bias, sink, ml_f_bias,
           ml_norm_g, na_rpb):
    bsz, s, d = x.shape
    depth = w_in.shape[0]
    alpha = (2 * depth) ** 0.25
    m = bsz * s
    nc = s // ML_CHUNK
    assert d == D_MODEL and s % NA_BLOCK == 0 and s // GRID_W >= NA_KH_MAX and nc <= 128

    res, hb = _input_norm(x.reshape(m, d), emb_ln_g, emb_ln_b)
    bias_a = _window_bias(t5_bias)
    bias_c = _na_bias(na_rpb, s // GRID_W)
    wn_all, wf_all = _prep_in_weights(w_in)
    wo_all = _prep_out_weights(w_out)
    for l in range(depth):
        bn = _gather_columns(b_in[l], _NATURAL_COPIES, U_WIDTH)
        bf = _gather_columns(b_in[l], _FEATURE_MAJOR_COPIES, T_WIDTH + ML_GATES)
        u3 = _in_projection(hb, wn_all, l, bn).reshape(bsz, s, U_WIDTH)
        ut, g = _t_projection(hb, wf_all, l, bf)
        g = g.reshape(ML_GATES, bsz, nc, ML_CHUNK)
        ya = _window_attention(u3, bias_a, sink[l])
        yb = _mlstm(u3, ut, g, ml_f_bias[l], ml_norm_g[l])
        yc = _neighbourhood_attention(u3, bias_c, l)
        last = l == depth - 1
        outs = _out_projection(ya.reshape(m, WA_WIDTH), yb.reshape(m, ML_WIDTH), yc.reshape(m, NA_WIDTH),
                               wo_all, l, b_out[l], res, ln_g[l], ln_b[l], alpha, not last)
        res = outs[0]
        hb = None if last else outs[1]
    return res.reshape(bsz, s, d)
```

```python
import functools
import math

import numpy as np
import jax
import jax.numpy as jnp
from jax import lax
from jax.experimental import pallas as pl
from jax.experimental.pallas import tpu as pltpu

F32 = jnp.float32
BF16 = jnp.bfloat16

D_MODEL = 2048
HEAD_DIM = 64
LN_EPS = 1e-5
NEG = -1e30
LOG2E = math.log2(math.e)
SM_CHUNK = 64

WA_HEADS = 12
WA_KV_HEADS = 4
WA_GROUP = WA_HEADS // WA_KV_HEADS
WA_WIDTH = WA_HEADS * HEAD_DIM
WA_KV_WIDTH = WA_KV_HEADS * HEAD_DIM
WA_BLOCK = 128
WINDOW = 128
T5_BUCKETS = 32
T5_MAX_DIST = 128
ML_HEADS = 4
ML_HEAD_DIM = 128
ML_WIDTH = ML_HEADS * ML_HEAD_DIM
ML_CHUNK = 128
ML_AUG = 2 * ML_HEAD_DIM
ML_GATES = 4 * ML_HEADS
ML_GROUP = 8
NA_HEADS = 12
NA_WIDTH = NA_HEADS * HEAD_DIM
GRID_W = 64
NA_KH_MAX = 8
NA_KW = 16
NA_ROWS = 4
NA_BLOCK = NA_ROWS * GRID_W

MIX_WIDTH = WA_WIDTH + ML_WIDTH + NA_WIDTH

C_Q, C_K, C_V, C_Z = 0, 768, 1536, 2304
A_Q, A_Z, A_K, A_V = 3072, 3840, 4608, 4864
B_K = 5120
U_WIDTH = 5632
T_WIDTH = 4 * ML_WIDTH

VMEM_LIMIT = 56 * 1024 * 1024
LN_TM = 512
IN_TM, IN_TN = 512, 2816
T_TM = 1024
OUT_TM = 512


def _cparams(sem):
    return pltpu.CompilerParams(dimension_semantics=sem, vmem_limit_bytes=VMEM_LIMIT)


def _dot_hi(a, b):
    return jnp.dot(a, b, preferred_element_type=F32, precision=lax.Precision.HIGHEST)


def _layer_norm_rows(x, g, b):
    mu = jnp.mean(x, axis=-1, keepdims=True)
    xc = x - mu
    var = jnp.mean(xc * xc, axis=-1, keepdims=True)
    return xc * lax.rsqrt(var + LN_EPS) * g + b


def _ln_kernel(x_ref, g_ref, b_ref, of_ref, ob_ref):
    y = _layer_norm_rows(x_ref[...].astype(F32), g_ref[...], b_ref[...])
    of_ref[...] = y
    ob_ref[...] = y.astype(BF16)


def _input_norm(x2, g, b):
    m, d = x2.shape
    tm = min(LN_TM, m)
    row = pl.BlockSpec((tm, d), lambda i: (i, 0))
    vec = pl.BlockSpec((1, d), lambda i: (0, 0))
    return pl.pallas_call(
        _ln_kernel,
        grid=(m // tm,),
        in_specs=[row, vec, vec],
        out_specs=[row, row],
        out_shape=[jax.ShapeDtypeStruct((m, d), F32), jax.ShapeDtypeStruct((m, d), BF16)],
        compiler_params=_cparams(("parallel",)),
        name="input_norm",
    )(x2, g.reshape(1, d), b.reshape(1, d))


_NT = (((1,), (1,)), ((), ()))


def _inproj_kernel(h_ref, wt_ref, b_ref, u_ref):
    acc = lax.dot_general(h_ref[...], wt_ref[...], _NT, preferred_element_type=F32)
    u_ref[...] = (acc + b_ref[...]).astype(u_ref.dtype)


def _in_projection(hb, wt_all, layer, b):
    m, d = hb.shape
    n = wt_all.shape[1]
    tm, tn = min(IN_TM, m), IN_TN
    return pl.pallas_call(
        _inproj_kernel,
        grid=(n // tn, m // tm),
        in_specs=[pl.BlockSpec((tm, d), lambda j, i: (i, 0)),
                  pl.BlockSpec((None, tn, d), lambda j, i: (layer, j, 0)),
                  pl.BlockSpec((1, tn), lambda j, i: (0, j))],
        out_specs=pl.BlockSpec((tm, tn), lambda j, i: (i, j)),
        out_shape=jax.ShapeDtypeStruct((m, n), BF16),
        compiler_params=_cparams(("parallel", "parallel")),
        name="in_projection",
    )(hb, wt_all, b.reshape(1, n))


def _tproj_kernel(h_ref, wt_ref, bt_ref, ut_ref, gate_ref):
    acc = lax.dot_general(wt_ref[...], h_ref[...], _NT, preferred_element_type=F32) + bt_ref[...]
    for c in range(ut_ref.shape[0]):
        ut_ref[c] = acc[:T_WIDTH, c * ML_CHUNK:(c + 1) * ML_CHUNK].astype(BF16)
    gate_ref[...] = acc[T_WIDTH:]


def _t_projection(hb, wt_all, layer, bt):
    m, d = hb.shape
    tm = min(T_TM, m)
    rows = wt_all.shape[1]
    return pl.pallas_call(
        _tproj_kernel,
        grid=(m // tm,),
        in_specs=[pl.BlockSpec((tm, d), lambda i: (i, 0)),
                  pl.BlockSpec((None, rows, d), lambda i: (layer, 0, 0)),
                  pl.BlockSpec((rows, 1), lambda i: (0, 0))],
        out_specs=[pl.BlockSpec((tm // ML_CHUNK, T_WIDTH, ML_CHUNK), lambda i: (i, 0, 0)),
                   pl.BlockSpec((ML_GATES, tm), lambda i: (0, i))],
        out_shape=[jax.ShapeDtypeStruct((m // ML_CHUNK, T_WIDTH, ML_CHUNK), BF16),
                   jax.ShapeDtypeStruct((ML_GATES, m), F32)],
        compiler_params=_cparams(("parallel",)),
        name="t_projection",
    )(hb, wt_all, bt.reshape(rows, 1))


def _outproj_kernel(ya_ref, yb_ref, yc_ref, w_ref, b_ref, res_ref, g_ref, beta_ref, *out_refs, alpha):
    half = ya_ref.shape[0] // 2
    for rows in (slice(0, half), slice(half, 2 * half)):
        y = jnp.concatenate([ya_ref[rows, :], yb_ref[rows, :], yc_ref[rows, :]], axis=1)
        out = jnp.dot(y, w_ref[...], preferred_element_type=F32) + b_ref[...]
        r = _layer_norm_rows(alpha * res_ref[rows, :] + out, g_ref[...], beta_ref[...])
        out_refs[0][rows, :] = r
        if len(out_refs) > 1:
            out_refs[1][rows, :] = r.astype(BF16)


def _out_projection(ya, yb, yc, w_all, layer, b, res, g, beta, alpha, want_bf16):
    m, d = res.shape
    tm = min(OUT_TM, m)
    row = lambda i: (i, 0)
    const = lambda i: (0, 0)
    out_specs = [pl.BlockSpec((tm, d), row)]
    out_shape = [jax.ShapeDtypeStruct((m, d), F32)]
    if want_bf16:
        out_specs.append(pl.BlockSpec((tm, d), row))
        out_shape.append(jax.ShapeDtypeStruct((m, d), BF16))
    vec = lambda v: v.reshape(1, d)
    return pl.pallas_call(
        functools.partial(_outproj_kernel, alpha=alpha),
        grid=(m // tm,),
        in_specs=[pl.BlockSpec((tm, WA_WIDTH), row),
                  pl.BlockSpec((tm, ML_WIDTH), row),
                  pl.BlockSpec((tm, NA_WIDTH), row),
                  pl.BlockSpec((None, MIX_WIDTH, d), lambda i: (layer, 0, 0)),
                  pl.BlockSpec((1, d), const),
                  pl.BlockSpec((tm, d), row),
                  pl.BlockSpec((1, d), const),
                  pl.BlockSpec((1, d), const)],
        out_specs=out_specs,
        out_shape=out_shape,
        compiler_params=_cparams(("parallel",)),
        name="out_projection",
    )(ya, yb, yc, w_all, vec(b), res, vec(g), vec(beta))


def _silu(z):
    return z * jax.nn.sigmoid(z)


def _scale_q(q):
    return (q.astype(F32) * (HEAD_DIM ** -0.5 * LOG2E)).astype(BF16)


def _wattn_kernel(q_ref, z_ref, k0_ref, k1_ref, k2_ref, k3_ref, v0_ref, v1_ref, v2_ref, v3_ref,
                  bias_e_ref, bias_o_ref, sink_ref, o_ref, s0_ref, s1_ref, p_ref):
    n = pl.program_id(1)
    blk = WA_BLOCK
    npair = WA_KV_HEADS // 2
    rows = 2 * WA_GROUP * blk
    lane = lax.broadcasted_iota(jnp.int32, (blk, 2 * HEAD_DIM), 1)
    left = lane < HEAD_DIM
    k_refs = (k0_ref, k1_ref, k2_ref, k3_ref)
    v_refs = (v0_ref, v1_ref, v2_ref, v3_ref)

    def step(s_read, s_write):
        for half in range(2):
            qrows = slice(half * blk, (half + 1) * blk)
            bias_ref = (bias_e_ref, bias_o_ref)[half]
            for i in range(npair):
                ks = slice(i * 2 * HEAD_DIM, (i + 1) * 2 * HEAD_DIM)
                k3 = jnp.concatenate([r[:, ks] for r in k_refs[half:half + 3]], axis=0)
                tiles = [_scale_q(q_ref[qrows, (WA_GROUP * i + j) * 2 * HEAD_DIM:(WA_GROUP * i + j + 1) * 2 * HEAD_DIM])
                         for j in range(WA_GROUP)]
                zero = jnp.zeros_like(tiles[0])
                lhs = jnp.concatenate([jnp.where(left, t, zero) for t in tiles]
                                      + [jnp.where(left, zero, t) for t in tiles], axis=0)
                s_write[half, i] = lax.dot_general(lhs, k3, (((1,), (1,)), ((), ())),
                                                   preferred_element_type=F32)
                v3 = jnp.concatenate([r[:, ks] for r in v_refs[half:half + 3]], axis=0)
                dens = []
                for c in range(rows // SM_CHUNK):
                    rs = slice(c * SM_CHUNK, (c + 1) * SM_CHUNK)
                    sk = sink_ref[0, 2 * WA_GROUP * i + c * SM_CHUNK // blk] * LOG2E
                    sc = s_read[half, i, rs, :] + bias_ref[i, rs, :]
                    m = jnp.maximum(jnp.max(sc, axis=-1, keepdims=True), sk)
                    p = jnp.exp2(sc - m)
                    dens.append(jnp.sum(p, axis=-1, keepdims=True) + jnp.exp2(sk - m))
                    p_ref[half, i, rs, :] = p.astype(BF16)
                pv = jnp.dot(p_ref[half, i], v3, preferred_element_type=F32)
                o = jnp.concatenate([pv[c * SM_CHUNK:(c + 1) * SM_CHUNK] / dens[c]
                                     for c in range(rows // SM_CHUNK)], axis=0)
                for j in range(WA_GROUP):
                    t = WA_GROUP * i + j
                    ot = jnp.where(left, o[j * blk:(j + 1) * blk],
                                   o[(WA_GROUP + j) * blk:(WA_GROUP + j + 1) * blk])
                    cs = slice(t * 2 * HEAD_DIM, (t + 1) * 2 * HEAD_DIM)
                    o_ref[qrows, cs] = (ot * _silu(z_ref[qrows, cs].astype(F32))).astype(BF16)

    @pl.when(n == 0)
    def _():
        s1_ref[...] = jnp.zeros(s1_ref.shape, F32)

    @pl.when(n % 2 == 0)
    def _():
        step(s1_ref, s0_ref)

    @pl.when(n % 2 == 1)
    def _():
        step(s0_ref, s1_ref)


def _window_attention(u3, bias, sink):
    bsz, s, _ = u3.shape
    nb = s // WA_BLOCK
    npr = nb // 2
    qw, kw = WA_WIDTH, WA_KV_WIDTH
    npair = WA_KV_HEADS // 2
    rows = 2 * WA_GROUP * WA_BLOCK
    clamp = lambda i: jnp.clip(i, 0, nb - 1)
    pair = lambda i: jnp.clip(i, 0, npr - 1)
    kspec = lambda off, c: pl.BlockSpec((None, WA_BLOCK, kw), lambda b, n: (b, clamp(2 * n + off), c))
    vspec = lambda off, c: pl.BlockSpec((None, WA_BLOCK, kw), lambda b, n: (b, clamp(2 * n - 2 + off), c))
    even_type = lambda b, n: (jnp.where(n <= 1, 0, 1), 0, 0, 0)
    odd_type = lambda b, n: (jnp.where(n == npr, 2, 1), 0, 0, 0)
    sshape = (2, npair, rows, 3 * WA_BLOCK)
    bias = bias.reshape(3, npair, rows, 3 * WA_BLOCK)
    return pl.pallas_call(
        _wattn_kernel,
        grid=(bsz, npr + 1),
        in_specs=[pl.BlockSpec((None, 2 * WA_BLOCK, qw), lambda b, n: (b, pair(n), A_Q // qw)),
                  pl.BlockSpec((None, 2 * WA_BLOCK, qw), lambda b, n: (b, pair(n - 1), A_Z // qw)),
                  kspec(-1, A_K // kw), kspec(0, A_K // kw), kspec(1, A_K // kw), kspec(2, A_K // kw),
                  vspec(-1, A_V // kw), vspec(0, A_V // kw), vspec(1, A_V // kw), vspec(2, A_V // kw),
                  pl.BlockSpec((None, npair, rows, 3 * WA_BLOCK), even_type),
                  pl.BlockSpec((None, npair, rows, 3 * WA_BLOCK), odd_type),
                  pl.BlockSpec(memory_space=pltpu.SMEM)],
        out_specs=pl.BlockSpec((None, 2 * WA_BLOCK, qw), lambda b, n: (b, pair(n - 1), 0)),
        out_shape=jax.ShapeDtypeStruct((bsz, s, qw), BF16),
        scratch_shapes=[pltpu.VMEM(sshape, F32), pltpu.VMEM(sshape, F32), pltpu.VMEM(sshape, BF16)],
        compiler_params=_cparams(("parallel", "arbitrary")),
        name="window_attention",
    )(u3, u3, u3, u3, u3, u3, u3, u3, u3, u3, bias, bias, sink.reshape(1, WA_HEADS))


def _t5_bucket_np(rel):
    half = T5_BUCKETS // 2
    max_exact = half // 2
    ret = np.where(rel > 0, half, 0)
    n = np.abs(rel)
    nf = np.maximum(n, 1).astype(np.float64)
    v = np.log(nf / max_exact) / math.log(T5_MAX_DIST / max_exact) * (half - max_exact)
    vr = np.round(v)
    v = np.where(np.abs(v - vr) < 1e-9, vr, v)
    large = np.minimum(max_exact + np.trunc(v).astype(np.int64), half - 1)
    return ret + np.where(n < max_exact, n, large)


def _wbias_kernel(t5t_ref, bucket_ref, o_ref):
    width = 4 * WA_BLOCK
    bk = bucket_ref[...]
    e = lax.broadcasted_iota(jnp.int32, (T5_BUCKETS, width), 0)
    onehot = jnp.where(e == bk, 1.0, 0.0).astype(F32)
    g = _dot_hi(t5t_ref[...], onehot) * LOG2E + jnp.where(bk < 0, NEG, 0.0)
    col = lax.broadcasted_iota(jnp.int32, (WA_BLOCK, 3 * WA_BLOCK), 1)
    for h in range(WA_HEADS):
        row = jnp.broadcast_to(g[h:h + 1, :], (WA_BLOCK, width))
        t = pltpu.roll(row, 3 * WA_BLOCK, 1, stride=1, stride_axis=0)[:, :3 * WA_BLOCK]
        o_ref[0, h] = jnp.where(col < WA_BLOCK, NEG, t)
        o_ref[1, h] = t
        o_ref[2, h] = jnp.where(col >= 2 * WA_BLOCK, NEG, t)


def _window_bias(t5_table):
    rel = np.arange(4 * WA_BLOCK) - 2 * WA_BLOCK
    bucket = np.where(np.abs(rel) <= WINDOW, _t5_bucket_np(rel), -1).astype(np.int32)
    t5t = jnp.pad(t5_table.astype(F32).T, ((0, 16 - WA_HEADS), (0, 0)))
    return pl.pallas_call(
        _wbias_kernel,
        out_shape=jax.ShapeDtypeStruct((3, WA_HEADS, WA_BLOCK, 3 * WA_BLOCK), F32),
        name="window_bias",
    )(t5t, jnp.asarray(bucket).reshape(1, -1))


def _natten_kernel(q_ref, z_ref, kp_ref, kc_ref, kn_ref, vp_ref, vc_ref, vn_ref, bias_ref, o_ref,
                   s0_ref, s1_ref, p_ref):
    n = pl.program_id(1)
    blk = NA_BLOCK
    lane = lax.broadcasted_iota(jnp.int32, (blk, 2 * HEAD_DIM), 1)
    left = lane < HEAD_DIM

    ones_col = jnp.where(lax.broadcasted_iota(jnp.int32, (3 * blk, 2 * HEAD_DIM), 1) == 0, 1.0, 0.0).astype(BF16)

    def step(s_read, s_write):
        for i in range(NA_HEADS // 2):
            cs = slice(i * 2 * HEAD_DIM, (i + 1) * 2 * HEAD_DIM)
            k3 = jnp.concatenate([kp_ref[:, cs], kc_ref[:, cs], kn_ref[:, cs]], axis=0)
            t = _scale_q(q_ref[:, cs])
            zero = jnp.zeros_like(t)
            lhs = jnp.concatenate([jnp.where(left, t, zero), jnp.where(left, zero, t)], axis=0)
            s_write[i] = lax.dot_general(lhs, k3, (((1,), (1,)), ((), ())), preferred_element_type=F32)
            pb = p_ref.at[i % 3]
            for c in range(2 * blk // SM_CHUNK):
                rs = slice(c * SM_CHUNK, (c + 1) * SM_CHUNK)
                sc = s_read[i, rs, :] + bias_ref[i, rs, :]
                m = jnp.max(sc, axis=-1, keepdims=True)
                pb[rs, :] = jnp.exp2(sc - m).astype(BF16)
            if i > 0:
                weighted_values(i - 1)
        weighted_values(NA_HEADS // 2 - 1)

    def weighted_values(i):
        cs = slice(i * 2 * HEAD_DIM, (i + 1) * 2 * HEAD_DIM)
        v3 = jnp.concatenate([vp_ref[:, cs], vc_ref[:, cs], vn_ref[:, cs]], axis=0)
        vaug = jnp.concatenate([v3, ones_col], axis=1)
        out = jnp.dot(p_ref[i % 3], vaug, preferred_element_type=F32)
        o = out[:, :2 * HEAD_DIM] / out[:, 2 * HEAD_DIM:2 * HEAD_DIM + 1]
        ot = jnp.where(left, o[:blk], o[blk:])
        o_ref[:, cs] = (ot * _silu(z_ref[:, cs].astype(F32))).astype(BF16)

    @pl.when(n == 0)
    def _():
        s1_ref[...] = jnp.zeros(s1_ref.shape, F32)

    @pl.when(n % 2 == 0)
    def _():
        step(s1_ref, s0_ref)

    @pl.when(n % 2 == 1)
    def _():
        step(s0_ref, s1_ref)


def _neighbourhood_attention(u3, bias, layer):
    bsz, s, _ = u3.shape
    nblk = s // NA_BLOCK
    w = NA_WIDTH
    npair = NA_HEADS // 2
    clamp = lambda i: jnp.clip(i, 0, nblk - 1)
    spec = lambda off, c: pl.BlockSpec((None, NA_BLOCK, w), lambda b, n: (b, clamp(n + off), c))
    btype = lambda b, n: (layer, jnp.where(n <= 1, 0, jnp.where(n == nblk, 2, 1)), 0, 0, 0)
    sshape = (npair, 2 * NA_BLOCK, 3 * NA_BLOCK)
    return pl.pallas_call(
        _natten_kernel,
        grid=(bsz, nblk + 1),
        in_specs=[spec(0, C_Q // w), spec(-1, C_Z // w),
                  spec(-1, C_K // w), spec(0, C_K // w), spec(1, C_K // w),
                  spec(-2, C_V // w), spec(-1, C_V // w), spec(0, C_V // w),
                  pl.BlockSpec((None, None) + sshape, btype)],
        out_specs=pl.BlockSpec((None, NA_BLOCK, w), lambda b, n: (b, clamp(n - 1), 0)),
        out_shape=jax.ShapeDtypeStruct((bsz, s, w), BF16),
        scratch_shapes=[pltpu.VMEM(sshape, F32), pltpu.VMEM(sshape, F32),
                        pltpu.VMEM((3,) + sshape[1:], BF16)],
        compiler_params=_cparams(("parallel", "arbitrary")),
        name="neighbourhood_attention",
    )(u3, u3, u3, u3, u3, u3, u3, u3, bias.reshape(bias.shape[:2] + sshape))


def _na_valid_rows(rows):
    kh = min(NA_KH_MAX, rows)
    nblk = rows // NA_ROWS
    out = []
    for j in (0, min(1, nblk - 1), nblk - 1):
        r = NA_ROWS * j + np.arange(NA_ROWS)[:, None]
        kr = NA_ROWS * (j - 1) + np.arange(3 * NA_ROWS)[None, :]
        rs = np.clip(r - kh // 2, 0, rows - kh)
        out.append((kr >= rs) & (kr < rs + kh))
    return np.stack(out)


def _nabias_kernel(rpb_ref, o_ref, *, valid):
    w = GRID_W
    j = lax.broadcasted_iota(jnp.int32, (32, 2 * w), 1)
    e = lax.broadcasted_iota(jnp.int32, (32, 2 * w), 0)
    dc = jnp.clip(j - w, -(NA_KW - 1), NA_KW - 1) + NA_KW - 1
    g = _dot_hi(rpb_ref[...], jnp.where(e == dc, 1.0, 0.0).astype(F32)) * LOG2E
    lane = lax.broadcasted_iota(jnp.int32, (w, 2 * w), 1)
    qc = lax.broadcasted_iota(jnp.int32, (w, 2 * w), 0)
    kc = lane & (w - 1)
    col_start = jnp.clip(qc - NA_KW // 2, 0, w - NA_KW)
    col_ok = (kc >= col_start) & (kc < col_start + NA_KW)
    left = lane < w
    neg = jnp.full((w, 2 * w), NEG, F32)

    def toeplitz(dr, shift):
        row = jnp.broadcast_to(g[dr:dr + 1, :], (w, 2 * w))
        return pltpu.roll(row, shift, 1, stride=1, stride_axis=0)

    pair = [jnp.where(col_ok, jnp.where(left, toeplitz(d, w), toeplitz(d + 1, 0)), NEG)
            for d in range(2 * NA_KH_MAX - 2)]
    for ty in range(3):
        for rl in range(NA_ROWS):
            for t in range(3 * NA_ROWS // 2):
                d = 2 * t - rl + NA_KH_MAX - 1 - NA_ROWS
                v0, v1 = bool(valid[ty, rl, 2 * t]), bool(valid[ty, rl, 2 * t + 1])
                if v0 and v1:
                    tile = pair[d]
                elif v0:
                    tile = jnp.where(left, pair[d], NEG)
                elif v1:
                    tile = jnp.where(left, NEG, pair[d])
                else:
                    tile = neg
                o_ref[ty, rl * w:(rl + 1) * w, 2 * t * w:(2 * t + 2) * w] = tile


def _na_bias(rpb_all, rows):
    depth = rpb_all.shape[0]
    rpb_p = jnp.pad(rpb_all.astype(F32), ((0, 0), (0, 0), (0, 1), (0, 1)))
    return pl.pallas_call(
        functools.partial(_nabias_kernel, valid=_na_valid_rows(rows)),
        grid=(depth, NA_HEADS),
        in_specs=[pl.BlockSpec((None, None, 16, 32), lambda l, h: (l, h, 0, 0))],
        out_specs=pl.BlockSpec((None, 3, None, NA_BLOCK, 3 * NA_BLOCK), lambda l, h: (l, 0, h, 0, 0)),
        out_shape=jax.ShapeDtypeStruct((depth, 3, NA_HEADS, NA_BLOCK, 3 * NA_BLOCK), F32),
        compiler_params=_cparams(("parallel", "parallel")),
        name="na_bias",
    )(rpb_p)


def _log_sigmoid(x):
    return jnp.minimum(x, 0.0) - jnp.log(1.0 + jnp.exp(-jnp.abs(x)))


def _tri(n, upper):
    r = lax.broadcasted_iota(jnp.int32, (n, n), 0)
    c = lax.broadcasted_iota(jnp.int32, (n, n), 1)
    return jnp.where((r <= c) if upper else (r >= c), 1.0, 0.0).astype(F32)


def _scan_max(x, axis, reverse, size):
    idx = lax.broadcasted_iota(jnp.int32, x.shape, axis)
    k = 1
    while k < size:
        if reverse:
            shifted = pltpu.roll(x, x.shape[axis] - k, axis)
            ok = idx < size - k
        else:
            shifted = pltpu.roll(x, k, axis)
            ok = idx >= k
        x = jnp.maximum(x, jnp.where(ok, shifted, NEG))
        k *= 2
    return x


def _mlstm_kernel(k_ref, qt_ref, vt_ref, ot_ref, zt_ref, gr_ref, fb_ref, ng_ref, y_ref,
                  state_ref, sprev_ref, rows_ref, gain_ref, et_ref, w2a_ref, w2b_ref, *, seq):
    L = ML_CHUNK
    nc = seq // L
    head = pl.program_id(1)
    scale = ML_HEAD_DIM ** -0.5
    lane = lax.broadcasted_iota(jnp.int32, (1, 128), 1)

    for d in range(2):
        rev = d == 1
        fb = fb_ref[d, head]
        i_r = gr_ref[8 * d + head]
        lf_r = _log_sigmoid(gr_ref[8 * d + 4 + head] + fb)
        b_r = _dot_hi(lf_r, _tri(L, upper=not rev))
        g = jnp.broadcast_to(jnp.sum(lf_r, axis=1, keepdims=True), (nc, L))
        a_r = g - b_r + i_r
        m_loc = jnp.broadcast_to(jnp.max(a_r, axis=1, keepdims=True), (nc, L))
        jr = lax.broadcasted_iota(jnp.int32, (nc, nc), 0)
        jc = lax.broadcasted_iota(jnp.int32, (nc, nc), 1)
        before = jnp.where((jc > jr) if rev else (jc < jr), 1.0, 0.0).astype(F32)
        g_ex = _dot_hi(before, g)
        x = m_loc - (g_ex + g)
        row = lax.broadcasted_iota(jnp.int32, (nc, L), 0)
        if rev:
            x_prev = jnp.where(row < nc - 1, pltpu.roll(x, nc - 1, 0), NEG)
        else:
            x_prev = jnp.where(row >= 1, pltpu.roll(x, 1, 0), NEG)
        m_prev = g_ex + jnp.maximum(_scan_max(x_prev, 0, rev, nc), 0.0)
        m_after = jnp.maximum(g + m_prev, m_loc)
        gain_ref[d, 0] = jnp.exp(g + m_prev - m_after)
        gain_ref[d, 1] = jnp.exp(m_loc - m_after)
        e_r = i_r - b_r
        mu = jnp.maximum(m_prev, _scan_max(e_r, 1, rev, L))
        rows_ref[d, 0] = mu - math.log(scale)
        rows_ref[d, 1] = jnp.exp(m_prev - mu) * scale
        rows_ref[d, 2] = jnp.exp(-b_r - mu)
        rows_ref[d, 3] = jnp.exp(a_r - m_loc)
        e_pad = jnp.concatenate([e_r, jnp.zeros((128 - nc, L), F32)], axis=0) if nc < 128 else e_r
        et_ref[d] = e_pad.T
        state_ref[d] = jnp.zeros((ML_AUG, ML_HEAD_DIM), F32)

    rr = lax.broadcasted_iota(jnp.int32, (L, L), 0)
    cc = lax.broadcasted_iota(jnp.int32, (L, L), 1)
    ones_row = jnp.where(lax.broadcasted_iota(jnp.int32, (ML_HEAD_DIM, L), 0) == 0, 1.0, 0.0).astype(BF16)

    def state_step(j, carry):
        for d in range(2):
            c = j if d == 0 else nc - 1 - j
            kc = k_ref[pl.ds(pl.multiple_of(c * L, L), L), :]
            vaug_t = jnp.concatenate([vt_ref[c], ones_row], axis=0)
            wv_t = (rows_ref[d, 3, pl.ds(c, 1), :] * vaug_t.astype(F32)).astype(BF16)
            s_loc = jnp.dot(wv_t, kc, preferred_element_type=F32)
            st = state_ref[d]
            sprev_ref[d, c] = st.astype(BF16)
            state_ref[d] = gain_ref[d, 0, pl.ds(c, 1), :] * st + gain_ref[d, 1, pl.ds(c, 1), :] * s_loc
        return carry

    lax.fori_loop(0, nc, state_step, 0, unroll=min(8, nc))

    ng_col = jnp.broadcast_to(ng_ref[...], (ML_HEAD_DIM, L))
    group = min(ML_GROUP, nc // 2)
    ngroups = nc // group

    def score_stage(g, w2_ref):
        for jj in range(group):
            c = g * group + jj
            kc = k_ref[pl.ds(pl.multiple_of(c * L, L), L), :]
            s_t = jnp.dot(kc, qt_ref[c], preferred_element_type=F32)
            for d in range(2):
                e_col = jnp.sum(jnp.where(lane == c, et_ref[d], 0.0), axis=1, keepdims=True)
                mask = (rr >= cc) if d == 1 else (rr <= cc)
                p_t = jnp.where(mask, jnp.exp(e_col - rows_ref[d, 0, pl.ds(c, 1), :]), 0.0)
                w2_ref[2 * jj + d] = (s_t * p_t).astype(BF16)

    def value_stage(g, w2_ref):
        for jj in range(group):
            c = g * group + jj
            q_t = qt_ref[c]
            vaug_t = jnp.concatenate([vt_ref[c], ones_row], axis=0)
            hs = None
            for d in range(2):
                out_t = (jnp.dot(vaug_t, w2_ref[2 * jj + d], preferred_element_type=F32)
                         + rows_ref[d, 1, pl.ds(c, 1), :]
                         * jnp.dot(sprev_ref[d, c], q_t, preferred_element_type=F32))
                den = out_t[ML_HEAD_DIM:ML_HEAD_DIM + 1]
                h_d = out_t[:ML_HEAD_DIM] / jnp.maximum(jnp.abs(den), rows_ref[d, 2, pl.ds(c, 1), :])
                hs = h_d if hs is None else hs + h_d
            hs = jax.nn.sigmoid(ot_ref[c].astype(F32)) * hs
            mu = jnp.mean(hs, axis=0, keepdims=True)
            hc = hs - mu
            var = jnp.mean(hc * hc, axis=0, keepdims=True)
            y_t = hc * lax.rsqrt(var + LN_EPS) * ng_col * _silu(zt_ref[c].astype(F32))
            y_ref[pl.ds(pl.multiple_of(c * L, L), L), :] = y_t.T.astype(BF16)

    w2b_ref[...] = jnp.zeros(w2b_ref.shape, BF16)

    def pipeline_step(g, carry):
        @pl.when(g % 2 == 0)
        def _():
            score_stage(g, w2a_ref)
            value_stage(jnp.maximum(g - 1, 0), w2b_ref)

        @pl.when(g % 2 == 1)
        def _():
            score_stage(g, w2b_ref)
            value_stage(g - 1, w2a_ref)
        return carry

    lax.fori_loop(0, ngroups, pipeline_step, 0)
    value_stage(ngroups - 1, w2b_ref if (ngroups - 1) % 2 else w2a_ref)


def _mlstm(u3, ut, gates, f_bias, norm_g):
    bsz, s, _ = u3.shape
    d = ML_HEAD_DIM
    nc = s // ML_CHUNK
    tcol = lambda part: pl.BlockSpec((nc, d, ML_CHUNK), lambda b, h: (b, part * ML_HEADS + h, 0))
    return pl.pallas_call(
        functools.partial(_mlstm_kernel, seq=s),
        grid=(bsz, ML_HEADS),
        in_specs=[pl.BlockSpec((None, s, d), lambda b, h: (b, 0, B_K // d + h)),
                  tcol(0), tcol(1), tcol(2), tcol(3),
                  pl.BlockSpec((ML_GATES, None, nc, ML_CHUNK), lambda b, h: (0, b, 0, 0)),
                  pl.BlockSpec(memory_space=pltpu.SMEM),
                  pl.BlockSpec((d, 1), lambda b, h: (h, 0))],
        out_specs=pl.BlockSpec((None, s, d), lambda b, h: (b, 0, h)),
        out_shape=jax.ShapeDtypeStruct((bsz, s, ML_WIDTH), BF16),
        scratch_shapes=[pltpu.VMEM((2, ML_AUG, d), F32),
                        pltpu.VMEM((2, nc, ML_AUG, d), BF16),
                        pltpu.VMEM((2, 4, nc, ML_CHUNK), F32),
                        pltpu.VMEM((2, 2, nc, ML_CHUNK), F32),
                        pltpu.VMEM((2, ML_CHUNK, 128), F32),
                        pltpu.VMEM((2 * min(ML_GROUP, nc // 2), ML_CHUNK, ML_CHUNK), BF16),
                        pltpu.VMEM((2 * min(ML_GROUP, nc // 2), ML_CHUNK, ML_CHUNK), BF16)],
        compiler_params=_cparams(("parallel", "parallel")),
        name="mlstm",
    )(u3, ut, ut, ut, ut, gates, f_bias, norm_g.reshape(ML_WIDTH, 1))


def _a_head_copies(src, dst):
    out = []
    for i in range(2):
        for j in range(WA_GROUP):
            for half in range(2):
                h = 2 * WA_GROUP * i + WA_GROUP * half + j
                out.append((src + h * HEAD_DIM, HEAD_DIM, dst + ((WA_GROUP * i + j) * 2 + half) * HEAD_DIM))
    return out


_NATURAL_COPIES = ([(4624, 3072, C_Q)] + _a_head_copies(0, A_Q) + _a_head_copies(1280, A_Z)
                   + [(768, 512, A_K), (2560, 512, B_K)])
_FEATURE_MAJOR_COPIES = [(2048, 512, 0), (3072, 1536, 512), (4608, ML_GATES, T_WIDTH)]
_OUT_ROW_COPIES = [(s0, n, d0) for s0, n, d0 in _a_head_copies(0, 0)] + [(WA_WIDTH, MIX_WIDTH - WA_WIDTH, WA_WIDTH)]


def _inprep_kernel(wt_ref, wn_ref, wf_ref):
    for src, width, dst in _NATURAL_COPIES:
        wn_ref[dst:dst + width, :] = wt_ref[src:src + width, :].astype(BF16)
    for src, width, dst in _FEATURE_MAJOR_COPIES:
        wf_ref[dst:dst + width, :] = wt_ref[src:src + width, :].astype(BF16)


def _prep_in_weights(w_in):
    depth, d, n = w_in.shape
    tk = 256
    return pl.pallas_call(
        _inprep_kernel,
        grid=(depth, d // tk),
        in_specs=[pl.BlockSpec((None, n, tk), lambda l, i: (l, 0, i))],
        out_specs=[pl.BlockSpec((None, U_WIDTH, tk), lambda l, i: (l, 0, i)),
                   pl.BlockSpec((None, T_WIDTH + ML_GATES, tk), lambda l, i: (l, 0, i))],
        out_shape=[jax.ShapeDtypeStruct((depth, U_WIDTH, d), BF16),
                   jax.ShapeDtypeStruct((depth, T_WIDTH + ML_GATES, d), BF16)],
        compiler_params=_cparams(("parallel", "parallel")),
        name="prep_in_weights",
    )(jnp.swapaxes(w_in, 1, 2))


def _outprep_kernel(w_ref, o_ref):
    for src, rows, dst in _OUT_ROW_COPIES:
        o_ref[dst:dst + rows, :] = w_ref[src:src + rows, :].astype(BF16)


def _prep_out_weights(w_out):
    depth, k, n = w_out.shape
    return pl.pallas_call(
        _outprep_kernel,
        grid=(depth,),
        in_specs=[pl.BlockSpec((None, k, n), lambda l: (l, 0, 0))],
        out_specs=pl.BlockSpec((None, k, n), lambda l: (l, 0, 0)),
        out_shape=jax.ShapeDtypeStruct((depth, k, n), BF16),
        compiler_params=_cparams(("parallel",)),
        name="prep_out_weights",
    )(w_out)


def _gather_columns(v, copies, width):
    out = jnp.zeros(v.shape[:-1] + (width,), v.dtype)
    for src, n, dst in copies:
        out = out.at[..., dst:dst + n].set(v[..., src:src + n])
    return out


def kernel(x, emb_ln_g, emb_ln_b, w_in, b_in, w_out, b_out, ln_g, ln_b, t5_bias, sink, ml_f_bias,
           ml_norm_g, na_rpb):
    bsz, s, d = x.shape
    depth = w_in.shape[0]
    alpha = (2 * depth) ** 0.25
    m = bsz * s
    nc = s // ML_CHUNK
    assert d == D_MODEL and s % NA_BLOCK == 0 and s // GRID_W >= NA_KH_MAX and nc <= 128

    res, hb = _input_norm(x.reshape(m, d), emb_ln_g, emb_ln_b)
    bias_a = _window_bias(t5_bias)
    bias_c = _na_bias(na_rpb, s // GRID_W)
    wn_all, wf_all = _prep_in_weights(w_in)
    wo_all = _prep_out_weights(w_out)
    for l in range(depth):
        bn = _gather_columns(b_in[l], _NATURAL_COPIES, U_WIDTH)
        bf = _gather_columns(b_in[l], _FEATURE_MAJOR_COPIES, T_WIDTH + ML_GATES)
        u3 = _in_projection(hb, wn_all, l, bn).reshape(bsz, s, U_WIDTH)
        ut, g = _t_projection(hb, wf_all, l, bf)
        g = g.reshape(ML_GATES, bsz, nc, ML_CHUNK)
        ya = _window_attention(u3, bias_a, sink[l])
        yb = _mlstm(u3, ut, g, ml_f_bias[l], ml_norm_g[l])
        yc = _neighbourhood_attention(u3, bias_c, l)
        last = l == depth - 1
        outs = _out_projection(ya.reshape(m, WA_WIDTH), yb.reshape(m, ML_WIDTH), yc.reshape(m, NA_WIDTH),
                               wo_all, l, b_out[l], res, ln_g[l], ln_b[l], alpha, not last)
        res = outs[0]
        hb = None if last else outs[1]
    return res.reshape(bsz, s, d)
```

```python
import functools
import math

import numpy as np
import jax
import jax.numpy as jnp
from jax import lax
from jax.experimental import pallas as pl
from jax.experimental.pallas import tpu as pltpu

F32 = jnp.float32
BF16 = jnp.bfloat16
LANES = 128

D_MODEL = 2048
HEAD_DIM = 64
LN_EPS = 1e-5
NEG = -1e30
LOG2E = math.log2(math.e)
SM_CHUNK = 64

WA_HEADS = 12
WA_KV_HEADS = 4
WA_GROUP = WA_HEADS // WA_KV_HEADS
WA_WIDTH = WA_HEADS * HEAD_DIM
WA_KV_WIDTH = WA_KV_HEADS * HEAD_DIM
WA_BLOCK = 128
WINDOW = 128
T5_BUCKETS = 32
T5_MAX_DIST = 128
ML_HEADS = 4
ML_HEAD_DIM = 128
ML_WIDTH = ML_HEADS * ML_HEAD_DIM
ML_CHUNK = 128
ML_AUG = 2 * ML_HEAD_DIM
ML_GATES = 4 * ML_HEADS
ML_GROUP = 8
NA_HEADS = 12
NA_WIDTH = NA_HEADS * HEAD_DIM
GRID_W = 64
NA_KH_MAX = 8
NA_KW = 16
NA_ROWS = 4
NA_BLOCK = NA_ROWS * GRID_W
NA_PBUF = 3

MIX_WIDTH = WA_WIDTH + ML_WIDTH + NA_WIDTH

C_Q, C_K, C_V, C_Z = 0, 768, 1536, 2304
A_Q, A_Z, A_K, A_V = 3072, 3840, 4608, 4864
B_K = 5120
U_WIDTH = 5632
T_WIDTH = 4 * ML_WIDTH

VMEM_LIMIT = 56 * 1024 * 1024
LN_TM = 1024
IN_TM, IN_TN = 512, 2816
T_TM = 1024
OUT_TM, OUT_SLABS = 512, 2


def _cparams(sem):
    return pltpu.CompilerParams(dimension_semantics=sem, vmem_limit_bytes=VMEM_LIMIT)


def _dot_hi(a, b):
    return jnp.dot(a, b, preferred_element_type=F32, precision=lax.Precision.HIGHEST)


def _layer_norm_rows(x, g, b):
    mu = jnp.mean(x, axis=-1, keepdims=True)
    xc = x - mu
    var = jnp.mean(xc * xc, axis=-1, keepdims=True)
    return xc * lax.rsqrt(var + LN_EPS) * g + b


def _ln_kernel(x_ref, g_ref, b_ref, of_ref, ob_ref):
    y = _layer_norm_rows(x_ref[...].astype(F32), g_ref[...], b_ref[...])
    of_ref[...] = y
    ob_ref[...] = y.astype(BF16)


def _input_norm(x2, g, b):
    m, d = x2.shape
    tm = min(LN_TM, m)
    row = pl.BlockSpec((tm, d), lambda i: (i, 0))
    vec = pl.BlockSpec((1, d), lambda i: (0, 0))
    return pl.pallas_call(
        _ln_kernel,
        grid=(m // tm,),
        in_specs=[row, vec, vec],
        out_specs=[row, row],
        out_shape=[jax.ShapeDtypeStruct((m, d), F32), jax.ShapeDtypeStruct((m, d), BF16)],
        compiler_params=_cparams(("parallel",)),
        name="input_norm",
    )(x2, g.reshape(1, d), b.reshape(1, d))


_NT = (((1,), (1,)), ((), ()))


def _inproj_kernel(h_ref, wt_ref, b_ref, u_ref):
    acc = lax.dot_general(h_ref[...], wt_ref[...], _NT, preferred_element_type=F32)
    u_ref[...] = (acc + b_ref[...]).astype(u_ref.dtype)


def _in_projection(hb, wt_all, layer, b):
    m, d = hb.shape
    n = wt_all.shape[1]
    tm, tn = min(IN_TM, m), IN_TN
    return pl.pallas_call(
        _inproj_kernel,
        grid=(n // tn, m // tm),
        in_specs=[pl.BlockSpec((tm, d), lambda j, i: (i, 0)),
                  pl.BlockSpec((None, tn, d), lambda j, i: (layer, j, 0)),
                  pl.BlockSpec((1, tn), lambda j, i: (0, j))],
        out_specs=pl.BlockSpec((tm, tn), lambda j, i: (i, j)),
        out_shape=jax.ShapeDtypeStruct((m, n), BF16),
        compiler_params=_cparams(("parallel", "parallel")),
        name="in_projection",
    )(hb, wt_all, b.reshape(1, n))


def _tproj_kernel(h_ref, wt_ref, bt_ref, ut_ref, gate_ref):
    acc = lax.dot_general(wt_ref[...], h_ref[...], _NT, preferred_element_type=F32) + bt_ref[...]
    for c in range(ut_ref.shape[0]):
        ut_ref[c] = acc[:T_WIDTH, c * ML_CHUNK:(c + 1) * ML_CHUNK].astype(BF16)
    gate_ref[...] = acc[T_WIDTH:]


def _t_projection(hb, wt_all, layer, bt):
    m, d = hb.shape
    tm = min(T_TM, m)
    rows = wt_all.shape[1]
    return pl.pallas_call(
        _tproj_kernel,
        grid=(m // tm,),
        in_specs=[pl.BlockSpec((tm, d), lambda i: (i, 0)),
                  pl.BlockSpec((None, rows, d), lambda i: (layer, 0, 0)),
                  pl.BlockSpec((rows, 1), lambda i: (0, 0))],
        out_specs=[pl.BlockSpec((tm // ML_CHUNK, T_WIDTH, ML_CHUNK), lambda i: (i, 0, 0)),
                   pl.BlockSpec((ML_GATES, tm), lambda i: (0, i))],
        out_shape=[jax.ShapeDtypeStruct((m // ML_CHUNK, T_WIDTH, ML_CHUNK), BF16),
                   jax.ShapeDtypeStruct((ML_GATES, m), F32)],
        compiler_params=_cparams(("parallel",)),
        name="t_projection",
    )(hb, wt_all, bt.reshape(rows, 1))


def _outproj_kernel(ya_ref, yb_ref, yc_ref, w_ref, b_ref, res_ref, g_ref, beta_ref, *out_refs, alpha):
    slab = ya_ref.shape[0] // OUT_SLABS
    for rows in (slice(i * slab, (i + 1) * slab) for i in range(OUT_SLABS)):
        y = jnp.concatenate([ya_ref[rows, :], yb_ref[rows, :], yc_ref[rows, :]], axis=1)
        out = jnp.dot(y, w_ref[...], preferred_element_type=F32) + b_ref[...]
        r = _layer_norm_rows(alpha * res_ref[rows, :] + out, g_ref[...], beta_ref[...])
        out_refs[0][rows, :] = r
        if len(out_refs) > 1:
            out_refs[1][rows, :] = r.astype(BF16)


def _out_projection(ya, yb, yc, w_all, layer, b, res, g, beta, alpha, want_bf16):
    m, d = res.shape
    tm = min(OUT_TM, m)
    row = lambda i: (i, 0)
    const = lambda i: (0, 0)
    out_specs = [pl.BlockSpec((tm, d), row)]
    out_shape = [jax.ShapeDtypeStruct((m, d), F32)]
    if want_bf16:
        out_specs.append(pl.BlockSpec((tm, d), row))
        out_shape.append(jax.ShapeDtypeStruct((m, d), BF16))
    vec = lambda v: v.reshape(1, d)
    return pl.pallas_call(
        functools.partial(_outproj_kernel, alpha=alpha),
        grid=(m // tm,),
        in_specs=[pl.BlockSpec((tm, WA_WIDTH), row),
                  pl.BlockSpec((tm, ML_WIDTH), row),
                  pl.BlockSpec((tm, NA_WIDTH), row),
                  pl.BlockSpec((None, MIX_WIDTH, d), lambda i: (layer, 0, 0)),
                  pl.BlockSpec((1, d), const),
                  pl.BlockSpec((tm, d), row),
                  pl.BlockSpec((1, d), const),
                  pl.BlockSpec((1, d), const)],
        out_specs=out_specs,
        out_shape=out_shape,
        compiler_params=_cparams(("parallel",)),
        name="out_projection",
    )(ya, yb, yc, w_all, vec(b), res, vec(g), vec(beta))


def _silu(z):
    return z * jax.nn.sigmoid(z)


def _scale_q(q):
    return (q.astype(F32) * (HEAD_DIM ** -0.5 * LOG2E)).astype(BF16)


def _wattn_kernel(q_ref, z_ref, k0_ref, k1_ref, k2_ref, k3_ref, v0_ref, v1_ref, v2_ref, v3_ref,
                  bias_e_ref, bias_o_ref, sink_ref, o_ref, s0_ref, s1_ref, p_ref):
    n = pl.program_id(1)
    blk = WA_BLOCK
    npair = WA_KV_HEADS // 2
    rows = 2 * WA_GROUP * blk
    lane = lax.broadcasted_iota(jnp.int32, (blk, 2 * HEAD_DIM), 1)
    left = lane < HEAD_DIM
    k_refs = (k0_ref, k1_ref, k2_ref, k3_ref)
    v_refs = (v0_ref, v1_ref, v2_ref, v3_ref)

    def step(s_read, s_write):
        for half in range(2):
            qrows = slice(half * blk, (half + 1) * blk)
            bias_ref = (bias_e_ref, bias_o_ref)[half]
            for i in range(npair):
                ks = slice(i * 2 * HEAD_DIM, (i + 1) * 2 * HEAD_DIM)
                k3 = jnp.concatenate([r[:, ks] for r in k_refs[half:half + 3]], axis=0)
                tiles = [_scale_q(q_ref[qrows, (WA_GROUP * i + j) * 2 * HEAD_DIM:(WA_GROUP * i + j + 1) * 2 * HEAD_DIM])
                         for j in range(WA_GROUP)]
                zero = jnp.zeros_like(tiles[0])
                lhs = jnp.concatenate([jnp.where(left, t, zero) for t in tiles]
                                      + [jnp.where(left, zero, t) for t in tiles], axis=0)
                s_write[half, i] = lax.dot_general(lhs, k3, (((1,), (1,)), ((), ())),
                                                   preferred_element_type=F32)
                v3 = jnp.concatenate([r[:, ks] for r in v_refs[half:half + 3]], axis=0)
                dens = []
                for c in range(rows // SM_CHUNK):
                    rs = slice(c * SM_CHUNK, (c + 1) * SM_CHUNK)
                    sk = sink_ref[0, 2 * WA_GROUP * i + c * SM_CHUNK // blk] * LOG2E
                    sc = s_read[half, i, rs, :] + bias_ref[i, rs, :]
                    m = jnp.maximum(jnp.max(sc, axis=-1, keepdims=True), sk)
                    p = jnp.exp2(sc - m)
                    dens.append(jnp.sum(p, axis=-1, keepdims=True) + jnp.exp2(sk - m))
                    p_ref[half, i, rs, :] = p.astype(BF16)
                pv = jnp.dot(p_ref[half, i], v3, preferred_element_type=F32)
                o = jnp.concatenate([pv[c * SM_CHUNK:(c + 1) * SM_CHUNK] / dens[c]
                                     for c in range(rows // SM_CHUNK)], axis=0)
                for j in range(WA_GROUP):
                    t = WA_GROUP * i + j
                    ot = jnp.where(left, o[j * blk:(j + 1) * blk],
                                   o[(WA_GROUP + j) * blk:(WA_GROUP + j + 1) * blk])
                    cs = slice(t * 2 * HEAD_DIM, (t + 1) * 2 * HEAD_DIM)
                    o_ref[qrows, cs] = (ot * _silu(z_ref[qrows, cs].astype(F32))).astype(BF16)

    @pl.when(n == 0)
    def _():
        s1_ref[...] = jnp.zeros(s1_ref.shape, F32)

    @pl.when(n % 2 == 0)
    def _():
        step(s1_ref, s0_ref)

    @pl.when(n % 2 == 1)
    def _():
        step(s0_ref, s1_ref)


def _window_attention(u3, bias, sink):
    bsz, s, _ = u3.shape
    nb = s // WA_BLOCK
    npr = nb // 2
    qw, kw = WA_WIDTH, WA_KV_WIDTH
    npair = WA_KV_HEADS // 2
    rows = 2 * WA_GROUP * WA_BLOCK
    clamp = lambda i: jnp.clip(i, 0, nb - 1)
    pair = lambda i: jnp.clip(i, 0, npr - 1)
    kspec = lambda off, c: pl.BlockSpec((None, WA_BLOCK, kw), lambda b, n: (b, clamp(2 * n + off), c))
    vspec = lambda off, c: pl.BlockSpec((None, WA_BLOCK, kw), lambda b, n: (b, clamp(2 * n - 2 + off), c))
    even_type = lambda b, n: (jnp.where(n <= 1, 0, 1), 0, 0, 0)
    odd_type = lambda b, n: (jnp.where(n == npr, 2, 1), 0, 0, 0)
    sshape = (2, npair, rows, 3 * WA_BLOCK)
    bias = bias.reshape(3, npair, rows, 3 * WA_BLOCK)
    return pl.pallas_call(
        _wattn_kernel,
        grid=(bsz, npr + 1),
        in_specs=[pl.BlockSpec((None, 2 * WA_BLOCK, qw), lambda b, n: (b, pair(n), A_Q // qw)),
                  pl.BlockSpec((None, 2 * WA_BLOCK, qw), lambda b, n: (b, pair(n - 1), A_Z // qw)),
                  kspec(-1, A_K // kw), kspec(0, A_K // kw), kspec(1, A_K // kw), kspec(2, A_K // kw),
                  vspec(-1, A_V // kw), vspec(0, A_V // kw), vspec(1, A_V // kw), vspec(2, A_V // kw),
                  pl.BlockSpec((None, npair, rows, 3 * WA_BLOCK), even_type),
                  pl.BlockSpec((None, npair, rows, 3 * WA_BLOCK), odd_type),
                  pl.BlockSpec(memory_space=pltpu.SMEM)],
        out_specs=pl.BlockSpec((None, 2 * WA_BLOCK, qw), lambda b, n: (b, pair(n - 1), 0)),
        out_shape=jax.ShapeDtypeStruct((bsz, s, qw), BF16),
        scratch_shapes=[pltpu.VMEM(sshape, F32), pltpu.VMEM(sshape, F32), pltpu.VMEM(sshape, BF16)],
        compiler_params=_cparams(("parallel", "arbitrary")),
        name="window_attention",
    )(u3, u3, u3, u3, u3, u3, u3, u3, u3, u3, bias, bias, sink.reshape(1, WA_HEADS))


def _t5_bucket_np(rel):
    half = T5_BUCKETS // 2
    max_exact = half // 2
    ret = np.where(rel > 0, half, 0)
    n = np.abs(rel)
    nf = np.maximum(n, 1).astype(np.float64)
    v = np.log(nf / max_exact) / math.log(T5_MAX_DIST / max_exact) * (half - max_exact)
    vr = np.round(v)
    v = np.where(np.abs(v - vr) < 1e-9, vr, v)
    large = np.minimum(max_exact + np.trunc(v).astype(np.int64), half - 1)
    return ret + np.where(n < max_exact, n, large)


def _wbias_kernel(t5t_ref, bucket_ref, o_ref):
    width = 4 * WA_BLOCK
    bk = bucket_ref[...]
    e = lax.broadcasted_iota(jnp.int32, (T5_BUCKETS, width), 0)
    onehot = jnp.where(e == bk, 1.0, 0.0).astype(F32)
    g = _dot_hi(t5t_ref[...], onehot) * LOG2E + jnp.where(bk < 0, NEG, 0.0)
    col = lax.broadcasted_iota(jnp.int32, (WA_BLOCK, 3 * WA_BLOCK), 1)
    for h in range(WA_HEADS):
        row = jnp.broadcast_to(g[h:h + 1, :], (WA_BLOCK, width))
        t = pltpu.roll(row, 3 * WA_BLOCK, 1, stride=1, stride_axis=0)[:, :3 * WA_BLOCK]
        o_ref[0, h] = jnp.where(col < WA_BLOCK, NEG, t)
        o_ref[1, h] = t
        o_ref[2, h] = jnp.where(col >= 2 * WA_BLOCK, NEG, t)


def _window_bias(t5_table):
    rel = np.arange(4 * WA_BLOCK) - 2 * WA_BLOCK
    bucket = np.where(np.abs(rel) <= WINDOW, _t5_bucket_np(rel), -1).astype(np.int32)
    t5t = jnp.pad(t5_table.astype(F32).T, ((0, 16 - WA_HEADS), (0, 0)))
    return pl.pallas_call(
        _wbias_kernel,
        out_shape=jax.ShapeDtypeStruct((3, WA_HEADS, WA_BLOCK, 3 * WA_BLOCK), F32),
        name="window_bias",
    )(t5t, jnp.asarray(bucket).reshape(1, -1))


def _natten_kernel(q_ref, z_ref, kp_ref, kc_ref, kn_ref, vp_ref, vc_ref, vn_ref, bias_ref, o_ref,
                   s0_ref, s1_ref, p_ref):
    n = pl.program_id(1)
    blk = NA_BLOCK
    lane = lax.broadcasted_iota(jnp.int32, (blk, 2 * HEAD_DIM), 1)
    left = lane < HEAD_DIM

    ones_col = jnp.where(lax.broadcasted_iota(jnp.int32, (3 * blk, 2 * HEAD_DIM), 1) == 0, 1.0, 0.0).astype(BF16)

    def step(s_read, s_write):
        for i in range(NA_HEADS // 2):
            cs = slice(i * 2 * HEAD_DIM, (i + 1) * 2 * HEAD_DIM)
            k3 = jnp.concatenate([kp_ref[:, cs], kc_ref[:, cs], kn_ref[:, cs]], axis=0)
            t = _scale_q(q_ref[:, cs])
            zero = jnp.zeros_like(t)
            lhs = jnp.concatenate([jnp.where(left, t, zero), jnp.where(left, zero, t)], axis=0)
            s_write[i] = lax.dot_general(lhs, k3, (((1,), (1,)), ((), ())), preferred_element_type=F32)
            pb = p_ref.at[i % NA_PBUF]
            for c in range(2 * blk // SM_CHUNK):
                rs = slice(c * SM_CHUNK, (c + 1) * SM_CHUNK)
                sc = s_read[i, rs, :] + bias_ref[i, rs, :]
                m = jnp.max(sc, axis=-1, keepdims=True)
                pb[rs, :] = jnp.exp2(sc - m).astype(BF16)
            if i > 0:
                weighted_values(i - 1)
        weighted_values(NA_HEADS // 2 - 1)

    def weighted_values(i):
        cs = slice(i * 2 * HEAD_DIM, (i + 1) * 2 * HEAD_DIM)
        v3 = jnp.concatenate([vp_ref[:, cs], vc_ref[:, cs], vn_ref[:, cs]], axis=0)
        vaug = jnp.concatenate([v3, ones_col], axis=1)
        out = jnp.dot(p_ref[i % NA_PBUF], vaug, preferred_element_type=F32)
        o = out[:, :2 * HEAD_DIM] / out[:, 2 * HEAD_DIM:2 * HEAD_DIM + 1]
        ot = jnp.where(left, o[:blk], o[blk:])
        o_ref[:, cs] = (ot * _silu(z_ref[:, cs].astype(F32))).astype(BF16)

    @pl.when(n == 0)
    def _():
        s1_ref[...] = jnp.zeros(s1_ref.shape, F32)

    @pl.when(n % 2 == 0)
    def _():
        step(s1_ref, s0_ref)

    @pl.when(n % 2 == 1)
    def _():
        step(s0_ref, s1_ref)


def _neighbourhood_attention(u3, bias, layer):
    bsz, s, _ = u3.shape
    nblk = s // NA_BLOCK
    w = NA_WIDTH
    npair = NA_HEADS // 2
    clamp = lambda i: jnp.clip(i, 0, nblk - 1)
    spec = lambda off, c: pl.BlockSpec((None, NA_BLOCK, w), lambda b, n: (b, clamp(n + off), c))
    btype = lambda b, n: (layer, jnp.where(n <= 1, 0, jnp.where(n == nblk, 2, 1)), 0, 0, 0)
    sshape = (npair, 2 * NA_BLOCK, 3 * NA_BLOCK)
    return pl.pallas_call(
        _natten_kernel,
        grid=(bsz, nblk + 1),
        in_specs=[spec(0, C_Q // w), spec(-1, C_Z // w),
                  spec(-1, C_K // w), spec(0, C_K // w), spec(1, C_K // w),
                  spec(-2, C_V // w), spec(-1, C_V // w), spec(0, C_V // w),
                  pl.BlockSpec((None, None) + sshape, btype)],
        out_specs=pl.BlockSpec((None, NA_BLOCK, w), lambda b, n: (b, clamp(n - 1), 0)),
        out_shape=jax.ShapeDtypeStruct((bsz, s, w), BF16),
        scratch_shapes=[pltpu.VMEM(sshape, F32), pltpu.VMEM(sshape, F32),
                        pltpu.VMEM((NA_PBUF,) + sshape[1:], BF16)],
        compiler_params=_cparams(("parallel", "arbitrary")),
        name="neighbourhood_attention",
    )(u3, u3, u3, u3, u3, u3, u3, u3, bias.reshape(bias.shape[:2] + sshape))


def _na_valid_rows(rows):
    kh = min(NA_KH_MAX, rows)
    nblk = rows // NA_ROWS
    out = []
    for j in (0, min(1, nblk - 1), nblk - 1):
        r = NA_ROWS * j + np.arange(NA_ROWS)[:, None]
        kr = NA_ROWS * (j - 1) + np.arange(3 * NA_ROWS)[None, :]
        rs = np.clip(r - kh // 2, 0, rows - kh)
        out.append((kr >= rs) & (kr < rs + kh))
    return np.stack(out)


def _nabias_kernel(rpb_ref, o_ref, *, valid):
    w = GRID_W
    j = lax.broadcasted_iota(jnp.int32, (32, 2 * w), 1)
    e = lax.broadcasted_iota(jnp.int32, (32, 2 * w), 0)
    dc = jnp.clip(j - w, -(NA_KW - 1), NA_KW - 1) + NA_KW - 1
    g = _dot_hi(rpb_ref[...], jnp.where(e == dc, 1.0, 0.0).astype(F32)) * LOG2E
    lane = lax.broadcasted_iota(jnp.int32, (w, 2 * w), 1)
    qc = lax.broadcasted_iota(jnp.int32, (w, 2 * w), 0)
    kc = lane & (w - 1)
    col_start = jnp.clip(qc - NA_KW // 2, 0, w - NA_KW)
    col_ok = (kc >= col_start) & (kc < col_start + NA_KW)
    left = lane < w
    neg = jnp.full((w, 2 * w), NEG, F32)

    def toeplitz(dr, shift):
        row = jnp.broadcast_to(g[dr:dr + 1, :], (w, 2 * w))
        return pltpu.roll(row, shift, 1, stride=1, stride_axis=0)

    pair = [jnp.where(col_ok, jnp.where(left, toeplitz(d, w), toeplitz(d + 1, 0)), NEG)
            for d in range(2 * NA_KH_MAX - 2)]
    for ty in range(3):
        for rl in range(NA_ROWS):
            for t in range(3 * NA_ROWS // 2):
                d = 2 * t - rl + NA_KH_MAX - 1 - NA_ROWS
                v0, v1 = bool(valid[ty, rl, 2 * t]), bool(valid[ty, rl, 2 * t + 1])
                if v0 and v1:
                    tile = pair[d]
                elif v0:
                    tile = jnp.where(left, pair[d], NEG)
                elif v1:
                    tile = jnp.where(left, NEG, pair[d])
                else:
                    tile = neg
                o_ref[ty, rl * w:(rl + 1) * w, 2 * t * w:(2 * t + 2) * w] = tile


def _na_bias(rpb_all, rows):
    depth = rpb_all.shape[0]
    rpb_p = jnp.pad(rpb_all.astype(F32), ((0, 0), (0, 0), (0, 1), (0, 1)))
    return pl.pallas_call(
        functools.partial(_nabias_kernel, valid=_na_valid_rows(rows)),
        grid=(depth, NA_HEADS),
        in_specs=[pl.BlockSpec((None, None, 16, 32), lambda l, h: (l, h, 0, 0))],
        out_specs=pl.BlockSpec((None, 3, None, NA_BLOCK, 3 * NA_BLOCK), lambda l, h: (l, 0, h, 0, 0)),
        out_shape=jax.ShapeDtypeStruct((depth, 3, NA_HEADS, NA_BLOCK, 3 * NA_BLOCK), F32),
        compiler_params=_cparams(("parallel", "parallel")),
        name="na_bias",
    )(rpb_p)


def _log_sigmoid(x):
    return jnp.minimum(x, 0.0) - jnp.log(1.0 + jnp.exp(-jnp.abs(x)))


def _tri(n, upper):
    r = lax.broadcasted_iota(jnp.int32, (n, n), 0)
    c = lax.broadcasted_iota(jnp.int32, (n, n), 1)
    return jnp.where((r <= c) if upper else (r >= c), 1.0, 0.0).astype(F32)


def _scan_max(x, axis, reverse, size):
    idx = lax.broadcasted_iota(jnp.int32, x.shape, axis)
    k = 1
    while k < size:
        if reverse:
            shifted = pltpu.roll(x, x.shape[axis] - k, axis)
            ok = idx < size - k
        else:
            shifted = pltpu.roll(x, k, axis)
            ok = idx >= k
        x = jnp.maximum(x, jnp.where(ok, shifted, NEG))
        k *= 2
    return x


def _mlstm_kernel(k_ref, qt_ref, vt_ref, ot_ref, zt_ref, gr_ref, fb_ref, ng_ref, y_ref,
                  state_ref, sprev_ref, rows_ref, gain_ref, et_ref, w2a_ref, w2b_ref, *, seq):
    L = ML_CHUNK
    nc = seq // L
    head = pl.program_id(1)
    scale = ML_HEAD_DIM ** -0.5
    lane = lax.broadcasted_iota(jnp.int32, (1, LANES), 1)

    for d in range(2):
        rev = d == 1
        fb = fb_ref[d, head]
        i_r = gr_ref[8 * d + head]
        lf_r = _log_sigmoid(gr_ref[8 * d + 4 + head] + fb)
        b_r = _dot_hi(lf_r, _tri(L, upper=not rev))
        g = jnp.broadcast_to(jnp.sum(lf_r, axis=1, keepdims=True), (nc, L))
        a_r = g - b_r + i_r
        m_loc = jnp.broadcast_to(jnp.max(a_r, axis=1, keepdims=True), (nc, L))
        jr = lax.broadcasted_iota(jnp.int32, (nc, nc), 0)
        jc = lax.broadcasted_iota(jnp.int32, (nc, nc), 1)
        before = jnp.where((jc > jr) if rev else (jc < jr), 1.0, 0.0).astype(F32)
        g_ex = _dot_hi(before, g)
        x = m_loc - (g_ex + g)
        row = lax.broadcasted_iota(jnp.int32, (nc, L), 0)
        if rev:
            x_prev = jnp.where(row < nc - 1, pltpu.roll(x, nc - 1, 0), NEG)
        else:
            x_prev = jnp.where(row >= 1, pltpu.roll(x, 1, 0), NEG)
        m_prev = g_ex + jnp.maximum(_scan_max(x_prev, 0, rev, nc), 0.0)
        m_after = jnp.maximum(g + m_prev, m_loc)
        gain_ref[d, 0] = jnp.exp(g + m_prev - m_after)
        gain_ref[d, 1] = jnp.exp(m_loc - m_after)
        e_r = i_r - b_r
        mu = jnp.maximum(m_prev, _scan_max(e_r, 1, rev, L))
        rows_ref[d, 0] = mu - math.log(scale)
        rows_ref[d, 1] = jnp.exp(m_prev - mu) * scale
        rows_ref[d, 2] = jnp.exp(-b_r - mu)
        rows_ref[d, 3] = jnp.exp(a_r - m_loc)
        e_pad = jnp.concatenate([e_r, jnp.zeros((LANES - nc, L), F32)], axis=0) if nc < LANES else e_r
        et_ref[d] = e_pad.T
        state_ref[d] = jnp.zeros((ML_AUG, ML_HEAD_DIM), F32)

    rr = lax.broadcasted_iota(jnp.int32, (L, L), 0)
    cc = lax.broadcasted_iota(jnp.int32, (L, L), 1)
    ones_row = jnp.where(lax.broadcasted_iota(jnp.int32, (ML_HEAD_DIM, L), 0) == 0, 1.0, 0.0).astype(BF16)

    def state_step(j, carry):
        for d in range(2):
            c = j if d == 0 else nc - 1 - j
            kc = k_ref[pl.ds(pl.multiple_of(c * L, L), L), :]
            vaug_t = jnp.concatenate([vt_ref[c], ones_row], axis=0)
            wv_t = (rows_ref[d, 3, pl.ds(c, 1), :] * vaug_t.astype(F32)).astype(BF16)
            s_loc = jnp.dot(wv_t, kc, preferred_element_type=F32)
            st = state_ref[d]
            sprev_ref[d, c] = st.astype(BF16)
            state_ref[d] = gain_ref[d, 0, pl.ds(c, 1), :] * st + gain_ref[d, 1, pl.ds(c, 1), :] * s_loc
        return carry

    lax.fori_loop(0, nc, state_step, 0, unroll=min(8, nc))

    ng_col = jnp.broadcast_to(ng_ref[...], (ML_HEAD_DIM, L))
    group = min(ML_GROUP, nc // 2)
    ngroups = nc // group

    def score_stage(g, w2_ref):
        for jj in range(group):
            c = g * group + jj
            kc = k_ref[pl.ds(pl.multiple_of(c * L, L), L), :]
            s_t = jnp.dot(kc, qt_ref[c], preferred_element_type=F32)
            for d in range(2):
                e_col = jnp.sum(jnp.where(lane == c, et_ref[d], 0.0), axis=1, keepdims=True)
                mask = (rr >= cc) if d == 1 else (rr <= cc)
                p_t = jnp.where(mask, jnp.exp(e_col - rows_ref[d, 0, pl.ds(c, 1), :]), 0.0)
                w2_ref[2 * jj + d] = (s_t * p_t).astype(BF16)

    def value_stage(g, w2_ref):
        for jj in range(group):
            c = g * group + jj
            q_t = qt_ref[c]
            vaug_t = jnp.concatenate([vt_ref[c], ones_row], axis=0)
            hs = None
            for d in range(2):
                out_t = (jnp.dot(vaug_t, w2_ref[2 * jj + d], preferred_element_type=F32)
                         + rows_ref[d, 1, pl.ds(c, 1), :]
                         * jnp.dot(sprev_ref[d, c], q_t, preferred_element_type=F32))
                den = out_t[ML_HEAD_DIM:ML_HEAD_DIM + 1]
                h_d = out_t[:ML_HEAD_DIM] / jnp.maximum(jnp.abs(den), rows_ref[d, 2, pl.ds(c, 1), :])
                hs = h_d if hs is None else hs + h_d
            hs = jax.nn.sigmoid(ot_ref[c].astype(F32)) * hs
            mu = jnp.mean(hs, axis=0, keepdims=True)
            hc = hs - mu
            var = jnp.mean(hc * hc, axis=0, keepdims=True)
            y_t = hc * lax.rsqrt(var + LN_EPS) * ng_col * _silu(zt_ref[c].astype(F32))
            y_ref[pl.ds(pl.multiple_of(c * L, L), L), :] = y_t.T.astype(BF16)

    w2b_ref[...] = jnp.zeros(w2b_ref.shape, BF16)

    def pipeline_step(g, carry):
        @pl.when(g % 2 == 0)
        def _():
            score_stage(g, w2a_ref)
            value_stage(jnp.maximum(g - 1, 0), w2b_ref)

        @pl.when(g % 2 == 1)
        def _():
            score_stage(g, w2b_ref)
            value_stage(g - 1, w2a_ref)
        return carry

    lax.fori_loop(0, ngroups, pipeline_step, 0)
    value_stage(ngroups - 1, w2b_ref if (ngroups - 1) % 2 else w2a_ref)


def _mlstm(u3, ut, gates, f_bias, norm_g):
    bsz, s, _ = u3.shape
    d = ML_HEAD_DIM
    nc = s // ML_CHUNK
    tcol = lambda part: pl.BlockSpec((nc, d, ML_CHUNK), lambda b, h: (b, part * ML_HEADS + h, 0))
    return pl.pallas_call(
        functools.partial(_mlstm_kernel, seq=s),
        grid=(bsz, ML_HEADS),
        in_specs=[pl.BlockSpec((None, s, d), lambda b, h: (b, 0, B_K // d + h)),
                  tcol(0), tcol(1), tcol(2), tcol(3),
                  pl.BlockSpec((ML_GATES, None, nc, ML_CHUNK), lambda b, h: (0, b, 0, 0)),
                  pl.BlockSpec(memory_space=pltpu.SMEM),
                  pl.BlockSpec((d, 1), lambda b, h: (h, 0))],
        out_specs=pl.BlockSpec((None, s, d), lambda b, h: (b, 0, h)),
        out_shape=jax.ShapeDtypeStruct((bsz, s, ML_WIDTH), BF16),
        scratch_shapes=[pltpu.VMEM((2, ML_AUG, d), F32),
                        pltpu.VMEM((2, nc, ML_AUG, d), BF16),
                        pltpu.VMEM((2, 4, nc, ML_CHUNK), F32),
                        pltpu.VMEM((2, 2, nc, ML_CHUNK), F32),
                        pltpu.VMEM((2, ML_CHUNK, LANES), F32),
                        pltpu.VMEM((2 * min(ML_GROUP, nc // 2), ML_CHUNK, ML_CHUNK), BF16),
                        pltpu.VMEM((2 * min(ML_GROUP, nc // 2), ML_CHUNK, ML_CHUNK), BF16)],
        compiler_params=_cparams(("parallel", "parallel")),
        name="mlstm",
    )(u3, ut, ut, ut, ut, gates, f_bias, norm_g.reshape(ML_WIDTH, 1))


def _a_head_copies(src, dst):
    out = []
    for i in range(2):
        for j in range(WA_GROUP):
            for half in range(2):
                h = 2 * WA_GROUP * i + WA_GROUP * half + j
                out.append((src + h * HEAD_DIM, HEAD_DIM, dst + ((WA_GROUP * i + j) * 2 + half) * HEAD_DIM))
    return out


_NATURAL_COPIES = ([(4624, 3072, C_Q)] + _a_head_copies(0, A_Q) + _a_head_copies(1280, A_Z)
                   + [(768, 512, A_K), (2560, 512, B_K)])
_FEATURE_MAJOR_COPIES = [(2048, 512, 0), (3072, 1536, 512), (4608, ML_GATES, T_WIDTH)]
_OUT_ROW_COPIES = [(s0, n, d0) for s0, n, d0 in _a_head_copies(0, 0)] + [(WA_WIDTH, MIX_WIDTH - WA_WIDTH, WA_WIDTH)]


def _inprep_kernel(wt_ref, wn_ref, wf_ref):
    for src, width, dst in _NATURAL_COPIES:
        wn_ref[dst:dst + width, :] = wt_ref[src:src + width, :].astype(BF16)
    for src, width, dst in _FEATURE_MAJOR_COPIES:
        wf_ref[dst:dst + width, :] = wt_ref[src:src + width, :].astype(BF16)


def _prep_in_weights(w_in):
    depth, d, n = w_in.shape
    tk = 256
    return pl.pallas_call(
        _inprep_kernel,
        grid=(depth, d // tk),
        in_specs=[pl.BlockSpec((None, n, tk), lambda l, i: (l, 0, i))],
        out_specs=[pl.BlockSpec((None, U_WIDTH, tk), lambda l, i: (l, 0, i)),
                   pl.BlockSpec((None, T_WIDTH + ML_GATES, tk), lambda l, i: (l, 0, i))],
        out_shape=[jax.ShapeDtypeStruct((depth, U_WIDTH, d), BF16),
                   jax.ShapeDtypeStruct((depth, T_WIDTH + ML_GATES, d), BF16)],
        compiler_params=_cparams(("parallel", "parallel")),
        name="prep_in_weights",
    )(jnp.swapaxes(w_in, 1, 2))


def _outprep_kernel(w_ref, o_ref):
    for src, rows, dst in _OUT_ROW_COPIES:
        o_ref[dst:dst + rows, :] = w_ref[src:src + rows, :].astype(BF16)


def _prep_out_weights(w_out):
    depth, k, n = w_out.shape
    return pl.pallas_call(
        _outprep_kernel,
        grid=(depth,),
        in_specs=[pl.BlockSpec((None, k, n), lambda l: (l, 0, 0))],
        out_specs=pl.BlockSpec((None, k, n), lambda l: (l, 0, 0)),
        out_shape=jax.ShapeDtypeStruct((depth, k, n), BF16),
        compiler_params=_cparams(("parallel",)),
        name="prep_out_weights",
    )(w_out)


def _gather_columns(v, copies, width):
    out = jnp.zeros(v.shape[:-1] + (width,), v.dtype)
    for src, n, dst in copies:
        out = out.at[..., dst:dst + n].set(v[..., src:src + n])
    return out


def kernel(x, emb_ln_g, emb_ln_b, w_in, b_in, w_out, b_out, ln_g, ln_b, t5_bias, sink, ml_f_bias,
           ml_norm_g, na_rpb):
    bsz, s, d = x.shape
    depth = w_in.shape[0]
    alpha = (2 * depth) ** 0.25
    m = bsz * s
    nc = s // ML_CHUNK
    assert d == D_MODEL and s % NA_BLOCK == 0 and s // GRID_W >= NA_KH_MAX and nc <= LANES

    res, hb = _input_norm(x.reshape(m, d), emb_ln_g, emb_ln_b)
    bias_a = _window_bias(t5_bias)
    bias_c = _na_bias(na_rpb, s // GRID_W)
    wn_all, wf_all = _prep_in_weights(w_in)
    wo_all = _prep_out_weights(w_out)
    for l in range(depth):
        bn = _gather_columns(b_in[l], _NATURAL_COPIES, U_WIDTH)
        bf = _gather_columns(b_in[l], _FEATURE_MAJOR_COPIES, T_WIDTH + ML_GATES)
        u3 = _in_projection(hb, wn_all, l, bn).reshape(bsz, s, U_WIDTH)
        ut, g = _t_projection(hb, wf_all, l, bf)
        g = g.reshape(ML_GATES, bsz, nc, ML_CHUNK)
        ya = _window_attention(u3, bias_a, sink[l])
        yb = _mlstm(u3, ut, g, ml_f_bias[l], ml_norm_g[l])
        yc = _neighbourhood_attention(u3, bias_c, l)
        last = l == depth - 1
        outs = _out_projection(ya.reshape(m, WA_WIDTH), yb.reshape(m, ML_WIDTH), yc.reshape(m, NA_WIDTH),
                               wo_all, l, b_out[l], res, ln_g[l], ln_b[l], alpha, not last)
        res = outs[0]
        hb = None if last else outs[1]
    return res.reshape(bsz, s, d)
```

```python
import functools
import math

import numpy as np
import jax
import jax.numpy as jnp
from jax import lax
from jax.experimental import pallas as pl
from jax.experimental.pallas import tpu as pltpu

F32 = jnp.float32
BF16 = jnp.bfloat16
LANES = 128

D_MODEL = 2048
HEAD_DIM = 64
LN_EPS = 1e-5
NEG = -1e30
LOG2E = math.log2(math.e)
SM_CHUNK = 64

WA_HEADS = 12
WA_KV_HEADS = 4
WA_GROUP = WA_HEADS // WA_KV_HEADS
WA_WIDTH = WA_HEADS * HEAD_DIM
WA_KV_WIDTH = WA_KV_HEADS * HEAD_DIM
WA_BLOCK = 128
WINDOW = 128
T5_BUCKETS = 32
T5_MAX_DIST = 128
ML_HEADS = 4
ML_HEAD_DIM = 128
ML_WIDTH = ML_HEADS * ML_HEAD_DIM
ML_CHUNK = 128
ML_AUG = 2 * ML_HEAD_DIM
ML_GATES = 4 * ML_HEADS
ML_GROUP = 8
NA_HEADS = 12
NA_WIDTH = NA_HEADS * HEAD_DIM
GRID_W = 64
NA_KH_MAX = 8
NA_KW = 16
NA_ROWS = 4
NA_BLOCK = NA_ROWS * GRID_W
NA_PBUF = 3

MIX_WIDTH = WA_WIDTH + ML_WIDTH + NA_WIDTH

C_Q, C_K, C_V, C_Z = 0, 768, 1536, 2304
A_Q, A_Z, A_K, A_V = 3072, 3840, 4608, 4864
B_K = 5120
U_WIDTH = 5632
T_WIDTH = 4 * ML_WIDTH

VMEM_LIMIT = 56 * 1024 * 1024
LN_TM = 1024
IN_TM, IN_TN = 512, 2816
T_TM = 1024
OUT_TM, OUT_SLABS = 512, 2


def _cparams(sem):
    return pltpu.CompilerParams(dimension_semantics=sem, vmem_limit_bytes=VMEM_LIMIT)


def _dot_hi(a, b):
    return jnp.dot(a, b, preferred_element_type=F32, precision=lax.Precision.HIGHEST)


def _layer_norm_rows(x, g, b):
    mu = jnp.mean(x, axis=-1, keepdims=True)
    xc = x - mu
    var = jnp.mean(xc * xc, axis=-1, keepdims=True)
    return xc * lax.rsqrt(var + LN_EPS) * g + b


def _ln_kernel(x_ref, g_ref, b_ref, of_ref, ob_ref):
    y = _layer_norm_rows(x_ref[...].astype(F32), g_ref[...], b_ref[...])
    of_ref[...] = y
    ob_ref[...] = y.astype(BF16)


def _input_norm(x2, g, b):
    m, d = x2.shape
    tm = min(LN_TM, m)
    row = pl.BlockSpec((tm, d), lambda i: (i, 0))
    vec = pl.BlockSpec((1, d), lambda i: (0, 0))
    return pl.pallas_call(
        _ln_kernel,
        grid=(m // tm,),
        in_specs=[row, vec, vec],
        out_specs=[row, row],
        out_shape=[jax.ShapeDtypeStruct((m, d), F32), jax.ShapeDtypeStruct((m, d), BF16)],
        compiler_params=_cparams(("parallel",)),
        name="input_norm",
    )(x2, g.reshape(1, d), b.reshape(1, d))


_NT = (((1,), (1,)), ((), ()))


def _inproj_kernel(h_ref, wt_ref, b_ref, u_ref):
    acc = lax.dot_general(h_ref[...], wt_ref[...], _NT, preferred_element_type=F32)
    u_ref[...] = (acc + b_ref[...]).astype(u_ref.dtype)


def _in_projection(hb, wt_all, layer, b):
    m, d = hb.shape
    n = wt_all.shape[1]
    tm, tn = min(IN_TM, m), IN_TN
    return pl.pallas_call(
        _inproj_kernel,
        grid=(n // tn, m // tm),
        in_specs=[pl.BlockSpec((tm, d), lambda j, i: (i, 0)),
                  pl.BlockSpec((None, tn, d), lambda j, i: (layer, j, 0)),
                  pl.BlockSpec((1, tn), lambda j, i: (0, j))],
        out_specs=pl.BlockSpec((tm, tn), lambda j, i: (i, j)),
        out_shape=jax.ShapeDtypeStruct((m, n), BF16),
        compiler_params=_cparams(("parallel", "parallel")),
        name="in_projection",
    )(hb, wt_all, b.reshape(1, n))


def _tproj_kernel(h_ref, wt_ref, bt_ref, ut_ref, gate_ref):
    acc = lax.dot_general(wt_ref[...], h_ref[...], _NT, preferred_element_type=F32) + bt_ref[...]
    for c in range(ut_ref.shape[0]):
        ut_ref[c] = acc[:T_WIDTH, c * ML_CHUNK:(c + 1) * ML_CHUNK].astype(BF16)
    gate_ref[...] = acc[T_WIDTH:]


def _t_projection(hb, wt_all, layer, bt):
    m, d = hb.shape
    tm = min(T_TM, m)
    rows = wt_all.shape[1]
    return pl.pallas_call(
        _tproj_kernel,
        grid=(m // tm,),
        in_specs=[pl.BlockSpec((tm, d), lambda i: (i, 0)),
                  pl.BlockSpec((None, rows, d), lambda i: (layer, 0, 0)),
                  pl.BlockSpec((rows, 1), lambda i: (0, 0))],
        out_specs=[pl.BlockSpec((tm // ML_CHUNK, T_WIDTH, ML_CHUNK), lambda i: (i, 0, 0)),
                   pl.BlockSpec((ML_GATES, tm), lambda i: (0, i))],
        out_shape=[jax.ShapeDtypeStruct((m // ML_CHUNK, T_WIDTH, ML_CHUNK), BF16),
                   jax.ShapeDtypeStruct((ML_GATES, m), F32)],
        compiler_params=_cparams(("parallel",)),
        name="t_projection",
    )(hb, wt_all, bt.reshape(rows, 1))


def _outproj_kernel(ya_ref, yb_ref, yc_ref, w_ref, b_ref, res_ref, g_ref, beta_ref, *out_refs, alpha):
    slab = ya_ref.shape[0] // OUT_SLABS
    for rows in (slice(i * slab, (i + 1) * slab) for i in range(OUT_SLABS)):
        y = jnp.concatenate([ya_ref[rows, :], yb_ref[rows, :], yc_ref[rows, :]], axis=1)
        out = jnp.dot(y, w_ref[...], preferred_element_type=F32) + b_ref[...]
        r = _layer_norm_rows(alpha * res_ref[rows, :] + out, g_ref[...], beta_ref[...])
        out_refs[0][rows, :] = r
        if len(out_refs) > 1:
            out_refs[1][rows, :] = r.astype(BF16)


def _out_projection(ya, yb, yc, w_all, layer, b, res, g, beta, alpha, want_bf16):
    m, d = res.shape
    tm = min(OUT_TM, m)
    row = lambda i: (i, 0)
    const = lambda i: (0, 0)
    out_specs = [pl.BlockSpec((tm, d), row)]
    out_shape = [jax.ShapeDtypeStruct((m, d), F32)]
    if want_bf16:
        out_specs.append(pl.BlockSpec((tm, d), row))
        out_shape.append(jax.ShapeDtypeStruct((m, d), BF16))
    vec = lambda v: v.reshape(1, d)
    return pl.pallas_call(
        functools.partial(_outproj_kernel, alpha=alpha),
        grid=(m // tm,),
        in_specs=[pl.BlockSpec((tm, WA_WIDTH), row),
                  pl.BlockSpec((tm, ML_WIDTH), row),
                  pl.BlockSpec((tm, NA_WIDTH), row),
                  pl.BlockSpec((None, MIX_WIDTH, d), lambda i: (layer, 0, 0)),
                  pl.BlockSpec((1, d), const),
                  pl.BlockSpec((tm, d), row),
                  pl.BlockSpec((1, d), const),
                  pl.BlockSpec((1, d), const)],
        out_specs=out_specs,
        out_shape=out_shape,
        compiler_params=_cparams(("parallel",)),
        name="out_projection",
    )(ya, yb, yc, w_all, vec(b), res, vec(g), vec(beta))


def _silu(z):
    return z * jax.nn.sigmoid(z)


def _scale_q(q):
    return (q.astype(F32) * (HEAD_DIM ** -0.5 * LOG2E)).astype(BF16)


def _wattn_kernel(q_ref, z_ref, k0_ref, k1_ref, k2_ref, k3_ref, v0_ref, v1_ref, v2_ref, v3_ref,
                  bias_e_ref, bias_o_ref, sink_ref, o_ref, s0_ref, s1_ref, p_ref):
    n = pl.program_id(1)
    blk = WA_BLOCK
    npair = WA_KV_HEADS // 2
    rows = 2 * WA_GROUP * blk
    lane = lax.broadcasted_iota(jnp.int32, (blk, 2 * HEAD_DIM), 1)
    left = lane < HEAD_DIM
    k_refs = (k0_ref, k1_ref, k2_ref, k3_ref)
    v_refs = (v0_ref, v1_ref, v2_ref, v3_ref)

    def step(s_read, s_write):
        for half in range(2):
            qrows = slice(half * blk, (half + 1) * blk)
            bias_ref = (bias_e_ref, bias_o_ref)[half]
            for i in range(npair):
                ks = slice(i * 2 * HEAD_DIM, (i + 1) * 2 * HEAD_DIM)
                k3 = jnp.concatenate([r[:, ks] for r in k_refs[half:half + 3]], axis=0)
                tiles = [_scale_q(q_ref[qrows, (WA_GROUP * i + j) * 2 * HEAD_DIM:(WA_GROUP * i + j + 1) * 2 * HEAD_DIM])
                         for j in range(WA_GROUP)]
                zero = jnp.zeros_like(tiles[0])
                lhs = jnp.concatenate([jnp.where(left, t, zero) for t in tiles]
                                      + [jnp.where(left, zero, t) for t in tiles], axis=0)
                s_write[half, i] = lax.dot_general(lhs, k3, (((1,), (1,)), ((), ())),
                                                   preferred_element_type=F32)
                v3 = jnp.concatenate([r[:, ks] for r in v_refs[half:half + 3]], axis=0)
                for c in range(rows // SM_CHUNK):
                    rs = slice(c * SM_CHUNK, (c + 1) * SM_CHUNK)
                    sk = sink_ref[0, 2 * WA_GROUP * i + c * SM_CHUNK // blk] * LOG2E
                    sc = s_read[half, i, rs, :] + bias_ref[i, rs, :]
                    m = jnp.maximum(jnp.max(sc, axis=-1, keepdims=True), sk)
                    p = jnp.exp2(sc - m)
                    den = jnp.sum(p, axis=-1, keepdims=True) + jnp.exp2(sk - m)
                    p_ref[half, i, rs, :] = (p / den).astype(BF16)
                o = jnp.dot(p_ref[half, i], v3, preferred_element_type=F32)
                for j in range(WA_GROUP):
                    t = WA_GROUP * i + j
                    ot = jnp.where(left, o[j * blk:(j + 1) * blk],
                                   o[(WA_GROUP + j) * blk:(WA_GROUP + j + 1) * blk])
                    cs = slice(t * 2 * HEAD_DIM, (t + 1) * 2 * HEAD_DIM)
                    o_ref[qrows, cs] = (ot * _silu(z_ref[qrows, cs].astype(F32))).astype(BF16)

    @pl.when(n == 0)
    def _():
        s1_ref[...] = jnp.zeros(s1_ref.shape, F32)

    @pl.when(n % 2 == 0)
    def _():
        step(s1_ref, s0_ref)

    @pl.when(n % 2 == 1)
    def _():
        step(s0_ref, s1_ref)


def _window_attention(u3, bias, sink):
    bsz, s, _ = u3.shape
    nb = s // WA_BLOCK
    npr = nb // 2
    qw, kw = WA_WIDTH, WA_KV_WIDTH
    npair = WA_KV_HEADS // 2
    rows = 2 * WA_GROUP * WA_BLOCK
    clamp = lambda i: jnp.clip(i, 0, nb - 1)
    pair = lambda i: jnp.clip(i, 0, npr - 1)
    kspec = lambda off, c: pl.BlockSpec((None, WA_BLOCK, kw), lambda b, n: (b, clamp(2 * n + off), c))
    vspec = lambda off, c: pl.BlockSpec((None, WA_BLOCK, kw), lambda b, n: (b, clamp(2 * n - 2 + off), c))
    even_type = lambda b, n: (jnp.where(n <= 1, 0, 1), 0, 0, 0)
    odd_type = lambda b, n: (jnp.where(n == npr, 2, 1), 0, 0, 0)
    sshape = (2, npair, rows, 3 * WA_BLOCK)
    bias = bias.reshape(3, npair, rows, 3 * WA_BLOCK)
    return pl.pallas_call(
        _wattn_kernel,
        grid=(bsz, npr + 1),
        in_specs=[pl.BlockSpec((None, 2 * WA_BLOCK, qw), lambda b, n: (b, pair(n), A_Q // qw)),
                  pl.BlockSpec((None, 2 * WA_BLOCK, qw), lambda b, n: (b, pair(n - 1), A_Z // qw)),
                  kspec(-1, A_K // kw), kspec(0, A_K // kw), kspec(1, A_K // kw), kspec(2, A_K // kw),
                  vspec(-1, A_V // kw), vspec(0, A_V // kw), vspec(1, A_V // kw), vspec(2, A_V // kw),
                  pl.BlockSpec((None, npair, rows, 3 * WA_BLOCK), even_type),
                  pl.BlockSpec((None, npair, rows, 3 * WA_BLOCK), odd_type),
                  pl.BlockSpec(memory_space=pltpu.SMEM)],
        out_specs=pl.BlockSpec((None, 2 * WA_BLOCK, qw), lambda b, n: (b, pair(n - 1), 0)),
        out_shape=jax.ShapeDtypeStruct((bsz, s, qw), BF16),
        scratch_shapes=[pltpu.VMEM(sshape, F32), pltpu.VMEM(sshape, F32), pltpu.VMEM(sshape, BF16)],
        compiler_params=_cparams(("parallel", "arbitrary")),
        name="window_attention",
    )(u3, u3, u3, u3, u3, u3, u3, u3, u3, u3, bias, bias, sink.reshape(1, WA_HEADS))


def _t5_bucket_np(rel):
    half = T5_BUCKETS // 2
    max_exact = half // 2
    ret = np.where(rel > 0, half, 0)
    n = np.abs(rel)
    nf = np.maximum(n, 1).astype(np.float64)
    v = np.log(nf / max_exact) / math.log(T5_MAX_DIST / max_exact) * (half - max_exact)
    vr = np.round(v)
    v = np.where(np.abs(v - vr) < 1e-9, vr, v)
    large = np.minimum(max_exact + np.trunc(v).astype(np.int64), half - 1)
    return ret + np.where(n < max_exact, n, large)


def _wbias_kernel(t5t_ref, bucket_ref, o_ref):
    width = 4 * WA_BLOCK
    bk = bucket_ref[...]
    e = lax.broadcasted_iota(jnp.int32, (T5_BUCKETS, width), 0)
    onehot = jnp.where(e == bk, 1.0, 0.0).astype(F32)
    g = _dot_hi(t5t_ref[...], onehot) * LOG2E + jnp.where(bk < 0, NEG, 0.0)
    col = lax.broadcasted_iota(jnp.int32, (WA_BLOCK, 3 * WA_BLOCK), 1)
    for h in range(WA_HEADS):
        row = jnp.broadcast_to(g[h:h + 1, :], (WA_BLOCK, width))
        t = pltpu.roll(row, 3 * WA_BLOCK, 1, stride=1, stride_axis=0)[:, :3 * WA_BLOCK]
        o_ref[0, h] = jnp.where(col < WA_BLOCK, NEG, t)
        o_ref[1, h] = t
        o_ref[2, h] = jnp.where(col >= 2 * WA_BLOCK, NEG, t)


def _window_bias(t5_table):
    rel = np.arange(4 * WA_BLOCK) - 2 * WA_BLOCK
    bucket = np.where(np.abs(rel) <= WINDOW, _t5_bucket_np(rel), -1).astype(np.int32)
    t5t = jnp.pad(t5_table.astype(F32).T, ((0, 16 - WA_HEADS), (0, 0)))
    return pl.pallas_call(
        _wbias_kernel,
        out_shape=jax.ShapeDtypeStruct((3, WA_HEADS, WA_BLOCK, 3 * WA_BLOCK), F32),
        name="window_bias",
    )(t5t, jnp.asarray(bucket).reshape(1, -1))


def _natten_kernel(q_ref, z_ref, kp_ref, kc_ref, kn_ref, vp_ref, vc_ref, vn_ref, bias_ref, o_ref,
                   s0_ref, s1_ref, p_ref):
    n = pl.program_id(1)
    blk = NA_BLOCK
    lane = lax.broadcasted_iota(jnp.int32, (blk, 2 * HEAD_DIM), 1)
    left = lane < HEAD_DIM

    ones_col = jnp.where(lax.broadcasted_iota(jnp.int32, (3 * blk, 2 * HEAD_DIM), 1) == 0, 1.0, 0.0).astype(BF16)

    def step(s_read, s_write):
        for i in range(NA_HEADS // 2):
            cs = slice(i * 2 * HEAD_DIM, (i + 1) * 2 * HEAD_DIM)
            k3 = jnp.concatenate([kp_ref[:, cs], kc_ref[:, cs], kn_ref[:, cs]], axis=0)
            t = _scale_q(q_ref[:, cs])
            zero = jnp.zeros_like(t)
            lhs = jnp.concatenate([jnp.where(left, t, zero), jnp.where(left, zero, t)], axis=0)
            s_write[i] = lax.dot_general(lhs, k3, (((1,), (1,)), ((), ())), preferred_element_type=F32)
            pb = p_ref.at[i % NA_PBUF]
            for c in range(2 * blk // SM_CHUNK):
                rs = slice(c * SM_CHUNK, (c + 1) * SM_CHUNK)
                sc = s_read[i, rs, :] + bias_ref[i, rs, :]
                m = jnp.max(sc, axis=-1, keepdims=True)
                pb[rs, :] = jnp.exp2(sc - m).astype(BF16)
            if i > 0:
                weighted_values(i - 1)
        weighted_values(NA_HEADS // 2 - 1)

    def weighted_values(i):
        cs = slice(i * 2 * HEAD_DIM, (i + 1) * 2 * HEAD_DIM)
        v3 = jnp.concatenate([vp_ref[:, cs], vc_ref[:, cs], vn_ref[:, cs]], axis=0)
        vaug = jnp.concatenate([v3, ones_col], axis=1)
        out = jnp.dot(p_ref[i % NA_PBUF], vaug, preferred_element_type=F32)
        o = out[:, :2 * HEAD_DIM] / out[:, 2 * HEAD_DIM:2 * HEAD_DIM + 1]
        ot = jnp.where(left, o[:blk], o[blk:])
        o_ref[:, cs] = (ot * _silu(z_ref[:, cs].astype(F32))).astype(BF16)

    @pl.when(n == 0)
    def _():
        s1_ref[...] = jnp.zeros(s1_ref.shape, F32)

    @pl.when(n % 2 == 0)
    def _():
        step(s1_ref, s0_ref)

    @pl.when(n % 2 == 1)
    def _():
        step(s0_ref, s1_ref)


def _neighbourhood_attention(u3, bias, layer):
    bsz, s, _ = u3.shape
    nblk = s // NA_BLOCK
    w = NA_WIDTH
    npair = NA_HEADS // 2
    clamp = lambda i: jnp.clip(i, 0, nblk - 1)
    spec = lambda off, c: pl.BlockSpec((None, NA_BLOCK, w), lambda b, n: (b, clamp(n + off), c))
    btype = lambda b, n: (layer, jnp.where(n <= 1, 0, jnp.where(n == nblk, 2, 1)), 0, 0, 0)
    sshape = (npair, 2 * NA_BLOCK, 3 * NA_BLOCK)
    return pl.pallas_call(
        _natten_kernel,
        grid=(bsz, nblk + 1),
        in_specs=[spec(0, C_Q // w), spec(-1, C_Z // w),
                  spec(-1, C_K // w), spec(0, C_K // w), spec(1, C_K // w),
                  spec(-2, C_V // w), spec(-1, C_V // w), spec(0, C_V // w),
                  pl.BlockSpec((None, None) + sshape, btype)],
        out_specs=pl.BlockSpec((None, NA_BLOCK, w), lambda b, n: (b, clamp(n - 1), 0)),
        out_shape=jax.ShapeDtypeStruct((bsz, s, w), BF16),
        scratch_shapes=[pltpu.VMEM(sshape, F32), pltpu.VMEM(sshape, F32),
                        pltpu.VMEM((NA_PBUF,) + sshape[1:], BF16)],
        compiler_params=_cparams(("parallel", "arbitrary")),
        name="neighbourhood_attention",
    )(u3, u3, u3, u3, u3, u3, u3, u3, bias.reshape(bias.shape[:2] + sshape))


def _na_valid_rows(rows):
    kh = min(NA_KH_MAX, rows)
    nblk = rows // NA_ROWS
    out = []
    for j in (0, min(1, nblk - 1), nblk - 1):
        r = NA_ROWS * j + np.arange(NA_ROWS)[:, None]
        kr = NA_ROWS * (j - 1) + np.arange(3 * NA_ROWS)[None, :]
        rs = np.clip(r - kh // 2, 0, rows - kh)
        out.append((kr >= rs) & (kr < rs + kh))
    return np.stack(out)


def _nabias_kernel(rpb_ref, o_ref, *, valid):
    w = GRID_W
    j = lax.broadcasted_iota(jnp.int32, (32, 2 * w), 1)
    e = lax.broadcasted_iota(jnp.int32, (32, 2 * w), 0)
    dc = jnp.clip(j - w, -(NA_KW - 1), NA_KW - 1) + NA_KW - 1
    g = _dot_hi(rpb_ref[...], jnp.where(e == dc, 1.0, 0.0).astype(F32)) * LOG2E
    lane = lax.broadcasted_iota(jnp.int32, (w, 2 * w), 1)
    qc = lax.broadcasted_iota(jnp.int32, (w, 2 * w), 0)
    kc = lane & (w - 1)
    col_start = jnp.clip(qc - NA_KW // 2, 0, w - NA_KW)
    col_ok = (kc >= col_start) & (kc < col_start + NA_KW)
    left = lane < w
    neg = jnp.full((w, 2 * w), NEG, F32)

    def toeplitz(dr, shift):
        row = jnp.broadcast_to(g[dr:dr + 1, :], (w, 2 * w))
        return pltpu.roll(row, shift, 1, stride=1, stride_axis=0)

    pair = [jnp.where(col_ok, jnp.where(left, toeplitz(d, w), toeplitz(d + 1, 0)), NEG)
            for d in range(2 * NA_KH_MAX - 2)]
    for ty in range(3):
        for rl in range(NA_ROWS):
            for t in range(3 * NA_ROWS // 2):
                d = 2 * t - rl + NA_KH_MAX - 1 - NA_ROWS
                v0, v1 = bool(valid[ty, rl, 2 * t]), bool(valid[ty, rl, 2 * t + 1])
                if v0 and v1:
                    tile = pair[d]
                elif v0:
                    tile = jnp.where(left, pair[d], NEG)
                elif v1:
                    tile = jnp.where(left, NEG, pair[d])
                else:
                    tile = neg
                o_ref[ty, rl * w:(rl + 1) * w, 2 * t * w:(2 * t + 2) * w] = tile


def _na_bias(rpb_all, rows):
    depth = rpb_all.shape[0]
    rpb_p = jnp.pad(rpb_all.astype(F32), ((0, 0), (0, 0), (0, 1), (0, 1)))
    return pl.pallas_call(
        functools.partial(_nabias_kernel, valid=_na_valid_rows(rows)),
        grid=(depth, NA_HEADS),
        in_specs=[pl.BlockSpec((None, None, 16, 32), lambda l, h: (l, h, 0, 0))],
        out_specs=pl.BlockSpec((None, 3, None, NA_BLOCK, 3 * NA_BLOCK), lambda l, h: (l, 0, h, 0, 0)),
        out_shape=jax.ShapeDtypeStruct((depth, 3, NA_HEADS, NA_BLOCK, 3 * NA_BLOCK), F32),
        compiler_params=_cparams(("parallel", "parallel")),
        name="na_bias",
    )(rpb_p)


def _log_sigmoid(x):
    return jnp.minimum(x, 0.0) - jnp.log(1.0 + jnp.exp(-jnp.abs(x)))


def _tri(n, upper):
    r = lax.broadcasted_iota(jnp.int32, (n, n), 0)
    c = lax.broadcasted_iota(jnp.int32, (n, n), 1)
    return jnp.where((r <= c) if upper else (r >= c), 1.0, 0.0).astype(F32)


def _scan_max(x, axis, reverse, size):
    idx = lax.broadcasted_iota(jnp.int32, x.shape, axis)
    k = 1
    while k < size:
        if reverse:
            shifted = pltpu.roll(x, x.shape[axis] - k, axis)
            ok = idx < size - k
        else:
            shifted = pltpu.roll(x, k, axis)
            ok = idx >= k
        x = jnp.maximum(x, jnp.where(ok, shifted, NEG))
        k *= 2
    return x


def _mlstm_kernel(k_ref, qt_ref, vt_ref, ot_ref, zt_ref, gr_ref, fb_ref, ng_ref, y_ref,
                  state_ref, sprev_ref, rows_ref, gain_ref, et_ref, w2a_ref, w2b_ref, *, seq):
    L = ML_CHUNK
    nc = seq // L
    head = pl.program_id(1)
    scale = ML_HEAD_DIM ** -0.5
    lane = lax.broadcasted_iota(jnp.int32, (1, LANES), 1)

    for d in range(2):
        rev = d == 1
        fb = fb_ref[d, head]
        i_r = gr_ref[8 * d + head]
        lf_r = _log_sigmoid(gr_ref[8 * d + 4 + head] + fb)
        b_r = _dot_hi(lf_r, _tri(L, upper=not rev))
        g = jnp.broadcast_to(jnp.sum(lf_r, axis=1, keepdims=True), (nc, L))
        a_r = g - b_r + i_r
        m_loc = jnp.broadcast_to(jnp.max(a_r, axis=1, keepdims=True), (nc, L))
        jr = lax.broadcasted_iota(jnp.int32, (nc, nc), 0)
        jc = lax.broadcasted_iota(jnp.int32, (nc, nc), 1)
        before = jnp.where((jc > jr) if rev else (jc < jr), 1.0, 0.0).astype(F32)
        g_ex = _dot_hi(before, g)
        x = m_loc - (g_ex + g)
        row = lax.broadcasted_iota(jnp.int32, (nc, L), 0)
        if rev:
            x_prev = jnp.where(row < nc - 1, pltpu.roll(x, nc - 1, 0), NEG)
        else:
            x_prev = jnp.where(row >= 1, pltpu.roll(x, 1, 0), NEG)
        m_prev = g_ex + jnp.maximum(_scan_max(x_prev, 0, rev, nc), 0.0)
        m_after = jnp.maximum(g + m_prev, m_loc)
        gain_ref[d, 0] = jnp.exp(g + m_prev - m_after)
        gain_ref[d, 1] = jnp.exp(m_loc - m_after)
        e_r = i_r - b_r
        mu = jnp.maximum(m_prev, _scan_max(e_r, 1, rev, L))
        rows_ref[d, 0] = mu - math.log(scale)
        rows_ref[d, 1] = jnp.exp(m_prev - mu) * scale
        rows_ref[d, 2] = jnp.exp(-b_r - mu)
        rows_ref[d, 3] = jnp.exp(a_r - m_loc)
        e_pad = jnp.concatenate([e_r, jnp.zeros((LANES - nc, L), F32)], axis=0) if nc < LANES else e_r
        et_ref[d] = e_pad.T
        state_ref[d] = jnp.zeros((ML_AUG, ML_HEAD_DIM), F32)

    rr = lax.broadcasted_iota(jnp.int32, (L, L), 0)
    cc = lax.broadcasted_iota(jnp.int32, (L, L), 1)
    ones_row = jnp.where(lax.broadcasted_iota(jnp.int32, (ML_HEAD_DIM, L), 0) == 0, 1.0, 0.0).astype(BF16)

    def state_step(j, carry):
        for d in range(2):
            c = j if d == 0 else nc - 1 - j
            kc = k_ref[pl.ds(pl.multiple_of(c * L, L), L), :]
            vaug_t = jnp.concatenate([vt_ref[c], ones_row], axis=0)
            wv_t = (rows_ref[d, 3, pl.ds(c, 1), :] * vaug_t.astype(F32)).astype(BF16)
            s_loc = jnp.dot(wv_t, kc, preferred_element_type=F32)
            st = state_ref[d]
            sprev_ref[d, c] = st.astype(BF16)
            state_ref[d] = gain_ref[d, 0, pl.ds(c, 1), :] * st + gain_ref[d, 1, pl.ds(c, 1), :] * s_loc
        return carry

    lax.fori_loop(0, nc, state_step, 0, unroll=min(8, nc))

    ng_col = jnp.broadcast_to(ng_ref[...], (ML_HEAD_DIM, L))
    group = min(ML_GROUP, nc // 2)
    ngroups = nc // group

    def score_stage(g, w2_ref):
        for jj in range(group):
            c = g * group + jj
            kc = k_ref[pl.ds(pl.multiple_of(c * L, L), L), :]
            s_t = jnp.dot(kc, qt_ref[c], preferred_element_type=F32)
            for d in range(2):
                e_col = jnp.sum(jnp.where(lane == c, et_ref[d], 0.0), axis=1, keepdims=True)
                mask = (rr >= cc) if d == 1 else (rr <= cc)
                p_t = jnp.where(mask, jnp.exp(e_col - rows_ref[d, 0, pl.ds(c, 1), :]), 0.0)
                w2_ref[2 * jj + d] = (s_t * p_t).astype(BF16)

    def value_stage(g, w2_ref):
        for jj in range(group):
            c = g * group + jj
            q_t = qt_ref[c]
            vaug_t = jnp.concatenate([vt_ref[c], ones_row], axis=0)
            hs = None
            for d in range(2):
                out_t = (jnp.dot(vaug_t, w2_ref[2 * jj + d], preferred_element_type=F32)
                         + rows_ref[d, 1, pl.ds(c, 1), :]
                         * jnp.dot(sprev_ref[d, c], q_t, preferred_element_type=F32))
                den = out_t[ML_HEAD_DIM:ML_HEAD_DIM + 1]
                h_d = out_t[:ML_HEAD_DIM] / jnp.maximum(jnp.abs(den), rows_ref[d, 2, pl.ds(c, 1), :])
                hs = h_d if hs is None else hs + h_d
            hs = jax.nn.sigmoid(ot_ref[c].astype(F32)) * hs
            mu = jnp.mean(hs, axis=0, keepdims=True)
            hc = hs - mu
            var = jnp.mean(hc * hc, axis=0, keepdims=True)
            y_t = hc * lax.rsqrt(var + LN_EPS) * ng_col * _silu(zt_ref[c].astype(F32))
            y_ref[pl.ds(pl.multiple_of(c * L, L), L), :] = y_t.T.astype(BF16)

    w2b_ref[...] = jnp.zeros(w2b_ref.shape, BF16)

    def pipeline_step(g, carry):
        @pl.when(g % 2 == 0)
        def _():
            score_stage(g, w2a_ref)
            value_stage(jnp.maximum(g - 1, 0), w2b_ref)

        @pl.when(g % 2 == 1)
        def _():
            score_stage(g, w2b_ref)
            value_stage(g - 1, w2a_ref)
        return carry

    lax.fori_loop(0, ngroups, pipeline_step, 0)
    value_stage(ngroups - 1, w2b_ref if (ngroups - 1) % 2 else w2a_ref)


def _mlstm(u3, ut, gates, f_bias, norm_g):
    bsz, s, _ = u3.shape
    d = ML_HEAD_DIM
    nc = s // ML_CHUNK
    tcol = lambda part: pl.BlockSpec((nc, d, ML_CHUNK), lambda b, h: (b, part * ML_HEADS + h, 0))
    return pl.pallas_call(
        functools.partial(_mlstm_kernel, seq=s),
        grid=(bsz, ML_HEADS),
        in_specs=[pl.BlockSpec((None, s, d), lambda b, h: (b, 0, B_K // d + h)),
                  tcol(0), tcol(1), tcol(2), tcol(3),
                  pl.BlockSpec((ML_GATES, None, nc, ML_CHUNK), lambda b, h: (0, b, 0, 0)),
                  pl.BlockSpec(memory_space=pltpu.SMEM),
                  pl.BlockSpec((d, 1), lambda b, h: (h, 0))],
        out_specs=pl.BlockSpec((None, s, d), lambda b, h: (b, 0, h)),
        out_shape=jax.ShapeDtypeStruct((bsz, s, ML_WIDTH), BF16),
        scratch_shapes=[pltpu.VMEM((2, ML_AUG, d), F32),
                        pltpu.VMEM((2, nc, ML_AUG, d), BF16),
                        pltpu.VMEM((2, 4, nc, ML_CHUNK), F32),
                        pltpu.VMEM((2, 2, nc, ML_CHUNK), F32),
                        pltpu.VMEM((2, ML_CHUNK, LANES), F32),
                        pltpu.VMEM((2 * min(ML_GROUP, nc // 2), ML_CHUNK, ML_CHUNK), BF16),
                        pltpu.VMEM((2 * min(ML_GROUP, nc // 2), ML_CHUNK, ML_CHUNK), BF16)],
        compiler_params=_cparams(("parallel", "parallel")),
        name="mlstm",
    )(u3, ut, ut, ut, ut, gates, f_bias, norm_g.reshape(ML_WIDTH, 1))


def _a_head_copies(src, dst):
    out = []
    for i in range(2):
        for j in range(WA_GROUP):
            for half in range(2):
                h = 2 * WA_GROUP * i + WA_GROUP * half + j
                out.append((src + h * HEAD_DIM, HEAD_DIM, dst + ((WA_GROUP * i + j) * 2 + half) * HEAD_DIM))
    return out


_NATURAL_COPIES = ([(4624, 3072, C_Q)] + _a_head_copies(0, A_Q) + _a_head_copies(1280, A_Z)
                   + [(768, 512, A_K), (2560, 512, B_K)])
_FEATURE_MAJOR_COPIES = [(2048, 512, 0), (3072, 1536, 512), (4608, ML_GATES, T_WIDTH)]
_OUT_ROW_COPIES = [(s0, n, d0) for s0, n, d0 in _a_head_copies(0, 0)] + [(WA_WIDTH, MIX_WIDTH - WA_WIDTH, WA_WIDTH)]


def _inprep_kernel(wt_ref, wn_ref, wf_ref):
    for src, width, dst in _NATURAL_COPIES:
        wn_ref[dst:dst + width, :] = wt_ref[src:src + width, :].astype(BF16)
    for src, width, dst in _FEATURE_MAJOR_COPIES:
        wf_ref[dst:dst + width, :] = wt_ref[src:src + width, :].astype(BF16)


def _prep_in_weights(w_in):
    depth, d, n = w_in.shape
    tk = 256
    return pl.pallas_call(
        _inprep_kernel,
        grid=(depth, d // tk),
        in_specs=[pl.BlockSpec((None, n, tk), lambda l, i: (l, 0, i))],
        out_specs=[pl.BlockSpec((None, U_WIDTH, tk), lambda l, i: (l, 0, i)),
                   pl.BlockSpec((None, T_WIDTH + ML_GATES, tk), lambda l, i: (l, 0, i))],
        out_shape=[jax.ShapeDtypeStruct((depth, U_WIDTH, d), BF16),
                   jax.ShapeDtypeStruct((depth, T_WIDTH + ML_GATES, d), BF16)],
        compiler_params=_cparams(("parallel", "parallel")),
        name="prep_in_weights",
    )(jnp.swapaxes(w_in, 1, 2))


def _outprep_kernel(w_ref, o_ref):
    for src, rows, dst in _OUT_ROW_COPIES:
        o_ref[dst:dst + rows, :] = w_ref[src:src + rows, :].astype(BF16)


def _prep_out_weights(w_out):
    depth, k, n = w_out.shape
    return pl.pallas_call(
        _outprep_kernel,
        grid=(depth,),
        in_specs=[pl.BlockSpec((None, k, n), lambda l: (l, 0, 0))],
        out_specs=pl.BlockSpec((None, k, n), lambda l: (l, 0, 0)),
        out_shape=jax.ShapeDtypeStruct((depth, k, n), BF16),
        compiler_params=_cparams(("parallel",)),
        name="prep_out_weights",
    )(w_out)


def _gather_columns(v, copies, width):
    out = jnp.zeros(v.shape[:-1] + (width,), v.dtype)
    for src, n, dst in copies:
        out = out.at[..., dst:dst + n].set(v[..., src:src + n])
    return out


def kernel(x, emb_ln_g, emb_ln_b, w_in, b_in, w_out, b_out, ln_g, ln_b, t5_bias, sink, ml_f_bias,
           ml_norm_g, na_rpb):
    bsz, s, d = x.shape
    depth = w_in.shape[0]
    alpha = (2 * depth) ** 0.25
    m = bsz * s
    nc = s // ML_CHUNK
    assert d == D_MODEL and s % NA_BLOCK == 0 and s // GRID_W >= NA_KH_MAX and nc <= LANES

    res, hb = _input_norm(x.reshape(m, d), emb_ln_g, emb_ln_b)
    bias_a = _window_bias(t5_bias)
    bias_c = _na_bias(na_rpb, s // GRID_W)
    wn_all, wf_all = _prep_in_weights(w_in)
    wo_all = _prep_out_weights(w_out)
    for l in range(depth):
        bn = _gather_columns(b_in[l], _NATURAL_COPIES, U_WIDTH)
        bf = _gather_columns(b_in[l], _FEATURE_MAJOR_COPIES, T_WIDTH + ML_GATES)
        u3 = _in_projection(hb, wn_all, l, bn).reshape(bsz, s, U_WIDTH)
        ut, g = _t_projection(hb, wf_all, l, bf)
        g = g.reshape(ML_GATES, bsz, nc, ML_CHUNK)
        ya = _window_attention(u3, bias_a, sink[l])
        yb = _mlstm(u3, ut, g, ml_f_bias[l], ml_norm_g[l])
        yc = _neighbourhood_attention(u3, bias_c, l)
        last = l == depth - 1
        outs = _out_projection(ya.reshape(m, WA_WIDTH), yb.reshape(m, ML_WIDTH), yc.reshape(m, NA_WIDTH),
                               wo_all, l, b_out[l], res, ln_g[l], ln_b[l], alpha, not last)
        res = outs[0]
        hb = None if last else outs[1]
    return res.reshape(bsz, s, d)
```

```python
import functools
import math

import numpy as np
import jax
import jax.numpy as jnp
from jax import lax
from jax.experimental import pallas as pl
from jax.experimental.pallas import tpu as pltpu

F32 = jnp.float32
BF16 = jnp.bfloat16
LANES = 128

D_MODEL = 2048
HEAD_DIM = 64
LN_EPS = 1e-5
NEG = -1e30
LOG2E = math.log2(math.e)
SM_CHUNK = 64

WA_HEADS = 12
WA_KV_HEADS = 4
WA_GROUP = WA_HEADS // WA_KV_HEADS
WA_WIDTH = WA_HEADS * HEAD_DIM
WA_KV_WIDTH = WA_KV_HEADS * HEAD_DIM
WA_BLOCK = 128
WINDOW = 128
T5_BUCKETS = 32
T5_MAX_DIST = 128
ML_HEADS = 4
ML_HEAD_DIM = 128
ML_WIDTH = ML_HEADS * ML_HEAD_DIM
ML_CHUNK = 128
ML_AUG = 2 * ML_HEAD_DIM
ML_GATES = 4 * ML_HEADS
ML_GROUP = 8
NA_HEADS = 12
NA_WIDTH = NA_HEADS * HEAD_DIM
GRID_W = 64
NA_KH_MAX = 8
NA_KW = 16
NA_ROWS = 4
NA_BLOCK = NA_ROWS * GRID_W
NA_PBUF = 3

MIX_WIDTH = WA_WIDTH + ML_WIDTH + NA_WIDTH

C_Q, C_K, C_V, C_Z = 0, 768, 1536, 2304
A_Q, A_Z, A_K, A_V = 3072, 3840, 4608, 4864
B_K = 5120
U_WIDTH = 5632
T_WIDTH = 4 * ML_WIDTH

VMEM_LIMIT = 56 * 1024 * 1024
LN_TM = 1024
IN_TM, IN_TN = 1024, 2816
T_TM = 1024
OUT_TM, OUT_SLABS = 512, 2


def _cparams(sem):
    return pltpu.CompilerParams(dimension_semantics=sem, vmem_limit_bytes=VMEM_LIMIT)


def _dot_hi(a, b):
    return jnp.dot(a, b, preferred_element_type=F32, precision=lax.Precision.HIGHEST)


def _layer_norm_rows(x, g, b):
    mu = jnp.mean(x, axis=-1, keepdims=True)
    xc = x - mu
    var = jnp.mean(xc * xc, axis=-1, keepdims=True)
    return xc * lax.rsqrt(var + LN_EPS) * g + b


def _ln_kernel(x_ref, g_ref, b_ref, of_ref, ob_ref):
    y = _layer_norm_rows(x_ref[...].astype(F32), g_ref[...], b_ref[...])
    of_ref[...] = y
    ob_ref[...] = y.astype(BF16)


def _input_norm(x2, g, b):
    m, d = x2.shape
    tm = min(LN_TM, m)
    row = pl.BlockSpec((tm, d), lambda i: (i, 0))
    vec = pl.BlockSpec((1, d), lambda i: (0, 0))
    return pl.pallas_call(
        _ln_kernel,
        grid=(m // tm,),
        in_specs=[row, vec, vec],
        out_specs=[row, row],
        out_shape=[jax.ShapeDtypeStruct((m, d), F32), jax.ShapeDtypeStruct((m, d), BF16)],
        compiler_params=_cparams(("parallel",)),
        name="input_norm",
    )(x2, g.reshape(1, d), b.reshape(1, d))


_NT = (((1,), (1,)), ((), ()))


def _inproj_kernel(h_ref, wt_ref, b_ref, u_ref):
    acc = lax.dot_general(h_ref[...], wt_ref[...], _NT, preferred_element_type=F32)
    u_ref[...] = (acc + b_ref[...]).astype(u_ref.dtype)


def _in_projection(hb, wt_all, layer, b):
    m, d = hb.shape
    n = wt_all.shape[1]
    tm, tn = min(IN_TM, m), IN_TN
    return pl.pallas_call(
        _inproj_kernel,
        grid=(n // tn, m // tm),
        in_specs=[pl.BlockSpec((tm, d), lambda j, i: (i, 0)),
                  pl.BlockSpec((None, tn, d), lambda j, i: (layer, j, 0), pipeline_mode=pl.Buffered(1)),
                  pl.BlockSpec((1, tn), lambda j, i: (0, j))],
        out_specs=pl.BlockSpec((tm, tn), lambda j, i: (i, j)),
        out_shape=jax.ShapeDtypeStruct((m, n), BF16),
        compiler_params=_cparams(("parallel", "parallel")),
        name="in_projection",
    )(hb, wt_all, b.reshape(1, n))


def _tproj_kernel(h_ref, wt_ref, bt_ref, ut_ref, gate_ref):
    acc = lax.dot_general(wt_ref[...], h_ref[...], _NT, preferred_element_type=F32) + bt_ref[...]
    for c in range(ut_ref.shape[0]):
        ut_ref[c] = acc[:T_WIDTH, c * ML_CHUNK:(c + 1) * ML_CHUNK].astype(BF16)
    gate_ref[...] = acc[T_WIDTH:]


def _t_projection(hb, wt_all, layer, bt):
    m, d = hb.shape
    tm = min(T_TM, m)
    rows = wt_all.shape[1]
    return pl.pallas_call(
        _tproj_kernel,
        grid=(m // tm,),
        in_specs=[pl.BlockSpec((tm, d), lambda i: (i, 0)),
                  pl.BlockSpec((None, rows, d), lambda i: (layer, 0, 0)),
                  pl.BlockSpec((rows, 1), lambda i: (0, 0))],
        out_specs=[pl.BlockSpec((tm // ML_CHUNK, T_WIDTH, ML_CHUNK), lambda i: (i, 0, 0)),
                   pl.BlockSpec((ML_GATES, tm), lambda i: (0, i))],
        out_shape=[jax.ShapeDtypeStruct((m // ML_CHUNK, T_WIDTH, ML_CHUNK), BF16),
                   jax.ShapeDtypeStruct((ML_GATES, m), F32)],
        compiler_params=_cparams(("parallel",)),
        name="t_projection",
    )(hb, wt_all, bt.reshape(rows, 1))


def _outproj_kernel(ya_ref, yb_ref, yc_ref, w_ref, b_ref, res_ref, g_ref, beta_ref, *out_refs, alpha):
    slab = ya_ref.shape[0] // OUT_SLABS
    for rows in (slice(i * slab, (i + 1) * slab) for i in range(OUT_SLABS)):
        y = jnp.concatenate([ya_ref[rows, :], yb_ref[rows, :], yc_ref[rows, :]], axis=1)
        out = jnp.dot(y, w_ref[...], preferred_element_type=F32) + b_ref[...]
        r = _layer_norm_rows(alpha * res_ref[rows, :] + out, g_ref[...], beta_ref[...])
        out_refs[0][rows, :] = r
        if len(out_refs) > 1:
            out_refs[1][rows, :] = r.astype(BF16)


def _out_projection(ya, yb, yc, w_all, layer, b, res, g, beta, alpha, want_bf16):
    m, d = res.shape
    tm = min(OUT_TM, m)
    row = lambda i: (i, 0)
    const = lambda i: (0, 0)
    out_specs = [pl.BlockSpec((tm, d), row)]
    out_shape = [jax.ShapeDtypeStruct((m, d), F32)]
    if want_bf16:
        out_specs.append(pl.BlockSpec((tm, d), row))
        out_shape.append(jax.ShapeDtypeStruct((m, d), BF16))
    vec = lambda v: v.reshape(1, d)
    return pl.pallas_call(
        functools.partial(_outproj_kernel, alpha=alpha),
        grid=(m // tm,),
        in_specs=[pl.BlockSpec((tm, WA_WIDTH), row),
                  pl.BlockSpec((tm, ML_WIDTH), row),
                  pl.BlockSpec((tm, NA_WIDTH), row),
                  pl.BlockSpec((None, MIX_WIDTH, d), lambda i: (layer, 0, 0)),
                  pl.BlockSpec((1, d), const),
                  pl.BlockSpec((tm, d), row),
                  pl.BlockSpec((1, d), const),
                  pl.BlockSpec((1, d), const)],
        out_specs=out_specs,
        out_shape=out_shape,
        compiler_params=_cparams(("parallel",)),
        name="out_projection",
    )(ya, yb, yc, w_all, vec(b), res, vec(g), vec(beta))


def _silu(z):
    return z * jax.nn.sigmoid(z)


def _scale_q(q):
    return (q.astype(F32) * (HEAD_DIM ** -0.5 * LOG2E)).astype(BF16)


def _wattn_kernel(q_ref, z_ref, k0_ref, k1_ref, k2_ref, k3_ref, v0_ref, v1_ref, v2_ref, v3_ref,
                  bias_e_ref, bias_o_ref, sink_ref, o_ref, s0_ref, s1_ref, p_ref):
    n = pl.program_id(1)
    blk = WA_BLOCK
    npair = WA_KV_HEADS // 2
    rows = 2 * WA_GROUP * blk
    lane = lax.broadcasted_iota(jnp.int32, (blk, 2 * HEAD_DIM), 1)
    left = lane < HEAD_DIM
    k_refs = (k0_ref, k1_ref, k2_ref, k3_ref)
    v_refs = (v0_ref, v1_ref, v2_ref, v3_ref)

    def step(s_read, s_write):
        for half in range(2):
            qrows = slice(half * blk, (half + 1) * blk)
            bias_ref = (bias_e_ref, bias_o_ref)[half]
            for i in range(npair):
                ks = slice(i * 2 * HEAD_DIM, (i + 1) * 2 * HEAD_DIM)
                k3 = jnp.concatenate([r[:, ks] for r in k_refs[half:half + 3]], axis=0)
                tiles = [_scale_q(q_ref[qrows, (WA_GROUP * i + j) * 2 * HEAD_DIM:(WA_GROUP * i + j + 1) * 2 * HEAD_DIM])
                         for j in range(WA_GROUP)]
                zero = jnp.zeros_like(tiles[0])
                lhs = jnp.concatenate([jnp.where(left, t, zero) for t in tiles]
                                      + [jnp.where(left, zero, t) for t in tiles], axis=0)
                s_write[half, i] = lax.dot_general(lhs, k3, (((1,), (1,)), ((), ())),
                                                   preferred_element_type=F32)
                v3 = jnp.concatenate([r[:, ks] for r in v_refs[half:half + 3]], axis=0)
                dens = []
                for c in range(rows // SM_CHUNK):
                    rs = slice(c * SM_CHUNK, (c + 1) * SM_CHUNK)
                    sk = sink_ref[0, 2 * WA_GROUP * i + c * SM_CHUNK // blk] * LOG2E
                    sc = s_read[half, i, rs, :] + bias_ref[i, rs, :]
                    m = jnp.maximum(jnp.max(sc, axis=-1, keepdims=True), sk)
                    p = jnp.exp2(sc - m)
                    dens.append(jnp.sum(p, axis=-1, keepdims=True) + jnp.exp2(sk - m))
                    p_ref[half, i, rs, :] = p.astype(BF16)
                pv = jnp.dot(p_ref[half, i], v3, preferred_element_type=F32)
                o = jnp.concatenate([pv[c * SM_CHUNK:(c + 1) * SM_CHUNK] / dens[c]
                                     for c in range(rows // SM_CHUNK)], axis=0)
                for j in range(WA_GROUP):
                    t = WA_GROUP * i + j
                    ot = jnp.where(left, o[j * blk:(j + 1) * blk],
                                   o[(WA_GROUP + j) * blk:(WA_GROUP + j + 1) * blk])
                    cs = slice(t * 2 * HEAD_DIM, (t + 1) * 2 * HEAD_DIM)
                    o_ref[qrows, cs] = (ot * _silu(z_ref[qrows, cs].astype(F32))).astype(BF16)

    @pl.when(n == 0)
    def _():
        s1_ref[...] = jnp.zeros(s1_ref.shape, F32)

    @pl.when(n % 2 == 0)
    def _():
        step(s1_ref, s0_ref)

    @pl.when(n % 2 == 1)
    def _():
        step(s0_ref, s1_ref)


def _window_attention(u3, bias, sink):
    bsz, s, _ = u3.shape
    nb = s // WA_BLOCK
    npr = nb // 2
    qw, kw = WA_WIDTH, WA_KV_WIDTH
    npair = WA_KV_HEADS // 2
    rows = 2 * WA_GROUP * WA_BLOCK
    clamp = lambda i: jnp.clip(i, 0, nb - 1)
    pair = lambda i: jnp.clip(i, 0, npr - 1)
    kspec = lambda off, c: pl.BlockSpec((None, WA_BLOCK, kw), lambda b, n: (b, clamp(2 * n + off), c))
    vspec = lambda off, c: pl.BlockSpec((None, WA_BLOCK, kw), lambda b, n: (b, clamp(2 * n - 2 + off), c))
    even_type = lambda b, n: (jnp.where(n <= 1, 0, 1), 0, 0, 0)
    odd_type = lambda b, n: (jnp.where(n == npr, 2, 1), 0, 0, 0)
    sshape = (2, npair, rows, 3 * WA_BLOCK)
    bias = bias.reshape(3, npair, rows, 3 * WA_BLOCK)
    return pl.pallas_call(
        _wattn_kernel,
        grid=(bsz, npr + 1),
        in_specs=[pl.BlockSpec((None, 2 * WA_BLOCK, qw), lambda b, n: (b, pair(n), A_Q // qw)),
                  pl.BlockSpec((None, 2 * WA_BLOCK, qw), lambda b, n: (b, pair(n - 1), A_Z // qw)),
                  kspec(-1, A_K // kw), kspec(0, A_K // kw), kspec(1, A_K // kw), kspec(2, A_K // kw),
                  vspec(-1, A_V // kw), vspec(0, A_V // kw), vspec(1, A_V // kw), vspec(2, A_V // kw),
                  pl.BlockSpec((None, npair, rows, 3 * WA_BLOCK), even_type),
                  pl.BlockSpec((None, npair, rows, 3 * WA_BLOCK), odd_type),
                  pl.BlockSpec(memory_space=pltpu.SMEM)],
        out_specs=pl.BlockSpec((None, 2 * WA_BLOCK, qw), lambda b, n: (b, pair(n - 1), 0)),
        out_shape=jax.ShapeDtypeStruct((bsz, s, qw), BF16),
        scratch_shapes=[pltpu.VMEM(sshape, F32), pltpu.VMEM(sshape, F32), pltpu.VMEM(sshape, BF16)],
        compiler_params=_cparams(("parallel", "arbitrary")),
        name="window_attention",
    )(u3, u3, u3, u3, u3, u3, u3, u3, u3, u3, bias, bias, sink.reshape(1, WA_HEADS))


def _t5_bucket_np(rel):
    half = T5_BUCKETS // 2
    max_exact = half // 2
    ret = np.where(rel > 0, half, 0)
    n = np.abs(rel)
    nf = np.maximum(n, 1).astype(np.float64)
    v = np.log(nf / max_exact) / math.log(T5_MAX_DIST / max_exact) * (half - max_exact)
    vr = np.round(v)
    v = np.where(np.abs(v - vr) < 1e-9, vr, v)
    large = np.minimum(max_exact + np.trunc(v).astype(np.int64), half - 1)
    return ret + np.where(n < max_exact, n, large)


def _wbias_kernel(t5t_ref, bucket_ref, o_ref):
    width = 4 * WA_BLOCK
    bk = bucket_ref[...]
    e = lax.broadcasted_iota(jnp.int32, (T5_BUCKETS, width), 0)
    onehot = jnp.where(e == bk, 1.0, 0.0).astype(F32)
    g = _dot_hi(t5t_ref[...], onehot) * LOG2E + jnp.where(bk < 0, NEG, 0.0)
    col = lax.broadcasted_iota(jnp.int32, (WA_BLOCK, 3 * WA_BLOCK), 1)
    for h in range(WA_HEADS):
        row = jnp.broadcast_to(g[h:h + 1, :], (WA_BLOCK, width))
        t = pltpu.roll(row, 3 * WA_BLOCK, 1, stride=1, stride_axis=0)[:, :3 * WA_BLOCK]
        o_ref[0, h] = jnp.where(col < WA_BLOCK, NEG, t)
        o_ref[1, h] = t
        o_ref[2, h] = jnp.where(col >= 2 * WA_BLOCK, NEG, t)


def _window_bias(t5_table):
    rel = np.arange(4 * WA_BLOCK) - 2 * WA_BLOCK
    bucket = np.where(np.abs(rel) <= WINDOW, _t5_bucket_np(rel), -1).astype(np.int32)
    t5t = jnp.pad(t5_table.astype(F32).T, ((0, 16 - WA_HEADS), (0, 0)))
    return pl.pallas_call(
        _wbias_kernel,
        out_shape=jax.ShapeDtypeStruct((3, WA_HEADS, WA_BLOCK, 3 * WA_BLOCK), F32),
        name="window_bias",
    )(t5t, jnp.asarray(bucket).reshape(1, -1))


def _natten_kernel(q_ref, z_ref, kp_ref, kc_ref, kn_ref, vp_ref, vc_ref, vn_ref, bias_ref, o_ref,
                   s0_ref, s1_ref, p_ref):
    n = pl.program_id(1)
    blk = NA_BLOCK
    lane = lax.broadcasted_iota(jnp.int32, (blk, 2 * HEAD_DIM), 1)
    left = lane < HEAD_DIM

    ones_col = jnp.where(lax.broadcasted_iota(jnp.int32, (3 * blk, 2 * HEAD_DIM), 1) == 0, 1.0, 0.0).astype(BF16)

    def step(s_read, s_write):
        for i in range(NA_HEADS // 2):
            cs = slice(i * 2 * HEAD_DIM, (i + 1) * 2 * HEAD_DIM)
            k3 = jnp.concatenate([kp_ref[:, cs], kc_ref[:, cs], kn_ref[:, cs]], axis=0)
            t = _scale_q(q_ref[:, cs])
            zero = jnp.zeros_like(t)
            lhs = jnp.concatenate([jnp.where(left, t, zero), jnp.where(left, zero, t)], axis=0)
            s_write[i] = lax.dot_general(lhs, k3, (((1,), (1,)), ((), ())), preferred_element_type=F32)
            pb = p_ref.at[i % NA_PBUF]
            for c in range(2 * blk // SM_CHUNK):
                rs = slice(c * SM_CHUNK, (c + 1) * SM_CHUNK)
                sc = s_read[i, rs, :] + bias_ref[i, rs, :]
                m = jnp.max(sc, axis=-1, keepdims=True)
                pb[rs, :] = jnp.exp2(sc - m).astype(BF16)
            if i > 0:
                weighted_values(i - 1)
        weighted_values(NA_HEADS // 2 - 1)

    def weighted_values(i):
        cs = slice(i * 2 * HEAD_DIM, (i + 1) * 2 * HEAD_DIM)
        v3 = jnp.concatenate([vp_ref[:, cs], vc_ref[:, cs], vn_ref[:, cs]], axis=0)
        vaug = jnp.concatenate([v3, ones_col], axis=1)
        out = jnp.dot(p_ref[i % NA_PBUF], vaug, preferred_element_type=F32)
        o = out[:, :2 * HEAD_DIM] / out[:, 2 * HEAD_DIM:2 * HEAD_DIM + 1]
        ot = jnp.where(left, o[:blk], o[blk:])
        o_ref[:, cs] = (ot * _silu(z_ref[:, cs].astype(F32))).astype(BF16)

    @pl.when(n == 0)
    def _():
        s1_ref[...] = jnp.zeros(s1_ref.shape, F32)

    @pl.when(n % 2 == 0)
    def _():
        step(s1_ref, s0_ref)

    @pl.when(n % 2 == 1)
    def _():
        step(s0_ref, s1_ref)


def _neighbourhood_attention(u3, bias, layer):
    bsz, s, _ = u3.shape
    nblk = s // NA_BLOCK
    w = NA_WIDTH
    npair = NA_HEADS // 2
    clamp = lambda i: jnp.clip(i, 0, nblk - 1)
    spec = lambda off, c: pl.BlockSpec((None, NA_BLOCK, w), lambda b, n: (b, clamp(n + off), c))
    btype = lambda b, n: (layer, jnp.where(n <= 1, 0, jnp.where(n == nblk, 2, 1)), 0, 0, 0)
    sshape = (npair, 2 * NA_BLOCK, 3 * NA_BLOCK)
    return pl.pallas_call(
        _natten_kernel,
        grid=(bsz, nblk + 1),
        in_specs=[spec(0, C_Q // w), spec(-1, C_Z // w),
                  spec(-1, C_K // w), spec(0, C_K // w), spec(1, C_K // w),
                  spec(-2, C_V // w), spec(-1, C_V // w), spec(0, C_V // w),
                  pl.BlockSpec((None, None) + sshape, btype)],
        out_specs=pl.BlockSpec((None, NA_BLOCK, w), lambda b, n: (b, clamp(n - 1), 0)),
        out_shape=jax.ShapeDtypeStruct((bsz, s, w), BF16),
        scratch_shapes=[pltpu.VMEM(sshape, F32), pltpu.VMEM(sshape, F32),
                        pltpu.VMEM((NA_PBUF,) + sshape[1:], BF16)],
        compiler_params=_cparams(("parallel", "arbitrary")),
        name="neighbourhood_attention",
    )(u3, u3, u3, u3, u3, u3, u3, u3, bias.reshape(bias.shape[:2] + sshape))


def _na_valid_rows(rows):
    kh = min(NA_KH_MAX, rows)
    nblk = rows // NA_ROWS
    out = []
    for j in (0, min(1, nblk - 1), nblk - 1):
        r = NA_ROWS * j + np.arange(NA_ROWS)[:, None]
        kr = NA_ROWS * (j - 1) + np.arange(3 * NA_ROWS)[None, :]
        rs = np.clip(r - kh // 2, 0, rows - kh)
        out.append((kr >= rs) & (kr < rs + kh))
    return np.stack(out)


def _nabias_kernel(rpb_ref, o_ref, *, valid):
    w = GRID_W
    j = lax.broadcasted_iota(jnp.int32, (32, 2 * w), 1)
    e = lax.broadcasted_iota(jnp.int32, (32, 2 * w), 0)
    dc = jnp.clip(j - w, -(NA_KW - 1), NA_KW - 1) + NA_KW - 1
    g = _dot_hi(rpb_ref[...], jnp.where(e == dc, 1.0, 0.0).astype(F32)) * LOG2E
    lane = lax.broadcasted_iota(jnp.int32, (w, 2 * w), 1)
    qc = lax.broadcasted_iota(jnp.int32, (w, 2 * w), 0)
    kc = lane & (w - 1)
    col_start = jnp.clip(qc - NA_KW // 2, 0, w - NA_KW)
    col_ok = (kc >= col_start) & (kc < col_start + NA_KW)
    left = lane < w
    neg = jnp.full((w, 2 * w), NEG, F32)

    def toeplitz(dr, shift):
        row = jnp.broadcast_to(g[dr:dr + 1, :], (w, 2 * w))
        return pltpu.roll(row, shift, 1, stride=1, stride_axis=0)

    pair = [jnp.where(col_ok, jnp.where(left, toeplitz(d, w), toeplitz(d + 1, 0)), NEG)
            for d in range(2 * NA_KH_MAX - 2)]
    for ty in range(3):
        for rl in range(NA_ROWS):
            for t in range(3 * NA_ROWS // 2):
                d = 2 * t - rl + NA_KH_MAX - 1 - NA_ROWS
                v0, v1 = bool(valid[ty, rl, 2 * t]), bool(valid[ty, rl, 2 * t + 1])
                if v0 and v1:
                    tile = pair[d]
                elif v0:
                    tile = jnp.where(left, pair[d], NEG)
                elif v1:
                    tile = jnp.where(left, NEG, pair[d])
                else:
                    tile = neg
                o_ref[ty, rl * w:(rl + 1) * w, 2 * t * w:(2 * t + 2) * w] = tile


def _na_bias(rpb_all, rows):
    depth = rpb_all.shape[0]
    rpb_p = jnp.pad(rpb_all.astype(F32), ((0, 0), (0, 0), (0, 1), (0, 1)))
    return pl.pallas_call(
        functools.partial(_nabias_kernel, valid=_na_valid_rows(rows)),
        grid=(depth, NA_HEADS),
        in_specs=[pl.BlockSpec((None, None, 16, 32), lambda l, h: (l, h, 0, 0))],
        out_specs=pl.BlockSpec((None, 3, None, NA_BLOCK, 3 * NA_BLOCK), lambda l, h: (l, 0, h, 0, 0)),
        out_shape=jax.ShapeDtypeStruct((depth, 3, NA_HEADS, NA_BLOCK, 3 * NA_BLOCK), F32),
        compiler_params=_cparams(("parallel", "parallel")),
        name="na_bias",
    )(rpb_p)


def _log_sigmoid(x):
    return jnp.minimum(x, 0.0) - jnp.log(1.0 + jnp.exp(-jnp.abs(x)))


def _tri(n, upper):
    r = lax.broadcasted_iota(jnp.int32, (n, n), 0)
    c = lax.broadcasted_iota(jnp.int32, (n, n), 1)
    return jnp.where((r <= c) if upper else (r >= c), 1.0, 0.0).astype(F32)


def _scan_max(x, axis, reverse, size):
    idx = lax.broadcasted_iota(jnp.int32, x.shape, axis)
    k = 1
    while k < size:
        if reverse:
            shifted = pltpu.roll(x, x.shape[axis] - k, axis)
            ok = idx < size - k
        else:
            shifted = pltpu.roll(x, k, axis)
            ok = idx >= k
        x = jnp.maximum(x, jnp.where(ok, shifted, NEG))
        k *= 2
    return x


def _mlstm_kernel(k_ref, qt_ref, vt_ref, ot_ref, zt_ref, gr_ref, fb_ref, ng_ref, y_ref,
                  state_ref, sprev_ref, rows_ref, gain_ref, et_ref, w2a_ref, w2b_ref, *, seq):
    L = ML_CHUNK
    nc = seq // L
    head = pl.program_id(1)
    scale = ML_HEAD_DIM ** -0.5
    lane = lax.broadcasted_iota(jnp.int32, (1, LANES), 1)

    for d in range(2):
        rev = d == 1
        fb = fb_ref[d, head]
        i_r = gr_ref[8 * d + head]
        lf_r = _log_sigmoid(gr_ref[8 * d + 4 + head] + fb)
        b_r = _dot_hi(lf_r, _tri(L, upper=not rev))
        g = jnp.broadcast_to(jnp.sum(lf_r, axis=1, keepdims=True), (nc, L))
        a_r = g - b_r + i_r
        m_loc = jnp.broadcast_to(jnp.max(a_r, axis=1, keepdims=True), (nc, L))
        jr = lax.broadcasted_iota(jnp.int32, (nc, nc), 0)
        jc = lax.broadcasted_iota(jnp.int32, (nc, nc), 1)
        before = jnp.where((jc > jr) if rev else (jc < jr), 1.0, 0.0).astype(F32)
        g_ex = _dot_hi(before, g)
        x = m_loc - (g_ex + g)
        row = lax.broadcasted_iota(jnp.int32, (nc, L), 0)
        if rev:
            x_prev = jnp.where(row < nc - 1, pltpu.roll(x, nc - 1, 0), NEG)
        else:
            x_prev = jnp.where(row >= 1, pltpu.roll(x, 1, 0), NEG)
        m_prev = g_ex + jnp.maximum(_scan_max(x_prev, 0, rev, nc), 0.0)
        m_after = jnp.maximum(g + m_prev, m_loc)
        gain_ref[d, 0] = jnp.exp(g + m_prev - m_after)
        gain_ref[d, 1] = jnp.exp(m_loc - m_after)
        e_r = i_r - b_r
        mu = jnp.maximum(m_prev, _scan_max(e_r, 1, rev, L))
        rows_ref[d, 0] = mu - math.log(scale)
        rows_ref[d, 1] = jnp.exp(m_prev - mu) * scale
        rows_ref[d, 2] = jnp.exp(-b_r - mu)
        rows_ref[d, 3] = jnp.exp(a_r - m_loc)
        e_pad = jnp.concatenate([e_r, jnp.zeros((LANES - nc, L), F32)], axis=0) if nc < LANES else e_r
        et_ref[d] = e_pad.T
        state_ref[d] = jnp.zeros((ML_AUG, ML_HEAD_DIM), F32)

    rr = lax.broadcasted_iota(jnp.int32, (L, L), 0)
    cc = lax.broadcasted_iota(jnp.int32, (L, L), 1)
    ones_row = jnp.where(lax.broadcasted_iota(jnp.int32, (ML_HEAD_DIM, L), 0) == 0, 1.0, 0.0).astype(BF16)

    def state_step(j, carry):
        for d in range(2):
            c = j if d == 0 else nc - 1 - j
            kc = k_ref[pl.ds(pl.multiple_of(c * L, L), L), :]
            vaug_t = jnp.concatenate([vt_ref[c], ones_row], axis=0)
            wv_t = (rows_ref[d, 3, pl.ds(c, 1), :] * vaug_t.astype(F32)).astype(BF16)
            s_loc = jnp.dot(wv_t, kc, preferred_element_type=F32)
            st = state_ref[d]
            sprev_ref[d, c] = st.astype(BF16)
            state_ref[d] = gain_ref[d, 0, pl.ds(c, 1), :] * st + gain_ref[d, 1, pl.ds(c, 1), :] * s_loc
        return carry

    lax.fori_loop(0, nc, state_step, 0, unroll=min(8, nc))

    ng_col = jnp.broadcast_to(ng_ref[...], (ML_HEAD_DIM, L))
    group = min(ML_GROUP, nc // 2)
    ngroups = nc // group

    def score_stage(g, w2_ref):
        for jj in range(group):
            c = g * group + jj
            kc = k_ref[pl.ds(pl.multiple_of(c * L, L), L), :]
            s_t = jnp.dot(kc, qt_ref[c], preferred_element_type=F32)
            for d in range(2):
                e_col = jnp.sum(jnp.where(lane == c, et_ref[d], 0.0), axis=1, keepdims=True)
                mask = (rr >= cc) if d == 1 else (rr <= cc)
                p_t = jnp.where(mask, jnp.exp(e_col - rows_ref[d, 0, pl.ds(c, 1), :]), 0.0)
                w2_ref[2 * jj + d] = (s_t * p_t).astype(BF16)

    def value_stage(g, w2_ref):
        for jj in range(group):
            c = g * group + jj
            q_t = qt_ref[c]
            vaug_t = jnp.concatenate([vt_ref[c], ones_row], axis=0)
            hs = None
            for d in range(2):
                out_t = (jnp.dot(vaug_t, w2_ref[2 * jj + d], preferred_element_type=F32)
                         + rows_ref[d, 1, pl.ds(c, 1), :]
                         * jnp.dot(sprev_ref[d, c], q_t, preferred_element_type=F32))
                den = out_t[ML_HEAD_DIM:ML_HEAD_DIM + 1]
                h_d = out_t[:ML_HEAD_DIM] / jnp.maximum(jnp.abs(den), rows_ref[d, 2, pl.ds(c, 1), :])
                hs = h_d if hs is None else hs + h_d
            hs = jax.nn.sigmoid(ot_ref[c].astype(F32)) * hs
            mu = jnp.mean(hs, axis=0, keepdims=True)
            hc = hs - mu
            var = jnp.mean(hc * hc, axis=0, keepdims=True)
            y_t = hc * lax.rsqrt(var + LN_EPS) * ng_col * _silu(zt_ref[c].astype(F32))
            y_ref[pl.ds(pl.multiple_of(c * L, L), L), :] = y_t.T.astype(BF16)

    w2b_ref[...] = jnp.zeros(w2b_ref.shape, BF16)

    def pipeline_step(g, carry):
        @pl.when(g % 2 == 0)
        def _():
            score_stage(g, w2a_ref)
            value_stage(jnp.maximum(g - 1, 0), w2b_ref)

        @pl.when(g % 2 == 1)
        def _():
            score_stage(g, w2b_ref)
            value_stage(g - 1, w2a_ref)
        return carry

    lax.fori_loop(0, ngroups, pipeline_step, 0)
    value_stage(ngroups - 1, w2b_ref if (ngroups - 1) % 2 else w2a_ref)


def _mlstm(u3, ut, gates, f_bias, norm_g):
    bsz, s, _ = u3.shape
    d = ML_HEAD_DIM
    nc = s // ML_CHUNK
    tcol = lambda part: pl.BlockSpec((nc, d, ML_CHUNK), lambda b, h: (b, part * ML_HEADS + h, 0))
    return pl.pallas_call(
        functools.partial(_mlstm_kernel, seq=s),
        grid=(bsz, ML_HEADS),
        in_specs=[pl.BlockSpec((None, s, d), lambda b, h: (b, 0, B_K // d + h)),
                  tcol(0), tcol(1), tcol(2), tcol(3),
                  pl.BlockSpec((ML_GATES, None, nc, ML_CHUNK), lambda b, h: (0, b, 0, 0)),
                  pl.BlockSpec(memory_space=pltpu.SMEM),
                  pl.BlockSpec((d, 1), lambda b, h: (h, 0))],
        out_specs=pl.BlockSpec((None, s, d), lambda b, h: (b, 0, h)),
        out_shape=jax.ShapeDtypeStruct((bsz, s, ML_WIDTH), BF16),
        scratch_shapes=[pltpu.VMEM((2, ML_AUG, d), F32),
                        pltpu.VMEM((2, nc, ML_AUG, d), BF16),
                        pltpu.VMEM((2, 4, nc, ML_CHUNK), F32),
                        pltpu.VMEM((2, 2, nc, ML_CHUNK), F32),
                        pltpu.VMEM((2, ML_CHUNK, LANES), F32),
                        pltpu.VMEM((2 * min(ML_GROUP, nc // 2), ML_CHUNK, ML_CHUNK), BF16),
                        pltpu.VMEM((2 * min(ML_GROUP, nc // 2), ML_CHUNK, ML_CHUNK), BF16)],
        compiler_params=_cparams(("parallel", "parallel")),
        name="mlstm",
    )(u3, ut, ut, ut, ut, gates, f_bias, norm_g.reshape(ML_WIDTH, 1))


def _a_head_copies(src, dst):
    out = []
    for i in range(2):
        for j in range(WA_GROUP):
            for half in range(2):
                h = 2 * WA_GROUP * i + WA_GROUP * half + j
                out.append((src + h * HEAD_DIM, HEAD_DIM, dst + ((WA_GROUP * i + j) * 2 + half) * HEAD_DIM))
    return out


_NATURAL_COPIES = ([(4624, 3072, C_Q)] + _a_head_copies(0, A_Q) + _a_head_copies(1280, A_Z)
                   + [(768, 512, A_K), (2560, 512, B_K)])
_FEATURE_MAJOR_COPIES = [(2048, 512, 0), (3072, 1536, 512), (4608, ML_GATES, T_WIDTH)]
_OUT_ROW_COPIES = [(s0, n, d0) for s0, n, d0 in _a_head_copies(0, 0)] + [(WA_WIDTH, MIX_WIDTH - WA_WIDTH, WA_WIDTH)]


def _inprep_kernel(wt_ref, wn_ref, wf_ref):
    for src, width, dst in _NATURAL_COPIES:
        wn_ref[dst:dst + width, :] = wt_ref[src:src + width, :].astype(BF16)
    for src, width, dst in _FEATURE_MAJOR_COPIES:
        wf_ref[dst:dst + width, :] = wt_ref[src:src + width, :].astype(BF16)


def _prep_in_weights(w_in):
    depth, d, n = w_in.shape
    tk = 256
    return pl.pallas_call(
        _inprep_kernel,
        grid=(depth, d // tk),
        in_specs=[pl.BlockSpec((None, n, tk), lambda l, i: (l, 0, i))],
        out_specs=[pl.BlockSpec((None, U_WIDTH, tk), lambda l, i: (l, 0, i)),
                   pl.BlockSpec((None, T_WIDTH + ML_GATES, tk), lambda l, i: (l, 0, i))],
        out_shape=[jax.ShapeDtypeStruct((depth, U_WIDTH, d), BF16),
                   jax.ShapeDtypeStruct((depth, T_WIDTH + ML_GATES, d), BF16)],
        compiler_params=_cparams(("parallel", "parallel")),
        name="prep_in_weights",
    )(jnp.swapaxes(w_in, 1, 2))


def _outprep_kernel(w_ref, o_ref):
    for src, rows, dst in _OUT_ROW_COPIES:
        o_ref[dst:dst + rows, :] = w_ref[src:src + rows, :].astype(BF16)


def _prep_out_weights(w_out):
    depth, k, n = w_out.shape
    return pl.pallas_call(
        _outprep_kernel,
        grid=(depth,),
        in_specs=[pl.BlockSpec((None, k, n), lambda l: (l, 0, 0))],
        out_specs=pl.BlockSpec((None, k, n), lambda l: (l, 0, 0)),
        out_shape=jax.ShapeDtypeStruct((depth, k, n), BF16),
        compiler_params=_cparams(("parallel",)),
        name="prep_out_weights",
    )(w_out)


def _gather_columns(v, copies, width):
    out = jnp.zeros(v.shape[:-1] + (width,), v.dtype)
    for src, n, dst in copies:
        out = out.at[..., dst:dst + n].set(v[..., src:src + n])
    return out


def kernel(x, emb_ln_g, emb_ln_b, w_in, b_in, w_out, b_out, ln_g, ln_b, t5_bias, sink, ml_f_bias,
           ml_norm_g, na_rpb):
    bsz, s, d = x.shape
    depth = w_in.shape[0]
    alpha = (2 * depth) ** 0.25
    m = bsz * s
    nc = s // ML_CHUNK
    assert d == D_MODEL and s % NA_BLOCK == 0 and s // GRID_W >= NA_KH_MAX and nc <= LANES

    res, hb = _input_norm(x.reshape(m, d), emb_ln_g, emb_ln_b)
    bias_a = _window_bias(t5_bias)
    bias_c = _na_bias(na_rpb, s // GRID_W)
    wn_all, wf_all = _prep_in_weights(w_in)
    wo_all = _prep_out_weights(w_out)
    for l in range(depth):
        bn = _gather_columns(b_in[l], _NATURAL_COPIES, U_WIDTH)
        bf = _gather_columns(b_in[l], _FEATURE_MAJOR_COPIES, T_WIDTH + ML_GATES)
        u3 = _in_projection(hb, wn_all, l, bn).reshape(bsz, s, U_WIDTH)
        ut, g = _t_projection(hb, wf_all, l, bf)
        g = g.reshape(ML_GATES, bsz, nc, ML_CHUNK)
        ya = _window_attention(u3, bias_a, sink[l])
        yb = _mlstm(u3, ut, g, ml_f_bias[l], ml_norm_g[l])
        yc = _neighbourhood_attention(u3, bias_c, l)
        last = l == depth - 1
        outs = _out_projection(ya.reshape(m, WA_WIDTH), yb.reshape(m, ML_WIDTH), yc.reshape(m, NA_WIDTH),
                               wo_all, l, b_out[l], res, ln_g[l], ln_b[l], alpha, not last)
        res = outs[0]
        hb = None if last else outs[1]
    return res.reshape(bsz, s, d)
```

```python
import functools
import math

import numpy as np
import jax
import jax.numpy as jnp
from jax import lax
from jax.experimental import pallas as pl
from jax.experimental.pallas import tpu as pltpu

F32 = jnp.float32
BF16 = jnp.bfloat16
LANES = 128

D_MODEL = 2048
HEAD_DIM = 64
LN_EPS = 1e-5
NEG = -1e30
LOG2E = math.log2(math.e)
SM_CHUNK = 64

WA_HEADS = 12
WA_KV_HEADS = 4
WA_GROUP = WA_HEADS // WA_KV_HEADS
WA_WIDTH = WA_HEADS * HEAD_DIM
WA_KV_WIDTH = WA_KV_HEADS * HEAD_DIM
WA_BLOCK = 128
WINDOW = 128
T5_BUCKETS = 32
T5_MAX_DIST = 128
ML_HEADS = 4
ML_HEAD_DIM = 128
ML_WIDTH = ML_HEADS * ML_HEAD_DIM
ML_CHUNK = 128
ML_AUG = 2 * ML_HEAD_DIM
ML_GATES = 4 * ML_HEADS
ML_GROUP = 8
NA_HEADS = 12
NA_WIDTH = NA_HEADS * HEAD_DIM
GRID_W = 64
NA_KH_MAX = 8
NA_KW = 16
NA_ROWS = 4
NA_BLOCK = NA_ROWS * GRID_W
NA_PBUF = 3

MIX_WIDTH = WA_WIDTH + ML_WIDTH + NA_WIDTH

C_Q, C_K, C_V, C_Z = 0, 768, 1536, 2304
A_Q, A_Z, A_K, A_V = 3072, 3840, 4608, 4864
B_K = 5120
U_WIDTH = 5632
T_WIDTH = 4 * ML_WIDTH

VMEM_LIMIT = 56 * 1024 * 1024
LN_TM = 1024
IN_TM, IN_TN = 1024, 2816
T_TM = 1024
OUT_TM, OUT_SLABS = 512, 2


def _cparams(sem):
    return pltpu.CompilerParams(dimension_semantics=sem, vmem_limit_bytes=VMEM_LIMIT)


def _dot_hi(a, b):
    return jnp.dot(a, b, preferred_element_type=F32, precision=lax.Precision.HIGHEST)


def _layer_norm_rows(x, g, b):
    mu = jnp.mean(x, axis=-1, keepdims=True)
    xc = x - mu
    var = jnp.mean(xc * xc, axis=-1, keepdims=True)
    return xc * lax.rsqrt(var + LN_EPS) * g + b


def _ln_kernel(x_ref, g_ref, b_ref, of_ref, ob_ref):
    y = _layer_norm_rows(x_ref[...].astype(F32), g_ref[...], b_ref[...])
    of_ref[...] = y
    ob_ref[...] = y.astype(BF16)


def _input_norm(x2, g, b):
    m, d = x2.shape
    tm = min(LN_TM, m)
    row = pl.BlockSpec((tm, d), lambda i: (i, 0))
    vec = pl.BlockSpec((1, d), lambda i: (0, 0))
    return pl.pallas_call(
        _ln_kernel,
        grid=(m // tm,),
        in_specs=[row, vec, vec],
        out_specs=[row, row],
        out_shape=[jax.ShapeDtypeStruct((m, d), F32), jax.ShapeDtypeStruct((m, d), BF16)],
        compiler_params=_cparams(("parallel",)),
        name="input_norm",
    )(x2, g.reshape(1, d), b.reshape(1, d))


_NT = (((1,), (1,)), ((), ()))


def _inproj_kernel(h_ref, wt_ref, b_ref, u_ref):
    acc = lax.dot_general(h_ref[...], wt_ref[...], _NT, preferred_element_type=F32)
    u_ref[...] = (acc + b_ref[...]).astype(u_ref.dtype)


def _in_projection(hb, wt_all, layer, b):
    m, d = hb.shape
    n = wt_all.shape[1]
    tm, tn = min(IN_TM, m), IN_TN
    return pl.pallas_call(
        _inproj_kernel,
        grid=(n // tn, m // tm),
        in_specs=[pl.BlockSpec((tm, d), lambda j, i: (i, 0)),
                  pl.BlockSpec((None, tn, d), lambda j, i: (layer, j, 0), pipeline_mode=pl.Buffered(1)),
                  pl.BlockSpec((1, tn), lambda j, i: (0, j))],
        out_specs=pl.BlockSpec((tm, tn), lambda j, i: (i, j)),
        out_shape=jax.ShapeDtypeStruct((m, n), BF16),
        compiler_params=_cparams(("parallel", "parallel")),
        name="in_projection",
    )(hb, wt_all, b.reshape(1, n))


def _tproj_kernel(h_ref, wt_ref, bt_ref, ut_ref, gate_ref):
    acc = lax.dot_general(wt_ref[...], h_ref[...], _NT, preferred_element_type=F32) + bt_ref[...]
    for c in range(ut_ref.shape[0]):
        ut_ref[c] = acc[:T_WIDTH, c * ML_CHUNK:(c + 1) * ML_CHUNK].astype(BF16)
    gate_ref[...] = acc[T_WIDTH:]


def _t_projection(hb, wt_all, layer, bt):
    m, d = hb.shape
    tm = min(T_TM, m)
    rows = wt_all.shape[1]
    return pl.pallas_call(
        _tproj_kernel,
        grid=(m // tm,),
        in_specs=[pl.BlockSpec((tm, d), lambda i: (i, 0)),
                  pl.BlockSpec((None, rows, d), lambda i: (layer, 0, 0)),
                  pl.BlockSpec((rows, 1), lambda i: (0, 0))],
        out_specs=[pl.BlockSpec((tm // ML_CHUNK, T_WIDTH, ML_CHUNK), lambda i: (i, 0, 0)),
                   pl.BlockSpec((ML_GATES, tm), lambda i: (0, i))],
        out_shape=[jax.ShapeDtypeStruct((m // ML_CHUNK, T_WIDTH, ML_CHUNK), BF16),
                   jax.ShapeDtypeStruct((ML_GATES, m), F32)],
        compiler_params=_cparams(("parallel",)),
        name="t_projection",
    )(hb, wt_all, bt.reshape(rows, 1))


def _outproj_kernel(ya_ref, yb_ref, yc_ref, w_ref, b_ref, res_ref, g_ref, beta_ref, *refs, alpha):
    out_refs, wb_ref = refs[:-1], refs[-1]

    @pl.when(pl.program_id(0) == 0)
    def _():
        for src, n, dst in _OUT_ROW_COPIES:
            wb_ref[dst:dst + n, :] = w_ref[src:src + n, :].astype(BF16)

    slab = ya_ref.shape[0] // OUT_SLABS
    for rows in (slice(i * slab, (i + 1) * slab) for i in range(OUT_SLABS)):
        y = jnp.concatenate([ya_ref[rows, :], yb_ref[rows, :], yc_ref[rows, :]], axis=1)
        out = jnp.dot(y, wb_ref[...], preferred_element_type=F32) + b_ref[...]
        r = _layer_norm_rows(alpha * res_ref[rows, :] + out, g_ref[...], beta_ref[...])
        out_refs[0][rows, :] = r
        if len(out_refs) > 1:
            out_refs[1][rows, :] = r.astype(BF16)


def _out_projection(ya, yb, yc, w_all, layer, b, res, g, beta, alpha, want_bf16):
    m, d = res.shape
    tm = min(OUT_TM, m)
    row = lambda i: (i, 0)
    const = lambda i: (0, 0)
    out_specs = [pl.BlockSpec((tm, d), row)]
    out_shape = [jax.ShapeDtypeStruct((m, d), F32)]
    if want_bf16:
        out_specs.append(pl.BlockSpec((tm, d), row))
        out_shape.append(jax.ShapeDtypeStruct((m, d), BF16))
    vec = lambda v: v.reshape(1, d)
    return pl.pallas_call(
        functools.partial(_outproj_kernel, alpha=alpha),
        grid=(m // tm,),
        in_specs=[pl.BlockSpec((tm, WA_WIDTH), row),
                  pl.BlockSpec((tm, ML_WIDTH), row),
                  pl.BlockSpec((tm, NA_WIDTH), row),
                  pl.BlockSpec((None, MIX_WIDTH, d), lambda i: (layer, 0, 0), pipeline_mode=pl.Buffered(1)),
                  pl.BlockSpec((1, d), const),
                  pl.BlockSpec((tm, d), row),
                  pl.BlockSpec((1, d), const),
                  pl.BlockSpec((1, d), const)],
        out_specs=out_specs,
        out_shape=out_shape,
        scratch_shapes=[pltpu.VMEM((MIX_WIDTH, d), BF16)],
        compiler_params=_cparams(("arbitrary",)),
        name="out_projection",
    )(ya, yb, yc, w_all, vec(b), res, vec(g), vec(beta))


def _silu(z):
    return z * jax.nn.sigmoid(z)


def _scale_q(q):
    return (q.astype(F32) * (HEAD_DIM ** -0.5 * LOG2E)).astype(BF16)


def _wattn_kernel(q_ref, z_ref, k0_ref, k1_ref, k2_ref, k3_ref, v0_ref, v1_ref, v2_ref, v3_ref,
                  bias_e_ref, bias_o_ref, sink_ref, o_ref, s0_ref, s1_ref, p_ref):
    n = pl.program_id(1)
    blk = WA_BLOCK
    npair = WA_KV_HEADS // 2
    rows = 2 * WA_GROUP * blk
    lane = lax.broadcasted_iota(jnp.int32, (blk, 2 * HEAD_DIM), 1)
    left = lane < HEAD_DIM
    k_refs = (k0_ref, k1_ref, k2_ref, k3_ref)
    v_refs = (v0_ref, v1_ref, v2_ref, v3_ref)

    def step(s_read, s_write):
        for half in range(2):
            qrows = slice(half * blk, (half + 1) * blk)
            bias_ref = (bias_e_ref, bias_o_ref)[half]
            for i in range(npair):
                ks = slice(i * 2 * HEAD_DIM, (i + 1) * 2 * HEAD_DIM)
                k3 = jnp.concatenate([r[:, ks] for r in k_refs[half:half + 3]], axis=0)
                tiles = [_scale_q(q_ref[qrows, (WA_GROUP * i + j) * 2 * HEAD_DIM:(WA_GROUP * i + j + 1) * 2 * HEAD_DIM])
                         for j in range(WA_GROUP)]
                zero = jnp.zeros_like(tiles[0])
                lhs = jnp.concatenate([jnp.where(left, t, zero) for t in tiles]
                                      + [jnp.where(left, zero, t) for t in tiles], axis=0)
                s_write[half, i] = lax.dot_general(lhs, k3, (((1,), (1,)), ((), ())),
                                                   preferred_element_type=F32)
                v3 = jnp.concatenate([r[:, ks] for r in v_refs[half:half + 3]], axis=0)
                dens = []
                for c in range(rows // SM_CHUNK):
                    rs = slice(c * SM_CHUNK, (c + 1) * SM_CHUNK)
                    sk = sink_ref[0, 2 * WA_GROUP * i + c * SM_CHUNK // blk] * LOG2E
                    sc = s_read[half, i, rs, :] + bias_ref[i, rs, :]
                    m = jnp.maximum(jnp.max(sc, axis=-1, keepdims=True), sk)
                    p = jnp.exp2(sc - m)
                    dens.append(jnp.sum(p, axis=-1, keepdims=True) + jnp.exp2(sk - m))
                    p_ref[half, i, rs, :] = p.astype(BF16)
                pv = jnp.dot(p_ref[half, i], v3, preferred_element_type=F32)
                o = jnp.concatenate([pv[c * SM_CHUNK:(c + 1) * SM_CHUNK] / dens[c]
                                     for c in range(rows // SM_CHUNK)], axis=0)
                for j in range(WA_GROUP):
                    t = WA_GROUP * i + j
                    ot = jnp.where(left, o[j * blk:(j + 1) * blk],
                                   o[(WA_GROUP + j) * blk:(WA_GROUP + j + 1) * blk])
                    cs = slice(t * 2 * HEAD_DIM, (t + 1) * 2 * HEAD_DIM)
                    o_ref[qrows, cs] = (ot * _silu(z_ref[qrows, cs].astype(F32))).astype(BF16)

    @pl.when(n == 0)
    def _():
        s1_ref[...] = jnp.zeros(s1_ref.shape, F32)

    @pl.when(n % 2 == 0)
    def _():
        step(s1_ref, s0_ref)

    @pl.when(n % 2 == 1)
    def _():
        step(s0_ref, s1_ref)


def _window_attention(u3, bias, sink):
    bsz, s, _ = u3.shape
    nb = s // WA_BLOCK
    npr = nb // 2
    qw, kw = WA_WIDTH, WA_KV_WIDTH
    npair = WA_KV_HEADS // 2
    rows = 2 * WA_GROUP * WA_BLOCK
    clamp = lambda i: jnp.clip(i, 0, nb - 1)
    pair = lambda i: jnp.clip(i, 0, npr - 1)
    kspec = lambda off, c: pl.BlockSpec((None, WA_BLOCK, kw), lambda b, n: (b, clamp(2 * n + off), c))
    vspec = lambda off, c: pl.BlockSpec((None, WA_BLOCK, kw), lambda b, n: (b, clamp(2 * n - 2 + off), c))
    even_type = lambda b, n: (jnp.where(n <= 1, 0, 1), 0, 0, 0)
    odd_type = lambda b, n: (jnp.where(n == npr, 2, 1), 0, 0, 0)
    sshape = (2, npair, rows, 3 * WA_BLOCK)
    bias = bias.reshape(3, npair, rows, 3 * WA_BLOCK)
    return pl.pallas_call(
        _wattn_kernel,
        grid=(bsz, npr + 1),
        in_specs=[pl.BlockSpec((None, 2 * WA_BLOCK, qw), lambda b, n: (b, pair(n), A_Q // qw)),
                  pl.BlockSpec((None, 2 * WA_BLOCK, qw), lambda b, n: (b, pair(n - 1), A_Z // qw)),
                  kspec(-1, A_K // kw), kspec(0, A_K // kw), kspec(1, A_K // kw), kspec(2, A_K // kw),
                  vspec(-1, A_V // kw), vspec(0, A_V // kw), vspec(1, A_V // kw), vspec(2, A_V // kw),
                  pl.BlockSpec((None, npair, rows, 3 * WA_BLOCK), even_type),
                  pl.BlockSpec((None, npair, rows, 3 * WA_BLOCK), odd_type),
                  pl.BlockSpec(memory_space=pltpu.SMEM)],
        out_specs=pl.BlockSpec((None, 2 * WA_BLOCK, qw), lambda b, n: (b, pair(n - 1), 0)),
        out_shape=jax.ShapeDtypeStruct((bsz, s, qw), BF16),
        scratch_shapes=[pltpu.VMEM(sshape, F32), pltpu.VMEM(sshape, F32), pltpu.VMEM(sshape, BF16)],
        compiler_params=_cparams(("parallel", "arbitrary")),
        name="window_attention",
    )(u3, u3, u3, u3, u3, u3, u3, u3, u3, u3, bias, bias, sink.reshape(1, WA_HEADS))


def _t5_bucket_np(rel):
    half = T5_BUCKETS // 2
    max_exact = half // 2
    ret = np.where(rel > 0, half, 0)
    n = np.abs(rel)
    nf = np.maximum(n, 1).astype(np.float64)
    v = np.log(nf / max_exact) / math.log(T5_MAX_DIST / max_exact) * (half - max_exact)
    vr = np.round(v)
    v = np.where(np.abs(v - vr) < 1e-9, vr, v)
    large = np.minimum(max_exact + np.trunc(v).astype(np.int64), half - 1)
    return ret + np.where(n < max_exact, n, large)


def _wbias_kernel(t5t_ref, bucket_ref, o_ref):
    width = 4 * WA_BLOCK
    bk = bucket_ref[...]
    e = lax.broadcasted_iota(jnp.int32, (T5_BUCKETS, width), 0)
    onehot = jnp.where(e == bk, 1.0, 0.0).astype(F32)
    g = _dot_hi(t5t_ref[...], onehot) * LOG2E + jnp.where(bk < 0, NEG, 0.0)
    col = lax.broadcasted_iota(jnp.int32, (WA_BLOCK, 3 * WA_BLOCK), 1)
    for h in range(WA_HEADS):
        row = jnp.broadcast_to(g[h:h + 1, :], (WA_BLOCK, width))
        t = pltpu.roll(row, 3 * WA_BLOCK, 1, stride=1, stride_axis=0)[:, :3 * WA_BLOCK]
        o_ref[0, h] = jnp.where(col < WA_BLOCK, NEG, t)
        o_ref[1, h] = t
        o_ref[2, h] = jnp.where(col >= 2 * WA_BLOCK, NEG, t)


def _window_bias(t5_table):
    rel = np.arange(4 * WA_BLOCK) - 2 * WA_BLOCK
    bucket = np.where(np.abs(rel) <= WINDOW, _t5_bucket_np(rel), -1).astype(np.int32)
    t5t = jnp.pad(t5_table.astype(F32).T, ((0, 16 - WA_HEADS), (0, 0)))
    return pl.pallas_call(
        _wbias_kernel,
        out_shape=jax.ShapeDtypeStruct((3, WA_HEADS, WA_BLOCK, 3 * WA_BLOCK), F32),
        name="window_bias",
    )(t5t, jnp.asarray(bucket).reshape(1, -1))


def _natten_kernel(q_ref, z_ref, kp_ref, kc_ref, kn_ref, vp_ref, vc_ref, vn_ref, bias_ref, o_ref,
                   s0_ref, s1_ref, p_ref):
    n = pl.program_id(1)
    blk = NA_BLOCK
    lane = lax.broadcasted_iota(jnp.int32, (blk, 2 * HEAD_DIM), 1)
    left = lane < HEAD_DIM

    ones_col = jnp.where(lax.broadcasted_iota(jnp.int32, (3 * blk, 2 * HEAD_DIM), 1) == 0, 1.0, 0.0).astype(BF16)

    def step(s_read, s_write):
        for i in range(NA_HEADS // 2):
            cs = slice(i * 2 * HEAD_DIM, (i + 1) * 2 * HEAD_DIM)
            k3 = jnp.concatenate([kp_ref[:, cs], kc_ref[:, cs], kn_ref[:, cs]], axis=0)
            t = _scale_q(q_ref[:, cs])
            zero = jnp.zeros_like(t)
            lhs = jnp.concatenate([jnp.where(left, t, zero), jnp.where(left, zero, t)], axis=0)
            s_write[i] = lax.dot_general(lhs, k3, (((1,), (1,)), ((), ())), preferred_element_type=F32)
            pb = p_ref.at[i % NA_PBUF]
            for c in range(2 * blk // SM_CHUNK):
                rs = slice(c * SM_CHUNK, (c + 1) * SM_CHUNK)
                sc = s_read[i, rs, :] + bias_ref[i, rs, :]
                m = jnp.max(sc, axis=-1, keepdims=True)
                pb[rs, :] = jnp.exp2(sc - m).astype(BF16)
            if i > 0:
                weighted_values(i - 1)
        weighted_values(NA_HEADS // 2 - 1)

    def weighted_values(i):
        cs = slice(i * 2 * HEAD_DIM, (i + 1) * 2 * HEAD_DIM)
        v3 = jnp.concatenate([vp_ref[:, cs], vc_ref[:, cs], vn_ref[:, cs]], axis=0)
        vaug = jnp.concatenate([v3, ones_col], axis=1)
        out = jnp.dot(p_ref[i % NA_PBUF], vaug, preferred_element_type=F32)
        o = out[:, :2 * HEAD_DIM] / out[:, 2 * HEAD_DIM:2 * HEAD_DIM + 1]
        ot = jnp.where(left, o[:blk], o[blk:])
        o_ref[:, cs] = (ot * _silu(z_ref[:, cs].astype(F32))).astype(BF16)

    @pl.when(n == 0)
    def _():
        s1_ref[...] = jnp.zeros(s1_ref.shape, F32)

    @pl.when(n % 2 == 0)
    def _():
        step(s1_ref, s0_ref)

    @pl.when(n % 2 == 1)
    def _():
        step(s0_ref, s1_ref)


def _neighbourhood_attention(u3, bias, layer):
    bsz, s, _ = u3.shape
    nblk = s // NA_BLOCK
    w = NA_WIDTH
    npair = NA_HEADS // 2
    clamp = lambda i: jnp.clip(i, 0, nblk - 1)
    spec = lambda off, c: pl.BlockSpec((None, NA_BLOCK, w), lambda b, n: (b, clamp(n + off), c))
    btype = lambda b, n: (layer, jnp.where(n <= 1, 0, jnp.where(n == nblk, 2, 1)), 0, 0, 0)
    sshape = (npair, 2 * NA_BLOCK, 3 * NA_BLOCK)
    return pl.pallas_call(
        _natten_kernel,
        grid=(bsz, nblk + 1),
        in_specs=[spec(0, C_Q // w), spec(-1, C_Z // w),
                  spec(-1, C_K // w), spec(0, C_K // w), spec(1, C_K // w),
                  spec(-2, C_V // w), spec(-1, C_V // w), spec(0, C_V // w),
                  pl.BlockSpec((None, None) + sshape, btype)],
        out_specs=pl.BlockSpec((None, NA_BLOCK, w), lambda b, n: (b, clamp(n - 1), 0)),
        out_shape=jax.ShapeDtypeStruct((bsz, s, w), BF16),
        scratch_shapes=[pltpu.VMEM(sshape, F32), pltpu.VMEM(sshape, F32),
                        pltpu.VMEM((NA_PBUF,) + sshape[1:], BF16)],
        compiler_params=_cparams(("parallel", "arbitrary")),
        name="neighbourhood_attention",
    )(u3, u3, u3, u3, u3, u3, u3, u3, bias.reshape(bias.shape[:2] + sshape))


def _na_valid_rows(rows):
    kh = min(NA_KH_MAX, rows)
    nblk = rows // NA_ROWS
    out = []
    for j in (0, min(1, nblk - 1), nblk - 1):
        r = NA_ROWS * j + np.arange(NA_ROWS)[:, None]
        kr = NA_ROWS * (j - 1) + np.arange(3 * NA_ROWS)[None, :]
        rs = np.clip(r - kh // 2, 0, rows - kh)
        out.append((kr >= rs) & (kr < rs + kh))
    return np.stack(out)


def _nabias_kernel(rpb_ref, o_ref, *, valid):
    w = GRID_W
    j = lax.broadcasted_iota(jnp.int32, (32, 2 * w), 1)
    e = lax.broadcasted_iota(jnp.int32, (32, 2 * w), 0)
    dc = jnp.clip(j - w, -(NA_KW - 1), NA_KW - 1) + NA_KW - 1
    g = _dot_hi(rpb_ref[...], jnp.where(e == dc, 1.0, 0.0).astype(F32)) * LOG2E
    lane = lax.broadcasted_iota(jnp.int32, (w, 2 * w), 1)
    qc = lax.broadcasted_iota(jnp.int32, (w, 2 * w), 0)
    kc = lane & (w - 1)
    col_start = jnp.clip(qc - NA_KW // 2, 0, w - NA_KW)
    col_ok = (kc >= col_start) & (kc < col_start + NA_KW)
    left = lane < w
    neg = jnp.full((w, 2 * w), NEG, F32)

    def toeplitz(dr, shift):
        row = jnp.broadcast_to(g[dr:dr + 1, :], (w, 2 * w))
        return pltpu.roll(row, shift, 1, stride=1, stride_axis=0)

    pair = [jnp.where(col_ok, jnp.where(left, toeplitz(d, w), toeplitz(d + 1, 0)), NEG)
            for d in range(2 * NA_KH_MAX - 2)]
    for ty in range(3):
        for rl in range(NA_ROWS):
            for t in range(3 * NA_ROWS // 2):
                d = 2 * t - rl + NA_KH_MAX - 1 - NA_ROWS
                v0, v1 = bool(valid[ty, rl, 2 * t]), bool(valid[ty, rl, 2 * t + 1])
                if v0 and v1:
                    tile = pair[d]
                elif v0:
                    tile = jnp.where(left, pair[d], NEG)
                elif v1:
                    tile = jnp.where(left, NEG, pair[d])
                else:
                    tile = neg
                o_ref[ty, rl * w:(rl + 1) * w, 2 * t * w:(2 * t + 2) * w] = tile


def _na_bias(rpb_all, rows):
    depth = rpb_all.shape[0]
    rpb_p = jnp.pad(rpb_all.astype(F32), ((0, 0), (0, 0), (0, 1), (0, 1)))
    return pl.pallas_call(
        functools.partial(_nabias_kernel, valid=_na_valid_rows(rows)),
        grid=(depth, NA_HEADS),
        in_specs=[pl.BlockSpec((None, None, 16, 32), lambda l, h: (l, h, 0, 0))],
        out_specs=pl.BlockSpec((None, 3, None, NA_BLOCK, 3 * NA_BLOCK), lambda l, h: (l, 0, h, 0, 0)),
        out_shape=jax.ShapeDtypeStruct((depth, 3, NA_HEADS, NA_BLOCK, 3 * NA_BLOCK), F32),
        compiler_params=_cparams(("parallel", "parallel")),
        name="na_bias",
    )(rpb_p)


def _log_sigmoid(x):
    return jnp.minimum(x, 0.0) - jnp.log(1.0 + jnp.exp(-jnp.abs(x)))


def _tri(n, upper):
    r = lax.broadcasted_iota(jnp.int32, (n, n), 0)
    c = lax.broadcasted_iota(jnp.int32, (n, n), 1)
    return jnp.where((r <= c) if upper else (r >= c), 1.0, 0.0).astype(F32)


def _scan_max(x, axis, reverse, size):
    idx = lax.broadcasted_iota(jnp.int32, x.shape, axis)
    k = 1
    while k < size:
        if reverse:
            shifted = pltpu.roll(x, x.shape[axis] - k, axis)
            ok = idx < size - k
        else:
            shifted = pltpu.roll(x, k, axis)
            ok = idx >= k
        x = jnp.maximum(x, jnp.where(ok, shifted, NEG))
        k *= 2
    return x


def _mlstm_kernel(k_ref, qt_ref, vt_ref, ot_ref, zt_ref, gr_ref, fb_ref, ng_ref, y_ref,
                  state_ref, sprev_ref, rows_ref, gain_ref, et_ref, w2a_ref, w2b_ref, *, seq):
    L = ML_CHUNK
    nc = seq // L
    head = pl.program_id(1)
    scale = ML_HEAD_DIM ** -0.5
    lane = lax.broadcasted_iota(jnp.int32, (1, LANES), 1)

    for d in range(2):
        rev = d == 1
        fb = fb_ref[d, head]
        i_r = gr_ref[8 * d + head]
        lf_r = _log_sigmoid(gr_ref[8 * d + 4 + head] + fb)
        b_r = _dot_hi(lf_r, _tri(L, upper=not rev))
        g = jnp.broadcast_to(jnp.sum(lf_r, axis=1, keepdims=True), (nc, L))
        a_r = g - b_r + i_r
        m_loc = jnp.broadcast_to(jnp.max(a_r, axis=1, keepdims=True), (nc, L))
        jr = lax.broadcasted_iota(jnp.int32, (nc, nc), 0)
        jc = lax.broadcasted_iota(jnp.int32, (nc, nc), 1)
        before = jnp.where((jc > jr) if rev else (jc < jr), 1.0, 0.0).astype(F32)
        g_ex = _dot_hi(before, g)
        x = m_loc - (g_ex + g)
        row = lax.broadcasted_iota(jnp.int32, (nc, L), 0)
        if rev:
            x_prev = jnp.where(row < nc - 1, pltpu.roll(x, nc - 1, 0), NEG)
        else:
            x_prev = jnp.where(row >= 1, pltpu.roll(x, 1, 0), NEG)
        m_prev = g_ex + jnp.maximum(_scan_max(x_prev, 0, rev, nc), 0.0)
        m_after = jnp.maximum(g + m_prev, m_loc)
        gain_ref[d, 0] = jnp.exp(g + m_prev - m_after)
        gain_ref[d, 1] = jnp.exp(m_loc - m_after)
        e_r = i_r - b_r
        mu = jnp.maximum(m_prev, _scan_max(e_r, 1, rev, L))
        rows_ref[d, 0] = mu - math.log(scale)
        rows_ref[d, 1] = jnp.exp(m_prev - mu) * scale
        rows_ref[d, 2] = jnp.exp(-b_r - mu)
        rows_ref[d, 3] = jnp.exp(a_r - m_loc)
        e_pad = jnp.concatenate([e_r, jnp.zeros((LANES - nc, L), F32)], axis=0) if nc < LANES else e_r
        et_ref[d] = e_pad.T
        state_ref[d] = jnp.zeros((ML_AUG, ML_HEAD_DIM), F32)

    rr = lax.broadcasted_iota(jnp.int32, (L, L), 0)
    cc = lax.broadcasted_iota(jnp.int32, (L, L), 1)
    ones_row = jnp.where(lax.broadcasted_iota(jnp.int32, (ML_HEAD_DIM, L), 0) == 0, 1.0, 0.0).astype(BF16)

    def state_step(j, carry):
        for d in range(2):
            c = j if d == 0 else nc - 1 - j
            kc = k_ref[pl.ds(pl.multiple_of(c * L, L), L), :]
            vaug_t = jnp.concatenate([vt_ref[c], ones_row], axis=0)
            wv_t = (rows_ref[d, 3, pl.ds(c, 1), :] * vaug_t.astype(F32)).astype(BF16)
            s_loc = jnp.dot(wv_t, kc, preferred_element_type=F32)
            st = state_ref[d]
            sprev_ref[d, c] = st.astype(BF16)
            state_ref[d] = gain_ref[d, 0, pl.ds(c, 1), :] * st + gain_ref[d, 1, pl.ds(c, 1), :] * s_loc
        return carry

    lax.fori_loop(0, nc, state_step, 0, unroll=min(8, nc))

    ng_col = jnp.broadcast_to(ng_ref[...], (ML_HEAD_DIM, L))
    group = min(ML_GROUP, nc // 2)
    ngroups = nc // group

    def score_stage(g, w2_ref):
        for jj in range(group):
            c = g * group + jj
            kc = k_ref[pl.ds(pl.multiple_of(c * L, L), L), :]
            s_t = jnp.dot(kc, qt_ref[c], preferred_element_type=F32)
            for d in range(2):
                e_col = jnp.sum(jnp.where(lane == c, et_ref[d], 0.0), axis=1, keepdims=True)
                mask = (rr >= cc) if d == 1 else (rr <= cc)
                p_t = jnp.where(mask, jnp.exp(e_col - rows_ref[d, 0, pl.ds(c, 1), :]), 0.0)
                w2_ref[2 * jj + d] = (s_t * p_t).astype(BF16)

    def value_stage(g, w2_ref):
        for jj in range(group):
            c = g * group + jj
            q_t = qt_ref[c]
            vaug_t = jnp.concatenate([vt_ref[c], ones_row], axis=0)
            hs = None
            for d in range(2):
                out_t = (jnp.dot(vaug_t, w2_ref[2 * jj + d], preferred_element_type=F32)
                         + rows_ref[d, 1, pl.ds(c, 1), :]
                         * jnp.dot(sprev_ref[d, c], q_t, preferred_element_type=F32))
                den = out_t[ML_HEAD_DIM:ML_HEAD_DIM + 1]
                h_d = out_t[:ML_HEAD_DIM] / jnp.maximum(jnp.abs(den), rows_ref[d, 2, pl.ds(c, 1), :])
                hs = h_d if hs is None else hs + h_d
            hs = jax.nn.sigmoid(ot_ref[c].astype(F32)) * hs
            mu = jnp.mean(hs, axis=0, keepdims=True)
            hc = hs - mu
            var = jnp.mean(hc * hc, axis=0, keepdims=True)
            y_t = hc * lax.rsqrt(var + LN_EPS) * ng_col * _silu(zt_ref[c].astype(F32))
            y_ref[pl.ds(pl.multiple_of(c * L, L), L), :] = y_t.T.astype(BF16)

    w2b_ref[...] = jnp.zeros(w2b_ref.shape, BF16)

    def pipeline_step(g, carry):
        @pl.when(g % 2 == 0)
        def _():
            score_stage(g, w2a_ref)
            value_stage(jnp.maximum(g - 1, 0), w2b_ref)

        @pl.when(g % 2 == 1)
        def _():
            score_stage(g, w2b_ref)
            value_stage(g - 1, w2a_ref)
        return carry

    lax.fori_loop(0, ngroups, pipeline_step, 0)
    value_stage(ngroups - 1, w2b_ref if (ngroups - 1) % 2 else w2a_ref)


def _mlstm(u3, ut, gates, f_bias, norm_g):
    bsz, s, _ = u3.shape
    d = ML_HEAD_DIM
    nc = s // ML_CHUNK
    tcol = lambda part: pl.BlockSpec((nc, d, ML_CHUNK), lambda b, h: (b, part * ML_HEADS + h, 0))
    return pl.pallas_call(
        functools.partial(_mlstm_kernel, seq=s),
        grid=(bsz, ML_HEADS),
        in_specs=[pl.BlockSpec((None, s, d), lambda b, h: (b, 0, B_K // d + h)),
                  tcol(0), tcol(1), tcol(2), tcol(3),
                  pl.BlockSpec((ML_GATES, None, nc, ML_CHUNK), lambda b, h: (0, b, 0, 0)),
                  pl.BlockSpec(memory_space=pltpu.SMEM),
                  pl.BlockSpec((d, 1), lambda b, h: (h, 0))],
        out_specs=pl.BlockSpec((None, s, d), lambda b, h: (b, 0, h)),
        out_shape=jax.ShapeDtypeStruct((bsz, s, ML_WIDTH), BF16),
        scratch_shapes=[pltpu.VMEM((2, ML_AUG, d), F32),
                        pltpu.VMEM((2, nc, ML_AUG, d), BF16),
                        pltpu.VMEM((2, 4, nc, ML_CHUNK), F32),
                        pltpu.VMEM((2, 2, nc, ML_CHUNK), F32),
                        pltpu.VMEM((2, ML_CHUNK, LANES), F32),
                        pltpu.VMEM((2 * min(ML_GROUP, nc // 2), ML_CHUNK, ML_CHUNK), BF16),
                        pltpu.VMEM((2 * min(ML_GROUP, nc // 2), ML_CHUNK, ML_CHUNK), BF16)],
        compiler_params=_cparams(("parallel", "parallel")),
        name="mlstm",
    )(u3, ut, ut, ut, ut, gates, f_bias, norm_g.reshape(ML_WIDTH, 1))


def _a_head_copies(src, dst):
    out = []
    for i in range(2):
        for j in range(WA_GROUP):
            for half in range(2):
                h = 2 * WA_GROUP * i + WA_GROUP * half + j
                out.append((src + h * HEAD_DIM, HEAD_DIM, dst + ((WA_GROUP * i + j) * 2 + half) * HEAD_DIM))
    return out


_NATURAL_COPIES = ([(4624, 3072, C_Q)] + _a_head_copies(0, A_Q) + _a_head_copies(1280, A_Z)
                   + [(768, 512, A_K), (2560, 512, B_K)])
_FEATURE_MAJOR_COPIES = [(2048, 512, 0), (3072, 1536, 512), (4608, ML_GATES, T_WIDTH)]
_OUT_ROW_COPIES = [(s0, n, d0) for s0, n, d0 in _a_head_copies(0, 0)] + [(WA_WIDTH, MIX_WIDTH - WA_WIDTH, WA_WIDTH)]


def _inprep_kernel(wt_ref, wn_ref, wf_ref):
    for src, width, dst in _NATURAL_COPIES:
        wn_ref[dst:dst + width, :] = wt_ref[src:src + width, :].astype(BF16)
    for src, width, dst in _FEATURE_MAJOR_COPIES:
        wf_ref[dst:dst + width, :] = wt_ref[src:src + width, :].astype(BF16)


def _prep_in_weights(w_in):
    depth, d, n = w_in.shape
    tk = 256
    return pl.pallas_call(
        _inprep_kernel,
        grid=(depth, d // tk),
        in_specs=[pl.BlockSpec((None, n, tk), lambda l, i: (l, 0, i))],
        out_specs=[pl.BlockSpec((None, U_WIDTH, tk), lambda l, i: (l, 0, i)),
                   pl.BlockSpec((None, T_WIDTH + ML_GATES, tk), lambda l, i: (l, 0, i))],
        out_shape=[jax.ShapeDtypeStruct((depth, U_WIDTH, d), BF16),
                   jax.ShapeDtypeStruct((depth, T_WIDTH + ML_GATES, d), BF16)],
        compiler_params=_cparams(("parallel", "parallel")),
        name="prep_in_weights",
    )(jnp.swapaxes(w_in, 1, 2))


def _gather_columns(v, copies, width):
    out = jnp.zeros(v.shape[:-1] + (width,), v.dtype)
    for src, n, dst in copies:
        out = out.at[..., dst:dst + n].set(v[..., src:src + n])
    return out


def kernel(x, emb_ln_g, emb_ln_b, w_in, b_in, w_out, b_out, ln_g, ln_b, t5_bias, sink, ml_f_bias,
           ml_norm_g, na_rpb):
    bsz, s, d = x.shape
    depth = w_in.shape[0]
    alpha = (2 * depth) ** 0.25
    m = bsz * s
    nc = s // ML_CHUNK
    assert d == D_MODEL and s % NA_BLOCK == 0 and s // GRID_W >= NA_KH_MAX and nc <= LANES

    res, hb = _input_norm(x.reshape(m, d), emb_ln_g, emb_ln_b)
    bias_a = _window_bias(t5_bias)
    bias_c = _na_bias(na_rpb, s // GRID_W)
    wn_all, wf_all = _prep_in_weights(w_in)
    for l in range(depth):
        bn = _gather_columns(b_in[l], _NATURAL_COPIES, U_WIDTH)
        bf = _gather_columns(b_in[l], _FEATURE_MAJOR_COPIES, T_WIDTH + ML_GATES)
        u3 = _in_projection(hb, wn_all, l, bn).reshape(bsz, s, U_WIDTH)
        ut, g = _t_projection(hb, wf_all, l, bf)
        g = g.reshape(ML_GATES, bsz, nc, ML_CHUNK)
        ya = _window_attention(u3, bias_a, sink[l])
        yb = _mlstm(u3, ut, g, ml_f_bias[l], ml_norm_g[l])
        yc = _neighbourhood_attention(u3, bias_c, l)
        last = l == depth - 1
        outs = _out_projection(ya.reshape(m, WA_WIDTH), yb.reshape(m, ML_WIDTH), yc.reshape(m, NA_WIDTH),
                               w_out, l, b_out[l], res, ln_g[l], ln_b[l], alpha, not last)
        res = outs[0]
        hb = None if last else outs[1]
    return res.reshape(bsz, s, d)
```
